```python
import jax, jax.numpy as jnp
from jax import lax
import numpy as np

D_MODEL = 2048
BATCH = 1
SEQ = 8192
DEPTH = 1

GRID_W = 64
DN_HEADS = 8
DN_HEAD_DIM = 128
DN_WIDTH = DN_HEADS * DN_HEAD_DIM
DN_CONV = 5
DN_CHUNK = 64
NA_HEADS = 16
NA_HEAD_DIM = 64
NA_WIDTH = NA_HEADS * NA_HEAD_DIM
NA_ROWS_MAX = 8
NA_COLS = 16
N_BRANCHES = 2
N_EXPERTS = 32
TOP_K = 4
D_EXPERT = D_MODEL
SWIGLU_LIMIT = 7.0
SWIGLU_ALPHA = 1.702
MOE_BLOCK = 128
PLE_DIM = 256
RMS_EPS = 1e-6

COLS = (3 * DN_WIDTH, DN_WIDTH, 2 * DN_HEADS, 2 * DN_HEADS, 3 * NA_WIDTH, N_BRANCHES * D_MODEL)
N_IN = 3 * DN_WIDTH + DN_WIDTH + 4 * DN_HEADS + 3 * NA_WIDTH + N_BRANCHES * D_MODEL

kernel_name = "hybrid_deltanet_natten_moe_block"


def rms_norm(x, g):
    xf = x.astype(jnp.float32)
    y = xf * lax.rsqrt(jnp.mean(xf * xf, axis=-1, keepdims=True) + RMS_EPS)
    return (y * g.astype(jnp.float32)).astype(x.dtype)


def l2_normalize(x):
    return x * lax.rsqrt(jnp.sum(x * x, axis=-1, keepdims=True) + 1e-6)


def split_cols(t, sizes):
    idx = [int(i) for i in np.cumsum(sizes)[:-1]]
    return jnp.split(t, idx, axis=-1)


def centred_depthwise_conv(x, w):
    c = x.shape[-1]
    pad = (w.shape[0] - 1) // 2
    return lax.conv_general_dilated(
        x, w[:, None, :].astype(x.dtype), window_strides=(1,), padding=[(pad, pad)],
        dimension_numbers=("NWC", "WIO", "NWC"), feature_group_count=c)


def chunked_gated_delta_rule(q, k, v, beta, g):
    b, h, s, dk = q.shape
    dv = v.shape[-1]
    c = DN_CHUNK
    n = s // c
    q = q.reshape(b, h, n, c, dk)
    k = k.reshape(b, h, n, c, dk)
    v = v.reshape(b, h, n, c, dv)
    beta = beta.reshape(b, h, n, c)
    g = jnp.cumsum(g.reshape(b, h, n, c), axis=-1)
    incl = jnp.tril(jnp.ones((c, c), dtype=bool))
    strict = jnp.tril(jnp.ones((c, c), dtype=bool), -1)
    decay = jnp.exp(jnp.where(incl, g[..., :, None] - g[..., None, :], -jnp.inf))
    k_beta = k * beta[..., None]
    lower = jnp.where(strict, jnp.einsum("bhnik,bhnjk->bhnij", k_beta, k) * decay, 0.0)
    lhs = lower + jnp.eye(c, dtype=jnp.float32)
    rhs = jnp.concatenate([v * beta[..., None], k_beta * jnp.exp(g)[..., None]], axis=-1)
    sol = lax.linalg.triangular_solve(lhs, rhs, left_side=True, lower=True, unit_diagonal=True)
    u, w = sol[..., :dv], sol[..., dv:]
    qk = jnp.einsum("bhnik,bhnjk->bhnij", q, k) * decay
    g_last = g[..., -1]
    q_dec = q * jnp.exp(g)[..., None]
    k_tail = k * jnp.exp(g_last[..., None] - g)[..., None]

    def step(state, xs):
        q_c, k_c, u_c, w_c, qk_c, gl_c = xs
        v_new = u_c - jnp.einsum("bhck,bhkv->bhcv", w_c, state)
        o_c = jnp.einsum("bhck,bhkv->bhcv", q_c, state) + jnp.einsum("bhij,bhjv->bhiv", qk_c, v_new)
        state = state * jnp.exp(gl_c)[..., None, None] + jnp.einsum("bhck,bhcv->bhkv", k_c, v_new)
        return state, o_c

    xs = tuple(jnp.moveaxis(t, 2, 0) for t in (q_dec, k_tail, u, w, qk, g_last))
    state0 = jnp.zeros((b, h, dk, dv), jnp.float32)
    _, o = lax.scan(step, state0, xs)
    return jnp.moveaxis(o, 0, 2).reshape(b, h, s, dv)


def bidirectional_gated_deltanet(qkv, z, b_raw, a_raw, conv_w, a_log, dt_bias, norm_g):
    bsz, s, _ = qkv.shape
    qkv = jax.nn.silu(centred_depthwise_conv(qkv, conv_w))
    q, k, v = jnp.split(qkv, 3, axis=-1)

    def heads(t):
        return t.reshape(bsz, s, DN_HEADS, DN_HEAD_DIM).transpose(0, 2, 1, 3).astype(jnp.float32)

    q = l2_normalize(heads(q)) * (DN_HEAD_DIM ** -0.5)
    k = l2_normalize(heads(k))
    v = heads(v)
    b_raw = b_raw.reshape(bsz, s, 2, DN_HEADS).astype(jnp.float32)
    a_raw = a_raw.reshape(bsz, s, 2, DN_HEADS).astype(jnp.float32)
    beta = jax.nn.sigmoid(b_raw)
    g = -jnp.exp(a_log.astype(jnp.float32)) * jax.nn.softplus(a_raw + dt_bias.astype(jnp.float32))
    to_bhs = lambda t: t.transpose(0, 2, 1)
    flip = lambda t: jnp.flip(t, axis=2)
    o_fwd = chunked_gated_delta_rule(q, k, v, to_bhs(beta[:, :, 0]), to_bhs(g[:, :, 0]))
    o_bwd = flip(chunked_gated_delta_rule(flip(q), flip(k), flip(v),
                                          flip(to_bhs(beta[:, :, 1])), flip(to_bhs(g[:, :, 1]))))
    o = (o_fwd + o_bwd).transpose(0, 2, 1, 3)
    zh = z.reshape(bsz, s, DN_HEADS, DN_HEAD_DIM).astype(jnp.float32)
    o = rms_norm(o, norm_g) * jax.nn.silu(zh)
    return o.reshape(bsz, s, DN_WIDTH).astype(qkv.dtype)


def neighbourhood_attention_2d(q, k, v, rpb):
    bsz, s, h, d = q.shape
    rows = s // GRID_W
    wr = min(NA_ROWS_MAX, rows)
    qg = q.reshape(bsz, rows, GRID_W, h, d)
    kg = k.reshape(bsz, rows, GRID_W, h, d)
    vg = v.reshape(bsz, rows, GRID_W, h, d)
    col = jnp.arange(GRID_W)
    col_start = jnp.clip(col - NA_COLS // 2, 0, GRID_W - NA_COLS)
    col_idx = col_start[:, None] + jnp.arange(NA_COLS)[None, :]
    dc = col_idx - col[:, None] + (NA_COLS - 1)
    scale = d ** -0.5

    def one_row(r):
        rs = jnp.clip(r - wr // 2, 0, rows - wr)
        k_rows = lax.dynamic_slice_in_dim(kg, rs, wr, axis=1)
        v_rows = lax.dynamic_slice_in_dim(vg, rs, wr, axis=1)
        k_nb = k_rows[:, :, col_idx]
        v_nb = v_rows[:, :, col_idx]
        q_r = lax.dynamic_index_in_dim(qg, r, axis=1, keepdims=False)
        dr = rs + jnp.arange(wr) - r + (NA_ROWS_MAX - 1)
        bias = rpb[:, dr][:, :, dc].transpose(0, 2, 1, 3).astype(jnp.float32)
        sc = jnp.einsum("bchd,bicjhd->bhcij", q_r, k_nb).astype(jnp.float32) * scale + bias[None]
        pr = jax.nn.softmax(sc.reshape(bsz, h, GRID_W, wr * NA_COLS), axis=-1)
        pr = pr.reshape(bsz, h, GRID_W, wr, NA_COLS).astype(v.dtype)
        return jnp.einsum("bhcij,bicjhd->bchd", pr, v_nb)

    out = lax.map(one_row, jnp.arange(rows))
    return jnp.moveaxis(out, 0, 1).reshape(bsz, s, h * d)


def moe_clamped_swiglu(h, w_router, b_router, w_gate_up, b_gate_up, w_down, b_down):
    bsz, s, dm = h.shape
    n_tok = bsz * s
    hf = h.reshape(n_tok, dm)
    logits = (hf @ w_router + b_router).astype(jnp.float32)
    top_logit, top_e = lax.top_k(logits, TOP_K)
    top_w = jax.nn.softmax(top_logit, axis=-1)
    n_assign = n_tok * TOP_K
    flat_e = top_e.reshape(-1)
    order = jnp.argsort(flat_e)
    sorted_e = flat_e[order]
    sorted_tok = order // TOP_K
    sorted_w = top_w.reshape(-1)[order]
    counts = jnp.bincount(flat_e, length=N_EXPERTS)
    padded = (counts + MOE_BLOCK - 1) // MOE_BLOCK * MOE_BLOCK
    pad_end = jnp.cumsum(padded)
    pad_start = pad_end - padded
    start = jnp.cumsum(counts) - counts
    dest = pad_start[sorted_e] + jnp.arange(n_assign) - start[sorted_e]
    n_rows = -(-(n_assign + N_EXPERTS * (MOE_BLOCK - 1)) // MOE_BLOCK) * MOE_BLOCK
    n_blocks = n_rows // MOE_BLOCK
    row_tok = jnp.full((n_rows,), n_tok, jnp.int32).at[dest].set(sorted_tok.astype(jnp.int32))
    x_rows = jnp.concatenate([hf, jnp.zeros((1, dm), hf.dtype)], axis=0)[row_tok]
    block_e = jnp.minimum(
        jnp.searchsorted(pad_end, jnp.arange(n_blocks) * MOE_BLOCK, side="right"), N_EXPERTS - 1)

    def expert_block(args):
        xb, e = args
        gu = xb @ w_gate_up[e] + b_gate_up[e]
        gate, up = jnp.split(gu, 2, axis=-1)
        gate = jnp.minimum(gate, SWIGLU_LIMIT)
        up = jnp.clip(up, -SWIGLU_LIMIT, SWIGLU_LIMIT)
        act = (up + 1.0) * (gate * jax.nn.sigmoid(SWIGLU_ALPHA * gate))
        return act @ w_down[e] + b_down[e]

    y_rows = lax.map(expert_block, (x_rows.reshape(n_blocks, MOE_BLOCK, dm), block_e))
    y_rows = y_rows.reshape(n_rows, dm)
    y_assign = y_rows[dest] * sorted_w[:, None].astype(y_rows.dtype)
    out = jnp.zeros_like(hf).at[sorted_tok].add(y_assign)
    return out.reshape(bsz, s, dm)


def setup_inputs(seed: int = 0) -> dict:
    key = jax.random.key(seed)
    ks = jax.random.split(key, 24)

    def nrm(k, shape, scale):
        return jax.random.normal(k, shape, jnp.float32) * scale

    a_init = jax.random.uniform(ks[4], (DEPTH, 2, DN_HEADS), jnp.float32, 1.0, 16.0)
    dt = jnp.exp(jax.random.uniform(ks[5], (DEPTH, 2, DN_HEADS), jnp.float32,
                                    float(np.log(1e-3)), float(np.log(1e-1))))
    return {
        "x": nrm(ks[0], (BATCH, SEQ, D_MODEL), 1.0),
        "p": nrm(ks[1], (DEPTH, BATCH, SEQ, PLE_DIM), 1.0),
        "norm_mix_g": 1.0 + nrm(ks[2], (DEPTH, D_MODEL), 0.1),
        "w_in": nrm(ks[3], (DEPTH, D_MODEL, N_IN), D_MODEL ** -0.5),
        "dn_conv_w": nrm(ks[6], (DEPTH, DN_CONV, 3 * DN_WIDTH), DN_CONV ** -0.5),
        "dn_a_log": jnp.log(a_init),
        "dn_dt_bias": dt + jnp.log(-jnp.expm1(-dt)),
        "dn_norm_g": 1.0 + nrm(ks[7], (DEPTH, DN_HEAD_DIM), 0.1),
        "na_rpb": nrm(ks[8], (DEPTH, NA_HEADS, 2 * NA_ROWS_MAX - 1, 2 * NA_COLS - 1), 0.5),
        "w_branch_a": nrm(ks[9], (DEPTH, DN_WIDTH, D_MODEL), DN_WIDTH ** -0.5),
        "w_branch_b": nrm(ks[10], (DEPTH, NA_WIDTH, D_MODEL), NA_WIDTH ** -0.5),
        "w_out": nrm(ks[11], (DEPTH, D_MODEL, D_MODEL), D_MODEL ** -0.5),
        "norm_ffn_g": 1.0 + nrm(ks[12], (DEPTH, D_MODEL), 0.1),
        "w_router": nrm(ks[13], (DEPTH, D_MODEL, N_EXPERTS), D_MODEL ** -0.5),
        "b_router": nrm(ks[14], (DEPTH, N_EXPERTS), 0.01),
        "w_gate_up": nrm(ks[15], (DEPTH, N_EXPERTS, D_MODEL, 2 * D_EXPERT), D_MODEL ** -0.5),
        "b_gate_up": nrm(ks[16], (DEPTH, N_EXPERTS, 2 * D_EXPERT), 0.01),
        "w_down": nrm(ks[17], (DEPTH, N_EXPERTS, D_EXPERT, D_MODEL), D_EXPERT ** -0.5),
        "b_down": nrm(ks[18], (DEPTH, N_EXPERTS, D_MODEL), 0.01),
        "norm_ple_g": 1.0 + nrm(ks[19], (DEPTH, D_MODEL), 0.1),
        "w_ple_gate": nrm(ks[20], (DEPTH, D_MODEL, D_MODEL), D_MODEL ** -0.5),
        "w_ple_proj": nrm(ks[21], (DEPTH, PLE_DIM, D_MODEL), PLE_DIM ** -0.5),
        "norm_final_g": 1.0 + nrm(ks[22], (D_MODEL,), 0.1),
    }


def reference(x, p, norm_mix_g, w_in, dn_conv_w, dn_a_log, dn_dt_bias, dn_norm_g, na_rpb,
              w_branch_a, w_branch_b, w_out, norm_ffn_g, w_router, b_router, w_gate_up,
              b_gate_up, w_down, b_down, norm_ple_g, w_ple_gate, w_ple_proj, norm_final_g):
    bsz, s, _ = x.shape
    for i in range(DEPTH):
        h = rms_norm(x, norm_mix_g[i])
        proj = h @ w_in[i]
        dn_qkv, dn_z, dn_b, dn_a, na_qkv, gates = split_cols(proj, COLS)
        y_a = bidirectional_gated_deltanet(dn_qkv, dn_z, dn_b, dn_a, dn_conv_w[i], dn_a_log[i],
                                           dn_dt_bias[i], dn_norm_g[i]) @ w_branch_a[i]
        na_q, na_k, na_v = [t.reshape(bsz, s, NA_HEADS, NA_HEAD_DIM)
                            for t in jnp.split(na_qkv, 3, axis=-1)]
        y_b = neighbourhood_attention_2d(na_q, na_k, na_v, na_rpb[i]) @ w_branch_b[i]
        g_a, g_b = jnp.split(gates, 2, axis=-1)
        mixed = jax.nn.sigmoid(g_a) * y_a + jax.nn.sigmoid(g_b) * y_b
        x = x + mixed @ w_out[i]
        x = x + moe_clamped_swiglu(rms_norm(x, norm_ffn_g[i]), w_router[i], b_router[i],
                                   w_gate_up[i], b_gate_up[i], w_down[i], b_down[i])
        ple_gate = jax.nn.sigmoid(rms_norm(x, norm_ple_g[i]) @ w_ple_gate[i])
        x = x + ple_gate * (p[i] @ w_ple_proj[i])
    return rms_norm(x, norm_final_g)
```

```python
import functools

import jax
import jax.numpy as jnp
from jax import lax
from jax.experimental import pallas as pl
from jax.experimental.pallas import tpu as pltpu

F32 = jnp.float32
BF16 = jnp.bfloat16
I32 = jnp.int32
U32 = jnp.uint32

GRID_W = 64
DN_HEADS = 8
DN_HEAD_DIM = 128
DN_WIDTH = DN_HEADS * DN_HEAD_DIM
DN_CONV = 5
NA_HEADS = 16
NA_HEAD_DIM = 64
NA_WIDTH = NA_HEADS * NA_HEAD_DIM
NA_ROWS = 8
NA_COLS = 16
N_EXPERTS = 32
TOP_K = 4
SWIGLU_LIMIT = 7.0
SWIGLU_ALPHA = 1.702
RMS_EPS = 1e-6

LANES = 128
VMEM_LIMIT = 56 * 1024 * 1024

DN_TILE = 256
DN_BLOCK = 16
MOE_SUB = 256
MOE_TM = 2048
MOE_TN = 256
NEG_BIG = -1e30


def _sigmoid(x):
    return 1.0 / (1.0 + jnp.exp(-x))


def _dot(a, b):
    return jnp.dot(a, b, preferred_element_type=F32)


def _dot_nt(a, b):
    return lax.dot_general(a, b, (((1,), (1,)), ((), ())), preferred_element_type=F32)


def _params(sem, limit=VMEM_LIMIT):
    return pltpu.CompilerParams(dimension_semantics=sem, vmem_limit_bytes=limit)


def _inproj_body(x_ref, g_ref, w_ref, ws_ref, o_ref, os_ref, h_ref):
    @pl.when(pl.program_id(1) == 0)
    def _():
        x = x_ref[...]
        r = lax.rsqrt(jnp.mean(x * x, axis=-1, keepdims=True) + RMS_EPS)
        h = (x * r * g_ref[...]).astype(BF16)
        h_ref[...] = h
        os_ref[...] = _dot(h, ws_ref[...])

    o_ref[...] = _dot(h_ref[...], w_ref[...]).astype(o_ref.dtype)


def in_projection(x, g, w_main, w_small, tm=512, tn=1024):
    s, d = x.shape
    n = w_main.shape[1]
    return pl.pallas_call(
        _inproj_body,
        grid=(s // tm, n // tn),
        in_specs=[
            pl.BlockSpec((tm, d), lambda i, j: (i, 0)),
            pl.BlockSpec((1, d), lambda i, j: (0, 0)),
            pl.BlockSpec((d, tn), lambda i, j: (0, j)),
            pl.BlockSpec((d, LANES), lambda i, j: (0, 0)),
        ],
        out_specs=[
            pl.BlockSpec((tm, tn), lambda i, j: (i, j)),
            pl.BlockSpec((tm, LANES), lambda i, j: (i, 0)),
        ],
        out_shape=[jax.ShapeDtypeStruct((s, n), BF16), jax.ShapeDtypeStruct((s, LANES), F32)],
        scratch_shapes=[pltpu.VMEM((tm, d), BF16)],
        compiler_params=_params(("parallel", "arbitrary")),
        name="in_proj",
    )(x, g.reshape(1, d), w_main, w_small)


def _dnprep_body(x_ref, w_ref, o_ref, pad_ref, *, seq, chunk):
    cb = pl.program_id(0)
    n_chunks = seq // chunk
    zeros = jnp.zeros((16, LANES), F32)
    pad_ref[0:16, :] = zeros
    pad_ref[seq + 16:seq + 32, :] = zeros

    def fill(c, carry):
        r0 = pl.multiple_of(c * chunk, chunk)
        pad_ref[pl.ds(r0 + 16, chunk), :] = x_ref[pl.ds(r0, chunk), :].astype(F32)
        return carry

    lax.fori_loop(0, n_chunks, fill, 0)

    w = w_ref[...]
    is_v = cb >= 2 * DN_HEADS
    scale = jnp.where(cb < DN_HEADS, DN_HEAD_DIM ** -0.5, 1.0).astype(F32)

    def body(c, carry):
        r0 = pl.multiple_of(c * chunk, chunk)
        win = pad_ref[pl.ds(r0 + 8, chunk + 16), :]
        y = win[6:6 + chunk] * w[0:1]
        for j in range(1, DN_CONV):
            y = y + win[6 + j:6 + j + chunk] * w[j:j + 1]
        y = y * _sigmoid(y)
        ss = jnp.sum(y * y, axis=-1, keepdims=True)
        yn = y * (lax.rsqrt(ss + 1e-6) * scale)
        o_ref[pl.ds(r0, chunk), :] = jnp.where(is_v, y, yn).astype(o_ref.dtype)
        return carry

    lax.fori_loop(0, n_chunks, body, 0)


def dn_prep(proj, conv_w, chunk=512):
    s = proj.shape[0]
    nb = 3 * DN_WIDTH // LANES
    w = jnp.zeros((8, 3 * DN_WIDTH), F32).at[:DN_CONV].set(conv_w)
    return pl.pallas_call(
        functools.partial(_dnprep_body, seq=s, chunk=chunk),
        grid=(nb,),
        in_specs=[
            pl.BlockSpec((s, LANES), lambda c: (0, c)),
            pl.BlockSpec((8, LANES), lambda c: (0, c)),
        ],
        out_specs=pl.BlockSpec((s, LANES), lambda c: (0, c)),
        out_shape=jax.ShapeDtypeStruct((s, 3 * DN_WIDTH), BF16),
        scratch_shapes=[pltpu.VMEM((s + 32, LANES), F32)],
        compiler_params=_params(("parallel",)),
        name="dn_prep",
    )(proj, w)


def _gates_body(s_ref, par_ref, col_ref, row_ref):
    x = s_ref[...]
    t = x.shape[0]
    lane = lax.broadcasted_iota(I32, x.shape, 1)
    beta = _sigmoid(x)
    z = x + par_ref[1:2, :]
    softplus = jnp.maximum(z, 0.0) + jnp.log(1.0 + jnp.exp(-jnp.abs(z)))
    g = par_ref[0:1, :] * softplus
    ri = lax.broadcasted_iota(I32, (t, t), 0)
    ci = lax.broadcasted_iota(I32, (t, t), 1)
    lower = jnp.where(ci <= ri, 1.0, 0.0).astype(F32)
    upper = jnp.where(ci >= ri, 1.0, 0.0).astype(F32)
    hi = lax.Precision.HIGHEST
    g_prefix = jnp.dot(lower, g, precision=hi, preferred_element_type=F32)
    g_suffix = jnp.dot(upper, g, precision=hi, preferred_element_type=F32)
    cum = jnp.where(lane < 16 + DN_HEADS, g_prefix, g_suffix)
    out = jnp.where(lane < 16, beta, jnp.where(lane < 32, cum, 0.0))
    col_ref[...] = out
    row_ref[...] = out.T


def dn_gates(small, a_log, dt_bias):
    s = small.shape[0]
    par = jnp.zeros((8, LANES), F32)
    par = par.at[0, 16:32].set(-jnp.exp(a_log.reshape(-1).astype(F32)))
    par = par.at[1, 16:32].set(dt_bias.reshape(-1).astype(F32))
    t = DN_TILE
    return pl.pallas_call(
        _gates_body,
        grid=(s // t,),
        in_specs=[
            pl.BlockSpec((t, LANES), lambda i: (i, 0)),
            pl.BlockSpec((8, LANES), lambda i: (0, 0)),
        ],
        out_specs=[
            pl.BlockSpec((t, LANES), lambda i: (i, 0)),
            pl.BlockSpec((LANES, t), lambda i: (0, i)),
        ],
        out_shape=[jax.ShapeDtypeStruct((s, LANES), F32), jax.ShapeDtypeStruct((LANES, s), F32)],
        compiler_params=_params(("parallel",)),
        name="dn_gates",
    )(small, par)


def _neumann_step(acc, x_b):
    x2 = _dot(x_b, x_b)
    x2_b = x2.astype(BF16)
    acc = acc + x2 + _dot(acc.astype(BF16), x2_b)
    return acc, x2_b


def _dnchunk_body(q_ref, k_ref, v_ref, col_ref, row_ref, u_ref, w_ref, qd_ref, kt_ref, qk_ref):
    h = pl.program_id(1)
    c = DN_TILE
    q = q_ref[...]
    k = k_ref[...]
    v = v_ref[...]
    col = col_ref[...]
    row = row_ref[...]
    lane = lax.broadcasted_iota(I32, col.shape, 1)
    sub = lax.broadcasted_iota(I32, row.shape, 0)

    def col_pick(idx):
        return jnp.sum(jnp.where(lane == idx, col, 0.0), axis=1, keepdims=True)

    def row_pick(idx):
        return jnp.sum(jnp.where(sub == idx, row, 0.0), axis=0, keepdims=True)

    ri = lax.broadcasted_iota(I32, (c, c), 0)
    ci = lax.broadcasted_iota(I32, (c, c), 1)
    same_block = (ri // DN_BLOCK) == (ci // DN_BLOCK)
    gram = _dot_nt(k, k)
    qk = _dot_nt(q, k)
    qf = q.astype(F32)
    kf = k.astype(F32)
    vf = v.astype(F32)
    n_steps = (DN_BLOCK - 1).bit_length() - 1

    for d in range(2):
        beta = col_pick(d * DN_HEADS + h)
        g_col = col_pick(16 + d * DN_HEADS + h)
        g_row = row_pick(16 + d * DN_HEADS + h)
        if d == 0:
            incl, strict = ri >= ci, ri > ci
            total = g_row[:, c - 1:c]
        else:
            incl, strict = ri <= ci, ri < ci
            total = g_row[:, 0:1]
        decay = jnp.where(incl, jnp.exp(jnp.minimum(g_col - g_row, 0.0)), 0.0)
        low = jnp.where(strict, beta * gram * decay, 0.0)
        l_diag = jnp.where(same_block, low, 0.0)
        l_off = (low - l_diag).astype(BF16)
        d_m = -l_diag
        x_b = l_diag.astype(BF16)
        for _ in range(n_steps):
            d_m, x_b = _neumann_step(d_m, x_b)
        m = l_off.astype(F32) + _dot(d_m.astype(BF16), l_off)
        q_m = -m
        x_b = m.astype(BF16)
        n_blocks_steps = (c // DN_BLOCK - 1).bit_length() - 1
        for _ in range(n_blocks_steps):
            q_m, x_b = _neumann_step(q_m, x_b)
        e_col = jnp.exp(g_col)
        kb = kf * beta
        rhs = jnp.concatenate([vf * beta, kb * e_col], axis=1)
        r1 = rhs + _dot(d_m.astype(BF16), rhs.astype(BF16))
        sol = r1 + _dot(q_m.astype(BF16), r1.astype(BF16))
        u_ref[d] = sol[:, :DN_HEAD_DIM]
        w_ref[d] = sol[:, DN_HEAD_DIM:].astype(BF16)
        qd_ref[d] = (qf * e_col).astype(BF16)
        kt_ref[d] = (kf * jnp.exp(total - g_col)).T.astype(BF16)
        qk_ref[d, 0] = (qk * decay).astype(BF16)


def dn_chunk(qkv, col, row):
    s = qkv.shape[0]
    c = DN_TILE
    nt = s // c
    hd = DN_HEAD_DIM
    return pl.pallas_call(
        _dnchunk_body,
        grid=(nt, DN_HEADS),
        in_specs=[
            pl.BlockSpec((c, hd), lambda t, h: (t, h)),
            pl.BlockSpec((c, hd), lambda t, h: (t, DN_HEADS + h)),
            pl.BlockSpec((c, hd), lambda t, h: (t, 2 * DN_HEADS + h)),
            pl.BlockSpec((c, LANES), lambda t, h: (t, 0)),
            pl.BlockSpec((LANES, c), lambda t, h: (0, t)),
        ],
        out_specs=[
            pl.BlockSpec((2, c, hd), lambda t, h: (0, t, h)),
            pl.BlockSpec((2, c, hd), lambda t, h: (0, t, h)),
            pl.BlockSpec((2, c, hd), lambda t, h: (0, t, h)),
            pl.BlockSpec((2, hd, c), lambda t, h: (0, h, t)),
            pl.BlockSpec((2, 1, c, c), lambda t, h: (0, h, t, 0)),
        ],
        out_shape=[
            jax.ShapeDtypeStruct((2, s, DN_WIDTH), F32),
            jax.ShapeDtypeStruct((2, s, DN_WIDTH), BF16),
            jax.ShapeDtypeStruct((2, s, DN_WIDTH), BF16),
            jax.ShapeDtypeStruct((2, DN_WIDTH, s), BF16),
            jax.ShapeDtypeStruct((2, DN_HEADS, s, c), BF16),
        ],
        compiler_params=_params(("parallel", "parallel")),
        name="dn_chunk",
    )(qkv, qkv, qkv, col, row)


def _dnscan_body(uf, wf, qdf, ktf, qkf, colf, ub, wb, qdb, ktb, qkb, colb, of_ref, ob_ref, st_ref):
    @pl.when(pl.program_id(0) == 0)
    def _():
        st_ref[...] = jnp.zeros(st_ref.shape, F32)

    c = DN_TILE
    hd = DN_HEAD_DIM
    dirs = ((uf, wf, qdf, ktf, qkf, colf, of_ref, c - 1), (ub, wb, qdb, ktb, qkb, colb, ob_ref, 0))
    for d, (u, w, qd, kt, qk, col, o_ref, tot_row) in enumerate(dirs):
        e_tot = jnp.exp(col[tot_row:tot_row + 1, :])
        for h in range(DN_HEADS):
            sl = slice(h * hd, (h + 1) * hd)
            lane = 16 + d * DN_HEADS + h
            state = st_ref[d * DN_HEADS + h]
            state_b = state.astype(BF16)
            v_new = u[0, :, sl] - _dot(w[0, :, sl], state_b)
            v_new_b = v_new.astype(BF16)
            o_ref[:, sl] = _dot(qd[0, :, sl], state_b) + _dot(qk[0, h], v_new_b)
            st_ref[d * DN_HEADS + h] = state * e_tot[:, lane:lane + 1] + _dot(kt[0, sl, :], v_new_b)


def dn_scan(u, w, qd, kt, qk, col):
    s = u.shape[1]
    c = DN_TILE
    nt = s // c
    wd = DN_WIDTH

    def specs(d):
        tile = (lambda t: t) if d == 0 else (lambda t: nt - 1 - t)
        return [
            pl.BlockSpec((1, c, wd), lambda t: (d, tile(t), 0)),
            pl.BlockSpec((1, c, wd), lambda t: (d, tile(t), 0)),
            pl.BlockSpec((1, c, wd), lambda t: (d, tile(t), 0)),
            pl.BlockSpec((1, wd, c), lambda t: (d, 0, tile(t))),
            pl.BlockSpec((1, DN_HEADS, c, c), lambda t: (d, 0, tile(t), 0)),
            pl.BlockSpec((c, LANES), lambda t: (tile(t), 0)),
        ]

    return pl.pallas_call(
        _dnscan_body,
        grid=(nt,),
        in_specs=specs(0) + specs(1),
        out_specs=[
            pl.BlockSpec((c, wd), lambda t: (t, 0)),
            pl.BlockSpec((c, wd), lambda t: (nt - 1 - t, 0)),
        ],
        out_shape=[jax.ShapeDtypeStruct((s, wd), F32), jax.ShapeDtypeStruct((s, wd), F32)],
        scratch_shapes=[pltpu.VMEM((2 * DN_HEADS, DN_HEAD_DIM, DN_HEAD_DIM), F32)],
        compiler_params=_params(("arbitrary",)),
        name="dn_scan",
    )(u, w, qd, kt, qk, col, u, w, qd, kt, qk, col)


def _na_bias_table(rpb):
    delta = jnp.arange(NA_ROWS)[:, None]
    i = jnp.arange(NA_ROWS)[None, :]
    dr = i - delta + (NA_ROWS - 1)
    c = jnp.arange(GRID_W)
    kc = jnp.arange(GRID_W)
    cs = jnp.clip(c - NA_COLS // 2, 0, GRID_W - NA_COLS)
    valid = (kc[None, :] >= cs[:, None]) & (kc[None, :] < cs[:, None] + NA_COLS)
    dc = jnp.clip(kc[None, :] - c[:, None] + (NA_COLS - 1), 0, 2 * NA_COLS - 2)
    t = rpb.astype(F32)[:, dr]
    t = t[:, :, :, dc]
    t = jnp.where(valid[None, None, None], t, NEG_BIG)
    t = t.transpose(1, 0, 3, 2, 4)
    return t.reshape(NA_ROWS, NA_HEADS, GRID_W, NA_ROWS * GRID_W)


def _na_body(q_ref, *refs):
    k_refs = refs[:NA_ROWS]
    v_refs = refs[NA_ROWS:2 * NA_ROWS]
    bias_ref = refs[2 * NA_ROWS]
    o_ref = refs[2 * NA_ROWS + 1]
    lane = lax.broadcasted_iota(I32, (1, LANES), 1)
    heads_per_block = LANES // NA_HEAD_DIM
    scale = NA_HEAD_DIM ** -0.5
    for pair in range(NA_WIDTH // LANES):
        sl = slice(pair * LANES, (pair + 1) * LANES)
        q2 = q_ref[:, sl]
        k2 = jnp.concatenate([r[:, sl] for r in k_refs], axis=0)
        v2 = jnp.concatenate([r[:, sl] for r in v_refs], axis=0)
        acc = jnp.zeros((GRID_W, LANES), F32)
        for hh in range(heads_per_block):
            own = (lane // NA_HEAD_DIM) == hh
            qm = jnp.where(own, q2, jnp.zeros_like(q2))
            s = _dot_nt(qm, k2) * scale + bias_ref[0, pair * heads_per_block + hh]
            m = jnp.max(s, axis=-1, keepdims=True)
            p = jnp.exp(s - m)
            denom = jnp.sum(p, axis=-1, keepdims=True)
            vm = jnp.where(own, v2, jnp.zeros_like(v2))
            acc = acc + _dot(p.astype(BF16), vm) / denom
        o_ref[:, sl] = acc.astype(o_ref.dtype)


def na_attention(proj, col0, rpb):
    s = proj.shape[0]
    rows = s // GRID_W
    assert rows >= NA_ROWS
    qb = col0 // NA_WIDTH
    table = _na_bias_table(rpb)

    def first_row(r):
        return jnp.clip(r - NA_ROWS // 2, 0, rows - NA_ROWS)

    def kv_spec(i, blk):
        return pl.BlockSpec((GRID_W, NA_WIDTH), lambda r: (first_row(r) + i, blk))

    in_specs = [pl.BlockSpec((GRID_W, NA_WIDTH), lambda r: (r, qb))]
    in_specs += [kv_spec(i, qb + 1) for i in range(NA_ROWS)]
    in_specs += [kv_spec(i, qb + 2) for i in range(NA_ROWS)]
    in_specs += [pl.BlockSpec((1, NA_HEADS, GRID_W, NA_ROWS * GRID_W), lambda r: (r - first_row(r), 0, 0, 0))]
    return pl.pallas_call(
        _na_body,
        grid=(rows,),
        in_specs=in_specs,
        out_specs=pl.BlockSpec((GRID_W, NA_WIDTH), lambda r: (r, 0)),
        out_shape=jax.ShapeDtypeStruct((s, NA_WIDTH), BF16),
        compiler_params=_params(("parallel",)),
        name="na_attn",
    )(proj, *([proj] * (2 * NA_ROWS)), table)


def _pack_bf16_pair(lo, hi):
    lo_bits = pltpu.bitcast(lo.astype(BF16).astype(F32), U32)
    hi_bits = pltpu.bitcast(hi.astype(BF16).astype(F32), U32)
    return (lo_bits >> 16) | (hi_bits & jnp.uint32(0xFFFF0000))


def _unpack_bf16_pair(packed):
    lo = pltpu.bitcast(packed << 16, F32)
    hi = pltpu.bitcast(packed & jnp.uint32(0xFFFF0000), F32)
    return lo, hi


def _merge_body(of_ref, ob_ref, z_ref, na_ref, ga_ref, gb_ref, x_ref, dng_ref, wa_ref, wb_ref, wo_ref,
                gffn_ref, wr_ref, br_ref, x1_ref, h2p_ref, lg_ref, dn_ref):
    hd = DN_HEAD_DIM
    for h in range(DN_HEADS):
        sl = slice(h * hd, (h + 1) * hd)
        o = of_ref[:, sl] + ob_ref[:, sl]
        r = lax.rsqrt(jnp.mean(o * o, axis=-1, keepdims=True) + RMS_EPS)
        z = z_ref[:, sl].astype(F32)
        dn_ref[:, sl] = (o * r * dng_ref[...] * (z * _sigmoid(z))).astype(BF16)
    y_a = _dot(dn_ref[...], wa_ref[...])
    y_b = _dot(na_ref[...], wb_ref[...])
    mixed = _sigmoid(ga_ref[...].astype(F32)) * y_a + _sigmoid(gb_ref[...].astype(F32)) * y_b
    x1 = x_ref[...] + _dot(mixed.astype(BF16), wo_ref[...])
    x1_ref[...] = x1
    r = lax.rsqrt(jnp.mean(x1 * x1, axis=-1, keepdims=True) + RMS_EPS)
    h2 = x1 * r * gffn_ref[...]
    lg_ref[...] = jnp.dot(h2, wr_ref[...], precision=lax.Precision.HIGHEST,
                          preferred_element_type=F32) + br_ref[...]
    half = h2.shape[1] // 2
    packed = _pack_bf16_pair(h2[:, :half], h2[:, half:])
    for c in range(half // LANES):
        h2p_ref[:, c, :] = packed[:, c * LANES:(c + 1) * LANES]


def merge(o_f, o_b, proj, z_blk, gate_blk, na_out, x, dn_norm_g, w_a, w_b, w_o, norm_ffn_g, w_router, b_router,
          tm=256):
    s, d = x.shape
    ne = w_router.shape[1]
    const = lambda i: (0, 0)
    single = pl.Buffered(1)
    return pl.pallas_call(
        _merge_body,
        grid=(s // tm,),
        in_specs=[
            pl.BlockSpec((tm, DN_WIDTH), lambda i: (i, 0)),
            pl.BlockSpec((tm, DN_WIDTH), lambda i: (i, 0)),
            pl.BlockSpec((tm, DN_WIDTH), lambda i: (i, z_blk)),
            pl.BlockSpec((tm, NA_WIDTH), lambda i: (i, 0)),
            pl.BlockSpec((tm, d), lambda i: (i, gate_blk)),
            pl.BlockSpec((tm, d), lambda i: (i, gate_blk + 1)),
            pl.BlockSpec((tm, d), lambda i: (i, 0)),
            pl.BlockSpec((1, DN_HEAD_DIM), const),
            pl.BlockSpec((DN_WIDTH, d), const, pipeline_mode=single),
            pl.BlockSpec((NA_WIDTH, d), const, pipeline_mode=single),
            pl.BlockSpec((d, d), const, pipeline_mode=single),
            pl.BlockSpec((1, d), const),
            pl.BlockSpec((d, ne), const),
            pl.BlockSpec((1, ne), const),
        ],
        out_specs=[
            pl.BlockSpec((tm, d), lambda i: (i, 0)),
            pl.BlockSpec((tm, d // 2 // LANES, LANES), lambda i: (i, 0, 0)),
            pl.BlockSpec((tm, ne), lambda i: (i, 0)),
        ],
        out_shape=[
            jax.ShapeDtypeStruct((s, d), F32),
            jax.ShapeDtypeStruct((s, d // 2 // LANES, LANES), U32),
            jax.ShapeDtypeStruct((s, ne), F32),
        ],
        scratch_shapes=[pltpu.VMEM((tm, DN_WIDTH), BF16)],
        compiler_params=_params(("parallel",)),
        name="merge",
    )(o_f, o_b, proj, na_out, proj, proj, x, dn_norm_g.reshape(1, -1).astype(F32), w_a, w_b, w_o,
      norm_ffn_g.reshape(1, d).astype(F32), w_router.astype(F32), b_router.reshape(1, ne).astype(F32))


def _route_body(lg_ref, ti_ref, tw_ref, cnt_ref, carry_ref):
    @pl.when(pl.program_id(0) == 0)
    def _():
        carry_ref[...] = jnp.zeros(carry_ref.shape, F32)

    lg = lg_ref[...]
    tm, ne = lg.shape
    lane = lax.broadcasted_iota(I32, (tm, ne), 1).astype(F32)
    work = lg
    vals, idxs = [], []
    onehot = jnp.zeros((tm, ne), F32)
    for _ in range(TOP_K):
        m = jnp.max(work, axis=-1, keepdims=True)
        idx = jnp.min(jnp.where(work == m, lane, float(ne)), axis=-1, keepdims=True)
        hit = lane == idx
        vals.append(m)
        idxs.append(idx)
        onehot = onehot + jnp.where(hit, 1.0, 0.0)
        work = jnp.where(hit, -jnp.inf, work)
    exps = [jnp.exp(v - vals[0]) for v in vals]
    denom = exps[0] + exps[1] + exps[2] + exps[3]
    ri = lax.broadcasted_iota(I32, (tm, tm), 0)
    ci = lax.broadcasted_iota(I32, (tm, tm), 1)
    strict = jnp.where(ci < ri, 1.0, 0.0).astype(BF16)
    before = _dot(strict, onehot.astype(BF16)) + carry_ref[0:1, 0:ne]
    lane_o = lax.broadcasted_iota(I32, (tm, LANES), 1)
    ti = jnp.zeros((tm, LANES), I32)
    tw = jnp.zeros((tm, LANES), F32)
    for kk in range(TOP_K):
        rank = jnp.sum(jnp.where(lane == idxs[kk], before, 0.0), axis=-1, keepdims=True).astype(I32)
        ti = jnp.where(lane_o == kk, idxs[kk].astype(I32), ti)
        ti = jnp.where(lane_o == TOP_K + kk, rank, ti)
        tw = jnp.where(lane_o == kk, exps[kk] / denom, tw)
    ti_ref[...] = ti
    tw_ref[...] = tw
    total = carry_ref[0:1, 0:ne] + jnp.sum(onehot, axis=0, keepdims=True)
    carry_ref[0:1, 0:ne] = total
    cnt_ref[...] = jnp.zeros(cnt_ref.shape, F32)
    cnt_ref[0:1, 0:ne] = total


def route(logits, tm=512):
    s, ne = logits.shape
    return pl.pallas_call(
        _route_body,
        grid=(s // tm,),
        in_specs=[pl.BlockSpec((tm, ne), lambda i: (i, 0))],
        out_specs=[
            pl.BlockSpec((tm, LANES), lambda i: (i, 0)),
            pl.BlockSpec((tm, LANES), lambda i: (i, 0)),
            pl.BlockSpec((8, LANES), lambda i: (0, 0)),
        ],
        out_shape=[
            jax.ShapeDtypeStruct((s, LANES), I32),
            jax.ShapeDtypeStruct((s, LANES), F32),
            jax.ShapeDtypeStruct((8, LANES), F32),
        ],
        scratch_shapes=[pltpu.VMEM((8, LANES), F32)],
        compiler_params=_params(("arbitrary",)),
        name="route",
    )(logits)


def _dest_body(ti_ref, ps_ref, d_ref):
    ti = ti_ref[...].astype(F32)
    tm = ti.shape[0]
    lane = lax.broadcasted_iota(I32, (tm, LANES), 1)
    lane_f = lane.astype(F32)
    ps = ps_ref[0:1, :].astype(F32)
    out = jnp.zeros((tm, LANES), F32)
    for kk in range(TOP_K):
        e = jnp.sum(jnp.where(lane == kk, ti, 0.0), axis=-1, keepdims=True)
        rank = jnp.sum(jnp.where(lane == TOP_K + kk, ti, 0.0), axis=-1, keepdims=True)
        start = jnp.sum(jnp.where(lane_f == e, ps, 0.0), axis=-1, keepdims=True)
        out = jnp.where(lane == kk, start + rank, out)
    d_ref[...] = out.astype(I32)


def route_dest(ti, pad_start, tm=512):
    s = ti.shape[0]
    ps = jnp.zeros((8, LANES), I32).at[0, :N_EXPERTS].set(pad_start)
    return pl.pallas_call(
        _dest_body,
        grid=(s // tm,),
        in_specs=[pl.BlockSpec((tm, LANES), lambda i: (i, 0)), pl.BlockSpec((8, LANES), lambda i: (0, 0))],
        out_specs=pl.BlockSpec((tm, LANES), lambda i: (i, 0)),
        out_shape=jax.ShapeDtypeStruct((s, LANES), I32),
        compiler_params=_params(("parallel",)),
        name="route_dest",
    )(ti, ps)


def _scatter_body(pend_ref, padded_ref, dest_ref, h_ref, xr_ref, zero_ref, sem):
    tm = h_ref.shape[0]

    def zero_copy(e):
        return pltpu.make_async_copy(zero_ref, xr_ref.at[pl.ds(pend_ref[e] - MOE_SUB, MOE_SUB)], sem)

    @pl.when(pl.program_id(0) == 0)
    def _():
        zero_ref[...] = jnp.zeros(zero_ref.shape, U32)

        def start(e, carry):
            @pl.when(padded_ref[e] > 0)
            def _():
                zero_copy(e).start()
            return carry

        def wait(e, carry):
            @pl.when(padded_ref[e] > 0)
            def _():
                zero_copy(e).wait()
            return carry

        lax.fori_loop(0, N_EXPERTS, start, 0)
        lax.fori_loop(0, N_EXPERTS, wait, 0)

        def slack_copy(b):
            return pltpu.make_async_copy(zero_ref, xr_ref.at[pl.ds(b * MOE_SUB, MOE_SUB)], sem)

        def slack_start(b, carry):
            slack_copy(b).start()
            return carry

        def slack_wait(b, carry):
            slack_copy(b).wait()
            return carry

        first_slack = pend_ref[N_EXPERTS - 1] // MOE_SUB
        lax.fori_loop(first_slack, xr_ref.shape[0] // MOE_SUB, slack_start, 0)
        lax.fori_loop(first_slack, xr_ref.shape[0] // MOE_SUB, slack_wait, 0)

    def row_copy(r, kk):
        d = dest_ref[r * TOP_K + kk]
        return pltpu.make_async_copy(h_ref.at[r], xr_ref.at[d], sem)

    def start(r, carry):
        for kk in range(TOP_K):
            row_copy(r, kk).start()
        return carry

    def wait(r, carry):
        for kk in range(TOP_K):
            row_copy(r, kk).wait()
        return carry

    lax.fori_loop(0, tm, start, 0)
    lax.fori_loop(0, tm, wait, 0)


def moe_scatter(h2p, dest, pad_end, padded, n_rows, tm=256):
    s, nt, _ = h2p.shape
    grid_spec = pltpu.PrefetchScalarGridSpec(
        num_scalar_prefetch=2,
        grid=(s // tm,),
        in_specs=[
            pl.BlockSpec((tm * TOP_K,), lambda i, *_: (i,), memory_space=pltpu.SMEM),
            pl.BlockSpec((tm, nt, LANES), lambda i, *_: (i, 0, 0)),
        ],
        out_specs=pl.BlockSpec(memory_space=pl.ANY),
        scratch_shapes=[pltpu.VMEM((MOE_SUB, nt, LANES), U32), pltpu.SemaphoreType.DMA(())],
    )
    return pl.pallas_call(
        _scatter_body,
        grid_spec=grid_spec,
        out_shape=jax.ShapeDtypeStruct((n_rows, nt, LANES), U32),
        compiler_params=_params(("arbitrary",)),
        name="moe_scatter",
    )(pad_end, padded, dest, h2p)


def _moe_body(ie_ref, ir_ref, inb_ref, ni_ref, xr_ref, wg_ref, wu_ref, wd_ref, bg_ref, bu_ref, bd_ref, y_ref,
              stage_ref, xb_ref, acc_ref, wgb_ref, wub_ref, wdb_ref, sem_in, sem_out, *, n_chunks):
    w = pl.program_id(0)
    j = pl.program_id(1)
    sb = MOE_SUB

    @pl.when(w < ni_ref[0])
    def _():
        nb = inb_ref[w]
        r0 = ir_ref[w]

        @pl.when(j == 0)
        def _():
            def load(i, carry):
                rows = pl.ds(pl.multiple_of(i * sb, sb), sb)
                cp = pltpu.make_async_copy(xr_ref.at[pl.ds(pl.multiple_of(r0 + i * sb, sb), sb)], stage_ref, sem_in)
                cp.start()
                cp.wait()
                half = xb_ref.shape[1] // 2
                for c in range(half // LANES):
                    lo, hi = _unpack_bf16_pair(stage_ref[:, c, :])
                    xb_ref[rows, c * LANES:(c + 1) * LANES] = lo.astype(BF16)
                    xb_ref[rows, half + c * LANES:half + (c + 1) * LANES] = hi.astype(BF16)
                acc_ref[rows, :] = jnp.zeros((sb, acc_ref.shape[1]), F32)
                return carry

            lax.fori_loop(0, nb, load, 0)

            @pl.when(w == 0)
            def _():
                def slack_copy(b):
                    dst = y_ref.at[pl.ds(pl.multiple_of(b * sb, sb), sb), :]
                    return pltpu.make_async_copy(acc_ref.at[0:sb, :], dst, sem_out)

                def slack_start(b, carry):
                    slack_copy(b).start()
                    return carry

                def slack_wait(b, carry):
                    slack_copy(b).wait()
                    return carry

                lax.fori_loop(ni_ref[1], y_ref.shape[0] // sb, slack_start, 0)
                lax.fori_loop(ni_ref[1], y_ref.shape[0] // sb, slack_wait, 0)

        wgb_ref[...] = wg_ref[0].astype(BF16)
        wub_ref[...] = wu_ref[0].astype(BF16)
        wdb_ref[...] = wd_ref[0].astype(BF16)

        def block(i, carry):
            rows = pl.ds(pl.multiple_of(i * sb, sb), sb)
            xs = xb_ref[rows, :]
            gate = jnp.minimum(_dot(xs, wgb_ref[...]) + bg_ref[0], SWIGLU_LIMIT)
            up = jnp.clip(_dot(xs, wub_ref[...]) + bu_ref[0], -SWIGLU_LIMIT, SWIGLU_LIMIT)
            act = (up + 1.0) * (gate * _sigmoid(SWIGLU_ALPHA * gate))
            acc_ref[rows, :] += _dot(act.astype(BF16), wdb_ref[...])
            return carry

        lax.fori_loop(0, nb, block, 0)

        @pl.when(j == n_chunks - 1)
        def _():
            def out_copy(i):
                rows = pl.ds(pl.multiple_of(i * sb, sb), sb)
                dst = y_ref.at[pl.ds(pl.multiple_of(r0 + i * sb, sb), sb), :]
                return pltpu.make_async_copy(acc_ref.at[rows, :], dst, sem_out)

            def store(i, carry):
                rows = pl.ds(pl.multiple_of(i * sb, sb), sb)
                acc_ref[rows, :] += bd_ref[0]
                out_copy(i).start()
                return carry

            def wait(i, carry):
                out_copy(i).wait()
                return carry

            lax.fori_loop(0, nb, store, 0)
            lax.fori_loop(0, nb, wait, 0)


def moe_ffn(x_rows, w_gate_up, b_gate_up, w_down, b_down, item_e, item_row, item_nb, n_items, max_items):
    n_rows, nt, _ = x_rows.shape
    ne, d, two_de = w_gate_up.shape
    de = two_de // 2
    tn = MOE_TN
    n_chunks = de // tn
    last = n_chunks - 1

    def chunk(w, j, ni):
        return jnp.where(w < ni[0], j, last)

    grid_spec = pltpu.PrefetchScalarGridSpec(
        num_scalar_prefetch=4,
        grid=(max_items, n_chunks),
        in_specs=[
            pl.BlockSpec(memory_space=pl.ANY),
            pl.BlockSpec((1, d, tn), lambda w, j, ie, ir, inb, ni: (ie[w], 0, chunk(w, j, ni))),
            pl.BlockSpec((1, d, tn), lambda w, j, ie, ir, inb, ni: (ie[w], 0, n_chunks + chunk(w, j, ni))),
            pl.BlockSpec((1, tn, d), lambda w, j, ie, ir, inb, ni: (ie[w], chunk(w, j, ni), 0)),
            pl.BlockSpec((1, 1, tn), lambda w, j, ie, ir, inb, ni: (ie[w], 0, chunk(w, j, ni))),
            pl.BlockSpec((1, 1, tn), lambda w, j, ie, ir, inb, ni: (ie[w], 0, n_chunks + chunk(w, j, ni))),
            pl.BlockSpec((1, 1, d), lambda w, j, ie, ir, inb, ni: (ie[w], 0, 0)),
        ],
        out_specs=pl.BlockSpec(memory_space=pl.ANY),
        scratch_shapes=[
            pltpu.VMEM((MOE_SUB, nt, LANES), U32),
            pltpu.VMEM((MOE_TM, d), BF16),
            pltpu.VMEM((MOE_TM, d), F32),
            pltpu.VMEM((d, tn), BF16),
            pltpu.VMEM((d, tn), BF16),
            pltpu.VMEM((tn, d), BF16),
            pltpu.SemaphoreType.DMA(()),
            pltpu.SemaphoreType.DMA(()),
        ],
    )
    return pl.pallas_call(
        functools.partial(_moe_body, n_chunks=n_chunks),
        grid_spec=grid_spec,
        out_shape=jax.ShapeDtypeStruct((n_rows, d), F32),
        compiler_params=_params(("arbitrary", "arbitrary")),
        name="moe_ffn",
    )(item_e, item_row, item_nb, n_items, x_rows, w_gate_up, w_gate_up, w_down,
      b_gate_up.reshape(ne, 1, two_de), b_gate_up.reshape(ne, 1, two_de), b_down.reshape(ne, 1, d))


def _final_body(dest_ref, x1_ref, tw_ref, p_ref, gple_ref, wg_ref, wp_ref, gfin_ref, y_ref, o_ref, ybuf_ref, sem,
                *, last_layer):
    tm = x1_ref.shape[0]

    def row_copy(r, kk):
        d = dest_ref[r * TOP_K + kk]
        return pltpu.make_async_copy(y_ref.at[d], ybuf_ref.at[kk, r], sem)

    def start(r, carry):
        for kk in range(TOP_K):
            row_copy(r, kk).start()
        return carry

    def wait(r, carry):
        for kk in range(TOP_K):
            row_copy(r, kk).wait()
        return carry

    lax.fori_loop(0, tm, start, 0)
    lax.fori_loop(0, tm, wait, 0)

    tw = tw_ref[...]
    chunks = []
    for c in range(x1_ref.shape[1] // LANES):
        acc = x1_ref[:, c * LANES:(c + 1) * LANES]
        for kk in range(TOP_K):
            acc = acc + tw[:, kk:kk + 1] * ybuf_ref[kk, :, c, :]
        chunks.append(acc)
    x2 = jnp.concatenate(chunks, axis=1)
    r = lax.rsqrt(jnp.mean(x2 * x2, axis=-1, keepdims=True) + RMS_EPS)
    n = (x2 * r * gple_ref[...]).astype(BF16)
    gate = _sigmoid(_dot(n, wg_ref[...]))
    x3 = x2 + gate * _dot(p_ref[...].astype(BF16), wp_ref[...])
    if last_layer:
        r = lax.rsqrt(jnp.mean(x3 * x3, axis=-1, keepdims=True) + RMS_EPS)
        x3 = x3 * r * gfin_ref[...]
    o_ref[...] = x3


def final(dest, x1, tw, p, norm_ple_g, w_gate, w_proj, norm_final_g, y_rows, last_layer, tm=256):
    s, d = x1.shape
    pd = p.shape[1]
    const = lambda i: (0, 0)
    return pl.pallas_call(
        functools.partial(_final_body, last_layer=last_layer),
        grid=(s // tm,),
        in_specs=[
            pl.BlockSpec((tm * TOP_K,), lambda i: (i,), memory_space=pltpu.SMEM),
            pl.BlockSpec((tm, d), lambda i: (i, 0)),
            pl.BlockSpec((tm, LANES), lambda i: (i, 0)),
            pl.BlockSpec((tm, pd), lambda i: (i, 0)),
            pl.BlockSpec((1, d), const),
            pl.BlockSpec((d, d), const),
            pl.BlockSpec((pd, d), const),
            pl.BlockSpec((1, d), const),
            pl.BlockSpec(memory_space=pl.ANY),
        ],
        out_specs=pl.BlockSpec((tm, d), lambda i: (i, 0)),
        out_shape=jax.ShapeDtypeStruct((s, d), F32),
        scratch_shapes=[pltpu.VMEM((TOP_K, tm, d // LANES, LANES), F32), pltpu.SemaphoreType.DMA(())],
        compiler_params=_params(("arbitrary",)),
        name="final",
    )(dest, x1, tw, p, norm_ple_g.reshape(1, d).astype(F32), w_gate, w_proj,
      norm_final_g.reshape(1, d).astype(F32), y_rows.reshape(y_rows.shape[0], d // LANES, LANES))


def _moe_tables(counts, n_rows):
    sub, tm = MOE_SUB, MOE_TM
    max_items = N_EXPERTS + n_rows // tm
    padded = (counts + sub - 1) // sub * sub
    pad_end = jnp.cumsum(padded)
    pad_start = pad_end - padded
    n_it = (padded + tm - 1) // tm
    it_end = jnp.cumsum(n_it)
    it_start = it_end - n_it
    n_items = it_end[-1]
    w = jnp.arange(max_items, dtype=I32)
    live = w < n_items
    w_eff = jnp.minimum(w, n_items - 1)
    e_w = jnp.minimum(jnp.searchsorted(it_end, w_eff, side="right"), N_EXPERTS - 1).astype(I32)
    m_w = w_eff - it_start[e_w]
    row_w = pad_start[e_w] + m_w * tm
    nb_w = jnp.clip((padded[e_w] - m_w * tm) // sub, 0, tm // sub)
    nb_w = jnp.where(live, nb_w, 0)
    as_i32 = lambda a: a.astype(I32)
    counts_w = jnp.stack([n_items, pad_end[-1] // sub])
    return (as_i32(pad_start), as_i32(pad_end), as_i32(padded), e_w, as_i32(row_w), as_i32(nb_w),
            as_i32(counts_w), max_items)


def _layer(x, p, norm_mix_g, w_in, dn_conv_w, dn_a_log, dn_dt_bias, dn_norm_g, na_rpb, w_branch_a, w_branch_b,
           w_out, norm_ffn_g, w_router, b_router, w_gate_up, b_gate_up, w_down, b_down, norm_ple_g,
           w_ple_gate, w_ple_proj, norm_final_g, last_layer):
    s, d = x.shape
    c_qkv, c_z, c_small = 3 * DN_WIDTH, DN_WIDTH, 4 * DN_HEADS
    o_small = c_qkv + c_z
    o_na = o_small + c_small
    o_gates = o_na + 3 * NA_WIDTH
    w_main = jnp.concatenate([w_in[:, :o_small], w_in[:, o_gates:], w_in[:, o_na:o_gates]], axis=1).astype(BF16)
    w_small = jnp.zeros((d, LANES), BF16).at[:, :c_small].set(w_in[:, o_small:o_na].astype(BF16))
    z_blk = c_qkv // DN_WIDTH
    gate_blk = (c_qkv + c_z) // d
    na_col0 = c_qkv + c_z + 2 * d

    proj, small = in_projection(x, norm_mix_g.astype(F32), w_main, w_small)

    qkv = dn_prep(proj, dn_conv_w.astype(F32))
    col, row = dn_gates(small, dn_a_log, dn_dt_bias)
    u, w, qd, kt, qk = dn_chunk(qkv, col, row)
    o_f, o_b = dn_scan(u, w, qd, kt, qk, col)

    na_out = na_attention(proj, na_col0, na_rpb)

    x1, h2p, logits = merge(o_f, o_b, proj, z_blk, gate_blk, na_out, x, dn_norm_g, w_branch_a.astype(BF16),
                            w_branch_b.astype(BF16), w_out.astype(BF16), norm_ffn_g, w_router, b_router)

    ti, tw, cnt = route(logits)
    counts = cnt[0, :N_EXPERTS].astype(I32)
    n_rows = (s * TOP_K + N_EXPERTS * (MOE_SUB - 1) + MOE_SUB - 1) // MOE_SUB * MOE_SUB
    pad_start, pad_end, padded, item_e, item_row, item_nb, n_items, max_items = _moe_tables(counts, n_rows)
    dest = route_dest(ti, pad_start)[:, :TOP_K].reshape(-1)

    x_rows = moe_scatter(h2p, dest, pad_end, padded, n_rows)
    y_rows = moe_ffn(x_rows, w_gate_up, b_gate_up, w_down, b_down, item_e, item_row, item_nb, n_items, max_items)

    return final(dest, x1, tw, p, norm_ple_g, w_ple_gate.astype(BF16), w_ple_proj.astype(BF16), norm_final_g,
                 y_rows, last_layer)


def kernel(x, p, norm_mix_g, w_in, dn_conv_w, dn_a_log, dn_dt_bias, dn_norm_g, na_rpb, w_branch_a, w_branch_b, w_out, norm_ffn_g, w_router, b_router, w_gate_up, b_gate_up, w_down, b_down, norm_ple_g, w_ple_gate, w_ple_proj, norm_final_g):
    bsz, s, d = x.shape
    depth = w_in.shape[0]
    outs = []
    for b in range(bsz):
        xb = x[b]
        for i in range(depth):
            xb = _layer(xb, p[i, b], norm_mix_g[i], w_in[i], dn_conv_w[i], dn_a_log[i], dn_dt_bias[i], dn_norm_g[i],
                        na_rpb[i], w_branch_a[i], w_branch_b[i], w_out[i], norm_ffn_g[i], w_router[i], b_router[i],
                        w_gate_up[i], b_gate_up[i], w_down[i], b_down[i], norm_ple_g[i], w_ple_gate[i],
                        w_ple_proj[i], norm_final_g, i == depth - 1)
        outs.append(xb)
    return jnp.stack(outs, axis=0)
```

```python
import functools

import jax
import jax.numpy as jnp
from jax import lax
from jax.experimental import pallas as pl
from jax.experimental.pallas import tpu as pltpu

F32 = jnp.float32
BF16 = jnp.bfloat16
I32 = jnp.int32
U32 = jnp.uint32

GRID_W = 64
DN_HEADS = 8
DN_HEAD_DIM = 128
DN_WIDTH = DN_HEADS * DN_HEAD_DIM
DN_CONV = 5
NA_HEADS = 16
NA_HEAD_DIM = 64
NA_WIDTH = NA_HEADS * NA_HEAD_DIM
NA_ROWS = 8
NA_COLS = 16
N_EXPERTS = 32
TOP_K = 4
SWIGLU_LIMIT = 7.0
SWIGLU_ALPHA = 1.702
RMS_EPS = 1e-6

LANES = 128
VMEM_LIMIT = 56 * 1024 * 1024

DN_TILE = 256
DN_BLOCK = 16
MOE_SUB = 256
MOE_TM = 2048
MOE_TN = 256
NEG_BIG = -1e30


def _sigmoid(x):
    return 1.0 / (1.0 + jnp.exp(-x))


def _dot(a, b):
    return jnp.dot(a, b, preferred_element_type=F32)


def _dot_nt(a, b):
    return lax.dot_general(a, b, (((1,), (1,)), ((), ())), preferred_element_type=F32)


def _params(sem, limit=VMEM_LIMIT):
    return pltpu.CompilerParams(dimension_semantics=sem, vmem_limit_bytes=limit)


def _inproj_body(x_ref, g_ref, w_ref, ws_ref, o_ref, os_ref, h_ref):
    @pl.when(pl.program_id(1) == 0)
    def _():
        x = x_ref[...]
        r = lax.rsqrt(jnp.mean(x * x, axis=-1, keepdims=True) + RMS_EPS)
        h = (x * r * g_ref[...]).astype(BF16)
        h_ref[...] = h
        os_ref[...] = _dot(h, ws_ref[...])

    o_ref[...] = _dot(h_ref[...], w_ref[...]).astype(o_ref.dtype)


def in_projection(x, g, w_main, w_small, tm=512, tn=1024):
    s, d = x.shape
    n = w_main.shape[1]
    return pl.pallas_call(
        _inproj_body,
        grid=(s // tm, n // tn),
        in_specs=[
            pl.BlockSpec((tm, d), lambda i, j: (i, 0)),
            pl.BlockSpec((1, d), lambda i, j: (0, 0)),
            pl.BlockSpec((d, tn), lambda i, j: (0, j)),
            pl.BlockSpec((d, LANES), lambda i, j: (0, 0)),
        ],
        out_specs=[
            pl.BlockSpec((tm, tn), lambda i, j: (i, j)),
            pl.BlockSpec((tm, LANES), lambda i, j: (i, 0)),
        ],
        out_shape=[jax.ShapeDtypeStruct((s, n), BF16), jax.ShapeDtypeStruct((s, LANES), F32)],
        scratch_shapes=[pltpu.VMEM((tm, d), BF16)],
        compiler_params=_params(("parallel", "arbitrary")),
        name="in_proj",
    )(x, g.reshape(1, d), w_main, w_small)


def _dnprep_body(x_ref, w_ref, o_ref, pad_ref, *, seq, chunk):
    cb = pl.program_id(0)
    n_chunks = seq // chunk
    zeros = jnp.zeros((16, LANES), F32)
    pad_ref[0:16, :] = zeros
    pad_ref[seq + 16:seq + 32, :] = zeros

    def fill(c, carry):
        r0 = pl.multiple_of(c * chunk, chunk)
        pad_ref[pl.ds(r0 + 16, chunk), :] = x_ref[pl.ds(r0, chunk), :].astype(F32)
        return carry

    lax.fori_loop(0, n_chunks, fill, 0)

    w = w_ref[...]
    is_v = cb >= 2 * DN_HEADS
    scale = jnp.where(cb < DN_HEADS, DN_HEAD_DIM ** -0.5, 1.0).astype(F32)

    def body(c, carry):
        r0 = pl.multiple_of(c * chunk, chunk)
        win = pad_ref[pl.ds(r0 + 8, chunk + 16), :]
        y = win[6:6 + chunk] * w[0:1]
        for j in range(1, DN_CONV):
            y = y + win[6 + j:6 + j + chunk] * w[j:j + 1]
        y = y * _sigmoid(y)
        ss = jnp.sum(y * y, axis=-1, keepdims=True)
        yn = y * (lax.rsqrt(ss + 1e-6) * scale)
        o_ref[pl.ds(r0, chunk), :] = jnp.where(is_v, y, yn).astype(o_ref.dtype)
        return carry

    lax.fori_loop(0, n_chunks, body, 0)


def dn_prep(proj, conv_w, chunk=512):
    s = proj.shape[0]
    nb = 3 * DN_WIDTH // LANES
    w = jnp.zeros((8, 3 * DN_WIDTH), F32).at[:DN_CONV].set(conv_w)
    return pl.pallas_call(
        functools.partial(_dnprep_body, seq=s, chunk=chunk),
        grid=(nb,),
        in_specs=[
            pl.BlockSpec((s, LANES), lambda c: (0, c)),
            pl.BlockSpec((8, LANES), lambda c: (0, c)),
        ],
        out_specs=pl.BlockSpec((s, LANES), lambda c: (0, c)),
        out_shape=jax.ShapeDtypeStruct((s, 3 * DN_WIDTH), BF16),
        scratch_shapes=[pltpu.VMEM((s + 32, LANES), F32)],
        compiler_params=_params(("parallel",)),
        name="dn_prep",
    )(proj, w)


def _gates_body(s_ref, par_ref, col_ref, row_ref):
    x = s_ref[...]
    t = x.shape[0]
    lane = lax.broadcasted_iota(I32, x.shape, 1)
    beta = _sigmoid(x)
    z = x + par_ref[1:2, :]
    softplus = jnp.maximum(z, 0.0) + jnp.log(1.0 + jnp.exp(-jnp.abs(z)))
    g = par_ref[0:1, :] * softplus
    ri = lax.broadcasted_iota(I32, (t, t), 0)
    ci = lax.broadcasted_iota(I32, (t, t), 1)
    lower = jnp.where(ci <= ri, 1.0, 0.0).astype(F32)
    upper = jnp.where(ci >= ri, 1.0, 0.0).astype(F32)
    hi = lax.Precision.HIGHEST
    g_prefix = jnp.dot(lower, g, precision=hi, preferred_element_type=F32)
    g_suffix = jnp.dot(upper, g, precision=hi, preferred_element_type=F32)
    cum = jnp.where(lane < 16 + DN_HEADS, g_prefix, g_suffix)
    out = jnp.where(lane < 16, beta, jnp.where(lane < 32, cum, 0.0))
    col_ref[...] = out
    row_ref[...] = out.T


def dn_gates(small, a_log, dt_bias):
    s = small.shape[0]
    par = jnp.zeros((8, LANES), F32)
    par = par.at[0, 16:32].set(-jnp.exp(a_log.reshape(-1).astype(F32)))
    par = par.at[1, 16:32].set(dt_bias.reshape(-1).astype(F32))
    t = DN_TILE
    return pl.pallas_call(
        _gates_body,
        grid=(s // t,),
        in_specs=[
            pl.BlockSpec((t, LANES), lambda i: (i, 0)),
            pl.BlockSpec((8, LANES), lambda i: (0, 0)),
        ],
        out_specs=[
            pl.BlockSpec((t, LANES), lambda i: (i, 0)),
            pl.BlockSpec((LANES, t), lambda i: (0, i)),
        ],
        out_shape=[jax.ShapeDtypeStruct((s, LANES), F32), jax.ShapeDtypeStruct((LANES, s), F32)],
        compiler_params=_params(("parallel",)),
        name="dn_gates",
    )(small, par)


def _neumann_step(acc, x_b):
    x2 = _dot(x_b, x_b)
    x2_b = x2.astype(BF16)
    acc = acc + x2 + _dot(acc.astype(BF16), x2_b)
    return acc, x2_b


def _dnchunk_body(q_ref, k_ref, v_ref, col_ref, row_ref, u_ref, w_ref, qd_ref, kt_ref, qk_ref):
    h = pl.program_id(1)
    c = DN_TILE
    q = q_ref[...]
    k = k_ref[...]
    v = v_ref[...]
    col = col_ref[...]
    row = row_ref[...]
    lane = lax.broadcasted_iota(I32, col.shape, 1)
    sub = lax.broadcasted_iota(I32, row.shape, 0)

    def col_pick(idx):
        return jnp.sum(jnp.where(lane == idx, col, 0.0), axis=1, keepdims=True)

    def row_pick(idx):
        return jnp.sum(jnp.where(sub == idx, row, 0.0), axis=0, keepdims=True)

    ri = lax.broadcasted_iota(I32, (c, c), 0)
    ci = lax.broadcasted_iota(I32, (c, c), 1)
    same_block = (ri // DN_BLOCK) == (ci // DN_BLOCK)
    gram = _dot_nt(k, k)
    qk = _dot_nt(q, k)
    qf = q.astype(F32)
    kf = k.astype(F32)
    vf = v.astype(F32)
    n_steps = (DN_BLOCK - 1).bit_length() - 1

    for d in range(2):
        beta = col_pick(d * DN_HEADS + h)
        g_col = col_pick(16 + d * DN_HEADS + h)
        g_row = row_pick(16 + d * DN_HEADS + h)
        if d == 0:
            incl, strict = ri >= ci, ri > ci
            total = g_row[:, c - 1:c]
        else:
            incl, strict = ri <= ci, ri < ci
            total = g_row[:, 0:1]
        decay = jnp.where(incl, jnp.exp(jnp.minimum(g_col - g_row, 0.0)), 0.0)
        low = jnp.where(strict, beta * gram * decay, 0.0)
        l_diag = jnp.where(same_block, low, 0.0)
        l_off = (low - l_diag).astype(BF16)
        d_m = -l_diag
        x_b = l_diag.astype(BF16)
        for _ in range(n_steps):
            d_m, x_b = _neumann_step(d_m, x_b)
        m = l_off.astype(F32) + _dot(d_m.astype(BF16), l_off)
        q_m = -m
        x_b = m.astype(BF16)
        n_blocks_steps = (c // DN_BLOCK - 1).bit_length() - 1
        for _ in range(n_blocks_steps):
            q_m, x_b = _neumann_step(q_m, x_b)
        e_col = jnp.exp(g_col)
        kb = kf * beta
        rhs = jnp.concatenate([vf * beta, kb * e_col], axis=1)
        r1 = rhs + _dot(d_m.astype(BF16), rhs.astype(BF16))
        sol = r1 + _dot(q_m.astype(BF16), r1.astype(BF16))
        u_ref[d] = sol[:, :DN_HEAD_DIM]
        w_ref[d] = sol[:, DN_HEAD_DIM:].astype(BF16)
        qd_ref[d] = (qf * e_col).astype(BF16)
        kt_ref[d] = (kf * jnp.exp(total - g_col)).T.astype(BF16)
        qk_ref[d, 0] = (qk * decay).astype(BF16)


def dn_chunk(qkv, col, row):
    s = qkv.shape[0]
    c = DN_TILE
    nt = s // c
    hd = DN_HEAD_DIM
    return pl.pallas_call(
        _dnchunk_body,
        grid=(nt, DN_HEADS),
        in_specs=[
            pl.BlockSpec((c, hd), lambda t, h: (t, h)),
            pl.BlockSpec((c, hd), lambda t, h: (t, DN_HEADS + h)),
            pl.BlockSpec((c, hd), lambda t, h: (t, 2 * DN_HEADS + h)),
            pl.BlockSpec((c, LANES), lambda t, h: (t, 0)),
            pl.BlockSpec((LANES, c), lambda t, h: (0, t)),
        ],
        out_specs=[
            pl.BlockSpec((2, c, hd), lambda t, h: (0, t, h)),
            pl.BlockSpec((2, c, hd), lambda t, h: (0, t, h)),
            pl.BlockSpec((2, c, hd), lambda t, h: (0, t, h)),
            pl.BlockSpec((2, hd, c), lambda t, h: (0, h, t)),
            pl.BlockSpec((2, 1, c, c), lambda t, h: (0, h, t, 0)),
        ],
        out_shape=[
            jax.ShapeDtypeStruct((2, s, DN_WIDTH), F32),
            jax.ShapeDtypeStruct((2, s, DN_WIDTH), BF16),
            jax.ShapeDtypeStruct((2, s, DN_WIDTH), BF16),
            jax.ShapeDtypeStruct((2, DN_WIDTH, s), BF16),
            jax.ShapeDtypeStruct((2, DN_HEADS, s, c), BF16),
        ],
        compiler_params=_params(("parallel", "parallel")),
        name="dn_chunk",
    )(qkv, qkv, qkv, col, row)


def _dnscan_body(uf, wf, qdf, ktf, qkf, colf, ub, wb, qdb, ktb, qkb, colb, of_ref, ob_ref, st_ref):
    @pl.when(pl.program_id(0) == 0)
    def _():
        st_ref[...] = jnp.zeros(st_ref.shape, F32)

    c = DN_TILE
    hd = DN_HEAD_DIM
    dirs = ((uf, wf, qdf, ktf, qkf, colf, of_ref, c - 1), (ub, wb, qdb, ktb, qkb, colb, ob_ref, 0))
    for d, (u, w, qd, kt, qk, col, o_ref, tot_row) in enumerate(dirs):
        e_tot = jnp.exp(col[tot_row:tot_row + 1, :])
        for h in range(DN_HEADS):
            sl = slice(h * hd, (h + 1) * hd)
            lane = 16 + d * DN_HEADS + h
            state = st_ref[d * DN_HEADS + h]
            state_b = state.astype(BF16)
            v_new = u[0, :, sl] - _dot(w[0, :, sl], state_b)
            v_new_b = v_new.astype(BF16)
            o_ref[:, sl] = _dot(qd[0, :, sl], state_b) + _dot(qk[0, h], v_new_b)
            st_ref[d * DN_HEADS + h] = state * e_tot[:, lane:lane + 1] + _dot(kt[0, sl, :], v_new_b)


def dn_scan(u, w, qd, kt, qk, col):
    s = u.shape[1]
    c = DN_TILE
    nt = s // c
    wd = DN_WIDTH

    def specs(d):
        tile = (lambda t: t) if d == 0 else (lambda t: nt - 1 - t)
        return [
            pl.BlockSpec((1, c, wd), lambda t: (d, tile(t), 0)),
            pl.BlockSpec((1, c, wd), lambda t: (d, tile(t), 0)),
            pl.BlockSpec((1, c, wd), lambda t: (d, tile(t), 0)),
            pl.BlockSpec((1, wd, c), lambda t: (d, 0, tile(t))),
            pl.BlockSpec((1, DN_HEADS, c, c), lambda t: (d, 0, tile(t), 0)),
            pl.BlockSpec((c, LANES), lambda t: (tile(t), 0)),
        ]

    return pl.pallas_call(
        _dnscan_body,
        grid=(nt,),
        in_specs=specs(0) + specs(1),
        out_specs=[
            pl.BlockSpec((c, wd), lambda t: (t, 0)),
            pl.BlockSpec((c, wd), lambda t: (nt - 1 - t, 0)),
        ],
        out_shape=[jax.ShapeDtypeStruct((s, wd), F32), jax.ShapeDtypeStruct((s, wd), F32)],
        scratch_shapes=[pltpu.VMEM((2 * DN_HEADS, DN_HEAD_DIM, DN_HEAD_DIM), F32)],
        compiler_params=_params(("arbitrary",)),
        name="dn_scan",
    )(u, w, qd, kt, qk, col, u, w, qd, kt, qk, col)


def _na_bias_table(rpb):
    delta = jnp.arange(NA_ROWS)[:, None]
    i = jnp.arange(NA_ROWS)[None, :]
    dr = i - delta + (NA_ROWS - 1)
    c = jnp.arange(GRID_W)
    kc = jnp.arange(GRID_W)
    cs = jnp.clip(c - NA_COLS // 2, 0, GRID_W - NA_COLS)
    valid = (kc[None, :] >= cs[:, None]) & (kc[None, :] < cs[:, None] + NA_COLS)
    dc = jnp.clip(kc[None, :] - c[:, None] + (NA_COLS - 1), 0, 2 * NA_COLS - 2)
    t = rpb.astype(F32)[:, dr]
    t = t[:, :, :, dc]
    t = jnp.where(valid[None, None, None], t, NEG_BIG)
    t = t.transpose(1, 0, 3, 2, 4)
    return t.reshape(NA_ROWS, NA_HEADS, GRID_W, NA_ROWS * GRID_W)


def _na_body(q_ref, *refs):
    k_refs = refs[:NA_ROWS]
    v_refs = refs[NA_ROWS:2 * NA_ROWS]
    bias_ref = refs[2 * NA_ROWS]
    o_ref = refs[2 * NA_ROWS + 1]
    lane = lax.broadcasted_iota(I32, (1, LANES), 1)
    heads_per_block = LANES // NA_HEAD_DIM
    scale = NA_HEAD_DIM ** -0.5
    for pair in range(NA_WIDTH // LANES):
        sl = slice(pair * LANES, (pair + 1) * LANES)
        q2 = q_ref[:, sl]
        k2 = jnp.concatenate([r[:, sl] for r in k_refs], axis=0)
        v2 = jnp.concatenate([r[:, sl] for r in v_refs], axis=0)
        acc = jnp.zeros((GRID_W, LANES), F32)
        for hh in range(heads_per_block):
            own = (lane // NA_HEAD_DIM) == hh
            qm = jnp.where(own, q2, jnp.zeros_like(q2))
            s = _dot_nt(qm, k2) * scale + bias_ref[0, pair * heads_per_block + hh]
            m = jnp.max(s, axis=-1, keepdims=True)
            p = jnp.exp(s - m)
            denom = jnp.sum(p, axis=-1, keepdims=True)
            vm = jnp.where(own, v2, jnp.zeros_like(v2))
            acc = acc + _dot(p.astype(BF16), vm) / denom
        o_ref[:, sl] = acc.astype(o_ref.dtype)


def na_attention(proj, col0, rpb):
    s = proj.shape[0]
    rows = s // GRID_W
    assert rows >= NA_ROWS
    qb = col0 // NA_WIDTH
    table = _na_bias_table(rpb)

    def first_row(r):
        return jnp.clip(r - NA_ROWS // 2, 0, rows - NA_ROWS)

    def kv_spec(i, blk):
        return pl.BlockSpec((GRID_W, NA_WIDTH), lambda r: (first_row(r) + i, blk))

    in_specs = [pl.BlockSpec((GRID_W, NA_WIDTH), lambda r: (r, qb))]
    in_specs += [kv_spec(i, qb + 1) for i in range(NA_ROWS)]
    in_specs += [kv_spec(i, qb + 2) for i in range(NA_ROWS)]
    in_specs += [pl.BlockSpec((1, NA_HEADS, GRID_W, NA_ROWS * GRID_W), lambda r: (r - first_row(r), 0, 0, 0))]
    return pl.pallas_call(
        _na_body,
        grid=(rows,),
        in_specs=in_specs,
        out_specs=pl.BlockSpec((GRID_W, NA_WIDTH), lambda r: (r, 0)),
        out_shape=jax.ShapeDtypeStruct((s, NA_WIDTH), BF16),
        compiler_params=_params(("parallel",)),
        name="na_attn",
    )(proj, *([proj] * (2 * NA_ROWS)), table)


def _pack_bf16_pair(lo, hi):
    lo_bits = pltpu.bitcast(lo.astype(BF16).astype(F32), U32)
    hi_bits = pltpu.bitcast(hi.astype(BF16).astype(F32), U32)
    return (lo_bits >> 16) | (hi_bits & jnp.uint32(0xFFFF0000))


def _unpack_bf16_pair(packed):
    lo = pltpu.bitcast(packed << 16, F32)
    hi = pltpu.bitcast(packed & jnp.uint32(0xFFFF0000), F32)
    return lo, hi


def _merge_body(of_ref, ob_ref, z_ref, na_ref, ga_ref, gb_ref, x_ref, dng_ref, wa_ref, wb_ref, wo_ref,
                gffn_ref, wr_ref, br_ref, x1_ref, h2p_ref, lg_ref, dn_ref):
    hd = DN_HEAD_DIM
    for h in range(DN_HEADS):
        sl = slice(h * hd, (h + 1) * hd)
        o = of_ref[:, sl] + ob_ref[:, sl]
        r = lax.rsqrt(jnp.mean(o * o, axis=-1, keepdims=True) + RMS_EPS)
        z = z_ref[:, sl].astype(F32)
        dn_ref[:, sl] = (o * r * dng_ref[...] * (z * _sigmoid(z))).astype(BF16)
    y_a = _dot(dn_ref[...], wa_ref[...])
    y_b = _dot(na_ref[...], wb_ref[...])
    mixed = _sigmoid(ga_ref[...].astype(F32)) * y_a + _sigmoid(gb_ref[...].astype(F32)) * y_b
    x1 = x_ref[...] + _dot(mixed.astype(BF16), wo_ref[...])
    x1_ref[...] = x1
    r = lax.rsqrt(jnp.mean(x1 * x1, axis=-1, keepdims=True) + RMS_EPS)
    h2 = x1 * r * gffn_ref[...]
    lg_ref[...] = jnp.dot(h2, wr_ref[...], precision=lax.Precision.HIGHEST,
                          preferred_element_type=F32) + br_ref[...]
    half = h2.shape[1] // 2
    n_tiles = half // LANES
    packed = _pack_bf16_pair(h2[:, :half], h2[:, half:])
    for c in range(n_tiles):
        h2p_ref[pl.ds(c, h2.shape[0], stride=n_tiles), :] = packed[:, c * LANES:(c + 1) * LANES]


def merge(o_f, o_b, proj, z_blk, gate_blk, na_out, x, dn_norm_g, w_a, w_b, w_o, norm_ffn_g, w_router, b_router,
          tm=256):
    s, d = x.shape
    ne = w_router.shape[1]
    const = lambda i: (0, 0)
    single = pl.Buffered(1)
    return pl.pallas_call(
        _merge_body,
        grid=(s // tm,),
        in_specs=[
            pl.BlockSpec((tm, DN_WIDTH), lambda i: (i, 0)),
            pl.BlockSpec((tm, DN_WIDTH), lambda i: (i, 0)),
            pl.BlockSpec((tm, DN_WIDTH), lambda i: (i, z_blk)),
            pl.BlockSpec((tm, NA_WIDTH), lambda i: (i, 0)),
            pl.BlockSpec((tm, d), lambda i: (i, gate_blk)),
            pl.BlockSpec((tm, d), lambda i: (i, gate_blk + 1)),
            pl.BlockSpec((tm, d), lambda i: (i, 0)),
            pl.BlockSpec((1, DN_HEAD_DIM), const),
            pl.BlockSpec((DN_WIDTH, d), const, pipeline_mode=single),
            pl.BlockSpec((NA_WIDTH, d), const, pipeline_mode=single),
            pl.BlockSpec((d, d), const, pipeline_mode=single),
            pl.BlockSpec((1, d), const),
            pl.BlockSpec((d, ne), const),
            pl.BlockSpec((1, ne), const),
        ],
        out_specs=[
            pl.BlockSpec((tm, d), lambda i: (i, 0)),
            pl.BlockSpec((tm * (d // 2 // LANES), LANES), lambda i: (i, 0)),
            pl.BlockSpec((tm, ne), lambda i: (i, 0)),
        ],
        out_shape=[
            jax.ShapeDtypeStruct((s, d), F32),
            jax.ShapeDtypeStruct((s * (d // 2 // LANES), LANES), U32),
            jax.ShapeDtypeStruct((s, ne), F32),
        ],
        scratch_shapes=[pltpu.VMEM((tm, DN_WIDTH), BF16)],
        compiler_params=_params(("parallel",)),
        name="merge",
    )(o_f, o_b, proj, na_out, proj, proj, x, dn_norm_g.reshape(1, -1).astype(F32), w_a, w_b, w_o,
      norm_ffn_g.reshape(1, d).astype(F32), w_router.astype(F32), b_router.reshape(1, ne).astype(F32))


def _route_body(lg_ref, ti_ref, tw_ref, cnt_ref, carry_ref):
    @pl.when(pl.program_id(0) == 0)
    def _():
        carry_ref[...] = jnp.zeros(carry_ref.shape, F32)

    lg = lg_ref[...]
    tm, ne = lg.shape
    lane = lax.broadcasted_iota(I32, (tm, ne), 1).astype(F32)
    work = lg
    vals, idxs = [], []
    onehot = jnp.zeros((tm, ne), F32)
    for _ in range(TOP_K):
        m = jnp.max(work, axis=-1, keepdims=True)
        idx = jnp.min(jnp.where(work == m, lane, float(ne)), axis=-1, keepdims=True)
        hit = lane == idx
        vals.append(m)
        idxs.append(idx)
        onehot = onehot + jnp.where(hit, 1.0, 0.0)
        work = jnp.where(hit, -jnp.inf, work)
    exps = [jnp.exp(v - vals[0]) for v in vals]
    denom = exps[0] + exps[1] + exps[2] + exps[3]
    ri = lax.broadcasted_iota(I32, (tm, tm), 0)
    ci = lax.broadcasted_iota(I32, (tm, tm), 1)
    strict = jnp.where(ci < ri, 1.0, 0.0).astype(BF16)
    before = _dot(strict, onehot.astype(BF16)) + carry_ref[0:1, 0:ne]
    lane_o = lax.broadcasted_iota(I32, (tm, LANES), 1)
    ti = jnp.zeros((tm, LANES), I32)
    tw = jnp.zeros((tm, LANES), F32)
    for kk in range(TOP_K):
        rank = jnp.sum(jnp.where(lane == idxs[kk], before, 0.0), axis=-1, keepdims=True).astype(I32)
        ti = jnp.where(lane_o == kk, idxs[kk].astype(I32), ti)
        ti = jnp.where(lane_o == TOP_K + kk, rank, ti)
        tw = jnp.where(lane_o == kk, exps[kk] / denom, tw)
    ti_ref[...] = ti
    tw_ref[...] = tw
    total = carry_ref[0:1, 0:ne] + jnp.sum(onehot, axis=0, keepdims=True)
    carry_ref[0:1, 0:ne] = total
    cnt_ref[...] = jnp.zeros(cnt_ref.shape, F32)
    cnt_ref[0:1, 0:ne] = total


def route(logits, tm=512):
    s, ne = logits.shape
    return pl.pallas_call(
        _route_body,
        grid=(s // tm,),
        in_specs=[pl.BlockSpec((tm, ne), lambda i: (i, 0))],
        out_specs=[
            pl.BlockSpec((tm, LANES), lambda i: (i, 0)),
            pl.BlockSpec((tm, LANES), lambda i: (i, 0)),
            pl.BlockSpec((8, LANES), lambda i: (0, 0)),
        ],
        out_shape=[
            jax.ShapeDtypeStruct((s, LANES), I32),
            jax.ShapeDtypeStruct((s, LANES), F32),
            jax.ShapeDtypeStruct((8, LANES), F32),
        ],
        scratch_shapes=[pltpu.VMEM((8, LANES), F32)],
        compiler_params=_params(("arbitrary",)),
        name="route",
    )(logits)


def _dest_body(ti_ref, ps_ref, d_ref):
    ti = ti_ref[...].astype(F32)
    tm = ti.shape[0]
    lane = lax.broadcasted_iota(I32, (tm, LANES), 1)
    lane_f = lane.astype(F32)
    ps = ps_ref[0:1, :].astype(F32)
    out = jnp.zeros((tm, LANES), F32)
    for kk in range(TOP_K):
        e = jnp.sum(jnp.where(lane == kk, ti, 0.0), axis=-1, keepdims=True)
        rank = jnp.sum(jnp.where(lane == TOP_K + kk, ti, 0.0), axis=-1, keepdims=True)
        start = jnp.sum(jnp.where(lane_f == e, ps, 0.0), axis=-1, keepdims=True)
        out = jnp.where(lane == kk, start + rank, out)
    d_ref[...] = out.astype(I32)


def route_dest(ti, pad_start, tm=512):
    s = ti.shape[0]
    ps = jnp.zeros((8, LANES), I32).at[0, :N_EXPERTS].set(pad_start)
    return pl.pallas_call(
        _dest_body,
        grid=(s // tm,),
        in_specs=[pl.BlockSpec((tm, LANES), lambda i: (i, 0)), pl.BlockSpec((8, LANES), lambda i: (0, 0))],
        out_specs=pl.BlockSpec((tm, LANES), lambda i: (i, 0)),
        out_shape=jax.ShapeDtypeStruct((s, LANES), I32),
        compiler_params=_params(("parallel",)),
        name="route_dest",
    )(ti, ps)


def _scatter_body(pend_ref, padded_ref, dest_ref, h_ref, xr_ref, zero_ref, sem):
    tm = h_ref.shape[0]

    def zero_copy(e):
        return pltpu.make_async_copy(zero_ref, xr_ref.at[pl.ds(pend_ref[e] - MOE_SUB, MOE_SUB)], sem)

    @pl.when(pl.program_id(0) == 0)
    def _():
        zero_ref[...] = jnp.zeros(zero_ref.shape, U32)

        def start(e, carry):
            @pl.when(padded_ref[e] > 0)
            def _():
                zero_copy(e).start()
            return carry

        def wait(e, carry):
            @pl.when(padded_ref[e] > 0)
            def _():
                zero_copy(e).wait()
            return carry

        lax.fori_loop(0, N_EXPERTS, start, 0)
        lax.fori_loop(0, N_EXPERTS, wait, 0)

        def slack_copy(b):
            return pltpu.make_async_copy(zero_ref, xr_ref.at[pl.ds(b * MOE_SUB, MOE_SUB)], sem)

        def slack_start(b, carry):
            slack_copy(b).start()
            return carry

        def slack_wait(b, carry):
            slack_copy(b).wait()
            return carry

        first_slack = pend_ref[N_EXPERTS - 1] // MOE_SUB
        lax.fori_loop(first_slack, xr_ref.shape[0] // MOE_SUB, slack_start, 0)
        lax.fori_loop(first_slack, xr_ref.shape[0] // MOE_SUB, slack_wait, 0)

    def row_copy(r, kk):
        d = dest_ref[r * TOP_K + kk]
        return pltpu.make_async_copy(h_ref.at[r], xr_ref.at[d], sem)

    def start(r, carry):
        for kk in range(TOP_K):
            row_copy(r, kk).start()
        return carry

    def wait(r, carry):
        for kk in range(TOP_K):
            row_copy(r, kk).wait()
        return carry

    lax.fori_loop(0, tm, start, 0)
    lax.fori_loop(0, tm, wait, 0)


def moe_scatter(h2p, dest, pad_end, padded, n_rows, tm=256):
    s, nt, _ = h2p.shape
    grid_spec = pltpu.PrefetchScalarGridSpec(
        num_scalar_prefetch=2,
        grid=(s // tm,),
        in_specs=[
            pl.BlockSpec((tm * TOP_K,), lambda i, *_: (i,), memory_space=pltpu.SMEM),
            pl.BlockSpec((tm, nt, LANES), lambda i, *_: (i, 0, 0)),
        ],
        out_specs=pl.BlockSpec(memory_space=pl.ANY),
        scratch_shapes=[pltpu.VMEM((MOE_SUB, nt, LANES), U32), pltpu.SemaphoreType.DMA(())],
    )
    return pl.pallas_call(
        _scatter_body,
        grid_spec=grid_spec,
        out_shape=jax.ShapeDtypeStruct((n_rows, nt, LANES), U32),
        compiler_params=_params(("arbitrary",)),
        name="moe_scatter",
    )(pad_end, padded, dest, h2p)


def _moe_body(ie_ref, ir_ref, inb_ref, ni_ref, xr_ref, wg_ref, wu_ref, wd_ref, bg_ref, bu_ref, bd_ref, y_ref,
              stage_ref, ystage_ref, xb_ref, acc_ref, wgb_ref, wub_ref, wdb_ref, sem_in, sem_out, *, n_chunks):
    w = pl.program_id(0)
    j = pl.program_id(1)
    sb = MOE_SUB
    d = acc_ref.shape[1]
    half = d // 2
    nx = half // LANES
    ny = d // LANES

    @pl.when(w < ni_ref[0])
    def _():
        nb = inb_ref[w]
        r0 = ir_ref[w]

        @pl.when(j == 0)
        def _():
            @pl.when(w == 0)
            def _():
                ystage_ref[0] = jnp.zeros(ystage_ref.shape[1:], F32)

                def slack_copy(b):
                    dst = y_ref.at[pl.ds(pl.multiple_of(b * (sb * ny), sb * ny), sb * ny), :]
                    return pltpu.make_async_copy(ystage_ref.at[0], dst, sem_out.at[0])

                def slack_start(b, carry):
                    slack_copy(b).start()
                    return carry

                def slack_wait(b, carry):
                    slack_copy(b).wait()
                    return carry

                n_blocks = y_ref.shape[0] // (sb * ny)
                lax.fori_loop(ni_ref[1], n_blocks, slack_start, 0)
                lax.fori_loop(ni_ref[1], n_blocks, slack_wait, 0)

            def in_copy(i, slot):
                src = xr_ref.at[pl.ds(pl.multiple_of((r0 + i * sb) * nx, sb * nx), sb * nx), :]
                return pltpu.make_async_copy(src, stage_ref.at[slot], sem_in.at[slot])

            in_copy(0, 0).start()

            def load(i, carry):
                slot = i % 2

                @pl.when(i + 1 < nb)
                def _():
                    in_copy(i + 1, 1 - slot).start()

                in_copy(i, slot).wait()
                rows = pl.ds(pl.multiple_of(i * sb, sb), sb)
                for c in range(nx):
                    lo, hi = _unpack_bf16_pair(stage_ref[slot, pl.ds(c, sb, stride=nx), :])
                    xb_ref[rows, c * LANES:(c + 1) * LANES] = lo.astype(BF16)
                    xb_ref[rows, half + c * LANES:half + (c + 1) * LANES] = hi.astype(BF16)
                return carry

            lax.fori_loop(0, nb, load, 0)

        wgb_ref[...] = wg_ref[0].astype(BF16)
        wub_ref[...] = wu_ref[0].astype(BF16)
        wdb_ref[...] = wd_ref[0].astype(BF16)

        def block(first, i, carry):
            rows = pl.ds(pl.multiple_of(i * sb, sb), sb)
            xs = xb_ref[rows, :]
            gate = jnp.minimum(_dot(xs, wgb_ref[...]) + bg_ref[0], SWIGLU_LIMIT)
            up = jnp.clip(_dot(xs, wub_ref[...]) + bu_ref[0], -SWIGLU_LIMIT, SWIGLU_LIMIT)
            act = (up + 1.0) * (gate * _sigmoid(SWIGLU_ALPHA * gate))
            contrib = _dot(act.astype(BF16), wdb_ref[...])
            if first:
                acc_ref[rows, :] = contrib
            else:
                acc_ref[rows, :] += contrib
            return carry

        @pl.when(j == 0)
        def _():
            lax.fori_loop(0, nb, functools.partial(block, True), 0)

        @pl.when(j > 0)
        def _():
            lax.fori_loop(0, nb, functools.partial(block, False), 0)

        @pl.when(j == n_chunks - 1)
        def _():
            def out_copy(i, slot):
                dst = y_ref.at[pl.ds(pl.multiple_of((r0 + i * sb) * ny, sb * ny), sb * ny), :]
                return pltpu.make_async_copy(ystage_ref.at[slot], dst, sem_out.at[slot])

            def store(i, carry):
                slot = i % 2

                @pl.when(i >= 2)
                def _():
                    out_copy(i - 2, slot).wait()

                rows = pl.ds(pl.multiple_of(i * sb, sb), sb)
                for c in range(ny):
                    cols = slice(c * LANES, (c + 1) * LANES)
                    ystage_ref[slot, pl.ds(c, sb, stride=ny), :] = acc_ref[rows, cols] + bd_ref[0, :, cols]
                out_copy(i, slot).start()
                return carry

            lax.fori_loop(0, nb, store, 0)

            @pl.when(nb >= 2)
            def _():
                out_copy(nb - 2, nb % 2).wait()

            out_copy(nb - 1, (nb - 1) % 2).wait()


def moe_ffn(x_rows, w_gate_up, b_gate_up, w_down, b_down, item_e, item_row, item_nb, n_items, max_items):
    ne, d, two_de = w_gate_up.shape
    nx = d // 2 // LANES
    ny = d // LANES
    n_rows = x_rows.shape[0] // nx
    de = two_de // 2
    tn = MOE_TN
    n_chunks = de // tn
    last = n_chunks - 1

    def chunk(w, j, ni):
        return jnp.where(w < ni[0], j, last)

    grid_spec = pltpu.PrefetchScalarGridSpec(
        num_scalar_prefetch=4,
        grid=(max_items, n_chunks),
        in_specs=[
            pl.BlockSpec(memory_space=pl.ANY),
            pl.BlockSpec((1, d, tn), lambda w, j, ie, ir, inb, ni: (ie[w], 0, chunk(w, j, ni))),
            pl.BlockSpec((1, d, tn), lambda w, j, ie, ir, inb, ni: (ie[w], 0, n_chunks + chunk(w, j, ni))),
            pl.BlockSpec((1, tn, d), lambda w, j, ie, ir, inb, ni: (ie[w], chunk(w, j, ni), 0)),
            pl.BlockSpec((1, 1, tn), lambda w, j, ie, ir, inb, ni: (ie[w], 0, chunk(w, j, ni))),
            pl.BlockSpec((1, 1, tn), lambda w, j, ie, ir, inb, ni: (ie[w], 0, n_chunks + chunk(w, j, ni))),
            pl.BlockSpec((1, 1, d), lambda w, j, ie, ir, inb, ni: (ie[w], 0, 0)),
        ],
        out_specs=pl.BlockSpec(memory_space=pl.ANY),
        scratch_shapes=[
            pltpu.VMEM((2, MOE_SUB * nx, LANES), U32),
            pltpu.VMEM((2, MOE_SUB * ny, LANES), F32),
            pltpu.VMEM((MOE_TM, d), BF16),
            pltpu.VMEM((MOE_TM, d), F32),
            pltpu.VMEM((d, tn), BF16),
            pltpu.VMEM((d, tn), BF16),
            pltpu.VMEM((tn, d), BF16),
            pltpu.SemaphoreType.DMA((2,)),
            pltpu.SemaphoreType.DMA((2,)),
        ],
    )
    return pl.pallas_call(
        functools.partial(_moe_body, n_chunks=n_chunks),
        grid_spec=grid_spec,
        out_shape=jax.ShapeDtypeStruct((n_rows * ny, LANES), F32),
        compiler_params=_params(("arbitrary", "arbitrary")),
        name="moe_ffn",
    )(item_e, item_row, item_nb, n_items, x_rows, w_gate_up, w_gate_up, w_down,
      b_gate_up.reshape(ne, 1, two_de), b_gate_up.reshape(ne, 1, two_de), b_down.reshape(ne, 1, d))


def _final_body(dest_ref, dest_next_ref, x1_ref, tw_ref, p_ref, gple_ref, wg_ref, wp_ref, gfin_ref, y_ref, o_ref,
                ybuf_ref, sem, *, last_layer):
    i = pl.program_id(0)
    tm = x1_ref.shape[0]
    ny = x1_ref.shape[1] // LANES
    slot = i % 2

    def row_copy(d_ref, to_slot, r, kk):
        dst = ybuf_ref.at[to_slot, kk, pl.ds(pl.multiple_of(r * ny, ny), ny), :]
        return pltpu.make_async_copy(y_ref.at[d_ref[r * TOP_K + kk]], dst, sem.at[to_slot])

    def issue(d_ref, to_slot):
        def body(r, carry):
            for kk in range(TOP_K):
                row_copy(d_ref, to_slot, r, kk).start()
            return carry

        lax.fori_loop(0, tm, body, 0)

    def drain(d_ref, to_slot):
        def body(r, carry):
            for kk in range(TOP_K):
                row_copy(d_ref, to_slot, r, kk).wait()
            return carry

        lax.fori_loop(0, tm, body, 0)

    @pl.when(i == 0)
    def _():
        issue(dest_ref, 0)

    @pl.when(i + 1 < pl.num_programs(0))
    def _():
        issue(dest_next_ref, 1 - slot)

    drain(dest_ref, slot)

    tw = tw_ref[...]
    chunks = []
    for c in range(ny):
        acc = x1_ref[:, c * LANES:(c + 1) * LANES]
        for kk in range(TOP_K):
            acc = acc + tw[:, kk:kk + 1] * ybuf_ref[slot, kk, pl.ds(c, tm, stride=ny), :]
        chunks.append(acc)
    x2 = jnp.concatenate(chunks, axis=1)
    r = lax.rsqrt(jnp.mean(x2 * x2, axis=-1, keepdims=True) + RMS_EPS)
    n = (x2 * r * gple_ref[...]).astype(BF16)
    gate = _sigmoid(_dot(n, wg_ref[...]))
    x3 = x2 + gate * _dot(p_ref[...].astype(BF16), wp_ref[...])
    if last_layer:
        r = lax.rsqrt(jnp.mean(x3 * x3, axis=-1, keepdims=True) + RMS_EPS)
        x3 = x3 * r * gfin_ref[...]
    o_ref[...] = x3


def final(dest, x1, tw, p, norm_ple_g, w_gate, w_proj, norm_final_g, y_rows, last_layer, tm=256):
    s, d = x1.shape
    pd = p.shape[1]
    ny = d // LANES
    n_steps = s // tm
    const = lambda i: (0, 0)
    single = pl.Buffered(1)
    return pl.pallas_call(
        functools.partial(_final_body, last_layer=last_layer),
        grid=(n_steps,),
        in_specs=[
            pl.BlockSpec((tm * TOP_K,), lambda i: (i,), memory_space=pltpu.SMEM),
            pl.BlockSpec((tm * TOP_K,), lambda i: (jnp.minimum(i + 1, n_steps - 1),), memory_space=pltpu.SMEM),
            pl.BlockSpec((tm, d), lambda i: (i, 0)),
            pl.BlockSpec((tm, LANES), lambda i: (i, 0)),
            pl.BlockSpec((tm, pd), lambda i: (i, 0)),
            pl.BlockSpec((1, d), const),
            pl.BlockSpec((d, d), const, pipeline_mode=single),
            pl.BlockSpec((pd, d), const, pipeline_mode=single),
            pl.BlockSpec((1, d), const),
            pl.BlockSpec(memory_space=pl.ANY),
        ],
        out_specs=pl.BlockSpec((tm, d), lambda i: (i, 0)),
        out_shape=jax.ShapeDtypeStruct((s, d), F32),
        scratch_shapes=[pltpu.VMEM((2, TOP_K, tm * ny, LANES), F32), pltpu.SemaphoreType.DMA((2,))],
        compiler_params=_params(("arbitrary",)),
        name="final",
    )(dest, dest, x1, tw, p, norm_ple_g.reshape(1, d).astype(F32), w_gate, w_proj,
      norm_final_g.reshape(1, d).astype(F32), y_rows.reshape(-1, ny, LANES))


def _moe_tables(counts, n_rows):
    sub, tm = MOE_SUB, MOE_TM
    max_items = N_EXPERTS + n_rows // tm
    padded = (counts + sub - 1) // sub * sub
    pad_end = jnp.cumsum(padded)
    pad_start = pad_end - padded
    n_it = (padded + tm - 1) // tm
    it_end = jnp.cumsum(n_it)
    it_start = it_end - n_it
    n_items = it_end[-1]
    w = jnp.arange(max_items, dtype=I32)
    live = w < n_items
    w_eff = jnp.minimum(w, n_items - 1)
    e_w = jnp.minimum(jnp.searchsorted(it_end, w_eff, side="right"), N_EXPERTS - 1).astype(I32)
    m_w = w_eff - it_start[e_w]
    row_w = pad_start[e_w] + m_w * tm
    nb_w = jnp.clip((padded[e_w] - m_w * tm) // sub, 0, tm // sub)
    nb_w = jnp.where(live, nb_w, 0)
    as_i32 = lambda a: a.astype(I32)
    counts_w = jnp.stack([n_items, pad_end[-1] // sub])
    return (as_i32(pad_start), as_i32(pad_end), as_i32(padded), e_w, as_i32(row_w), as_i32(nb_w),
            as_i32(counts_w), max_items)


def _layer(x, p, norm_mix_g, w_in, dn_conv_w, dn_a_log, dn_dt_bias, dn_norm_g, na_rpb, w_branch_a, w_branch_b,
           w_out, norm_ffn_g, w_router, b_router, w_gate_up, b_gate_up, w_down, b_down, norm_ple_g,
           w_ple_gate, w_ple_proj, norm_final_g, last_layer):
    s, d = x.shape
    c_qkv, c_z, c_small = 3 * DN_WIDTH, DN_WIDTH, 4 * DN_HEADS
    o_small = c_qkv + c_z
    o_na = o_small + c_small
    o_gates = o_na + 3 * NA_WIDTH
    w_main = jnp.concatenate([w_in[:, :o_small], w_in[:, o_gates:], w_in[:, o_na:o_gates]], axis=1).astype(BF16)
    w_small = jnp.zeros((d, LANES), BF16).at[:, :c_small].set(w_in[:, o_small:o_na].astype(BF16))
    z_blk = c_qkv // DN_WIDTH
    gate_blk = (c_qkv + c_z) // d
    na_col0 = c_qkv + c_z + 2 * d

    proj, small = in_projection(x, norm_mix_g.astype(F32), w_main, w_small)

    qkv = dn_prep(proj, dn_conv_w.astype(F32))
    col, row = dn_gates(small, dn_a_log, dn_dt_bias)
    u, w, qd, kt, qk = dn_chunk(qkv, col, row)
    o_f, o_b = dn_scan(u, w, qd, kt, qk, col)

    na_out = na_attention(proj, na_col0, na_rpb)

    x1, h2p, logits = merge(o_f, o_b, proj, z_blk, gate_blk, na_out, x, dn_norm_g, w_branch_a.astype(BF16),
                            w_branch_b.astype(BF16), w_out.astype(BF16), norm_ffn_g, w_router, b_router)

    ti, tw, cnt = route(logits)
    counts = cnt[0, :N_EXPERTS].astype(I32)
    n_rows = (s * TOP_K + N_EXPERTS * (MOE_SUB - 1) + MOE_SUB - 1) // MOE_SUB * MOE_SUB
    pad_start, pad_end, padded, item_e, item_row, item_nb, n_items, max_items = _moe_tables(counts, n_rows)
    dest = route_dest(ti, pad_start)[:, :TOP_K].reshape(-1)

    nx = d // 2 // LANES
    x_rows = moe_scatter(h2p.reshape(s, nx, LANES), dest, pad_end, padded, n_rows).reshape(n_rows * nx, LANES)
    y_rows = moe_ffn(x_rows, w_gate_up, b_gate_up, w_down, b_down, item_e, item_row, item_nb, n_items, max_items)

    return final(dest, x1, tw, p, norm_ple_g, w_ple_gate.astype(BF16), w_ple_proj.astype(BF16), norm_final_g,
                 y_rows, last_layer)


def kernel(x, p, norm_mix_g, w_in, dn_conv_w, dn_a_log, dn_dt_bias, dn_norm_g, na_rpb, w_branch_a, w_branch_b, w_out, norm_ffn_g, w_router, b_router, w_gate_up, b_gate_up, w_down, b_down, norm_ple_g, w_ple_gate, w_ple_proj, norm_final_g):
    bsz, s, d = x.shape
    depth = w_in.shape[0]
    outs = []
    for b in range(bsz):
        xb = x[b]
        for i in range(depth):
            xb = _layer(xb, p[i, b], norm_mix_g[i], w_in[i], dn_conv_w[i], dn_a_log[i], dn_dt_bias[i], dn_norm_g[i],
                        na_rpb[i], w_branch_a[i], w_branch_b[i], w_out[i], norm_ffn_g[i], w_router[i], b_router[i],
                        w_gate_up[i], b_gate_up[i], w_down[i], b_down[i], norm_ple_g[i], w_ple_gate[i],
                        w_ple_proj[i], norm_final_g, i == depth - 1)
        outs.append(xb)
    return jnp.stack(outs, axis=0)
```

```python
import functools

import jax
import jax.numpy as jnp
import numpy as np
from jax import lax
from jax.experimental import pallas as pl
from jax.experimental.pallas import tpu as pltpu

F32 = jnp.float32
BF16 = jnp.bfloat16
I32 = jnp.int32
U32 = jnp.uint32

GRID_W = 64
DN_HEADS = 8
DN_HEAD_DIM = 128
DN_WIDTH = DN_HEADS * DN_HEAD_DIM
DN_CONV = 5
NA_HEADS = 16
NA_HEAD_DIM = 64
NA_WIDTH = NA_HEADS * NA_HEAD_DIM
NA_ROWS = 8
NA_COLS = 16
N_EXPERTS = 32
TOP_K = 4
SWIGLU_LIMIT = 7.0
SWIGLU_ALPHA = 1.702
RMS_EPS = 1e-6

LANES = 128
VMEM_LIMIT = 56 * 1024 * 1024

DN_TILE = 256
DN_BLOCK = 16
MOE_SUB = 256
MOE_TM = 2048
MOE_TN = 256
NEG_BIG = -1e30


def _sigmoid(x):
    return 1.0 / (1.0 + jnp.exp(-x))


def _dot(a, b):
    return jnp.dot(a, b, preferred_element_type=F32)


def _dot_nt(a, b):
    return lax.dot_general(a, b, (((1,), (1,)), ((), ())), preferred_element_type=F32)


def _params(sem, limit=VMEM_LIMIT):
    return pltpu.CompilerParams(dimension_semantics=sem, vmem_limit_bytes=limit)


def _inproj_body(x_ref, g_ref, w_ref, ws_ref, o_ref, os_ref, h_ref):
    @pl.when(pl.program_id(1) == 0)
    def _():
        x = x_ref[...]
        r = lax.rsqrt(jnp.mean(x * x, axis=-1, keepdims=True) + RMS_EPS)
        h = (x * r * g_ref[...]).astype(BF16)
        h_ref[...] = h
        os_ref[...] = _dot(h, ws_ref[...])

    o_ref[...] = _dot(h_ref[...], w_ref[...]).astype(o_ref.dtype)


def in_projection(x, g, w_main, w_small, tm=512, tn=1024):
    s, d = x.shape
    n = w_main.shape[1]
    return pl.pallas_call(
        _inproj_body,
        grid=(s // tm, n // tn),
        in_specs=[
            pl.BlockSpec((tm, d), lambda i, j: (i, 0)),
            pl.BlockSpec((1, d), lambda i, j: (0, 0)),
            pl.BlockSpec((d, tn), lambda i, j: (0, j)),
            pl.BlockSpec((d, LANES), lambda i, j: (0, 0)),
        ],
        out_specs=[
            pl.BlockSpec((tm, tn), lambda i, j: (i, j)),
            pl.BlockSpec((tm, LANES), lambda i, j: (i, 0)),
        ],
        out_shape=[jax.ShapeDtypeStruct((s, n), BF16), jax.ShapeDtypeStruct((s, LANES), F32)],
        scratch_shapes=[pltpu.VMEM((tm, d), BF16)],
        compiler_params=_params(("parallel", "arbitrary")),
        name="in_proj",
    )(x, g.reshape(1, d), w_main, w_small)


def _dnprep_body(x_ref, w_ref, o_ref, pad_ref, *, seq, chunk):
    cb = pl.program_id(0)
    n_chunks = seq // chunk
    zeros = jnp.zeros((16, LANES), F32)
    pad_ref[0:16, :] = zeros
    pad_ref[seq + 16:seq + 32, :] = zeros

    def fill(c, carry):
        r0 = pl.multiple_of(c * chunk, chunk)
        pad_ref[pl.ds(r0 + 16, chunk), :] = x_ref[pl.ds(r0, chunk), :].astype(F32)
        return carry

    lax.fori_loop(0, n_chunks, fill, 0)

    w = w_ref[...]
    is_v = cb >= 2 * DN_HEADS
    scale = jnp.where(cb < DN_HEADS, DN_HEAD_DIM ** -0.5, 1.0).astype(F32)

    def body(c, carry):
        r0 = pl.multiple_of(c * chunk, chunk)
        win = pad_ref[pl.ds(r0 + 8, chunk + 16), :]
        y = win[6:6 + chunk] * w[0:1]
        for j in range(1, DN_CONV):
            y = y + win[6 + j:6 + j + chunk] * w[j:j + 1]
        y = y * _sigmoid(y)
        ss = jnp.sum(y * y, axis=-1, keepdims=True)
        yn = y * (lax.rsqrt(ss + 1e-6) * scale)
        o_ref[pl.ds(r0, chunk), :] = jnp.where(is_v, y, yn).astype(o_ref.dtype)
        return carry

    lax.fori_loop(0, n_chunks, body, 0)


def dn_prep(proj, conv_w, chunk=512):
    s = proj.shape[0]
    nb = 3 * DN_WIDTH // LANES
    w = jnp.zeros((8, 3 * DN_WIDTH), F32).at[:DN_CONV].set(conv_w)
    return pl.pallas_call(
        functools.partial(_dnprep_body, seq=s, chunk=chunk),
        grid=(nb,),
        in_specs=[
            pl.BlockSpec((s, LANES), lambda c: (0, c)),
            pl.BlockSpec((8, LANES), lambda c: (0, c)),
        ],
        out_specs=pl.BlockSpec((s, LANES), lambda c: (0, c)),
        out_shape=jax.ShapeDtypeStruct((s, 3 * DN_WIDTH), BF16),
        scratch_shapes=[pltpu.VMEM((s + 32, LANES), F32)],
        compiler_params=_params(("parallel",)),
        name="dn_prep",
    )(proj, w)


def _gates_body(s_ref, par_ref, col_ref, row_ref):
    x = s_ref[...]
    t = x.shape[0]
    lane = lax.broadcasted_iota(I32, x.shape, 1)
    beta = _sigmoid(x)
    z = x + par_ref[1:2, :]
    softplus = jnp.maximum(z, 0.0) + jnp.log(1.0 + jnp.exp(-jnp.abs(z)))
    g = par_ref[0:1, :] * softplus
    ri = lax.broadcasted_iota(I32, (t, t), 0)
    ci = lax.broadcasted_iota(I32, (t, t), 1)
    lower = jnp.where(ci <= ri, 1.0, 0.0).astype(F32)
    upper = jnp.where(ci >= ri, 1.0, 0.0).astype(F32)
    hi = lax.Precision.HIGHEST
    g_prefix = jnp.dot(lower, g, precision=hi, preferred_element_type=F32)
    g_suffix = jnp.dot(upper, g, precision=hi, preferred_element_type=F32)
    cum = jnp.where(lane < 16 + DN_HEADS, g_prefix, g_suffix)
    out = jnp.where(lane < 16, beta, jnp.where(lane < 32, cum, 0.0))
    col_ref[...] = out
    row_ref[...] = out.T


def dn_gates(small, a_log, dt_bias):
    s = small.shape[0]
    par = jnp.zeros((8, LANES), F32)
    par = par.at[0, 16:32].set(-jnp.exp(a_log.reshape(-1).astype(F32)))
    par = par.at[1, 16:32].set(dt_bias.reshape(-1).astype(F32))
    t = DN_TILE
    return pl.pallas_call(
        _gates_body,
        grid=(s // t,),
        in_specs=[
            pl.BlockSpec((t, LANES), lambda i: (i, 0)),
            pl.BlockSpec((8, LANES), lambda i: (0, 0)),
        ],
        out_specs=[
            pl.BlockSpec((t, LANES), lambda i: (i, 0)),
            pl.BlockSpec((LANES, t), lambda i: (0, i)),
        ],
        out_shape=[jax.ShapeDtypeStruct((s, LANES), F32), jax.ShapeDtypeStruct((LANES, s), F32)],
        compiler_params=_params(("parallel",)),
        name="dn_gates",
    )(small, par)


def _dnchunk_body(q_ref, k_ref, v_ref, col_ref, row_ref, u_ref, w_ref, qd_ref, kt_ref, qk_ref, *, heads_per_step):
    c = DN_TILE
    hd = DN_HEAD_DIM
    head0 = pl.program_id(1) * heads_per_step
    col = col_ref[...]
    row = row_ref[...]
    lane = lax.broadcasted_iota(I32, col.shape, 1)
    sub = lax.broadcasted_iota(I32, row.shape, 0)

    def col_pick(idx):
        return jnp.sum(jnp.where(lane == idx, col, 0.0), axis=1, keepdims=True)

    def row_pick(idx):
        return jnp.sum(jnp.where(sub == idx, row, 0.0), axis=0, keepdims=True)

    ri = lax.broadcasted_iota(I32, (c, c), 0)
    ci = lax.broadcasted_iota(I32, (c, c), 1)
    same_block = (ri // DN_BLOCK) == (ci // DN_BLOCK)
    incl = (ri >= ci, ri <= ci)
    strict = (ri > ci, ri < ci)
    heads = range(heads_per_step)
    chains = [(hh, d) for hh in heads for d in range(2)]
    sls = [slice(hh * hd, (hh + 1) * hd) for hh in heads]
    qs = [q_ref[:, sl] for sl in sls]
    ks = [k_ref[:, sl] for sl in sls]
    vs = [v_ref[:, sl] for sl in sls]
    grams = [_dot_nt(k, k) for k in ks]
    qks = [_dot_nt(q, k) for q, k in zip(qs, ks)]
    qfs = [q.astype(F32) for q in qs]
    kfs = [k.astype(F32) for k in ks]
    vfs = [v.astype(F32) for v in vs]

    betas = [col_pick(d * DN_HEADS + head0 + hh) for hh, d in chains]
    g_cols = [col_pick(16 + d * DN_HEADS + head0 + hh) for hh, d in chains]
    g_rows = [row_pick(16 + d * DN_HEADS + head0 + hh) for hh, d in chains]
    totals = [gr[:, c - 1:c] if d == 0 else gr[:, 0:1] for (hh, d), gr in zip(chains, g_rows)]
    decays = [jnp.where(incl[d], jnp.exp(jnp.minimum(gc - gr, 0.0)), 0.0)
              for (hh, d), gc, gr in zip(chains, g_cols, g_rows)]
    lows = [jnp.where(strict[d], b * grams[hh] * dec, 0.0) for (hh, d), b, dec in zip(chains, betas, decays)]
    l_diags = [jnp.where(same_block, low, 0.0) for low in lows]
    l_offs = [(low - ld).astype(BF16) for low, ld in zip(lows, l_diags)]

    def neumann(accs, x_bs, n_steps):
        for _ in range(n_steps):
            x2s = [_dot(x, x) for x in x_bs]
            x_bs = [x2.astype(BF16) for x2 in x2s]
            accs = [a + x2 + _dot(a.astype(BF16), xb) for a, x2, xb in zip(accs, x2s, x_bs)]
        return accs

    d_ms = neumann([-ld for ld in l_diags], [ld.astype(BF16) for ld in l_diags], (DN_BLOCK - 1).bit_length() - 1)
    d_bs = [dm.astype(BF16) for dm in d_ms]
    ms = [lo.astype(F32) + _dot(db, lo) for db, lo in zip(d_bs, l_offs)]
    q_ms = neumann([-m for m in ms], [m.astype(BF16) for m in ms], (c // DN_BLOCK - 1).bit_length() - 1)
    e_cols = [jnp.exp(gc) for gc in g_cols]
    rhss = [jnp.concatenate([vfs[hh] * b, kfs[hh] * b * ec], axis=1) for (hh, d), b, ec in zip(chains, betas, e_cols)]
    r1s = [rhs + _dot(db, rhs.astype(BF16)) for rhs, db in zip(rhss, d_bs)]
    sols = [r1 + _dot(qm.astype(BF16), r1.astype(BF16)) for r1, qm in zip(r1s, q_ms)]
    for i, (hh, d) in enumerate(chains):
        sl = sls[hh]
        u_ref[d, :, sl] = sols[i][:, :hd]
        w_ref[d, :, sl] = sols[i][:, hd:].astype(BF16)
        qd_ref[d, :, sl] = (qfs[hh] * e_cols[i]).astype(BF16)
        kt_ref[d, sl, :] = (kfs[hh] * jnp.exp(totals[i] - g_cols[i])).T.astype(BF16)
        qk_ref[d, hh] = (qks[hh] * decays[i]).astype(BF16)


def dn_chunk(qkv, col, row, heads_per_step=4):
    s = qkv.shape[0]
    c = DN_TILE
    nt = s // c
    hd = DN_HEAD_DIM * heads_per_step
    groups = DN_HEADS // heads_per_step
    return pl.pallas_call(
        functools.partial(_dnchunk_body, heads_per_step=heads_per_step),
        grid=(nt, groups),
        in_specs=[
            pl.BlockSpec((c, hd), lambda t, h: (t, h)),
            pl.BlockSpec((c, hd), lambda t, h: (t, groups + h)),
            pl.BlockSpec((c, hd), lambda t, h: (t, 2 * groups + h)),
            pl.BlockSpec((c, LANES), lambda t, h: (t, 0)),
            pl.BlockSpec((LANES, c), lambda t, h: (0, t)),
        ],
        out_specs=[
            pl.BlockSpec((2, c, hd), lambda t, h: (0, t, h)),
            pl.BlockSpec((2, c, hd), lambda t, h: (0, t, h)),
            pl.BlockSpec((2, c, hd), lambda t, h: (0, t, h)),
            pl.BlockSpec((2, hd, c), lambda t, h: (0, h, t)),
            pl.BlockSpec((2, heads_per_step, c, c), lambda t, h: (0, h, t, 0)),
        ],
        out_shape=[
            jax.ShapeDtypeStruct((2, s, DN_WIDTH), F32),
            jax.ShapeDtypeStruct((2, s, DN_WIDTH), BF16),
            jax.ShapeDtypeStruct((2, s, DN_WIDTH), BF16),
            jax.ShapeDtypeStruct((2, DN_WIDTH, s), BF16),
            jax.ShapeDtypeStruct((2, DN_HEADS, s, c), BF16),
        ],
        compiler_params=_params(("parallel", "parallel")),
        name="dn_chunk",
    )(qkv, qkv, qkv, col, row)


def _dnscan_body(uf, wf, qdf, ktf, qkf, colf, ub, wb, qdb, ktb, qkb, colb, of_ref, ob_ref, st_ref):
    @pl.when(pl.program_id(0) == 0)
    def _():
        st_ref[...] = jnp.zeros(st_ref.shape, F32)

    c = DN_TILE
    hd = DN_HEAD_DIM
    dirs = ((uf, wf, qdf, ktf, qkf, colf, of_ref, c - 1), (ub, wb, qdb, ktb, qkb, colb, ob_ref, 0))
    chains = [(d, h) for d in range(2) for h in range(DN_HEADS)]
    sls = [slice(h * hd, (h + 1) * hd) for h in range(DN_HEADS)]
    e_tots = [jnp.exp(dirs[d][5][dirs[d][7]:dirs[d][7] + 1, :]) for d in range(2)]
    states = [st_ref[d * DN_HEADS + h] for d, h in chains]
    states_b = [st.astype(BF16) for st in states]
    v_news = [dirs[d][0][0, :, sls[h]] - _dot(dirs[d][1][0, :, sls[h]], sb) for (d, h), sb in zip(chains, states_b)]
    v_news_b = [vn.astype(BF16) for vn in v_news]
    outs = [_dot(dirs[d][2][0, :, sls[h]], sb) + _dot(dirs[d][4][0, h], vb)
            for (d, h), sb, vb in zip(chains, states_b, v_news_b)]
    news = [st * e_tots[d][:, 16 + d * DN_HEADS + h:17 + d * DN_HEADS + h] + _dot(dirs[d][3][0, sls[h], :], vb)
            for (d, h), st, vb in zip(chains, states, v_news_b)]
    for (d, h), out, new in zip(chains, outs, news):
        dirs[d][6][:, sls[h]] = out
        st_ref[d * DN_HEADS + h] = new


def dn_scan(u, w, qd, kt, qk, col):
    s = u.shape[1]
    c = DN_TILE
    nt = s // c
    wd = DN_WIDTH

    def specs(d):
        tile = (lambda t: t) if d == 0 else (lambda t: nt - 1 - t)
        return [
            pl.BlockSpec((1, c, wd), lambda t: (d, tile(t), 0)),
            pl.BlockSpec((1, c, wd), lambda t: (d, tile(t), 0)),
            pl.BlockSpec((1, c, wd), lambda t: (d, tile(t), 0)),
            pl.BlockSpec((1, wd, c), lambda t: (d, 0, tile(t))),
            pl.BlockSpec((1, DN_HEADS, c, c), lambda t: (d, 0, tile(t), 0)),
            pl.BlockSpec((c, LANES), lambda t: (tile(t), 0)),
        ]

    return pl.pallas_call(
        _dnscan_body,
        grid=(nt,),
        in_specs=specs(0) + specs(1),
        out_specs=[
            pl.BlockSpec((c, wd), lambda t: (t, 0)),
            pl.BlockSpec((c, wd), lambda t: (nt - 1 - t, 0)),
        ],
        out_shape=[jax.ShapeDtypeStruct((s, wd), F32), jax.ShapeDtypeStruct((s, wd), F32)],
        scratch_shapes=[pltpu.VMEM((2 * DN_HEADS, DN_HEAD_DIM, DN_HEAD_DIM), F32)],
        compiler_params=_params(("arbitrary",)),
        name="dn_scan",
    )(u, w, qd, kt, qk, col, u, w, qd, kt, qk, col)


def _na_bias_table(rpb):
    c = np.arange(GRID_W)
    kc = np.arange(GRID_W)
    cs = np.clip(c - NA_COLS // 2, 0, GRID_W - NA_COLS)
    valid = (kc[None, :] >= cs[:, None]) & (kc[None, :] < cs[:, None] + NA_COLS)
    dc = kc[None, :] - c[:, None] + (NA_COLS - 1)
    rpb = rpb.astype(F32)
    blocks = jnp.full(rpb.shape[:2] + (GRID_W, GRID_W), NEG_BIG, F32)
    for j in range(2 * NA_COLS - 1):
        blocks = jnp.where(valid & (dc == j), rpb[:, :, j][:, :, None, None], blocks)
    t = jnp.stack([blocks[:, NA_ROWS - 1 - dl:2 * NA_ROWS - 1 - dl] for dl in range(NA_ROWS)])
    t = t.transpose(0, 1, 3, 2, 4)
    return t.reshape(NA_ROWS, NA_HEADS, GRID_W, NA_ROWS * GRID_W)


def _na_body(q_ref, *refs):
    k_refs = refs[:NA_ROWS]
    v_refs = refs[NA_ROWS:2 * NA_ROWS]
    bias_ref = refs[2 * NA_ROWS]
    o_ref = refs[2 * NA_ROWS + 1]
    lane = lax.broadcasted_iota(I32, (1, LANES), 1)
    heads_per_block = LANES // NA_HEAD_DIM
    scale = NA_HEAD_DIM ** -0.5
    n_pairs = NA_WIDTH // LANES
    heads = [(pair, hh) for pair in range(n_pairs) for hh in range(heads_per_block)]
    sls = [slice(pair * LANES, (pair + 1) * LANES) for pair in range(n_pairs)]
    owns = [(lane // NA_HEAD_DIM) == hh for hh in range(heads_per_block)]
    q2s = [q_ref[:, sl] for sl in sls]
    k2s = [jnp.concatenate([r[:, sl] for r in k_refs], axis=0) for sl in sls]
    scores = [_dot_nt(jnp.where(owns[hh], q2s[pair], jnp.zeros_like(q2s[pair])), k2s[pair]) * scale
              + bias_ref[0, pair * heads_per_block + hh] for pair, hh in heads]
    maxes = [jnp.max(s, axis=-1, keepdims=True) for s in scores]
    probs = [jnp.exp(s - m) for s, m in zip(scores, maxes)]
    denoms = [jnp.sum(p, axis=-1, keepdims=True) for p in probs]
    v2s = [jnp.concatenate([r[:, sl] for r in v_refs], axis=0) for sl in sls]
    outs = [_dot(p.astype(BF16), jnp.where(owns[hh], v2s[pair], jnp.zeros_like(v2s[pair]))) / den
            for (pair, hh), p, den in zip(heads, probs, denoms)]
    for pair in range(n_pairs):
        acc = outs[pair * heads_per_block]
        for hh in range(1, heads_per_block):
            acc = acc + outs[pair * heads_per_block + hh]
        o_ref[:, sls[pair]] = acc.astype(o_ref.dtype)


def na_attention(proj, col0, rpb):
    s = proj.shape[0]
    rows = s // GRID_W
    assert rows >= NA_ROWS
    qb = col0 // NA_WIDTH
    table = _na_bias_table(rpb)

    def first_row(r):
        return jnp.clip(r - NA_ROWS // 2, 0, rows - NA_ROWS)

    def kv_spec(i, blk):
        return pl.BlockSpec((GRID_W, NA_WIDTH), lambda r: (first_row(r) + i, blk))

    in_specs = [pl.BlockSpec((GRID_W, NA_WIDTH), lambda r: (r, qb))]
    in_specs += [kv_spec(i, qb + 1) for i in range(NA_ROWS)]
    in_specs += [kv_spec(i, qb + 2) for i in range(NA_ROWS)]
    in_specs += [pl.BlockSpec((1, NA_HEADS, GRID_W, NA_ROWS * GRID_W), lambda r: (r - first_row(r), 0, 0, 0))]
    return pl.pallas_call(
        _na_body,
        grid=(rows,),
        in_specs=in_specs,
        out_specs=pl.BlockSpec((GRID_W, NA_WIDTH), lambda r: (r, 0)),
        out_shape=jax.ShapeDtypeStruct((s, NA_WIDTH), BF16),
        compiler_params=_params(("parallel",)),
        name="na_attn",
    )(proj, *([proj] * (2 * NA_ROWS)), table)


def _pack_bf16_pair(lo, hi):
    lo_bits = pltpu.bitcast(lo.astype(BF16).astype(F32), U32)
    hi_bits = pltpu.bitcast(hi.astype(BF16).astype(F32), U32)
    return (lo_bits >> 16) | (hi_bits & jnp.uint32(0xFFFF0000))


def _unpack_bf16_pair(packed):
    lo = pltpu.bitcast(packed << 16, F32)
    hi = pltpu.bitcast(packed & jnp.uint32(0xFFFF0000), F32)
    return lo, hi


def _merge_body(of_ref, ob_ref, z_ref, na_ref, ga_ref, gb_ref, x_ref, dng_ref, wa_ref, wb_ref, wo_ref,
                gffn_ref, wr_ref, br_ref, x1_ref, h2p_ref, lg_ref, dn_ref):
    hd = DN_HEAD_DIM
    for h in range(DN_HEADS):
        sl = slice(h * hd, (h + 1) * hd)
        o = of_ref[:, sl] + ob_ref[:, sl]
        r = lax.rsqrt(jnp.mean(o * o, axis=-1, keepdims=True) + RMS_EPS)
        z = z_ref[:, sl].astype(F32)
        dn_ref[:, sl] = (o * r * dng_ref[...] * (z * _sigmoid(z))).astype(BF16)
    y_a = _dot(dn_ref[...], wa_ref[...])
    y_b = _dot(na_ref[...], wb_ref[...])
    mixed = _sigmoid(ga_ref[...].astype(F32)) * y_a + _sigmoid(gb_ref[...].astype(F32)) * y_b
    x1 = x_ref[...] + _dot(mixed.astype(BF16), wo_ref[...])
    x1_ref[...] = x1
    r = lax.rsqrt(jnp.mean(x1 * x1, axis=-1, keepdims=True) + RMS_EPS)
    h2 = x1 * r * gffn_ref[...]
    h2_hi = h2.astype(BF16)
    h2_lo = (h2 - h2_hi.astype(F32)).astype(BF16)
    lg_ref[...] = (_dot(h2_hi, wr_ref[0]) + _dot(h2_lo, wr_ref[0]) + _dot(h2_hi, wr_ref[1])) + br_ref[...]
    half = h2.shape[1] // 2
    n_tiles = half // LANES
    packed = _pack_bf16_pair(h2[:, :half], h2[:, half:])
    for c in range(n_tiles):
        h2p_ref[pl.ds(c, h2.shape[0], stride=n_tiles), :] = packed[:, c * LANES:(c + 1) * LANES]


def merge(o_f, o_b, proj, z_blk, gate_blk, na_out, x, dn_norm_g, w_a, w_b, w_o, norm_ffn_g, w_router, b_router,
          tm=256):
    s, d = x.shape
    ne = w_router.shape[1]
    const = lambda i: (0, 0)
    single = pl.Buffered(1)
    w_router_hi = w_router.astype(BF16)
    w_router_lo = (w_router.astype(F32) - w_router_hi.astype(F32)).astype(BF16)
    w_router_split = jnp.stack([w_router_hi, w_router_lo])
    return pl.pallas_call(
        _merge_body,
        grid=(s // tm,),
        in_specs=[
            pl.BlockSpec((tm, DN_WIDTH), lambda i: (i, 0)),
            pl.BlockSpec((tm, DN_WIDTH), lambda i: (i, 0)),
            pl.BlockSpec((tm, DN_WIDTH), lambda i: (i, z_blk)),
            pl.BlockSpec((tm, NA_WIDTH), lambda i: (i, 0)),
            pl.BlockSpec((tm, d), lambda i: (i, gate_blk)),
            pl.BlockSpec((tm, d), lambda i: (i, gate_blk + 1)),
            pl.BlockSpec((tm, d), lambda i: (i, 0)),
            pl.BlockSpec((1, DN_HEAD_DIM), const),
            pl.BlockSpec((DN_WIDTH, d), const, pipeline_mode=single),
            pl.BlockSpec((NA_WIDTH, d), const, pipeline_mode=single),
            pl.BlockSpec((d, d), const, pipeline_mode=single),
            pl.BlockSpec((1, d), const),
            pl.BlockSpec((2, d, ne), lambda i: (0, 0, 0)),
            pl.BlockSpec((1, ne), const),
        ],
        out_specs=[
            pl.BlockSpec((tm, d), lambda i: (i, 0)),
            pl.BlockSpec((tm * (d // 2 // LANES), LANES), lambda i: (i, 0)),
            pl.BlockSpec((tm, ne), lambda i: (i, 0)),
        ],
        out_shape=[
            jax.ShapeDtypeStruct((s, d), F32),
            jax.ShapeDtypeStruct((s * (d // 2 // LANES), LANES), U32),
            jax.ShapeDtypeStruct((s, ne), F32),
        ],
        scratch_shapes=[pltpu.VMEM((tm, DN_WIDTH), BF16)],
        compiler_params=_params(("parallel",)),
        name="merge",
    )(o_f, o_b, proj, na_out, proj, proj, x, dn_norm_g.reshape(1, -1).astype(F32), w_a, w_b, w_o,
      norm_ffn_g.reshape(1, d).astype(F32), w_router_split, b_router.reshape(1, ne).astype(F32))


def _route_body(lg_ref, ti_ref, tw_ref, cnt_ref, carry_ref):
    @pl.when(pl.program_id(0) == 0)
    def _():
        carry_ref[...] = jnp.zeros(carry_ref.shape, F32)

    lg = lg_ref[...]
    tm, ne = lg.shape
    lane = lax.broadcasted_iota(I32, (tm, ne), 1).astype(F32)
    work = lg
    vals, idxs = [], []
    onehot = jnp.zeros((tm, ne), F32)
    for _ in range(TOP_K):
        m = jnp.max(work, axis=-1, keepdims=True)
        idx = jnp.min(jnp.where(work == m, lane, float(ne)), axis=-1, keepdims=True)
        hit = lane == idx
        vals.append(m)
        idxs.append(idx)
        onehot = onehot + jnp.where(hit, 1.0, 0.0)
        work = jnp.where(hit, -jnp.inf, work)
    exps = [jnp.exp(v - vals[0]) for v in vals]
    denom = exps[0] + exps[1] + exps[2] + exps[3]
    ri = lax.broadcasted_iota(I32, (tm, tm), 0)
    ci = lax.broadcasted_iota(I32, (tm, tm), 1)
    strict = jnp.where(ci < ri, 1.0, 0.0).astype(BF16)
    before = _dot(strict, onehot.astype(BF16)) + carry_ref[0:1, 0:ne]
    lane_o = lax.broadcasted_iota(I32, (tm, LANES), 1)
    ti = jnp.zeros((tm, LANES), I32)
    tw = jnp.zeros((tm, LANES), F32)
    for kk in range(TOP_K):
        rank = jnp.sum(jnp.where(lane == idxs[kk], before, 0.0), axis=-1, keepdims=True).astype(I32)
        ti = jnp.where(lane_o == kk, idxs[kk].astype(I32), ti)
        ti = jnp.where(lane_o == TOP_K + kk, rank, ti)
        tw = jnp.where(lane_o == kk, exps[kk] / denom, tw)
    ti_ref[...] = ti
    tw_ref[...] = tw
    total = carry_ref[0:1, 0:ne] + jnp.sum(onehot, axis=0, keepdims=True)
    carry_ref[0:1, 0:ne] = total
    cnt_ref[...] = jnp.zeros(cnt_ref.shape, F32)
    cnt_ref[0:1, 0:ne] = total


def route(logits, tm=512):
    s, ne = logits.shape
    return pl.pallas_call(
        _route_body,
        grid=(s // tm,),
        in_specs=[pl.BlockSpec((tm, ne), lambda i: (i, 0))],
        out_specs=[
            pl.BlockSpec((tm, LANES), lambda i: (i, 0)),
            pl.BlockSpec((tm, LANES), lambda i: (i, 0)),
            pl.BlockSpec((8, LANES), lambda i: (0, 0)),
        ],
        out_shape=[
            jax.ShapeDtypeStruct((s, LANES), I32),
            jax.ShapeDtypeStruct((s, LANES), F32),
            jax.ShapeDtypeStruct((8, LANES), F32),
        ],
        scratch_shapes=[pltpu.VMEM((8, LANES), F32)],
        compiler_params=_params(("arbitrary",)),
        name="route",
    )(logits)


def _dest_body(ti_ref, ps_ref, d_ref):
    ti = ti_ref[...].astype(F32)
    tm = ti.shape[0]
    lane = lax.broadcasted_iota(I32, (tm, LANES), 1)
    lane_f = lane.astype(F32)
    ps = ps_ref[0:1, :].astype(F32)
    out = jnp.zeros((tm, LANES), F32)
    for kk in range(TOP_K):
        e = jnp.sum(jnp.where(lane == kk, ti, 0.0), axis=-1, keepdims=True)
        rank = jnp.sum(jnp.where(lane == TOP_K + kk, ti, 0.0), axis=-1, keepdims=True)
        start = jnp.sum(jnp.where(lane_f == e, ps, 0.0), axis=-1, keepdims=True)
        out = jnp.where(lane == kk, start + rank, out)
    d_ref[...] = out.astype(I32)


def route_dest(ti, pad_start, tm=512):
    s = ti.shape[0]
    ps = jnp.zeros((8, LANES), I32).at[0, :N_EXPERTS].set(pad_start)
    return pl.pallas_call(
        _dest_body,
        grid=(s // tm,),
        in_specs=[pl.BlockSpec((tm, LANES), lambda i: (i, 0)), pl.BlockSpec((8, LANES), lambda i: (0, 0))],
        out_specs=pl.BlockSpec((tm, LANES), lambda i: (i, 0)),
        out_shape=jax.ShapeDtypeStruct((s, LANES), I32),
        compiler_params=_params(("parallel",)),
        name="route_dest",
    )(ti, ps)


def _scatter_body(pend_ref, padded_ref, dest_ref, h_ref, xr_ref, zero_ref, sem):
    tm = h_ref.shape[0]

    def zero_copy(e):
        return pltpu.make_async_copy(zero_ref, xr_ref.at[pl.ds(pend_ref[e] - MOE_SUB, MOE_SUB)], sem)

    @pl.when(pl.program_id(0) == 0)
    def _():
        zero_ref[...] = jnp.zeros(zero_ref.shape, U32)

        def start(e, carry):
            @pl.when(padded_ref[e] > 0)
            def _():
                zero_copy(e).start()
            return carry

        def wait(e, carry):
            @pl.when(padded_ref[e] > 0)
            def _():
                zero_copy(e).wait()
            return carry

        lax.fori_loop(0, N_EXPERTS, start, 0)
        lax.fori_loop(0, N_EXPERTS, wait, 0)

        def slack_copy(b):
            return pltpu.make_async_copy(zero_ref, xr_ref.at[pl.ds(b * MOE_SUB, MOE_SUB)], sem)

        def slack_start(b, carry):
            slack_copy(b).start()
            return carry

        def slack_wait(b, carry):
            slack_copy(b).wait()
            return carry

        first_slack = pend_ref[N_EXPERTS - 1] // MOE_SUB
        lax.fori_loop(first_slack, xr_ref.shape[0] // MOE_SUB, slack_start, 0)
        lax.fori_loop(first_slack, xr_ref.shape[0] // MOE_SUB, slack_wait, 0)

    def row_copy(r, kk):
        d = dest_ref[r * TOP_K + kk]
        return pltpu.make_async_copy(h_ref.at[r], xr_ref.at[d], sem)

    def start(r, carry):
        for kk in range(TOP_K):
            row_copy(r, kk).start()
        return carry

    def wait(r, carry):
        for kk in range(TOP_K):
            row_copy(r, kk).wait()
        return carry

    lax.fori_loop(0, tm, start, 0)
    lax.fori_loop(0, tm, wait, 0)


def moe_scatter(h2p, dest, pad_end, padded, n_rows, tm=256):
    s, nt, _ = h2p.shape
    grid_spec = pltpu.PrefetchScalarGridSpec(
        num_scalar_prefetch=2,
        grid=(s // tm,),
        in_specs=[
            pl.BlockSpec((tm * TOP_K,), lambda i, *_: (i,), memory_space=pltpu.SMEM),
            pl.BlockSpec((tm, nt, LANES), lambda i, *_: (i, 0, 0)),
        ],
        out_specs=pl.BlockSpec(memory_space=pl.ANY),
        scratch_shapes=[pltpu.VMEM((MOE_SUB, nt, LANES), U32), pltpu.SemaphoreType.DMA(())],
    )
    return pl.pallas_call(
        _scatter_body,
        grid_spec=grid_spec,
        out_shape=jax.ShapeDtypeStruct((n_rows, nt, LANES), U32),
        compiler_params=_params(("arbitrary",)),
        name="moe_scatter",
    )(pad_end, padded, dest, h2p)


def _moe_body(ie_ref, ir_ref, inb_ref, ni_ref, xr_ref, wg_ref, wu_ref, wd_ref, bg_ref, bu_ref, bd_ref, y_ref,
              stage_ref, ystage_ref, xb_ref, acc_ref, wgb_ref, wub_ref, wdb_ref, sem_in, sem_out, *, n_chunks):
    w = pl.program_id(0)
    j = pl.program_id(1)
    sb = MOE_SUB
    d = acc_ref.shape[1]
    half = d // 2
    nx = half // LANES
    ny = d // LANES

    @pl.when(w < ni_ref[0])
    def _():
        nb = inb_ref[w]
        r0 = ir_ref[w]

        @pl.when(j == 0)
        def _():
            @pl.when(w == 0)
            def _():
                ystage_ref[0] = jnp.zeros(ystage_ref.shape[1:], F32)

                def slack_copy(b):
                    dst = y_ref.at[pl.ds(pl.multiple_of(b * (sb * ny), sb * ny), sb * ny), :]
                    return pltpu.make_async_copy(ystage_ref.at[0], dst, sem_out.at[0])

                def slack_start(b, carry):
                    slack_copy(b).start()
                    return carry

                def slack_wait(b, carry):
                    slack_copy(b).wait()
                    return carry

                n_blocks = y_ref.shape[0] // (sb * ny)
                lax.fori_loop(ni_ref[1], n_blocks, slack_start, 0)
                lax.fori_loop(ni_ref[1], n_blocks, slack_wait, 0)

            def in_copy(i, slot):
                src = xr_ref.at[pl.ds(pl.multiple_of((r0 + i * sb) * nx, sb * nx), sb * nx), :]
                return pltpu.make_async_copy(src, stage_ref.at[slot], sem_in.at[slot])

            in_copy(0, 0).start()

            def load(i, carry):
                slot = i % 2

                @pl.when(i + 1 < nb)
                def _():
                    in_copy(i + 1, 1 - slot).start()

                in_copy(i, slot).wait()
                rows = pl.ds(pl.multiple_of(i * sb, sb), sb)
                for c in range(nx):
                    lo, hi = _unpack_bf16_pair(stage_ref[slot, pl.ds(c, sb, stride=nx), :])
                    xb_ref[rows, c * LANES:(c + 1) * LANES] = lo.astype(BF16)
                    xb_ref[rows, half + c * LANES:half + (c + 1) * LANES] = hi.astype(BF16)
                return carry

            lax.fori_loop(0, nb, load, 0)

        wgb_ref[...] = wg_ref[0].astype(BF16)
        wub_ref[...] = wu_ref[0].astype(BF16)
        wdb_ref[...] = wd_ref[0].astype(BF16)

        def block(first, n_sub, i, carry):
            rows = pl.ds(pl.multiple_of(i * sb, sb), n_sub * sb)
            xs = xb_ref[rows, :]
            gate = jnp.minimum(_dot(xs, wgb_ref[...]) + bg_ref[0], SWIGLU_LIMIT)
            up = jnp.clip(_dot(xs, wub_ref[...]) + bu_ref[0], -SWIGLU_LIMIT, SWIGLU_LIMIT)
            act = (up + 1.0) * (gate * _sigmoid(SWIGLU_ALPHA * gate))
            contrib = _dot(act.astype(BF16), wdb_ref[...])
            if first:
                acc_ref[rows, :] = contrib
            else:
                acc_ref[rows, :] += contrib
            return carry

        def all_blocks(first):
            def pair(i2, carry):
                return block(first, 2, 2 * i2, carry)

            lax.fori_loop(0, nb // 2, pair, 0)

            @pl.when(nb % 2 == 1)
            def _():
                block(first, 1, nb - 1, 0)

        @pl.when(j == 0)
        def _():
            all_blocks(True)

        @pl.when(j > 0)
        def _():
            all_blocks(False)

        @pl.when(j == n_chunks - 1)
        def _():
            def out_copy(i, slot):
                dst = y_ref.at[pl.ds(pl.multiple_of((r0 + i * sb) * ny, sb * ny), sb * ny), :]
                return pltpu.make_async_copy(ystage_ref.at[slot], dst, sem_out.at[slot])

            def store(i, carry):
                slot = i % 2

                @pl.when(i >= 2)
                def _():
                    out_copy(i - 2, slot).wait()

                rows = pl.ds(pl.multiple_of(i * sb, sb), sb)
                for c in range(ny):
                    cols = slice(c * LANES, (c + 1) * LANES)
                    ystage_ref[slot, pl.ds(c, sb, stride=ny), :] = acc_ref[rows, cols] + bd_ref[0, :, cols]
                out_copy(i, slot).start()
                return carry

            lax.fori_loop(0, nb, store, 0)

            @pl.when(nb >= 2)
            def _():
                out_copy(nb - 2, nb % 2).wait()

            out_copy(nb - 1, (nb - 1) % 2).wait()


def moe_ffn(x_rows, w_gate_up, b_gate_up, w_down, b_down, item_e, item_row, item_nb, n_items, max_items):
    ne, d, two_de = w_gate_up.shape
    nx = d // 2 // LANES
    ny = d // LANES
    n_rows = x_rows.shape[0] // nx
    de = two_de // 2
    tn = MOE_TN
    n_chunks = de // tn
    last = n_chunks - 1

    def chunk(w, j, ni):
        return jnp.where(w < ni[0], j, last)

    grid_spec = pltpu.PrefetchScalarGridSpec(
        num_scalar_prefetch=4,
        grid=(max_items, n_chunks),
        in_specs=[
            pl.BlockSpec(memory_space=pl.ANY),
            pl.BlockSpec((1, d, tn), lambda w, j, ie, ir, inb, ni: (ie[w], 0, chunk(w, j, ni))),
            pl.BlockSpec((1, d, tn), lambda w, j, ie, ir, inb, ni: (ie[w], 0, n_chunks + chunk(w, j, ni))),
            pl.BlockSpec((1, tn, d), lambda w, j, ie, ir, inb, ni: (ie[w], chunk(w, j, ni), 0)),
            pl.BlockSpec((1, 1, tn), lambda w, j, ie, ir, inb, ni: (ie[w], 0, chunk(w, j, ni))),
            pl.BlockSpec((1, 1, tn), lambda w, j, ie, ir, inb, ni: (ie[w], 0, n_chunks + chunk(w, j, ni))),
            pl.BlockSpec((1, 1, d), lambda w, j, ie, ir, inb, ni: (ie[w], 0, 0)),
        ],
        out_specs=pl.BlockSpec(memory_space=pl.ANY),
        scratch_shapes=[
            pltpu.VMEM((2, MOE_SUB * nx, LANES), U32),
            pltpu.VMEM((2, MOE_SUB * ny, LANES), F32),
            pltpu.VMEM((MOE_TM, d), BF16),
            pltpu.VMEM((MOE_TM, d), F32),
            pltpu.VMEM((d, tn), BF16),
            pltpu.VMEM((d, tn), BF16),
            pltpu.VMEM((tn, d), BF16),
            pltpu.SemaphoreType.DMA((2,)),
            pltpu.SemaphoreType.DMA((2,)),
        ],
    )
    return pl.pallas_call(
        functools.partial(_moe_body, n_chunks=n_chunks),
        grid_spec=grid_spec,
        out_shape=jax.ShapeDtypeStruct((n_rows * ny, LANES), F32),
        compiler_params=_params(("arbitrary", "arbitrary")),
        name="moe_ffn",
    )(item_e, item_row, item_nb, n_items, x_rows, w_gate_up, w_gate_up, w_down,
      b_gate_up.reshape(ne, 1, two_de), b_gate_up.reshape(ne, 1, two_de), b_down.reshape(ne, 1, d))


def _final_body(dest_ref, dest_next_ref, x1_ref, tw_ref, p_ref, gple_ref, wg_ref, wp_ref, gfin_ref, y_ref, o_ref,
                ybuf_ref, sem, *, last_layer):
    i = pl.program_id(0)
    tm = x1_ref.shape[0]
    ny = x1_ref.shape[1] // LANES
    slot = i % 2

    def row_copy(d_ref, to_slot, r, kk):
        dst = ybuf_ref.at[to_slot, kk, pl.ds(pl.multiple_of(r * ny, ny), ny), :]
        return pltpu.make_async_copy(y_ref.at[d_ref[r * TOP_K + kk]], dst, sem.at[to_slot])

    def issue(d_ref, to_slot):
        def body(r, carry):
            for kk in range(TOP_K):
                row_copy(d_ref, to_slot, r, kk).start()
            return carry

        lax.fori_loop(0, tm, body, 0)

    def drain(d_ref, to_slot):
        def body(r, carry):
            for kk in range(TOP_K):
                row_copy(d_ref, to_slot, r, kk).wait()
            return carry

        lax.fori_loop(0, tm, body, 0)

    @pl.when(i == 0)
    def _():
        issue(dest_ref, 0)

    @pl.when(i + 1 < pl.num_programs(0))
    def _():
        issue(dest_next_ref, 1 - slot)

    drain(dest_ref, slot)

    tw = tw_ref[...]
    chunks = []
    for c in range(ny):
        acc = x1_ref[:, c * LANES:(c + 1) * LANES]
        for kk in range(TOP_K):
            acc = acc + tw[:, kk:kk + 1] * ybuf_ref[slot, kk, pl.ds(c, tm, stride=ny), :]
        chunks.append(acc)
    x2 = jnp.concatenate(chunks, axis=1)
    r = lax.rsqrt(jnp.mean(x2 * x2, axis=-1, keepdims=True) + RMS_EPS)
    n = (x2 * r * gple_ref[...]).astype(BF16)
    gate = _sigmoid(_dot(n, wg_ref[...]))
    x3 = x2 + gate * _dot(p_ref[...].astype(BF16), wp_ref[...])
    if last_layer:
        r = lax.rsqrt(jnp.mean(x3 * x3, axis=-1, keepdims=True) + RMS_EPS)
        x3 = x3 * r * gfin_ref[...]
    o_ref[...] = x3


def final(dest, x1, tw, p, norm_ple_g, w_gate, w_proj, norm_final_g, y_rows, last_layer, tm=256):
    s, d = x1.shape
    pd = p.shape[1]
    ny = d // LANES
    n_steps = s // tm
    const = lambda i: (0, 0)
    single = pl.Buffered(1)
    return pl.pallas_call(
        functools.partial(_final_body, last_layer=last_layer),
        grid=(n_steps,),
        in_specs=[
            pl.BlockSpec((tm * TOP_K,), lambda i: (i,), memory_space=pltpu.SMEM),
            pl.BlockSpec((tm * TOP_K,), lambda i: (jnp.minimum(i + 1, n_steps - 1),), memory_space=pltpu.SMEM),
            pl.BlockSpec((tm, d), lambda i: (i, 0)),
            pl.BlockSpec((tm, LANES), lambda i: (i, 0)),
            pl.BlockSpec((tm, pd), lambda i: (i, 0)),
            pl.BlockSpec((1, d), const),
            pl.BlockSpec((d, d), const, pipeline_mode=single),
            pl.BlockSpec((pd, d), const, pipeline_mode=single),
            pl.BlockSpec((1, d), const),
            pl.BlockSpec(memory_space=pl.ANY),
        ],
        out_specs=pl.BlockSpec((tm, d), lambda i: (i, 0)),
        out_shape=jax.ShapeDtypeStruct((s, d), F32),
        scratch_shapes=[pltpu.VMEM((2, TOP_K, tm * ny, LANES), F32), pltpu.SemaphoreType.DMA((2,))],
        compiler_params=_params(("arbitrary",)),
        name="final",
    )(dest, dest, x1, tw, p, norm_ple_g.reshape(1, d).astype(F32), w_gate, w_proj,
      norm_final_g.reshape(1, d).astype(F32), y_rows.reshape(-1, ny, LANES))


def _moe_tables(counts, n_rows):
    sub, tm = MOE_SUB, MOE_TM
    max_items = N_EXPERTS + n_rows // tm
    padded = (counts + sub - 1) // sub * sub
    pad_end = jnp.cumsum(padded)
    pad_start = pad_end - padded
    n_it = (padded + tm - 1) // tm
    it_end = jnp.cumsum(n_it)
    it_start = it_end - n_it
    n_items = it_end[-1]
    w = jnp.arange(max_items, dtype=I32)
    live = w < n_items
    w_eff = jnp.minimum(w, n_items - 1)
    e_w = jnp.minimum(jnp.searchsorted(it_end, w_eff, side="right"), N_EXPERTS - 1).astype(I32)
    m_w = w_eff - it_start[e_w]
    row_w = pad_start[e_w] + m_w * tm
    nb_w = jnp.clip((padded[e_w] - m_w * tm) // sub, 0, tm // sub)
    nb_w = jnp.where(live, nb_w, 0)
    as_i32 = lambda a: a.astype(I32)
    counts_w = jnp.stack([n_items, pad_end[-1] // sub])
    return (as_i32(pad_start), as_i32(pad_end), as_i32(padded), e_w, as_i32(row_w), as_i32(nb_w),
            as_i32(counts_w), max_items)


def _layer(x, p, norm_mix_g, w_in, dn_conv_w, dn_a_log, dn_dt_bias, dn_norm_g, na_rpb, w_branch_a, w_branch_b,
           w_out, norm_ffn_g, w_router, b_router, w_gate_up, b_gate_up, w_down, b_down, norm_ple_g,
           w_ple_gate, w_ple_proj, norm_final_g, last_layer):
    s, d = x.shape
    c_qkv, c_z, c_small = 3 * DN_WIDTH, DN_WIDTH, 4 * DN_HEADS
    o_small = c_qkv + c_z
    o_na = o_small + c_small
    o_gates = o_na + 3 * NA_WIDTH
    w_main = jnp.concatenate([w_in[:, :o_small], w_in[:, o_gates:], w_in[:, o_na:o_gates]], axis=1).astype(BF16)
    w_small = jnp.zeros((d, LANES), BF16).at[:, :c_small].set(w_in[:, o_small:o_na].astype(BF16))
    z_blk = c_qkv // DN_WIDTH
    gate_blk = (c_qkv + c_z) // d
    na_col0 = c_qkv + c_z + 2 * d

    proj, small = in_projection(x, norm_mix_g.astype(F32), w_main, w_small)

    qkv = dn_prep(proj, dn_conv_w.astype(F32))
    col, row = dn_gates(small, dn_a_log, dn_dt_bias)
    u, w, qd, kt, qk = dn_chunk(qkv, col, row)
    o_f, o_b = dn_scan(u, w, qd, kt, qk, col)

    na_out = na_attention(proj, na_col0, na_rpb)

    x1, h2p, logits = merge(o_f, o_b, proj, z_blk, gate_blk, na_out, x, dn_norm_g, w_branch_a.astype(BF16),
                            w_branch_b.astype(BF16), w_out.astype(BF16), norm_ffn_g, w_router, b_router)

    ti, tw, cnt = route(logits)
    counts = cnt[0, :N_EXPERTS].astype(I32)
    n_rows = (s * TOP_K + N_EXPERTS * (MOE_SUB - 1) + MOE_SUB - 1) // MOE_SUB * MOE_SUB
    pad_start, pad_end, padded, item_e, item_row, item_nb, n_items, max_items = _moe_tables(counts, n_rows)
    dest = route_dest(ti, pad_start)[:, :TOP_K].reshape(-1)

    nx = d // 2 // LANES
    x_rows = moe_scatter(h2p.reshape(s, nx, LANES), dest, pad_end, padded, n_rows).reshape(n_rows * nx, LANES)
    y_rows = moe_ffn(x_rows, w_gate_up, b_gate_up, w_down, b_down, item_e, item_row, item_nb, n_items, max_items)

    return final(dest, x1, tw, p, norm_ple_g, w_ple_gate.astype(BF16), w_ple_proj.astype(BF16), norm_final_g,
                 y_rows, last_layer)


def kernel(x, p, norm_mix_g, w_in, dn_conv_w, dn_a_log, dn_dt_bias, dn_norm_g, na_rpb, w_branch_a, w_branch_b, w_out, norm_ffn_g, w_router, b_router, w_gate_up, b_gate_up, w_down, b_down, norm_ple_g, w_ple_gate, w_ple_proj, norm_final_g):
    bsz, s, d = x.shape
    depth = w_in.shape[0]
    outs = []
    for b in range(bsz):
        xb = x[b]
        for i in range(depth):
            xb = _layer(xb, p[i, b], norm_mix_g[i], w_in[i], dn_conv_w[i], dn_a_log[i], dn_dt_bias[i], dn_norm_g[i],
                        na_rpb[i], w_branch_a[i], w_branch_b[i], w_out[i], norm_ffn_g[i], w_router[i], b_router[i],
                        w_gate_up[i], b_gate_up[i], w_down[i], b_down[i], norm_ple_g[i], w_ple_gate[i],
                        w_ple_proj[i], norm_final_g, i == depth - 1)
        outs.append(xb)
    return jnp.stack(outs, axis=0)
```

```python
import functools

import jax
import jax.numpy as jnp
import numpy as np
from jax import lax
from jax.experimental import pallas as pl
from jax.experimental.pallas import tpu as pltpu

F32 = jnp.float32
BF16 = jnp.bfloat16
I32 = jnp.int32
U32 = jnp.uint32

GRID_W = 64
DN_HEADS = 8
DN_HEAD_DIM = 128
DN_WIDTH = DN_HEADS * DN_HEAD_DIM
DN_CONV = 5
NA_HEADS = 16
NA_HEAD_DIM = 64
NA_WIDTH = NA_HEADS * NA_HEAD_DIM
NA_ROWS = 8
NA_COLS = 16
N_EXPERTS = 32
TOP_K = 4
SWIGLU_LIMIT = 7.0
SWIGLU_ALPHA = 1.702
RMS_EPS = 1e-6

LANES = 128
VMEM_LIMIT = 56 * 1024 * 1024

DN_TILE = 256
DN_BLOCK = 16
MOE_SUB = 256
MOE_TM = 2048
MOE_TN = 256
NEG_BIG = -1e30


def _sigmoid(x):
    return 1.0 / (1.0 + jnp.exp(-x))


def _dot(a, b):
    return jnp.dot(a, b, preferred_element_type=F32)


def _dot_nt(a, b):
    return lax.dot_general(a, b, (((1,), (1,)), ((), ())), preferred_element_type=F32)


def _params(sem, limit=VMEM_LIMIT):
    return pltpu.CompilerParams(dimension_semantics=sem, vmem_limit_bytes=limit)


def _wprep_body(a_ref, b_ref, o_ref, *, n_aligned, shift):
    @pl.when(pl.program_id(1) < n_aligned)
    def _():
        o_ref[...] = a_ref[...].astype(o_ref.dtype)

    @pl.when(pl.program_id(1) >= n_aligned)
    def _():
        o_ref[...] = jnp.concatenate([a_ref[:, shift:], b_ref[:, :shift]], axis=1).astype(o_ref.dtype)


def w_prep(w_in, tr=1024, tn=1024):
    d, n_in = w_in.shape
    c_small = 4 * DN_HEADS
    o_small = 4 * DN_WIDTH
    o_na = o_small + c_small
    o_gates = o_na + 3 * NA_WIDTH
    assert o_small % tn == 0 and (3 * NA_WIDTH) % tn == 0 and (n_in - o_gates) % tn == 0 and c_small < LANES
    n_a, n_g, n_n = o_small // tn, (n_in - o_gates) // tn, 3 * NA_WIDTH // tn
    a_blk_gates = (o_gates - c_small) // tn
    a_blk_na = o_small // tn

    def a_index(t):
        return jnp.where(t < n_a, t, jnp.where(t < n_a + n_g, a_blk_gates + t - n_a, a_blk_na + t - n_a - n_g))

    return pl.pallas_call(
        functools.partial(_wprep_body, n_aligned=n_a, shift=c_small),
        grid=(d // tr, n_a + n_g + n_n),
        in_specs=[
            pl.BlockSpec((tr, tn), lambda r, t: (r, a_index(t))),
            pl.BlockSpec((tr, LANES), lambda r, t: (r, (a_index(t) + 1) * (tn // LANES))),
        ],
        out_specs=pl.BlockSpec((tr, tn), lambda r, t: (r, t)),
        out_shape=jax.ShapeDtypeStruct((d, (n_a + n_g + n_n) * tn), BF16),
        compiler_params=_params(("parallel", "parallel")),
        name="w_prep",
    )(w_in, w_in)


def _inproj_body(x_ref, g_ref, w_ref, ws_ref, o_ref, os_ref, h_ref):
    @pl.when(pl.program_id(1) == 0)
    def _():
        x = x_ref[...]
        r = lax.rsqrt(jnp.mean(x * x, axis=-1, keepdims=True) + RMS_EPS)
        h = (x * r * g_ref[...]).astype(BF16)
        h_ref[...] = h
        os_ref[...] = _dot(h, ws_ref[...].astype(BF16))

    o_ref[...] = _dot(h_ref[...], w_ref[...]).astype(o_ref.dtype)


def in_projection(x, g, w_main, w_in, tm=512, tn=1024):
    s, d = x.shape
    n = w_main.shape[1]
    small_blk = 4 * DN_WIDTH // LANES
    return pl.pallas_call(
        _inproj_body,
        grid=(s // tm, n // tn),
        in_specs=[
            pl.BlockSpec((tm, d), lambda i, j: (i, 0)),
            pl.BlockSpec((1, d), lambda i, j: (0, 0)),
            pl.BlockSpec((d, tn), lambda i, j: (0, j)),
            pl.BlockSpec((d, LANES), lambda i, j: (0, small_blk)),
        ],
        out_specs=[
            pl.BlockSpec((tm, tn), lambda i, j: (i, j)),
            pl.BlockSpec((tm, LANES), lambda i, j: (i, 0)),
        ],
        out_shape=[jax.ShapeDtypeStruct((s, n), BF16), jax.ShapeDtypeStruct((s, LANES), F32)],
        scratch_shapes=[pltpu.VMEM((tm, d), BF16)],
        compiler_params=_params(("parallel", "arbitrary")),
        name="in_proj",
    )(x, g.reshape(1, d), w_main, w_in)


def _dnprep_body(x_ref, w_ref, o_ref, pad_ref, *, seq, chunk):
    cb = pl.program_id(0)
    n_chunks = seq // chunk
    zeros = jnp.zeros((16, LANES), F32)
    pad_ref[0:16, :] = zeros
    pad_ref[seq + 16:seq + 32, :] = zeros

    def fill(c, carry):
        r0 = pl.multiple_of(c * chunk, chunk)
        pad_ref[pl.ds(r0 + 16, chunk), :] = x_ref[pl.ds(r0, chunk), :].astype(F32)
        return carry

    lax.fori_loop(0, n_chunks, fill, 0)

    w = w_ref[...]
    is_v = cb >= 2 * DN_HEADS
    scale = jnp.where(cb < DN_HEADS, DN_HEAD_DIM ** -0.5, 1.0).astype(F32)

    def body(c, carry):
        r0 = pl.multiple_of(c * chunk, chunk)
        win = pad_ref[pl.ds(r0 + 8, chunk + 16), :]
        y = win[6:6 + chunk] * w[0:1]
        for j in range(1, DN_CONV):
            y = y + win[6 + j:6 + j + chunk] * w[j:j + 1]
        y = y * _sigmoid(y)
        ss = jnp.sum(y * y, axis=-1, keepdims=True)
        yn = y * (lax.rsqrt(ss + 1e-6) * scale)
        o_ref[pl.ds(r0, chunk), :] = jnp.where(is_v, y, yn).astype(o_ref.dtype)
        return carry

    lax.fori_loop(0, n_chunks, body, 0)


def dn_prep(proj, conv_w, chunk=512):
    s = proj.shape[0]
    nb = 3 * DN_WIDTH // LANES
    w = jnp.zeros((8, 3 * DN_WIDTH), F32).at[:DN_CONV].set(conv_w)
    return pl.pallas_call(
        functools.partial(_dnprep_body, seq=s, chunk=chunk),
        grid=(nb,),
        in_specs=[
            pl.BlockSpec((s, LANES), lambda c: (0, c)),
            pl.BlockSpec((8, LANES), lambda c: (0, c)),
        ],
        out_specs=pl.BlockSpec((s, LANES), lambda c: (0, c)),
        out_shape=jax.ShapeDtypeStruct((s, 3 * DN_WIDTH), BF16),
        scratch_shapes=[pltpu.VMEM((s + 32, LANES), F32)],
        compiler_params=_params(("parallel",)),
        name="dn_prep",
    )(proj, w)


def _gates_body(s_ref, par_ref, col_ref, row_ref):
    x = s_ref[...]
    t = x.shape[0]
    lane = lax.broadcasted_iota(I32, x.shape, 1)
    beta = _sigmoid(x)
    z = x + par_ref[1:2, :]
    softplus = jnp.maximum(z, 0.0) + jnp.log(1.0 + jnp.exp(-jnp.abs(z)))
    g = par_ref[0:1, :] * softplus
    ri = lax.broadcasted_iota(I32, (t, t), 0)
    ci = lax.broadcasted_iota(I32, (t, t), 1)
    lower = jnp.where(ci <= ri, 1.0, 0.0).astype(F32)
    upper = jnp.where(ci >= ri, 1.0, 0.0).astype(F32)
    hi = lax.Precision.HIGHEST
    g_prefix = jnp.dot(lower, g, precision=hi, preferred_element_type=F32)
    g_suffix = jnp.dot(upper, g, precision=hi, preferred_element_type=F32)
    cum = jnp.where(lane < 16 + DN_HEADS, g_prefix, g_suffix)
    out = jnp.where(lane < 16, beta, jnp.where(lane < 32, cum, 0.0))
    col_ref[...] = out
    row_ref[...] = out.T


def dn_gates(small, a_log, dt_bias):
    s = small.shape[0]
    par = jnp.zeros((8, LANES), F32)
    par = par.at[0, 16:32].set(-jnp.exp(a_log.reshape(-1).astype(F32)))
    par = par.at[1, 16:32].set(dt_bias.reshape(-1).astype(F32))
    t = DN_TILE
    return pl.pallas_call(
        _gates_body,
        grid=(s // t,),
        in_specs=[
            pl.BlockSpec((t, LANES), lambda i: (i, 0)),
            pl.BlockSpec((8, LANES), lambda i: (0, 0)),
        ],
        out_specs=[
            pl.BlockSpec((t, LANES), lambda i: (i, 0)),
            pl.BlockSpec((LANES, t), lambda i: (0, i)),
        ],
        out_shape=[jax.ShapeDtypeStruct((s, LANES), F32), jax.ShapeDtypeStruct((LANES, s), F32)],
        compiler_params=_params(("parallel",)),
        name="dn_gates",
    )(small, par)


def _dnchunk_body(q_ref, k_ref, v_ref, col_ref, row_ref, u_ref, w_ref, qd_ref, kt_ref, qk_ref, *, heads_per_step):
    c = DN_TILE
    hd = DN_HEAD_DIM
    head0 = pl.program_id(1) * heads_per_step
    col = col_ref[...]
    row = row_ref[...]
    lane = lax.broadcasted_iota(I32, col.shape, 1)
    sub = lax.broadcasted_iota(I32, row.shape, 0)

    def col_pick(idx):
        return jnp.sum(jnp.where(lane == idx, col, 0.0), axis=1, keepdims=True)

    def row_pick(idx):
        return jnp.sum(jnp.where(sub == idx, row, 0.0), axis=0, keepdims=True)

    ri = lax.broadcasted_iota(I32, (c, c), 0)
    ci = lax.broadcasted_iota(I32, (c, c), 1)
    same_block = (ri // DN_BLOCK) == (ci // DN_BLOCK)
    incl = (ri >= ci, ri <= ci)
    strict = (ri > ci, ri < ci)
    heads = range(heads_per_step)
    chains = [(hh, d) for hh in heads for d in range(2)]
    sls = [slice(hh * hd, (hh + 1) * hd) for hh in heads]
    qs = [q_ref[:, sl] for sl in sls]
    ks = [k_ref[:, sl] for sl in sls]
    vs = [v_ref[:, sl] for sl in sls]
    grams = [_dot_nt(k, k) for k in ks]
    qks = [_dot_nt(q, k) for q, k in zip(qs, ks)]
    qfs = [q.astype(F32) for q in qs]
    kfs = [k.astype(F32) for k in ks]
    vfs = [v.astype(F32) for v in vs]

    betas = [col_pick(d * DN_HEADS + head0 + hh) for hh, d in chains]
    g_cols = [col_pick(16 + d * DN_HEADS + head0 + hh) for hh, d in chains]
    g_rows = [row_pick(16 + d * DN_HEADS + head0 + hh) for hh, d in chains]
    totals = [gr[:, c - 1:c] if d == 0 else gr[:, 0:1] for (hh, d), gr in zip(chains, g_rows)]
    decays = [jnp.where(incl[d], jnp.exp(jnp.minimum(gc - gr, 0.0)), 0.0)
              for (hh, d), gc, gr in zip(chains, g_cols, g_rows)]
    lows = [jnp.where(strict[d], b * grams[hh] * dec, 0.0) for (hh, d), b, dec in zip(chains, betas, decays)]
    l_diags = [jnp.where(same_block, low, 0.0) for low in lows]
    l_offs = [(low - ld).astype(BF16) for low, ld in zip(lows, l_diags)]

    def neumann(accs, x_bs, n_steps):
        for _ in range(n_steps):
            x2s = [_dot(x, x) for x in x_bs]
            x_bs = [x2.astype(BF16) for x2 in x2s]
            accs = [a + x2 + _dot(a.astype(BF16), xb) for a, x2, xb in zip(accs, x2s, x_bs)]
        return accs

    d_ms = neumann([-ld for ld in l_diags], [ld.astype(BF16) for ld in l_diags], (DN_BLOCK - 1).bit_length() - 1)
    d_bs = [dm.astype(BF16) for dm in d_ms]
    ms = [lo.astype(F32) + _dot(db, lo) for db, lo in zip(d_bs, l_offs)]
    q_ms = neumann([-m for m in ms], [m.astype(BF16) for m in ms], (c // DN_BLOCK - 1).bit_length() - 1)
    e_cols = [jnp.exp(gc) for gc in g_cols]
    rhss = [jnp.concatenate([vfs[hh] * b, kfs[hh] * b * ec], axis=1) for (hh, d), b, ec in zip(chains, betas, e_cols)]
    r1s = [rhs + _dot(db, rhs.astype(BF16)) for rhs, db in zip(rhss, d_bs)]
    sols = [r1 + _dot(qm.astype(BF16), r1.astype(BF16)) for r1, qm in zip(r1s, q_ms)]
    for i, (hh, d) in enumerate(chains):
        sl = sls[hh]
        u_ref[d, :, sl] = sols[i][:, :hd]
        w_ref[d, :, sl] = sols[i][:, hd:].astype(BF16)
        qd_ref[d, :, sl] = (qfs[hh] * e_cols[i]).astype(BF16)
        kt_ref[d, sl, :] = (kfs[hh] * jnp.exp(totals[i] - g_cols[i])).T.astype(BF16)
        qk_ref[d, hh] = (qks[hh] * decays[i]).astype(BF16)


def dn_chunk(qkv, col, row, heads_per_step=4):
    s = qkv.shape[0]
    c = DN_TILE
    nt = s // c
    hd = DN_HEAD_DIM * heads_per_step
    groups = DN_HEADS // heads_per_step
    return pl.pallas_call(
        functools.partial(_dnchunk_body, heads_per_step=heads_per_step),
        grid=(nt, groups),
        in_specs=[
            pl.BlockSpec((c, hd), lambda t, h: (t, h)),
            pl.BlockSpec((c, hd), lambda t, h: (t, groups + h)),
            pl.BlockSpec((c, hd), lambda t, h: (t, 2 * groups + h)),
            pl.BlockSpec((c, LANES), lambda t, h: (t, 0)),
            pl.BlockSpec((LANES, c), lambda t, h: (0, t)),
        ],
        out_specs=[
            pl.BlockSpec((2, c, hd), lambda t, h: (0, t, h)),
            pl.BlockSpec((2, c, hd), lambda t, h: (0, t, h)),
            pl.BlockSpec((2, c, hd), lambda t, h: (0, t, h)),
            pl.BlockSpec((2, hd, c), lambda t, h: (0, h, t)),
            pl.BlockSpec((2, heads_per_step, c, c), lambda t, h: (0, h, t, 0)),
        ],
        out_shape=[
            jax.ShapeDtypeStruct((2, s, DN_WIDTH), F32),
            jax.ShapeDtypeStruct((2, s, DN_WIDTH), BF16),
            jax.ShapeDtypeStruct((2, s, DN_WIDTH), BF16),
            jax.ShapeDtypeStruct((2, DN_WIDTH, s), BF16),
            jax.ShapeDtypeStruct((2, DN_HEADS, s, c), BF16),
        ],
        compiler_params=_params(("parallel", "parallel")),
        name="dn_chunk",
    )(qkv, qkv, qkv, col, row)


def _dnscan_body(uf, wf, qdf, ktf, qkf, colf, ub, wb, qdb, ktb, qkb, colb, of_ref, ob_ref, st_ref):
    @pl.when(pl.program_id(0) == 0)
    def _():
        st_ref[...] = jnp.zeros(st_ref.shape, F32)

    c = DN_TILE
    hd = DN_HEAD_DIM
    dirs = ((uf, wf, qdf, ktf, qkf, colf, of_ref, c - 1), (ub, wb, qdb, ktb, qkb, colb, ob_ref, 0))
    chains = [(d, h) for d in range(2) for h in range(DN_HEADS)]
    sls = [slice(h * hd, (h + 1) * hd) for h in range(DN_HEADS)]
    e_tots = [jnp.exp(dirs[d][5][dirs[d][7]:dirs[d][7] + 1, :]) for d in range(2)]
    states = [st_ref[d * DN_HEADS + h] for d, h in chains]
    states_b = [st.astype(BF16) for st in states]
    v_news = [dirs[d][0][0, :, sls[h]] - _dot(dirs[d][1][0, :, sls[h]], sb) for (d, h), sb in zip(chains, states_b)]
    v_news_b = [vn.astype(BF16) for vn in v_news]
    outs = [_dot(dirs[d][2][0, :, sls[h]], sb) + _dot(dirs[d][4][0, h], vb)
            for (d, h), sb, vb in zip(chains, states_b, v_news_b)]
    news = [st * e_tots[d][:, 16 + d * DN_HEADS + h:17 + d * DN_HEADS + h] + _dot(dirs[d][3][0, sls[h], :], vb)
            for (d, h), st, vb in zip(chains, states, v_news_b)]
    for (d, h), out, new in zip(chains, outs, news):
        dirs[d][6][:, sls[h]] = out
        st_ref[d * DN_HEADS + h] = new


def dn_scan(u, w, qd, kt, qk, col):
    s = u.shape[1]
    c = DN_TILE
    nt = s // c
    wd = DN_WIDTH

    def specs(d):
        tile = (lambda t: t) if d == 0 else (lambda t: nt - 1 - t)
        return [
            pl.BlockSpec((1, c, wd), lambda t: (d, tile(t), 0)),
            pl.BlockSpec((1, c, wd), lambda t: (d, tile(t), 0)),
            pl.BlockSpec((1, c, wd), lambda t: (d, tile(t), 0)),
            pl.BlockSpec((1, wd, c), lambda t: (d, 0, tile(t))),
            pl.BlockSpec((1, DN_HEADS, c, c), lambda t: (d, 0, tile(t), 0)),
            pl.BlockSpec((c, LANES), lambda t: (tile(t), 0)),
        ]

    return pl.pallas_call(
        _dnscan_body,
        grid=(nt,),
        in_specs=specs(0) + specs(1),
        out_specs=[
            pl.BlockSpec((c, wd), lambda t: (t, 0)),
            pl.BlockSpec((c, wd), lambda t: (nt - 1 - t, 0)),
        ],
        out_shape=[jax.ShapeDtypeStruct((s, wd), F32), jax.ShapeDtypeStruct((s, wd), F32)],
        scratch_shapes=[pltpu.VMEM((2 * DN_HEADS, DN_HEAD_DIM, DN_HEAD_DIM), F32)],
        compiler_params=_params(("arbitrary",)),
        name="dn_scan",
    )(u, w, qd, kt, qk, col, u, w, qd, kt, qk, col)


def _na_bias_table(rpb):
    c = np.arange(GRID_W)
    kc = np.arange(GRID_W)
    cs = np.clip(c - NA_COLS // 2, 0, GRID_W - NA_COLS)
    valid = (kc[None, :] >= cs[:, None]) & (kc[None, :] < cs[:, None] + NA_COLS)
    dc = kc[None, :] - c[:, None] + (NA_COLS - 1)
    rpb = rpb.astype(F32)
    blocks = jnp.full(rpb.shape[:2] + (GRID_W, GRID_W), NEG_BIG, F32)
    for j in range(2 * NA_COLS - 1):
        blocks = jnp.where(valid & (dc == j), rpb[:, :, j][:, :, None, None], blocks)
    return jnp.concatenate([blocks[:, :-1], blocks[:, 1:]], axis=-1)


def _na_body(q_ref, *refs):
    k_refs = refs[:NA_ROWS]
    v_refs = refs[NA_ROWS:2 * NA_ROWS]
    bias_ref = refs[2 * NA_ROWS]
    o_ref = refs[2 * NA_ROWS + 1]
    lane = lax.broadcasted_iota(I32, (1, LANES), 1)
    heads_per_block = LANES // NA_HEAD_DIM
    scale = NA_HEAD_DIM ** -0.5
    n_pairs = NA_WIDTH // LANES
    heads = [(pair, hh) for pair in range(n_pairs) for hh in range(heads_per_block)]
    sls = [slice(pair * LANES, (pair + 1) * LANES) for pair in range(n_pairs)]
    owns = [(lane // NA_HEAD_DIM) == hh for hh in range(heads_per_block)]
    row_id = pl.program_id(0)
    first_offset = jnp.clip(row_id - NA_ROWS // 2, 0, pl.num_programs(0) - NA_ROWS) - row_id + (NA_ROWS - 1)

    def bias(h):
        return jnp.concatenate([bias_ref[h, first_offset + 2 * m] for m in range(NA_ROWS // 2)], axis=1)

    q2s = [q_ref[:, sl] for sl in sls]
    k2s = [jnp.concatenate([r[:, sl] for r in k_refs], axis=0) for sl in sls]
    scores = [_dot_nt(jnp.where(owns[hh], q2s[pair], jnp.zeros_like(q2s[pair])), k2s[pair]) * scale
              + bias(pair * heads_per_block + hh) for pair, hh in heads]
    maxes = [jnp.max(s, axis=-1, keepdims=True) for s in scores]
    probs = [jnp.exp(s - m) for s, m in zip(scores, maxes)]
    denoms = [jnp.sum(p, axis=-1, keepdims=True) for p in probs]
    v2s = [jnp.concatenate([r[:, sl] for r in v_refs], axis=0) for sl in sls]
    outs = [_dot(p.astype(BF16), jnp.where(owns[hh], v2s[pair], jnp.zeros_like(v2s[pair]))) / den
            for (pair, hh), p, den in zip(heads, probs, denoms)]
    for pair in range(n_pairs):
        acc = outs[pair * heads_per_block]
        for hh in range(1, heads_per_block):
            acc = acc + outs[pair * heads_per_block + hh]
        o_ref[:, sls[pair]] = acc.astype(o_ref.dtype)


def na_attention(proj, col0, rpb):
    s = proj.shape[0]
    rows = s // GRID_W
    assert rows >= NA_ROWS
    qb = col0 // NA_WIDTH
    table = _na_bias_table(rpb)

    def first_row(r):
        return jnp.clip(r - NA_ROWS // 2, 0, rows - NA_ROWS)

    def kv_spec(i, blk):
        return pl.BlockSpec((GRID_W, NA_WIDTH), lambda r: (first_row(r) + i, blk))

    in_specs = [pl.BlockSpec((GRID_W, NA_WIDTH), lambda r: (r, qb))]
    in_specs += [kv_spec(i, qb + 1) for i in range(NA_ROWS)]
    in_specs += [kv_spec(i, qb + 2) for i in range(NA_ROWS)]
    in_specs += [pl.BlockSpec(table.shape, lambda r: (0, 0, 0, 0), pipeline_mode=pl.Buffered(1))]
    return pl.pallas_call(
        _na_body,
        grid=(rows,),
        in_specs=in_specs,
        out_specs=pl.BlockSpec((GRID_W, NA_WIDTH), lambda r: (r, 0)),
        out_shape=jax.ShapeDtypeStruct((s, NA_WIDTH), BF16),
        compiler_params=_params(("parallel",)),
        name="na_attn",
    )(proj, *([proj] * (2 * NA_ROWS)), table)


def _pack_bf16_pair(lo, hi):
    lo_bits = pltpu.bitcast(lo.astype(BF16).astype(F32), U32)
    hi_bits = pltpu.bitcast(hi.astype(BF16).astype(F32), U32)
    return (lo_bits >> 16) | (hi_bits & jnp.uint32(0xFFFF0000))


def _unpack_bf16_pair(packed):
    lo = pltpu.bitcast(packed << 16, F32)
    hi = pltpu.bitcast(packed & jnp.uint32(0xFFFF0000), F32)
    return lo, hi


def _merge_body(of_ref, ob_ref, z_ref, na_ref, ga_ref, gb_ref, x_ref, dng_ref, wa_ref, wb_ref, wo_ref,
                gffn_ref, wr_ref, br_ref, x1_ref, h2p_ref, lg_ref, dn_ref):
    hd = DN_HEAD_DIM
    for h in range(DN_HEADS):
        sl = slice(h * hd, (h + 1) * hd)
        o = of_ref[:, sl] + ob_ref[:, sl]
        r = lax.rsqrt(jnp.mean(o * o, axis=-1, keepdims=True) + RMS_EPS)
        z = z_ref[:, sl].astype(F32)
        dn_ref[:, sl] = (o * r * dng_ref[...] * (z * _sigmoid(z))).astype(BF16)
    y_a = _dot(dn_ref[...], wa_ref[...])
    y_b = _dot(na_ref[...], wb_ref[...])
    mixed = _sigmoid(ga_ref[...].astype(F32)) * y_a + _sigmoid(gb_ref[...].astype(F32)) * y_b
    x1 = x_ref[...] + _dot(mixed.astype(BF16), wo_ref[...])
    x1_ref[...] = x1
    r = lax.rsqrt(jnp.mean(x1 * x1, axis=-1, keepdims=True) + RMS_EPS)
    h2 = x1 * r * gffn_ref[...]
    h2_hi = h2.astype(BF16)
    h2_lo = (h2 - h2_hi.astype(F32)).astype(BF16)
    lg_ref[...] = (_dot(h2_hi, wr_ref[0]) + _dot(h2_lo, wr_ref[0]) + _dot(h2_hi, wr_ref[1])) + br_ref[...]
    half = h2.shape[1] // 2
    n_tiles = half // LANES
    packed = _pack_bf16_pair(h2[:, :half], h2[:, half:])
    for c in range(n_tiles):
        h2p_ref[pl.ds(c, h2.shape[0], stride=n_tiles), :] = packed[:, c * LANES:(c + 1) * LANES]


def merge(o_f, o_b, proj, z_blk, gate_blk, na_out, x, dn_norm_g, w_a, w_b, w_o, norm_ffn_g, w_router, b_router,
          tm=256):
    s, d = x.shape
    ne = w_router.shape[1]
    const = lambda i: (0, 0)
    single = pl.Buffered(1)
    w_router_hi = w_router.astype(BF16)
    w_router_lo = (w_router.astype(F32) - w_router_hi.astype(F32)).astype(BF16)
    w_router_split = jnp.stack([w_router_hi, w_router_lo])
    return pl.pallas_call(
        _merge_body,
        grid=(s // tm,),
        in_specs=[
            pl.BlockSpec((tm, DN_WIDTH), lambda i: (i, 0)),
            pl.BlockSpec((tm, DN_WIDTH), lambda i: (i, 0)),
            pl.BlockSpec((tm, DN_WIDTH), lambda i: (i, z_blk)),
            pl.BlockSpec((tm, NA_WIDTH), lambda i: (i, 0)),
            pl.BlockSpec((tm, d), lambda i: (i, gate_blk)),
            pl.BlockSpec((tm, d), lambda i: (i, gate_blk + 1)),
            pl.BlockSpec((tm, d), lambda i: (i, 0)),
            pl.BlockSpec((1, DN_HEAD_DIM), const),
            pl.BlockSpec((DN_WIDTH, d), const, pipeline_mode=single),
            pl.BlockSpec((NA_WIDTH, d), const, pipeline_mode=single),
            pl.BlockSpec((d, d), const, pipeline_mode=single),
            pl.BlockSpec((1, d), const),
            pl.BlockSpec((2, d, ne), lambda i: (0, 0, 0)),
            pl.BlockSpec((1, ne), const),
        ],
        out_specs=[
            pl.BlockSpec((tm, d), lambda i: (i, 0)),
            pl.BlockSpec((tm * (d // 2 // LANES), LANES), lambda i: (i, 0)),
            pl.BlockSpec((tm, ne), lambda i: (i, 0)),
        ],
        out_shape=[
            jax.ShapeDtypeStruct((s, d), F32),
            jax.ShapeDtypeStruct((s * (d // 2 // LANES), LANES), U32),
            jax.ShapeDtypeStruct((s, ne), F32),
        ],
        scratch_shapes=[pltpu.VMEM((tm, DN_WIDTH), BF16)],
        compiler_params=_params(("parallel",)),
        name="merge",
    )(o_f, o_b, proj, na_out, proj, proj, x, dn_norm_g.reshape(1, -1).astype(F32), w_a, w_b, w_o,
      norm_ffn_g.reshape(1, d).astype(F32), w_router_split, b_router.reshape(1, ne).astype(F32))


def _route_body(lg_ref, ti_ref, tw_ref, cnt_ref, carry_ref):
    @pl.when(pl.program_id(0) == 0)
    def _():
        carry_ref[...] = jnp.zeros(carry_ref.shape, F32)

    lg = lg_ref[...]
    tm, ne = lg.shape
    lane = lax.broadcasted_iota(I32, (tm, ne), 1).astype(F32)
    work = lg
    vals, idxs = [], []
    onehot = jnp.zeros((tm, ne), F32)
    for _ in range(TOP_K):
        m = jnp.max(work, axis=-1, keepdims=True)
        idx = jnp.min(jnp.where(work == m, lane, float(ne)), axis=-1, keepdims=True)
        hit = lane == idx
        vals.append(m)
        idxs.append(idx)
        onehot = onehot + jnp.where(hit, 1.0, 0.0)
        work = jnp.where(hit, -jnp.inf, work)
    exps = [jnp.exp(v - vals[0]) for v in vals]
    denom = exps[0] + exps[1] + exps[2] + exps[3]
    ri = lax.broadcasted_iota(I32, (tm, tm), 0)
    ci = lax.broadcasted_iota(I32, (tm, tm), 1)
    strict = jnp.where(ci < ri, 1.0, 0.0).astype(BF16)
    before = _dot(strict, onehot.astype(BF16)) + carry_ref[0:1, 0:ne]
    lane_o = lax.broadcasted_iota(I32, (tm, LANES), 1)
    ti = jnp.zeros((tm, LANES), I32)
    tw = jnp.zeros((tm, LANES), F32)
    for kk in range(TOP_K):
        rank = jnp.sum(jnp.where(lane == idxs[kk], before, 0.0), axis=-1, keepdims=True).astype(I32)
        ti = jnp.where(lane_o == kk, idxs[kk].astype(I32), ti)
        ti = jnp.where(lane_o == TOP_K + kk, rank, ti)
        tw = jnp.where(lane_o == kk, exps[kk] / denom, tw)
    ti_ref[...] = ti
    tw_ref[...] = tw
    total = carry_ref[0:1, 0:ne] + jnp.sum(onehot, axis=0, keepdims=True)
    carry_ref[0:1, 0:ne] = total
    cnt_ref[...] = jnp.zeros(cnt_ref.shape, F32)
    cnt_ref[0:1, 0:ne] = total


def route(logits, tm=512):
    s, ne = logits.shape
    return pl.pallas_call(
        _route_body,
        grid=(s // tm,),
        in_specs=[pl.BlockSpec((tm, ne), lambda i: (i, 0))],
        out_specs=[
            pl.BlockSpec((tm, LANES), lambda i: (i, 0)),
            pl.BlockSpec((tm, LANES), lambda i: (i, 0)),
            pl.BlockSpec((8, LANES), lambda i: (0, 0)),
        ],
        out_shape=[
            jax.ShapeDtypeStruct((s, LANES), I32),
            jax.ShapeDtypeStruct((s, LANES), F32),
            jax.ShapeDtypeStruct((8, LANES), F32),
        ],
        scratch_shapes=[pltpu.VMEM((8, LANES), F32)],
        compiler_params=_params(("arbitrary",)),
        name="route",
    )(logits)


def _dest_body(ti_ref, ps_ref, d_ref):
    ti = ti_ref[...].astype(F32)
    tm = ti.shape[0]
    lane = lax.broadcasted_iota(I32, (tm, LANES), 1)
    lane_f = lane.astype(F32)
    ps = ps_ref[0:1, :].astype(F32)
    out = jnp.zeros((tm, LANES), F32)
    for kk in range(TOP_K):
        e = jnp.sum(jnp.where(lane == kk, ti, 0.0), axis=-1, keepdims=True)
        rank = jnp.sum(jnp.where(lane == TOP_K + kk, ti, 0.0), axis=-1, keepdims=True)
        start = jnp.sum(jnp.where(lane_f == e, ps, 0.0), axis=-1, keepdims=True)
        out = jnp.where(lane == kk, start + rank, out)
    d_ref[...] = out.astype(I32)


def route_dest(ti, pad_start, tm=512):
    s = ti.shape[0]
    ps = jnp.zeros((8, LANES), I32).at[0, :N_EXPERTS].set(pad_start)
    return pl.pallas_call(
        _dest_body,
        grid=(s // tm,),
        in_specs=[pl.BlockSpec((tm, LANES), lambda i: (i, 0)), pl.BlockSpec((8, LANES), lambda i: (0, 0))],
        out_specs=pl.BlockSpec((tm, LANES), lambda i: (i, 0)),
        out_shape=jax.ShapeDtypeStruct((s, LANES), I32),
        compiler_params=_params(("parallel",)),
        name="route_dest",
    )(ti, ps)


def _scatter_body(pend_ref, padded_ref, dest_ref, h_ref, xr_ref, zero_ref, sem):
    tm = h_ref.shape[0]

    def zero_copy(e):
        return pltpu.make_async_copy(zero_ref, xr_ref.at[pl.ds(pend_ref[e] - MOE_SUB, MOE_SUB)], sem)

    @pl.when(pl.program_id(0) == 0)
    def _():
        zero_ref[...] = jnp.zeros(zero_ref.shape, U32)

        def start(e, carry):
            @pl.when(padded_ref[e] > 0)
            def _():
                zero_copy(e).start()
            return carry

        def wait(e, carry):
            @pl.when(padded_ref[e] > 0)
            def _():
                zero_copy(e).wait()
            return carry

        lax.fori_loop(0, N_EXPERTS, start, 0)
        lax.fori_loop(0, N_EXPERTS, wait, 0)

        def slack_copy(b):
            return pltpu.make_async_copy(zero_ref, xr_ref.at[pl.ds(b * MOE_SUB, MOE_SUB)], sem)

        def slack_start(b, carry):
            slack_copy(b).start()
            return carry

        def slack_wait(b, carry):
            slack_copy(b).wait()
            return carry

        first_slack = pend_ref[N_EXPERTS - 1] // MOE_SUB
        lax.fori_loop(first_slack, xr_ref.shape[0] // MOE_SUB, slack_start, 0)
        lax.fori_loop(first_slack, xr_ref.shape[0] // MOE_SUB, slack_wait, 0)

    def row_copy(r, kk):
        d = dest_ref[r * TOP_K + kk]
        return pltpu.make_async_copy(h_ref.at[r], xr_ref.at[d], sem)

    def start(r, carry):
        for kk in range(TOP_K):
            row_copy(r, kk).start()
        return carry

    def wait(r, carry):
        for kk in range(TOP_K):
            row_copy(r, kk).wait()
        return carry

    lax.fori_loop(0, tm, start, 0)
    lax.fori_loop(0, tm, wait, 0)


def moe_scatter(h2p, dest, pad_end, padded, n_rows, tm=256):
    s, nt, _ = h2p.shape
    grid_spec = pltpu.PrefetchScalarGridSpec(
        num_scalar_prefetch=2,
        grid=(s // tm,),
        in_specs=[
            pl.BlockSpec((tm * TOP_K,), lambda i, *_: (i,), memory_space=pltpu.SMEM),
            pl.BlockSpec((tm, nt, LANES), lambda i, *_: (i, 0, 0)),
        ],
        out_specs=pl.BlockSpec(memory_space=pl.ANY),
        scratch_shapes=[pltpu.VMEM((MOE_SUB, nt, LANES), U32), pltpu.SemaphoreType.DMA(())],
    )
    return pl.pallas_call(
        _scatter_body,
        grid_spec=grid_spec,
        out_shape=jax.ShapeDtypeStruct((n_rows, nt, LANES), U32),
        compiler_params=_params(("arbitrary",)),
        name="moe_scatter",
    )(pad_end, padded, dest, h2p)


def _moe_body(ie_ref, ir_ref, inb_ref, ni_ref, xr_ref, wg_ref, wu_ref, wd_ref, bg_ref, bu_ref, bd_ref, y_ref,
              stage_ref, ystage_ref, xb_ref, acc_ref, wgb_ref, wub_ref, wdb_ref, sem_in, sem_out, *, n_chunks):
    w = pl.program_id(0)
    j = pl.program_id(1)
    sb = MOE_SUB
    d = acc_ref.shape[1]
    half = d // 2
    nx = half // LANES
    ny = half // LANES

    @pl.when(w < ni_ref[0])
    def _():
        nb = inb_ref[w]
        r0 = ir_ref[w]

        @pl.when(j == 0)
        def _():
            @pl.when(w == 0)
            def _():
                ystage_ref[0] = jnp.zeros(ystage_ref.shape[1:], U32)

                def slack_copy(b):
                    dst = y_ref.at[pl.ds(pl.multiple_of(b * (sb * ny), sb * ny), sb * ny), :]
                    return pltpu.make_async_copy(ystage_ref.at[0], dst, sem_out.at[0])

                def slack_start(b, carry):
                    slack_copy(b).start()
                    return carry

                def slack_wait(b, carry):
                    slack_copy(b).wait()
                    return carry

                n_blocks = y_ref.shape[0] // (sb * ny)
                lax.fori_loop(ni_ref[1], n_blocks, slack_start, 0)
                lax.fori_loop(ni_ref[1], n_blocks, slack_wait, 0)

            def in_copy(i, slot):
                src = xr_ref.at[pl.ds(pl.multiple_of((r0 + i * sb) * nx, sb * nx), sb * nx), :]
                return pltpu.make_async_copy(src, stage_ref.at[slot], sem_in.at[slot])

            in_copy(0, 0).start()

            def load(i, carry):
                slot = i % 2

                @pl.when(i + 1 < nb)
                def _():
                    in_copy(i + 1, 1 - slot).start()

                in_copy(i, slot).wait()
                rows = pl.ds(pl.multiple_of(i * sb, sb), sb)
                for c in range(nx):
                    lo, hi = _unpack_bf16_pair(stage_ref[slot, pl.ds(c, sb, stride=nx), :])
                    xb_ref[rows, c * LANES:(c + 1) * LANES] = lo.astype(BF16)
                    xb_ref[rows, half + c * LANES:half + (c + 1) * LANES] = hi.astype(BF16)
                return carry

            lax.fori_loop(0, nb, load, 0)

        wgb_ref[...] = wg_ref[0].astype(BF16)
        wub_ref[...] = wu_ref[0].astype(BF16)
        wdb_ref[...] = wd_ref[0].astype(BF16)

        def block(first, n_sub, i, carry):
            rows = pl.ds(pl.multiple_of(i * sb, sb), n_sub * sb)
            xs = xb_ref[rows, :]
            gate = jnp.minimum(_dot(xs, wgb_ref[...]) + bg_ref[0], SWIGLU_LIMIT)
            up = jnp.clip(_dot(xs, wub_ref[...]) + bu_ref[0], -SWIGLU_LIMIT, SWIGLU_LIMIT)
            act = (up + 1.0) * (gate * _sigmoid(SWIGLU_ALPHA * gate))
            contrib = _dot(act.astype(BF16), wdb_ref[...])
            if first:
                acc_ref[rows, :] = contrib
            else:
                acc_ref[rows, :] += contrib
            return carry

        def all_blocks(first):
            def pair(i2, carry):
                return block(first, 2, 2 * i2, carry)

            lax.fori_loop(0, nb // 2, pair, 0)

            @pl.when(nb % 2 == 1)
            def _():
                block(first, 1, nb - 1, 0)

        @pl.when(j == 0)
        def _():
            all_blocks(True)

        @pl.when(j > 0)
        def _():
            all_blocks(False)

        @pl.when(j == n_chunks - 1)
        def _():
            def out_copy(i, slot):
                dst = y_ref.at[pl.ds(pl.multiple_of((r0 + i * sb) * ny, sb * ny), sb * ny), :]
                return pltpu.make_async_copy(ystage_ref.at[slot], dst, sem_out.at[slot])

            def store(i, carry):
                slot = i % 2

                @pl.when(i >= 2)
                def _():
                    out_copy(i - 2, slot).wait()

                rows = pl.ds(pl.multiple_of(i * sb, sb), sb)
                for c in range(ny):
                    lo = slice(c * LANES, (c + 1) * LANES)
                    hi = slice(half + c * LANES, half + (c + 1) * LANES)
                    ystage_ref[slot, pl.ds(c, sb, stride=ny), :] = _pack_bf16_pair(
                        acc_ref[rows, lo] + bd_ref[0, :, lo], acc_ref[rows, hi] + bd_ref[0, :, hi])
                out_copy(i, slot).start()
                return carry

            lax.fori_loop(0, nb, store, 0)

            @pl.when(nb >= 2)
            def _():
                out_copy(nb - 2, nb % 2).wait()

            out_copy(nb - 1, (nb - 1) % 2).wait()


def moe_ffn(x_rows, w_gate_up, b_gate_up, w_down, b_down, item_e, item_row, item_nb, n_items, max_items):
    ne, d, two_de = w_gate_up.shape
    nx = d // 2 // LANES
    ny = d // 2 // LANES
    n_rows = x_rows.shape[0] // nx
    de = two_de // 2
    tn = MOE_TN
    n_chunks = de // tn
    last = n_chunks - 1

    def chunk(w, j, ni):
        return jnp.where(w < ni[0], j, last)

    grid_spec = pltpu.PrefetchScalarGridSpec(
        num_scalar_prefetch=4,
        grid=(max_items, n_chunks),
        in_specs=[
            pl.BlockSpec(memory_space=pl.ANY),
            pl.BlockSpec((1, d, tn), lambda w, j, ie, ir, inb, ni: (ie[w], 0, chunk(w, j, ni))),
            pl.BlockSpec((1, d, tn), lambda w, j, ie, ir, inb, ni: (ie[w], 0, n_chunks + chunk(w, j, ni))),
            pl.BlockSpec((1, tn, d), lambda w, j, ie, ir, inb, ni: (ie[w], chunk(w, j, ni), 0)),
            pl.BlockSpec((1, 1, tn), lambda w, j, ie, ir, inb, ni: (ie[w], 0, chunk(w, j, ni))),
            pl.BlockSpec((1, 1, tn), lambda w, j, ie, ir, inb, ni: (ie[w], 0, n_chunks + chunk(w, j, ni))),
            pl.BlockSpec((1, 1, d), lambda w, j, ie, ir, inb, ni: (ie[w], 0, 0)),
        ],
        out_specs=pl.BlockSpec(memory_space=pl.ANY),
        scratch_shapes=[
            pltpu.VMEM((2, MOE_SUB * nx, LANES), U32),
            pltpu.VMEM((2, MOE_SUB * ny, LANES), U32),
            pltpu.VMEM((MOE_TM, d), BF16),
            pltpu.VMEM((MOE_TM, d), F32),
            pltpu.VMEM((d, tn), BF16),
            pltpu.VMEM((d, tn), BF16),
            pltpu.VMEM((tn, d), BF16),
            pltpu.SemaphoreType.DMA((2,)),
            pltpu.SemaphoreType.DMA((2,)),
        ],
    )
    return pl.pallas_call(
        functools.partial(_moe_body, n_chunks=n_chunks),
        grid_spec=grid_spec,
        out_shape=jax.ShapeDtypeStruct((n_rows * ny, LANES), U32),
        compiler_params=_params(("arbitrary", "arbitrary")),
        name="moe_ffn",
    )(item_e, item_row, item_nb, n_items, x_rows, w_gate_up, w_gate_up, w_down,
      b_gate_up.reshape(ne, 1, two_de), b_gate_up.reshape(ne, 1, two_de), b_down.reshape(ne, 1, d))


def _final_body(dest_ref, dest_next_ref, x1_ref, tw_ref, p_ref, gple_ref, wg_ref, wp_ref, gfin_ref, y_ref, o_ref,
                ybuf_ref, sem, *, last_layer):
    i = pl.program_id(0)
    tm = x1_ref.shape[0]
    half = x1_ref.shape[1] // 2
    ny = half // LANES
    slot = i % 2

    def row_copy(d_ref, to_slot, r, kk):
        dst = ybuf_ref.at[to_slot, kk, pl.ds(pl.multiple_of(r * ny, ny), ny), :]
        return pltpu.make_async_copy(y_ref.at[d_ref[r * TOP_K + kk]], dst, sem.at[to_slot])

    def issue(d_ref, to_slot):
        def body(r, carry):
            for kk in range(TOP_K):
                row_copy(d_ref, to_slot, r, kk).start()
            return carry

        lax.fori_loop(0, tm, body, 0)

    def drain(d_ref, to_slot):
        def body(r, carry):
            for kk in range(TOP_K):
                row_copy(d_ref, to_slot, r, kk).wait()
            return carry

        lax.fori_loop(0, tm, body, 0)

    @pl.when(i == 0)
    def _():
        issue(dest_ref, 0)

    @pl.when(i + 1 < pl.num_programs(0))
    def _():
        issue(dest_next_ref, 1 - slot)

    drain(dest_ref, slot)

    tw = tw_ref[...]
    los, his = [], []
    for c in range(ny):
        acc_lo = x1_ref[:, c * LANES:(c + 1) * LANES]
        acc_hi = x1_ref[:, half + c * LANES:half + (c + 1) * LANES]
        for kk in range(TOP_K):
            lo, hi = _unpack_bf16_pair(ybuf_ref[slot, kk, pl.ds(c, tm, stride=ny), :])
            acc_lo = acc_lo + tw[:, kk:kk + 1] * lo
            acc_hi = acc_hi + tw[:, kk:kk + 1] * hi
        los.append(acc_lo)
        his.append(acc_hi)
    x2 = jnp.concatenate(los + his, axis=1)
    r = lax.rsqrt(jnp.mean(x2 * x2, axis=-1, keepdims=True) + RMS_EPS)
    n = (x2 * r * gple_ref[...]).astype(BF16)
    gate = _sigmoid(_dot(n, wg_ref[...]))
    x3 = x2 + gate * _dot(p_ref[...].astype(BF16), wp_ref[...])
    if last_layer:
        r = lax.rsqrt(jnp.mean(x3 * x3, axis=-1, keepdims=True) + RMS_EPS)
        x3 = x3 * r * gfin_ref[...]
    o_ref[...] = x3


def final(dest, x1, tw, p, norm_ple_g, w_gate, w_proj, norm_final_g, y_rows, last_layer, tm=256):
    s, d = x1.shape
    pd = p.shape[1]
    ny = d // 2 // LANES
    n_steps = s // tm
    const = lambda i: (0, 0)
    single = pl.Buffered(1)
    return pl.pallas_call(
        functools.partial(_final_body, last_layer=last_layer),
        grid=(n_steps,),
        in_specs=[
            pl.BlockSpec((tm * TOP_K,), lambda i: (i,), memory_space=pltpu.SMEM),
            pl.BlockSpec((tm * TOP_K,), lambda i: (jnp.minimum(i + 1, n_steps - 1),), memory_space=pltpu.SMEM),
            pl.BlockSpec((tm, d), lambda i: (i, 0)),
            pl.BlockSpec((tm, LANES), lambda i: (i, 0)),
            pl.BlockSpec((tm, pd), lambda i: (i, 0)),
            pl.BlockSpec((1, d), const),
            pl.BlockSpec((d, d), const, pipeline_mode=single),
            pl.BlockSpec((pd, d), const, pipeline_mode=single),
            pl.BlockSpec((1, d), const),
            pl.BlockSpec(memory_space=pl.ANY),
        ],
        out_specs=pl.BlockSpec((tm, d), lambda i: (i, 0)),
        out_shape=jax.ShapeDtypeStruct((s, d), F32),
        scratch_shapes=[pltpu.VMEM((2, TOP_K, tm * ny, LANES), U32), pltpu.SemaphoreType.DMA((2,))],
        compiler_params=_params(("arbitrary",)),
        name="final",
    )(dest, dest, x1, tw, p, norm_ple_g.reshape(1, d).astype(F32), w_gate, w_proj,
      norm_final_g.reshape(1, d).astype(F32), y_rows.reshape(-1, ny, LANES))


def _moe_tables(counts, n_rows):
    sub, tm = MOE_SUB, MOE_TM
    max_items = N_EXPERTS + n_rows // tm
    padded = (counts + sub - 1) // sub * sub
    pad_end = jnp.cumsum(padded)
    pad_start = pad_end - padded
    n_it = (padded + tm - 1) // tm
    it_end = jnp.cumsum(n_it)
    it_start = it_end - n_it
    n_items = it_end[-1]
    w = jnp.arange(max_items, dtype=I32)
    live = w < n_items
    w_eff = jnp.minimum(w, n_items - 1)
    e_w = jnp.minimum(jnp.searchsorted(it_end, w_eff, side="right"), N_EXPERTS - 1).astype(I32)
    m_w = w_eff - it_start[e_w]
    row_w = pad_start[e_w] + m_w * tm
    nb_w = jnp.clip((padded[e_w] - m_w * tm) // sub, 0, tm // sub)
    nb_w = jnp.where(live, nb_w, 0)
    as_i32 = lambda a: a.astype(I32)
    counts_w = jnp.stack([n_items, pad_end[-1] // sub])
    return (as_i32(pad_start), as_i32(pad_end), as_i32(padded), e_w, as_i32(row_w), as_i32(nb_w),
            as_i32(counts_w), max_items)


def _layer(x, p, norm_mix_g, w_in, dn_conv_w, dn_a_log, dn_dt_bias, dn_norm_g, na_rpb, w_branch_a, w_branch_b,
           w_out, norm_ffn_g, w_router, b_router, w_gate_up, b_gate_up, w_down, b_down, norm_ple_g,
           w_ple_gate, w_ple_proj, norm_final_g, last_layer):
    s, d = x.shape
    c_qkv, c_z = 3 * DN_WIDTH, DN_WIDTH
    w_main = w_prep(w_in)
    z_blk = c_qkv // DN_WIDTH
    gate_blk = (c_qkv + c_z) // d
    na_col0 = c_qkv + c_z + 2 * d

    proj, small = in_projection(x, norm_mix_g.astype(F32), w_main, w_in)

    qkv = dn_prep(proj, dn_conv_w.astype(F32))
    col, row = dn_gates(small, dn_a_log, dn_dt_bias)
    u, w, qd, kt, qk = dn_chunk(qkv, col, row)
    o_f, o_b = dn_scan(u, w, qd, kt, qk, col)

    na_out = na_attention(proj, na_col0, na_rpb)

    x1, h2p, logits = merge(o_f, o_b, proj, z_blk, gate_blk, na_out, x, dn_norm_g, w_branch_a.astype(BF16),
                            w_branch_b.astype(BF16), w_out.astype(BF16), norm_ffn_g, w_router, b_router)

    ti, tw, cnt = route(logits)
    counts = cnt[0, :N_EXPERTS].astype(I32)
    n_rows = (s * TOP_K + N_EXPERTS * (MOE_SUB - 1) + MOE_SUB - 1) // MOE_SUB * MOE_SUB
    pad_start, pad_end, padded, item_e, item_row, item_nb, n_items, max_items = _moe_tables(counts, n_rows)
    dest = route_dest(ti, pad_start)[:, :TOP_K].reshape(-1)

    nx = d // 2 // LANES
    x_rows = moe_scatter(h2p.reshape(s, nx, LANES), dest, pad_end, padded, n_rows).reshape(n_rows * nx, LANES)
    y_rows = moe_ffn(x_rows, w_gate_up, b_gate_up, w_down, b_down, item_e, item_row, item_nb, n_items, max_items)

    return final(dest, x1, tw, p, norm_ple_g, w_ple_gate.astype(BF16), w_ple_proj.astype(BF16), norm_final_g,
                 y_rows, last_layer)


def kernel(x, p, norm_mix_g, w_in, dn_conv_w, dn_a_log, dn_dt_bias, dn_norm_g, na_rpb, w_branch_a, w_branch_b, w_out, norm_ffn_g, w_router, b_router, w_gate_up, b_gate_up, w_down, b_down, norm_ple_g, w_ple_gate, w_ple_proj, norm_final_g):
    bsz, s, d = x.shape
    depth = w_in.shape[0]
    outs = []
    for b in range(bsz):
        xb = x[b]
        for i in range(depth):
            xb = _layer(xb, p[i, b], norm_mix_g[i], w_in[i], dn_conv_w[i], dn_a_log[i], dn_dt_bias[i], dn_norm_g[i],
                        na_rpb[i], w_branch_a[i], w_branch_b[i], w_out[i], norm_ffn_g[i], w_router[i], b_router[i],
                        w_gate_up[i], b_gate_up[i], w_down[i], b_down[i], norm_ple_g[i], w_ple_gate[i],
                        w_ple_proj[i], norm_final_g, i == depth - 1)
        outs.append(xb)
    return jnp.stack(outs, axis=0)
```

```python
import functools

import jax
import jax.numpy as jnp
import numpy as np
from jax import lax
from jax.experimental import pallas as pl
from jax.experimental.pallas import tpu as pltpu

F32 = jnp.float32
BF16 = jnp.bfloat16
I32 = jnp.int32
U32 = jnp.uint32

GRID_W = 64
DN_HEADS = 8
DN_HEAD_DIM = 128
DN_WIDTH = DN_HEADS * DN_HEAD_DIM
DN_CONV = 5
NA_HEADS = 16
NA_HEAD_DIM = 64
NA_WIDTH = NA_HEADS * NA_HEAD_DIM
NA_ROWS = 8
NA_COLS = 16
N_EXPERTS = 32
TOP_K = 4
SWIGLU_LIMIT = 7.0
SWIGLU_ALPHA = 1.702
RMS_EPS = 1e-6

LANES = 128
VMEM_LIMIT = 56 * 1024 * 1024

DN_TILE = 256
DN_BLOCK = 16
MOE_SUB = 256
MOE_TM = 2048
MOE_TN = 256
NEG_BIG = -1e30


def _sigmoid(x):
    return 1.0 / (1.0 + jnp.exp(-x))


def _dot(a, b):
    return jnp.dot(a, b, preferred_element_type=F32)


def _dot_nt(a, b):
    return lax.dot_general(a, b, (((1,), (1,)), ((), ())), preferred_element_type=F32)


def _params(sem, limit=VMEM_LIMIT):
    return pltpu.CompilerParams(dimension_semantics=sem, vmem_limit_bytes=limit)


def _wprep_body(src_ref, o_ref, buf_ref, sem, *, starts, tn):
    t = pl.program_id(0)
    slot = t % 2

    def tile_copy(tt, to_slot):
        start = jnp.int32(starts[0])
        for k in range(1, len(starts)):
            start = jnp.where(tt >= k, starts[k], start)
        src = src_ref.at[pl.ds(pl.multiple_of(start, 8), tn), :]
        return pltpu.make_async_copy(src, buf_ref.at[to_slot], sem.at[to_slot])

    @pl.when(t == 0)
    def _():
        tile_copy(0, 0).start()

    @pl.when(t + 1 < pl.num_programs(0))
    def _():
        tile_copy(t + 1, 1 - slot).start()

    tile_copy(t, slot).wait()
    o_ref[...] = buf_ref[slot].T.astype(o_ref.dtype)


def w_prep(w_t, tn=1024):
    n_in, d = w_t.shape
    c_small = 4 * DN_HEADS
    o_small = 4 * DN_WIDTH
    o_na = o_small + c_small
    o_gates = o_na + 3 * NA_WIDTH
    assert o_small % tn == 0 and (3 * NA_WIDTH) % tn == 0 and (n_in - o_gates) % tn == 0
    assert o_na % 8 == 0 and o_gates % 8 == 0
    starts = list(range(0, o_small, tn)) + list(range(o_gates, n_in, tn)) + list(range(o_na, o_gates, tn))
    return pl.pallas_call(
        functools.partial(_wprep_body, starts=tuple(starts), tn=tn),
        grid=(len(starts),),
        in_specs=[pl.BlockSpec(memory_space=pl.ANY)],
        out_specs=pl.BlockSpec((d, tn), lambda t: (0, t)),
        out_shape=jax.ShapeDtypeStruct((d, len(starts) * tn), BF16),
        scratch_shapes=[pltpu.VMEM((2, tn, d), F32), pltpu.SemaphoreType.DMA((2,))],
        compiler_params=_params(("arbitrary",)),
        name="w_prep",
    )(w_t)


def _inproj_body(x_ref, g_ref, w_ref, ws_ref, o_ref, os_ref, h_ref):
    @pl.when(pl.program_id(1) == 0)
    def _():
        x = x_ref[...]
        r = lax.rsqrt(jnp.mean(x * x, axis=-1, keepdims=True) + RMS_EPS)
        h = (x * r * g_ref[...]).astype(BF16)
        h_ref[...] = h
        os_ref[...] = _dot_nt(h, ws_ref[...].astype(BF16))

    o_ref[...] = _dot(h_ref[...], w_ref[...]).astype(o_ref.dtype)


def in_projection(x, g, w_main, w_t, tm=512, tn=1024):
    s, d = x.shape
    n = w_main.shape[1]
    small_blk = 4 * DN_WIDTH // LANES
    return pl.pallas_call(
        _inproj_body,
        grid=(s // tm, n // tn),
        in_specs=[
            pl.BlockSpec((tm, d), lambda i, j: (i, 0)),
            pl.BlockSpec((1, d), lambda i, j: (0, 0)),
            pl.BlockSpec((d, tn), lambda i, j: (0, j)),
            pl.BlockSpec((LANES, d), lambda i, j: (small_blk, 0)),
        ],
        out_specs=[
            pl.BlockSpec((tm, tn), lambda i, j: (i, j)),
            pl.BlockSpec((tm, LANES), lambda i, j: (i, 0)),
        ],
        out_shape=[jax.ShapeDtypeStruct((s, n), BF16), jax.ShapeDtypeStruct((s, LANES), F32)],
        scratch_shapes=[pltpu.VMEM((tm, d), BF16)],
        compiler_params=_params(("parallel", "arbitrary")),
        name="in_proj",
    )(x, g.reshape(1, d), w_main, w_t)


def _dnprep_body(x_ref, w_ref, o_ref, pad_ref, *, seq, chunk):
    cb = pl.program_id(0)
    n_chunks = seq // chunk
    zeros = jnp.zeros((16, LANES), F32)
    pad_ref[0:16, :] = zeros
    pad_ref[seq + 16:seq + 32, :] = zeros

    def fill(c, carry):
        r0 = pl.multiple_of(c * chunk, chunk)
        pad_ref[pl.ds(r0 + 16, chunk), :] = x_ref[pl.ds(r0, chunk), :].astype(F32)
        return carry

    lax.fori_loop(0, n_chunks, fill, 0)

    w = w_ref[...]
    is_v = cb >= 2 * DN_HEADS
    scale = jnp.where(cb < DN_HEADS, DN_HEAD_DIM ** -0.5, 1.0).astype(F32)

    def body(c, carry):
        r0 = pl.multiple_of(c * chunk, chunk)
        y = pad_ref[pl.ds(r0 + 14, chunk), :] * w[0:1]
        for j in range(1, DN_CONV):
            y = y + pad_ref[pl.ds(r0 + 14 + j, chunk), :] * w[j:j + 1]
        y = y * _sigmoid(y)
        ss = jnp.sum(y * y, axis=-1, keepdims=True)
        yn = y * (lax.rsqrt(ss + 1e-6) * scale)
        o_ref[pl.ds(r0, chunk), :] = jnp.where(is_v, y, yn).astype(o_ref.dtype)
        return carry

    lax.fori_loop(0, n_chunks, body, 0)


def dn_prep(proj, conv_w, chunk=512):
    s = proj.shape[0]
    nb = 3 * DN_WIDTH // LANES
    w = jnp.zeros((8, 3 * DN_WIDTH), F32).at[:DN_CONV].set(conv_w)
    return pl.pallas_call(
        functools.partial(_dnprep_body, seq=s, chunk=chunk),
        grid=(nb,),
        in_specs=[
            pl.BlockSpec((s, LANES), lambda c: (0, c)),
            pl.BlockSpec((8, LANES), lambda c: (0, c)),
        ],
        out_specs=pl.BlockSpec((s, LANES), lambda c: (0, c)),
        out_shape=jax.ShapeDtypeStruct((s, 3 * DN_WIDTH), BF16),
        scratch_shapes=[pltpu.VMEM((s + 32, LANES), F32)],
        compiler_params=_params(("parallel",)),
        name="dn_prep",
    )(proj, w)


def _gates_body(s_ref, par_ref, col_ref, row_ref):
    x = s_ref[...]
    t = x.shape[0]
    lane = lax.broadcasted_iota(I32, x.shape, 1)
    beta = _sigmoid(x)
    z = x + par_ref[1:2, :]
    softplus = jnp.maximum(z, 0.0) + jnp.log(1.0 + jnp.exp(-jnp.abs(z)))
    g = par_ref[0:1, :] * softplus
    ri = lax.broadcasted_iota(I32, (t, t), 0)
    ci = lax.broadcasted_iota(I32, (t, t), 1)
    lower = jnp.where(ci <= ri, 1.0, 0.0).astype(F32)
    upper = jnp.where(ci >= ri, 1.0, 0.0).astype(F32)
    hi = lax.Precision.HIGHEST
    g_prefix = jnp.dot(lower, g, precision=hi, preferred_element_type=F32)
    g_suffix = jnp.dot(upper, g, precision=hi, preferred_element_type=F32)
    cum = jnp.where(lane < 16 + DN_HEADS, g_prefix, g_suffix)
    out = jnp.where(lane < 16, beta, jnp.where(lane < 32, cum, 0.0))
    col_ref[...] = out
    row_ref[...] = out.T


def dn_gates(small, a_log, dt_bias):
    s = small.shape[0]
    par = jnp.zeros((8, LANES), F32)
    par = par.at[0, 16:32].set(-jnp.exp(a_log.reshape(-1).astype(F32)))
    par = par.at[1, 16:32].set(dt_bias.reshape(-1).astype(F32))
    t = DN_TILE
    return pl.pallas_call(
        _gates_body,
        grid=(s // t,),
        in_specs=[
            pl.BlockSpec((t, LANES), lambda i: (i, 0)),
            pl.BlockSpec((8, LANES), lambda i: (0, 0)),
        ],
        out_specs=[
            pl.BlockSpec((t, LANES), lambda i: (i, 0)),
            pl.BlockSpec((LANES, t), lambda i: (0, i)),
        ],
        out_shape=[jax.ShapeDtypeStruct((s, LANES), F32), jax.ShapeDtypeStruct((LANES, s), F32)],
        compiler_params=_params(("parallel",)),
        name="dn_gates",
    )(small, par)


def _dnchunk_body(q_ref, k_ref, v_ref, col_ref, row_ref, u_ref, w_ref, qd_ref, kt_ref, qk_ref, *, heads_per_step):
    c = DN_TILE
    hd = DN_HEAD_DIM
    head0 = pl.program_id(1) * heads_per_step
    col = col_ref[...]
    row = row_ref[...]
    lane = lax.broadcasted_iota(I32, col.shape, 1)
    sub = lax.broadcasted_iota(I32, row.shape, 0)

    def col_pick(idx):
        return jnp.sum(jnp.where(lane == idx, col, 0.0), axis=1, keepdims=True)

    def row_pick(idx):
        return jnp.sum(jnp.where(sub == idx, row, 0.0), axis=0, keepdims=True)

    ri = lax.broadcasted_iota(I32, (c, c), 0)
    ci = lax.broadcasted_iota(I32, (c, c), 1)
    same_block = (ri // DN_BLOCK) == (ci // DN_BLOCK)
    incl = (ri >= ci, ri <= ci)
    strict = (ri > ci, ri < ci)
    heads = range(heads_per_step)
    chains = [(hh, d) for hh in heads for d in range(2)]
    sls = [slice(hh * hd, (hh + 1) * hd) for hh in heads]
    qs = [q_ref[:, sl] for sl in sls]
    ks = [k_ref[:, sl] for sl in sls]
    vs = [v_ref[:, sl] for sl in sls]
    grams = [_dot_nt(k, k) for k in ks]
    qks = [_dot_nt(q, k) for q, k in zip(qs, ks)]
    qfs = [q.astype(F32) for q in qs]
    kfs = [k.astype(F32) for k in ks]
    vfs = [v.astype(F32) for v in vs]

    betas = [col_pick(d * DN_HEADS + head0 + hh) for hh, d in chains]
    g_cols = [col_pick(16 + d * DN_HEADS + head0 + hh) for hh, d in chains]
    g_rows = [row_pick(16 + d * DN_HEADS + head0 + hh) for hh, d in chains]
    totals = [gr[:, c - 1:c] if d == 0 else gr[:, 0:1] for (hh, d), gr in zip(chains, g_rows)]
    decays = [jnp.where(incl[d], jnp.exp(jnp.minimum(gc - gr, 0.0)), 0.0)
              for (hh, d), gc, gr in zip(chains, g_cols, g_rows)]
    lows = [jnp.where(strict[d], b * grams[hh] * dec, 0.0) for (hh, d), b, dec in zip(chains, betas, decays)]
    l_diags = [jnp.where(same_block, low, 0.0) for low in lows]
    l_offs = [(low - ld).astype(BF16) for low, ld in zip(lows, l_diags)]

    def neumann(accs, x_bs, n_steps):
        for _ in range(n_steps):
            x2s = [_dot(x, x) for x in x_bs]
            x_bs = [x2.astype(BF16) for x2 in x2s]
            accs = [a + x2 + _dot(a.astype(BF16), xb) for a, x2, xb in zip(accs, x2s, x_bs)]
        return accs

    d_ms = neumann([-ld for ld in l_diags], [ld.astype(BF16) for ld in l_diags], (DN_BLOCK - 1).bit_length() - 1)
    d_bs = [dm.astype(BF16) for dm in d_ms]
    ms = [lo.astype(F32) + _dot(db, lo) for db, lo in zip(d_bs, l_offs)]
    q_ms = neumann([-m for m in ms], [m.astype(BF16) for m in ms], (c // DN_BLOCK - 1).bit_length() - 1)
    e_cols = [jnp.exp(gc) for gc in g_cols]
    rhss = [jnp.concatenate([vfs[hh] * b, kfs[hh] * b * ec], axis=1) for (hh, d), b, ec in zip(chains, betas, e_cols)]
    r1s = [rhs + _dot(db, rhs.astype(BF16)) for rhs, db in zip(rhss, d_bs)]
    sols = [r1 + _dot(qm.astype(BF16), r1.astype(BF16)) for r1, qm in zip(r1s, q_ms)]
    for i, (hh, d) in enumerate(chains):
        sl = sls[hh]
        u_ref[d, :, sl] = sols[i][:, :hd]
        w_ref[d, :, sl] = sols[i][:, hd:].astype(BF16)
        qd_ref[d, :, sl] = (qfs[hh] * e_cols[i]).astype(BF16)
        kt_ref[d, sl, :] = (kfs[hh] * jnp.exp(totals[i] - g_cols[i])).T.astype(BF16)
        qk_ref[d, hh] = (qks[hh] * decays[i]).astype(BF16)


def dn_chunk(qkv, col, row, heads_per_step=4):
    s = qkv.shape[0]
    c = DN_TILE
    nt = s // c
    hd = DN_HEAD_DIM * heads_per_step
    groups = DN_HEADS // heads_per_step
    return pl.pallas_call(
        functools.partial(_dnchunk_body, heads_per_step=heads_per_step),
        grid=(nt, groups),
        in_specs=[
            pl.BlockSpec((c, hd), lambda t, h: (t, h)),
            pl.BlockSpec((c, hd), lambda t, h: (t, groups + h)),
            pl.BlockSpec((c, hd), lambda t, h: (t, 2 * groups + h)),
            pl.BlockSpec((c, LANES), lambda t, h: (t, 0)),
            pl.BlockSpec((LANES, c), lambda t, h: (0, t)),
        ],
        out_specs=[
            pl.BlockSpec((2, c, hd), lambda t, h: (0, t, h)),
            pl.BlockSpec((2, c, hd), lambda t, h: (0, t, h)),
            pl.BlockSpec((2, c, hd), lambda t, h: (0, t, h)),
            pl.BlockSpec((2, hd, c), lambda t, h: (0, h, t)),
            pl.BlockSpec((2, heads_per_step, c, c), lambda t, h: (0, h, t, 0)),
        ],
        out_shape=[
            jax.ShapeDtypeStruct((2, s, DN_WIDTH), F32),
            jax.ShapeDtypeStruct((2, s, DN_WIDTH), BF16),
            jax.ShapeDtypeStruct((2, s, DN_WIDTH), BF16),
            jax.ShapeDtypeStruct((2, DN_WIDTH, s), BF16),
            jax.ShapeDtypeStruct((2, DN_HEADS, s, c), BF16),
        ],
        compiler_params=_params(("parallel", "parallel")),
        name="dn_chunk",
    )(qkv, qkv, qkv, col, row)


def _dnscan_body(uf, wf, qdf, ktf, qkf, colf, ub, wb, qdb, ktb, qkb, colb, of_ref, ob_ref, st_ref):
    @pl.when(pl.program_id(0) == 0)
    def _():
        st_ref[...] = jnp.zeros(st_ref.shape, F32)

    c = DN_TILE
    hd = DN_HEAD_DIM
    dirs = ((uf, wf, qdf, ktf, qkf, colf, of_ref, c - 1), (ub, wb, qdb, ktb, qkb, colb, ob_ref, 0))
    chains = [(d, h) for d in range(2) for h in range(DN_HEADS)]
    sls = [slice(h * hd, (h + 1) * hd) for h in range(DN_HEADS)]
    e_tots = [jnp.exp(dirs[d][5][dirs[d][7]:dirs[d][7] + 1, :]) for d in range(2)]
    states = [st_ref[d * DN_HEADS + h] for d, h in chains]
    states_b = [st.astype(BF16) for st in states]
    v_news = [dirs[d][0][0, :, sls[h]] - _dot(dirs[d][1][0, :, sls[h]], sb) for (d, h), sb in zip(chains, states_b)]
    v_news_b = [vn.astype(BF16) for vn in v_news]
    outs = [_dot(dirs[d][2][0, :, sls[h]], sb) + _dot(dirs[d][4][0, h], vb)
            for (d, h), sb, vb in zip(chains, states_b, v_news_b)]
    news = [st * e_tots[d][:, 16 + d * DN_HEADS + h:17 + d * DN_HEADS + h] + _dot(dirs[d][3][0, sls[h], :], vb)
            for (d, h), st, vb in zip(chains, states, v_news_b)]
    for (d, h), out, new in zip(chains, outs, news):
        dirs[d][6][:, sls[h]] = out
        st_ref[d * DN_HEADS + h] = new


def dn_scan(u, w, qd, kt, qk, col):
    s = u.shape[1]
    c = DN_TILE
    nt = s // c
    wd = DN_WIDTH

    def specs(d):
        tile = (lambda t: t) if d == 0 else (lambda t: nt - 1 - t)
        return [
            pl.BlockSpec((1, c, wd), lambda t: (d, tile(t), 0)),
            pl.BlockSpec((1, c, wd), lambda t: (d, tile(t), 0)),
            pl.BlockSpec((1, c, wd), lambda t: (d, tile(t), 0)),
            pl.BlockSpec((1, wd, c), lambda t: (d, 0, tile(t))),
            pl.BlockSpec((1, DN_HEADS, c, c), lambda t: (d, 0, tile(t), 0)),
            pl.BlockSpec((c, LANES), lambda t: (tile(t), 0)),
        ]

    return pl.pallas_call(
        _dnscan_body,
        grid=(nt,),
        in_specs=specs(0) + specs(1),
        out_specs=[
            pl.BlockSpec((c, wd), lambda t: (t, 0)),
            pl.BlockSpec((c, wd), lambda t: (nt - 1 - t, 0)),
        ],
        out_shape=[jax.ShapeDtypeStruct((s, wd), F32), jax.ShapeDtypeStruct((s, wd), F32)],
        scratch_shapes=[pltpu.VMEM((2 * DN_HEADS, DN_HEAD_DIM, DN_HEAD_DIM), F32)],
        compiler_params=_params(("arbitrary",)),
        name="dn_scan",
    )(u, w, qd, kt, qk, col, u, w, qd, kt, qk, col)


def _na_bias_table(rpb):
    c = np.arange(GRID_W)
    kc = np.arange(GRID_W)
    cs = np.clip(c - NA_COLS // 2, 0, GRID_W - NA_COLS)
    valid = (kc[None, :] >= cs[:, None]) & (kc[None, :] < cs[:, None] + NA_COLS)
    dc = kc[None, :] - c[:, None] + (NA_COLS - 1)
    rpb = rpb.astype(F32)
    n_off = rpb.shape[1] - 1
    table = jnp.full((rpb.shape[0], n_off, GRID_W, 2 * GRID_W), NEG_BIG, F32)
    pad = np.zeros_like(valid)
    for j in range(2 * NA_COLS - 1):
        hit = valid & (dc == j)
        left = np.concatenate([hit, pad], axis=1)
        right = np.concatenate([pad, hit], axis=1)
        table = jnp.where(left, rpb[:, :n_off, j][:, :, None, None],
                          jnp.where(right, rpb[:, 1:, j][:, :, None, None], table))
    return table


def _na_body(q_ref, *refs):
    k_refs = refs[:NA_ROWS]
    v_refs = refs[NA_ROWS:2 * NA_ROWS]
    bias_ref = refs[2 * NA_ROWS]
    o_ref = refs[2 * NA_ROWS + 1]
    lane = lax.broadcasted_iota(I32, (1, LANES), 1)
    heads_per_block = LANES // NA_HEAD_DIM
    scale = NA_HEAD_DIM ** -0.5
    n_pairs = NA_WIDTH // LANES
    heads = [(pair, hh) for pair in range(n_pairs) for hh in range(heads_per_block)]
    sls = [slice(pair * LANES, (pair + 1) * LANES) for pair in range(n_pairs)]
    owns = [(lane // NA_HEAD_DIM) == hh for hh in range(heads_per_block)]
    row_id = pl.program_id(0)
    first_offset = jnp.clip(row_id - NA_ROWS // 2, 0, pl.num_programs(0) - NA_ROWS) - row_id + (NA_ROWS - 1)

    def bias(h):
        return jnp.concatenate([bias_ref[h, first_offset + 2 * m] for m in range(NA_ROWS // 2)], axis=1)

    q2s = [q_ref[:, sl] for sl in sls]
    k2s = [jnp.concatenate([r[:, sl] for r in k_refs], axis=0) for sl in sls]
    scores = [_dot_nt(jnp.where(owns[hh], q2s[pair], jnp.zeros_like(q2s[pair])), k2s[pair]) * scale
              + bias(pair * heads_per_block + hh) for pair, hh in heads]
    maxes = [jnp.max(s, axis=-1, keepdims=True) for s in scores]
    probs = [jnp.exp(s - m) for s, m in zip(scores, maxes)]
    denoms = [jnp.sum(p, axis=-1, keepdims=True) for p in probs]
    v2s = [jnp.concatenate([r[:, sl] for r in v_refs], axis=0) for sl in sls]
    outs = [_dot(p.astype(BF16), jnp.where(owns[hh], v2s[pair], jnp.zeros_like(v2s[pair]))) / den
            for (pair, hh), p, den in zip(heads, probs, denoms)]
    for pair in range(n_pairs):
        acc = outs[pair * heads_per_block]
        for hh in range(1, heads_per_block):
            acc = acc + outs[pair * heads_per_block + hh]
        o_ref[:, sls[pair]] = acc.astype(o_ref.dtype)


def na_attention(proj, col0, rpb):
    s = proj.shape[0]
    rows = s // GRID_W
    assert rows >= NA_ROWS
    qb = col0 // NA_WIDTH
    table = _na_bias_table(rpb)

    def first_row(r):
        return jnp.clip(r - NA_ROWS // 2, 0, rows - NA_ROWS)

    def kv_spec(i, blk):
        return pl.BlockSpec((GRID_W, NA_WIDTH), lambda r: (first_row(r) + i, blk))

    in_specs = [pl.BlockSpec((GRID_W, NA_WIDTH), lambda r: (r, qb))]
    in_specs += [kv_spec(i, qb + 1) for i in range(NA_ROWS)]
    in_specs += [kv_spec(i, qb + 2) for i in range(NA_ROWS)]
    in_specs += [pl.BlockSpec(table.shape, lambda r: (0, 0, 0, 0), pipeline_mode=pl.Buffered(1))]
    return pl.pallas_call(
        _na_body,
        grid=(rows,),
        in_specs=in_specs,
        out_specs=pl.BlockSpec((GRID_W, NA_WIDTH), lambda r: (r, 0)),
        out_shape=jax.ShapeDtypeStruct((s, NA_WIDTH), BF16),
        compiler_params=_params(("parallel",)),
        name="na_attn",
    )(proj, *([proj] * (2 * NA_ROWS)), table)


def _pack_bf16_pair(lo, hi):
    lo_bits = pltpu.bitcast(lo.astype(BF16).astype(F32), U32)
    hi_bits = pltpu.bitcast(hi.astype(BF16).astype(F32), U32)
    return (lo_bits >> 16) | (hi_bits & jnp.uint32(0xFFFF0000))


def _unpack_bf16_pair(packed):
    lo = pltpu.bitcast(packed << 16, F32)
    hi = pltpu.bitcast(packed & jnp.uint32(0xFFFF0000), F32)
    return lo, hi


def _merge_body(of_ref, ob_ref, z_ref, na_ref, ga_ref, gb_ref, x_ref, dng_ref, wa_ref, wb_ref, wo_ref,
                gffn_ref, wr_ref, br_ref, x1_ref, h2p_ref, lg_ref, dn_ref):
    hd = DN_HEAD_DIM
    for h in range(DN_HEADS):
        sl = slice(h * hd, (h + 1) * hd)
        o = of_ref[:, sl] + ob_ref[:, sl]
        r = lax.rsqrt(jnp.mean(o * o, axis=-1, keepdims=True) + RMS_EPS)
        z = z_ref[:, sl].astype(F32)
        dn_ref[:, sl] = (o * r * dng_ref[...] * (z * _sigmoid(z))).astype(BF16)
    y_a = _dot(dn_ref[...], wa_ref[...])
    y_b = _dot(na_ref[...], wb_ref[...])
    mixed = _sigmoid(ga_ref[...].astype(F32)) * y_a + _sigmoid(gb_ref[...].astype(F32)) * y_b
    x1 = x_ref[...] + _dot(mixed.astype(BF16), wo_ref[...])
    x1_ref[...] = x1
    r = lax.rsqrt(jnp.mean(x1 * x1, axis=-1, keepdims=True) + RMS_EPS)
    h2 = x1 * r * gffn_ref[...]
    h2_hi = h2.astype(BF16)
    h2_lo = (h2 - h2_hi.astype(F32)).astype(BF16)
    lg_ref[...] = (_dot(h2_hi, wr_ref[0]) + _dot(h2_lo, wr_ref[0]) + _dot(h2_hi, wr_ref[1])) + br_ref[...]
    half = h2.shape[1] // 2
    n_tiles = half // LANES
    packed = _pack_bf16_pair(h2[:, :half], h2[:, half:])
    for c in range(n_tiles):
        h2p_ref[pl.ds(c, h2.shape[0], stride=n_tiles), :] = packed[:, c * LANES:(c + 1) * LANES]


def merge(o_f, o_b, proj, z_blk, gate_blk, na_out, x, dn_norm_g, w_a, w_b, w_o, norm_ffn_g, w_router, b_router,
          tm=256):
    s, d = x.shape
    ne = w_router.shape[1]
    const = lambda i: (0, 0)
    single = pl.Buffered(1)
    w_router_hi = w_router.astype(BF16)
    w_router_lo = (w_router.astype(F32) - w_router_hi.astype(F32)).astype(BF16)
    w_router_split = jnp.stack([w_router_hi, w_router_lo])
    return pl.pallas_call(
        _merge_body,
        grid=(s // tm,),
        in_specs=[
            pl.BlockSpec((tm, DN_WIDTH), lambda i: (i, 0)),
            pl.BlockSpec((tm, DN_WIDTH), lambda i: (i, 0)),
            pl.BlockSpec((tm, DN_WIDTH), lambda i: (i, z_blk)),
            pl.BlockSpec((tm, NA_WIDTH), lambda i: (i, 0)),
            pl.BlockSpec((tm, d), lambda i: (i, gate_blk)),
            pl.BlockSpec((tm, d), lambda i: (i, gate_blk + 1)),
            pl.BlockSpec((tm, d), lambda i: (i, 0)),
            pl.BlockSpec((1, DN_HEAD_DIM), const),
            pl.BlockSpec((DN_WIDTH, d), const, pipeline_mode=single),
            pl.BlockSpec((NA_WIDTH, d), const, pipeline_mode=single),
            pl.BlockSpec((d, d), const, pipeline_mode=single),
            pl.BlockSpec((1, d), const),
            pl.BlockSpec((2, d, ne), lambda i: (0, 0, 0)),
            pl.BlockSpec((1, ne), const),
        ],
        out_specs=[
            pl.BlockSpec((tm, d), lambda i: (i, 0)),
            pl.BlockSpec((tm * (d // 2 // LANES), LANES), lambda i: (i, 0)),
            pl.BlockSpec((tm, ne), lambda i: (i, 0)),
        ],
        out_shape=[
            jax.ShapeDtypeStruct((s, d), F32),
            jax.ShapeDtypeStruct((s * (d // 2 // LANES), LANES), U32),
            jax.ShapeDtypeStruct((s, ne), F32),
        ],
        scratch_shapes=[pltpu.VMEM((tm, DN_WIDTH), BF16)],
        compiler_params=_params(("parallel",)),
        name="merge",
    )(o_f, o_b, proj, na_out, proj, proj, x, dn_norm_g.reshape(1, -1).astype(F32), w_a, w_b, w_o,
      norm_ffn_g.reshape(1, d).astype(F32), w_router_split, b_router.reshape(1, ne).astype(F32))


def _route_body(lg_ref, ti_ref, tw_ref, cnt_ref, carry_ref):
    @pl.when(pl.program_id(0) == 0)
    def _():
        carry_ref[...] = jnp.zeros(carry_ref.shape, F32)

    lg = lg_ref[...]
    tm, ne = lg.shape
    lane = lax.broadcasted_iota(I32, (tm, ne), 1).astype(F32)
    work = lg
    vals, idxs = [], []
    onehot = jnp.zeros((tm, ne), F32)
    for _ in range(TOP_K):
        m = jnp.max(work, axis=-1, keepdims=True)
        idx = jnp.min(jnp.where(work == m, lane, float(ne)), axis=-1, keepdims=True)
        hit = lane == idx
        vals.append(m)
        idxs.append(idx)
        onehot = onehot + jnp.where(hit, 1.0, 0.0)
        work = jnp.where(hit, -jnp.inf, work)
    exps = [jnp.exp(v - vals[0]) for v in vals]
    denom = exps[0] + exps[1] + exps[2] + exps[3]
    ri = lax.broadcasted_iota(I32, (tm, tm), 0)
    ci = lax.broadcasted_iota(I32, (tm, tm), 1)
    strict = jnp.where(ci < ri, 1.0, 0.0).astype(BF16)
    before = _dot(strict, onehot.astype(BF16)) + carry_ref[0:1, 0:ne]
    lane_o = lax.broadcasted_iota(I32, (tm, LANES), 1)
    ti = jnp.zeros((tm, LANES), I32)
    tw = jnp.zeros((tm, LANES), F32)
    for kk in range(TOP_K):
        rank = jnp.sum(jnp.where(lane == idxs[kk], before, 0.0), axis=-1, keepdims=True).astype(I32)
        ti = jnp.where(lane_o == kk, idxs[kk].astype(I32), ti)
        ti = jnp.where(lane_o == TOP_K + kk, rank, ti)
        tw = jnp.where(lane_o == kk, exps[kk] / denom, tw)
    ti_ref[...] = ti
    tw_ref[...] = tw
    total = carry_ref[0:1, 0:ne] + jnp.sum(onehot, axis=0, keepdims=True)
    carry_ref[0:1, 0:ne] = total
    cnt_ref[...] = jnp.zeros(cnt_ref.shape, F32)
    cnt_ref[0:1, 0:ne] = total


def route(logits, tm=512):
    s, ne = logits.shape
    return pl.pallas_call(
        _route_body,
        grid=(s // tm,),
        in_specs=[pl.BlockSpec((tm, ne), lambda i: (i, 0))],
        out_specs=[
            pl.BlockSpec((tm, LANES), lambda i: (i, 0)),
            pl.BlockSpec((tm, LANES), lambda i: (i, 0)),
            pl.BlockSpec((8, LANES), lambda i: (0, 0)),
        ],
        out_shape=[
            jax.ShapeDtypeStruct((s, LANES), I32),
            jax.ShapeDtypeStruct((s, LANES), F32),
            jax.ShapeDtypeStruct((8, LANES), F32),
        ],
        scratch_shapes=[pltpu.VMEM((8, LANES), F32)],
        compiler_params=_params(("arbitrary",)),
        name="route",
    )(logits)


def _dest_body(ti_ref, ps_ref, d_ref):
    ti = ti_ref[...].astype(F32)
    tm = ti.shape[0]
    lane = lax.broadcasted_iota(I32, (tm, LANES), 1)
    lane_f = lane.astype(F32)
    ps = ps_ref[0:1, :].astype(F32)
    out = jnp.zeros((tm, LANES), F32)
    for kk in range(TOP_K):
        e = jnp.sum(jnp.where(lane == kk, ti, 0.0), axis=-1, keepdims=True)
        rank = jnp.sum(jnp.where(lane == TOP_K + kk, ti, 0.0), axis=-1, keepdims=True)
        start = jnp.sum(jnp.where(lane_f == e, ps, 0.0), axis=-1, keepdims=True)
        out = jnp.where(lane == kk, start + rank, out)
    d_ref[...] = out.astype(I32)


def route_dest(ti, pad_start, tm=512):
    s = ti.shape[0]
    ps = jnp.zeros((8, LANES), I32).at[0, :N_EXPERTS].set(pad_start)
    return pl.pallas_call(
        _dest_body,
        grid=(s // tm,),
        in_specs=[pl.BlockSpec((tm, LANES), lambda i: (i, 0)), pl.BlockSpec((8, LANES), lambda i: (0, 0))],
        out_specs=pl.BlockSpec((tm, LANES), lambda i: (i, 0)),
        out_shape=jax.ShapeDtypeStruct((s, LANES), I32),
        compiler_params=_params(("parallel",)),
        name="route_dest",
    )(ti, ps)


def _scatter_body(pend_ref, padded_ref, dest_ref, h_ref, xr_ref, zero_ref, sem):
    tm = h_ref.shape[0]

    def zero_copy(e):
        return pltpu.make_async_copy(zero_ref, xr_ref.at[pl.ds(pend_ref[e] - MOE_SUB, MOE_SUB)], sem)

    @pl.when(pl.program_id(0) == 0)
    def _():
        zero_ref[...] = jnp.zeros(zero_ref.shape, U32)

        def start(e, carry):
            @pl.when(padded_ref[e] > 0)
            def _():
                zero_copy(e).start()
            return carry

        def wait(e, carry):
            @pl.when(padded_ref[e] > 0)
            def _():
                zero_copy(e).wait()
            return carry

        lax.fori_loop(0, N_EXPERTS, start, 0)
        lax.fori_loop(0, N_EXPERTS, wait, 0)

        def slack_copy(b):
            return pltpu.make_async_copy(zero_ref, xr_ref.at[pl.ds(b * MOE_SUB, MOE_SUB)], sem)

        def slack_start(b, carry):
            slack_copy(b).start()
            return carry

        def slack_wait(b, carry):
            slack_copy(b).wait()
            return carry

        first_slack = pend_ref[N_EXPERTS - 1] // MOE_SUB
        lax.fori_loop(first_slack, xr_ref.shape[0] // MOE_SUB, slack_start, 0)
        lax.fori_loop(first_slack, xr_ref.shape[0] // MOE_SUB, slack_wait, 0)

    def row_copy(r, kk):
        d = dest_ref[r * TOP_K + kk]
        return pltpu.make_async_copy(h_ref.at[r], xr_ref.at[d], sem)

    def start(r, carry):
        for kk in range(TOP_K):
            row_copy(r, kk).start()
        return carry

    lax.fori_loop(0, tm, start, 0)
    all_rows = xr_ref.at[pl.ds(0, tm * TOP_K)]
    pltpu.make_async_copy(all_rows, all_rows, sem).wait()


def moe_scatter(h2p, dest, pad_end, padded, n_rows, tm=256):
    s, nt, _ = h2p.shape
    grid_spec = pltpu.PrefetchScalarGridSpec(
        num_scalar_prefetch=2,
        grid=(s // tm,),
        in_specs=[
            pl.BlockSpec((tm * TOP_K,), lambda i, *_: (i,), memory_space=pltpu.SMEM),
            pl.BlockSpec((tm, nt, LANES), lambda i, *_: (i, 0, 0)),
        ],
        out_specs=pl.BlockSpec(memory_space=pl.ANY),
        scratch_shapes=[pltpu.VMEM((MOE_SUB, nt, LANES), U32), pltpu.SemaphoreType.DMA(())],
    )
    return pl.pallas_call(
        _scatter_body,
        grid_spec=grid_spec,
        out_shape=jax.ShapeDtypeStruct((n_rows, nt, LANES), U32),
        compiler_params=_params(("arbitrary",)),
        name="moe_scatter",
    )(pad_end, padded, dest, h2p)


def _moe_body(ie_ref, ir_ref, inb_ref, ni_ref, xr_ref, wg_ref, wu_ref, wd_ref, bg_ref, bu_ref, bd_ref, y_ref,
              stage_ref, ystage_ref, xb_ref, acc_ref, wgb_ref, wub_ref, wdb_ref, sem_in, sem_out, *, n_chunks):
    w = pl.program_id(0)
    j = pl.program_id(1)
    sb = MOE_SUB
    d = acc_ref.shape[1]
    half = d // 2
    nx = half // LANES
    ny = half // LANES

    @pl.when(w < ni_ref[0])
    def _():
        nb = inb_ref[w]
        r0 = ir_ref[w]

        @pl.when(j == 0)
        def _():
            @pl.when(w == 0)
            def _():
                ystage_ref[0] = jnp.zeros(ystage_ref.shape[1:], U32)

                def slack_copy(b):
                    dst = y_ref.at[pl.ds(pl.multiple_of(b * (sb * ny), sb * ny), sb * ny), :]
                    return pltpu.make_async_copy(ystage_ref.at[0], dst, sem_out.at[0])

                def slack_start(b, carry):
                    slack_copy(b).start()
                    return carry

                def slack_wait(b, carry):
                    slack_copy(b).wait()
                    return carry

                n_blocks = y_ref.shape[0] // (sb * ny)
                lax.fori_loop(ni_ref[1], n_blocks, slack_start, 0)
                lax.fori_loop(ni_ref[1], n_blocks, slack_wait, 0)

            def in_copy(i, slot):
                src = xr_ref.at[pl.ds(pl.multiple_of((r0 + i * sb) * nx, sb * nx), sb * nx), :]
                return pltpu.make_async_copy(src, stage_ref.at[slot], sem_in.at[slot])

            in_copy(0, 0).start()

            def load(i, carry):
                slot = i % 2

                @pl.when(i + 1 < nb)
                def _():
                    in_copy(i + 1, 1 - slot).start()

                in_copy(i, slot).wait()
                rows = pl.ds(pl.multiple_of(i * sb, sb), sb)
                for c in range(nx):
                    lo, hi = _unpack_bf16_pair(stage_ref[slot, pl.ds(c, sb, stride=nx), :])
                    xb_ref[rows, c * LANES:(c + 1) * LANES] = lo.astype(BF16)
                    xb_ref[rows, half + c * LANES:half + (c + 1) * LANES] = hi.astype(BF16)
                return carry

            lax.fori_loop(0, nb, load, 0)

        wgb_ref[...] = wg_ref[0].astype(BF16)
        wub_ref[...] = wu_ref[0].astype(BF16)
        wdb_ref[...] = wd_ref[0].astype(BF16)

        def block(first, n_sub, i, carry):
            rows = pl.ds(pl.multiple_of(i * sb, sb), n_sub * sb)
            xs = xb_ref[rows, :]
            gate = jnp.minimum(_dot(xs, wgb_ref[...]) + bg_ref[0], SWIGLU_LIMIT)
            up = jnp.clip(_dot(xs, wub_ref[...]) + bu_ref[0], -SWIGLU_LIMIT, SWIGLU_LIMIT)
            act = (up + 1.0) * (gate * _sigmoid(SWIGLU_ALPHA * gate))
            contrib = _dot(act.astype(BF16), wdb_ref[...])
            if first:
                acc_ref[rows, :] = contrib
            else:
                acc_ref[rows, :] += contrib
            return carry

        def all_blocks(first):
            def pair(i2, carry):
                return block(first, 2, 2 * i2, carry)

            lax.fori_loop(0, nb // 2, pair, 0)

            @pl.when(nb % 2 == 1)
            def _():
                block(first, 1, nb - 1, 0)

        @pl.when(j == 0)
        def _():
            all_blocks(True)

        @pl.when(j > 0)
        def _():
            all_blocks(False)

        @pl.when(j == n_chunks - 1)
        def _():
            def out_copy(i, slot):
                dst = y_ref.at[pl.ds(pl.multiple_of((r0 + i * sb) * ny, sb * ny), sb * ny), :]
                return pltpu.make_async_copy(ystage_ref.at[slot], dst, sem_out.at[slot])

            def store(i, carry):
                slot = i % 2

                @pl.when(i >= 2)
                def _():
                    out_copy(i - 2, slot).wait()

                rows = pl.ds(pl.multiple_of(i * sb, sb), sb)
                for c in range(ny):
                    lo = slice(c * LANES, (c + 1) * LANES)
                    hi = slice(half + c * LANES, half + (c + 1) * LANES)
                    ystage_ref[slot, pl.ds(c, sb, stride=ny), :] = _pack_bf16_pair(
                        acc_ref[rows, lo] + bd_ref[0, :, lo], acc_ref[rows, hi] + bd_ref[0, :, hi])
                out_copy(i, slot).start()
                return carry

            lax.fori_loop(0, nb, store, 0)

            @pl.when(nb >= 2)
            def _():
                out_copy(nb - 2, nb % 2).wait()

            out_copy(nb - 1, (nb - 1) % 2).wait()


def moe_ffn(x_rows, w_gate_up, b_gate_up, w_down, b_down, item_e, item_row, item_nb, n_items, max_items):
    ne, d, two_de = w_gate_up.shape
    nx = d // 2 // LANES
    ny = d // 2 // LANES
    n_rows = x_rows.shape[0] // nx
    de = two_de // 2
    tn = MOE_TN
    n_chunks = de // tn
    last = n_chunks - 1

    def chunk(w, j, ni):
        return jnp.where(w < ni[0], j, last)

    grid_spec = pltpu.PrefetchScalarGridSpec(
        num_scalar_prefetch=4,
        grid=(max_items, n_chunks),
        in_specs=[
            pl.BlockSpec(memory_space=pl.ANY),
            pl.BlockSpec((1, d, tn), lambda w, j, ie, ir, inb, ni: (ie[w], 0, chunk(w, j, ni))),
            pl.BlockSpec((1, d, tn), lambda w, j, ie, ir, inb, ni: (ie[w], 0, n_chunks + chunk(w, j, ni))),
            pl.BlockSpec((1, tn, d), lambda w, j, ie, ir, inb, ni: (ie[w], chunk(w, j, ni), 0)),
            pl.BlockSpec((1, 1, tn), lambda w, j, ie, ir, inb, ni: (ie[w], 0, chunk(w, j, ni))),
            pl.BlockSpec((1, 1, tn), lambda w, j, ie, ir, inb, ni: (ie[w], 0, n_chunks + chunk(w, j, ni))),
            pl.BlockSpec((1, 1, d), lambda w, j, ie, ir, inb, ni: (ie[w], 0, 0)),
        ],
        out_specs=pl.BlockSpec(memory_space=pl.ANY),
        scratch_shapes=[
            pltpu.VMEM((2, MOE_SUB * nx, LANES), U32),
            pltpu.VMEM((2, MOE_SUB * ny, LANES), U32),
            pltpu.VMEM((MOE_TM, d), BF16),
            pltpu.VMEM((MOE_TM, d), F32),
            pltpu.VMEM((d, tn), BF16),
            pltpu.VMEM((d, tn), BF16),
            pltpu.VMEM((tn, d), BF16),
            pltpu.SemaphoreType.DMA((2,)),
            pltpu.SemaphoreType.DMA((2,)),
        ],
    )
    return pl.pallas_call(
        functools.partial(_moe_body, n_chunks=n_chunks),
        grid_spec=grid_spec,
        out_shape=jax.ShapeDtypeStruct((n_rows * ny, LANES), U32),
        compiler_params=_params(("arbitrary", "arbitrary")),
        name="moe_ffn",
    )(item_e, item_row, item_nb, n_items, x_rows, w_gate_up, w_gate_up, w_down,
      b_gate_up.reshape(ne, 1, two_de), b_gate_up.reshape(ne, 1, two_de), b_down.reshape(ne, 1, d))


def _final_body(dest_ref, dest_next_ref, x1_ref, tw_ref, p_ref, gple_ref, wg_ref, wp_ref, gfin_ref, y_ref, o_ref,
                ybuf_ref, sem, *, last_layer):
    i = pl.program_id(0)
    tm = x1_ref.shape[0]
    half = x1_ref.shape[1] // 2
    ny = half // LANES
    slot = i % 2

    def row_copy(d_ref, to_slot, r, kk):
        dst = ybuf_ref.at[to_slot, kk, pl.ds(pl.multiple_of(r * ny, ny), ny), :]
        return pltpu.make_async_copy(y_ref.at[d_ref[r * TOP_K + kk]], dst, sem.at[to_slot])

    def issue(d_ref, to_slot):
        def body(r, carry):
            for kk in range(TOP_K):
                row_copy(d_ref, to_slot, r, kk).start()
            return carry

        lax.fori_loop(0, tm, body, 0)

    def drain(to_slot):
        pltpu.make_async_copy(ybuf_ref.at[to_slot], ybuf_ref.at[to_slot], sem.at[to_slot]).wait()

    @pl.when(i == 0)
    def _():
        issue(dest_ref, 0)

    @pl.when(i + 1 < pl.num_programs(0))
    def _():
        issue(dest_next_ref, 1 - slot)

    drain(slot)

    tw = tw_ref[...]
    los, his = [], []
    for c in range(ny):
        acc_lo = x1_ref[:, c * LANES:(c + 1) * LANES]
        acc_hi = x1_ref[:, half + c * LANES:half + (c + 1) * LANES]
        for kk in range(TOP_K):
            lo, hi = _unpack_bf16_pair(ybuf_ref[slot, kk, pl.ds(c, tm, stride=ny), :])
            acc_lo = acc_lo + tw[:, kk:kk + 1] * lo
            acc_hi = acc_hi + tw[:, kk:kk + 1] * hi
        los.append(acc_lo)
        his.append(acc_hi)
    x2 = jnp.concatenate(los + his, axis=1)
    r = lax.rsqrt(jnp.mean(x2 * x2, axis=-1, keepdims=True) + RMS_EPS)
    n = (x2 * r * gple_ref[...]).astype(BF16)
    gate = _sigmoid(_dot(n, wg_ref[...]))
    x3 = x2 + gate * _dot(p_ref[...].astype(BF16), wp_ref[...])
    if last_layer:
        r = lax.rsqrt(jnp.mean(x3 * x3, axis=-1, keepdims=True) + RMS_EPS)
        x3 = x3 * r * gfin_ref[...]
    o_ref[...] = x3


def final(dest, x1, tw, p, norm_ple_g, w_gate, w_proj, norm_final_g, y_rows, last_layer, tm=256):
    s, d = x1.shape
    pd = p.shape[1]
    ny = d // 2 // LANES
    n_steps = s // tm
    const = lambda i: (0, 0)
    single = pl.Buffered(1)
    return pl.pallas_call(
        functools.partial(_final_body, last_layer=last_layer),
        grid=(n_steps,),
        in_specs=[
            pl.BlockSpec((tm * TOP_K,), lambda i: (i,), memory_space=pltpu.SMEM),
            pl.BlockSpec((tm * TOP_K,), lambda i: (jnp.minimum(i + 1, n_steps - 1),), memory_space=pltpu.SMEM),
            pl.BlockSpec((tm, d), lambda i: (i, 0)),
            pl.BlockSpec((tm, LANES), lambda i: (i, 0)),
            pl.BlockSpec((tm, pd), lambda i: (i, 0)),
            pl.BlockSpec((1, d), const),
            pl.BlockSpec((d, d), const, pipeline_mode=single),
            pl.BlockSpec((pd, d), const, pipeline_mode=single),
            pl.BlockSpec((1, d), const),
            pl.BlockSpec(memory_space=pl.ANY),
        ],
        out_specs=pl.BlockSpec((tm, d), lambda i: (i, 0)),
        out_shape=jax.ShapeDtypeStruct((s, d), F32),
        scratch_shapes=[pltpu.VMEM((2, TOP_K, tm * ny, LANES), U32), pltpu.SemaphoreType.DMA((2,))],
        compiler_params=_params(("arbitrary",)),
        name="final",
    )(dest, dest, x1, tw, p, norm_ple_g.reshape(1, d).astype(F32), w_gate, w_proj,
      norm_final_g.reshape(1, d).astype(F32), y_rows.reshape(-1, ny, LANES))


def _moe_tables(counts, n_rows):
    sub, tm = MOE_SUB, MOE_TM
    max_items = N_EXPERTS + n_rows // tm
    padded = (counts + sub - 1) // sub * sub
    pad_end = jnp.cumsum(padded)
    pad_start = pad_end - padded
    n_it = (padded + tm - 1) // tm
    it_end = jnp.cumsum(n_it)
    it_start = it_end - n_it
    n_items = it_end[-1]
    w = jnp.arange(max_items, dtype=I32)
    live = w < n_items
    w_eff = jnp.minimum(w, n_items - 1)
    e_w = jnp.minimum(jnp.searchsorted(it_end, w_eff, side="right"), N_EXPERTS - 1).astype(I32)
    m_w = w_eff - it_start[e_w]
    row_w = pad_start[e_w] + m_w * tm
    nb_w = jnp.clip((padded[e_w] - m_w * tm) // sub, 0, tm // sub)
    nb_w = jnp.where(live, nb_w, 0)
    as_i32 = lambda a: a.astype(I32)
    counts_w = jnp.stack([n_items, pad_end[-1] // sub])
    return (as_i32(pad_start), as_i32(pad_end), as_i32(padded), e_w, as_i32(row_w), as_i32(nb_w),
            as_i32(counts_w), max_items)


def _layer(x, p, norm_mix_g, w_in, dn_conv_w, dn_a_log, dn_dt_bias, dn_norm_g, na_rpb, w_branch_a, w_branch_b,
           w_out, norm_ffn_g, w_router, b_router, w_gate_up, b_gate_up, w_down, b_down, norm_ple_g,
           w_ple_gate, w_ple_proj, norm_final_g, last_layer):
    s, d = x.shape
    c_qkv, c_z = 3 * DN_WIDTH, DN_WIDTH
    w_t = jnp.swapaxes(w_in, 0, 1)
    w_main = w_prep(w_t)
    z_blk = c_qkv // DN_WIDTH
    gate_blk = (c_qkv + c_z) // d
    na_col0 = c_qkv + c_z + 2 * d

    proj, small = in_projection(x, norm_mix_g.astype(F32), w_main, w_t)

    qkv = dn_prep(proj, dn_conv_w.astype(F32))
    col, row = dn_gates(small, dn_a_log, dn_dt_bias)
    u, w, qd, kt, qk = dn_chunk(qkv, col, row)
    o_f, o_b = dn_scan(u, w, qd, kt, qk, col)

    na_out = na_attention(proj, na_col0, na_rpb)

    x1, h2p, logits = merge(o_f, o_b, proj, z_blk, gate_blk, na_out, x, dn_norm_g, w_branch_a.astype(BF16),
                            w_branch_b.astype(BF16), w_out.astype(BF16), norm_ffn_g, w_router, b_router)

    ti, tw, cnt = route(logits)
    counts = cnt[0, :N_EXPERTS].astype(I32)
    n_rows = (s * TOP_K + N_EXPERTS * (MOE_SUB - 1) + MOE_SUB - 1) // MOE_SUB * MOE_SUB
    pad_start, pad_end, padded, item_e, item_row, item_nb, n_items, max_items = _moe_tables(counts, n_rows)
    dest = route_dest(ti, pad_start)[:, :TOP_K].reshape(-1)

    nx = d // 2 // LANES
    x_rows = moe_scatter(h2p.reshape(s, nx, LANES), dest, pad_end, padded, n_rows).reshape(n_rows * nx, LANES)
    y_rows = moe_ffn(x_rows, w_gate_up, b_gate_up, w_down, b_down, item_e, item_row, item_nb, n_items, max_items)

    return final(dest, x1, tw, p, norm_ple_g, w_ple_gate.astype(BF16), w_ple_proj.astype(BF16), norm_final_g,
                 y_rows, last_layer)


def kernel(x, p, norm_mix_g, w_in, dn_conv_w, dn_a_log, dn_dt_bias, dn_norm_g, na_rpb, w_branch_a, w_branch_b, w_out, norm_ffn_g, w_router, b_router, w_gate_up, b_gate_up, w_down, b_down, norm_ple_g, w_ple_gate, w_ple_proj, norm_final_g):
    bsz, s, d = x.shape
    depth = w_in.shape[0]
    outs = []
    for b in range(bsz):
        xb = x[b]
        for i in range(depth):
            xb = _layer(xb, p[i, b], norm_mix_g[i], w_in[i], dn_conv_w[i], dn_a_log[i], dn_dt_bias[i], dn_norm_g[i],
                        na_rpb[i], w_branch_a[i], w_branch_b[i], w_out[i], norm_ffn_g[i], w_router[i], b_router[i],
                        w_gate_up[i], b_gate_up[i], w_down[i], b_down[i], norm_ple_g[i], w_ple_gate[i],
                        w_ple_proj[i], norm_final_g, i == depth - 1)
        outs.append(xb)
    return jnp.stack(outs, axis=0)
```

```python
import functools

import jax
import jax.numpy as jnp
import numpy as np
from jax import lax
from jax.experimental import pallas as pl
from jax.experimental.pallas import tpu as pltpu

F32 = jnp.float32
BF16 = jnp.bfloat16
I32 = jnp.int32
U32 = jnp.uint32

GRID_W = 64
DN_HEADS = 8
DN_HEAD_DIM = 128
DN_WIDTH = DN_HEADS * DN_HEAD_DIM
DN_CONV = 5
NA_HEADS = 16
NA_HEAD_DIM = 64
NA_WIDTH = NA_HEADS * NA_HEAD_DIM
NA_ROWS = 8
NA_COLS = 16
N_EXPERTS = 32
TOP_K = 4
SWIGLU_LIMIT = 7.0
SWIGLU_ALPHA = 1.702
RMS_EPS = 1e-6

LANES = 128
VMEM_LIMIT = 56 * 1024 * 1024

DN_TILE = 256
DN_BLOCK = 16
MOE_SUB = 256
MOE_TM = 1536
MOE_TN = 512
NEG_BIG = -1e30


def _sigmoid(x):
    return 1.0 / (1.0 + jnp.exp(-x))


def _dot(a, b):
    return jnp.dot(a, b, preferred_element_type=F32)


def _dot_nt(a, b):
    return lax.dot_general(a, b, (((1,), (1,)), ((), ())), preferred_element_type=F32)


def _params(sem, limit=VMEM_LIMIT):
    return pltpu.CompilerParams(dimension_semantics=sem, vmem_limit_bytes=limit)


def _wprep_body(src_ref, o_ref, buf_ref, sem, *, starts, tn):
    t = pl.program_id(0)
    slot = t % 2

    def tile_copy(tt, to_slot):
        start = jnp.int32(starts[0])
        for k in range(1, len(starts)):
            start = jnp.where(tt >= k, starts[k], start)
        src = src_ref.at[pl.ds(pl.multiple_of(start, 8), tn), :]
        return pltpu.make_async_copy(src, buf_ref.at[to_slot], sem.at[to_slot])

    @pl.when(t == 0)
    def _():
        tile_copy(0, 0).start()

    @pl.when(t + 1 < pl.num_programs(0))
    def _():
        tile_copy(t + 1, 1 - slot).start()

    tile_copy(t, slot).wait()
    o_ref[...] = buf_ref[slot].T.astype(o_ref.dtype)


def w_prep(w_t, tn=1024):
    n_in, d = w_t.shape
    c_small = 4 * DN_HEADS
    o_small = 4 * DN_WIDTH
    o_na = o_small + c_small
    o_gates = o_na + 3 * NA_WIDTH
    assert o_small % tn == 0 and (3 * NA_WIDTH) % tn == 0 and (n_in - o_gates) % tn == 0
    assert o_na % 8 == 0 and o_gates % 8 == 0
    starts = list(range(0, o_small, tn)) + list(range(o_gates, n_in, tn)) + list(range(o_na, o_gates, tn))
    return pl.pallas_call(
        functools.partial(_wprep_body, starts=tuple(starts), tn=tn),
        grid=(len(starts),),
        in_specs=[pl.BlockSpec(memory_space=pl.ANY)],
        out_specs=pl.BlockSpec((d, tn), lambda t: (0, t)),
        out_shape=jax.ShapeDtypeStruct((d, len(starts) * tn), BF16),
        scratch_shapes=[pltpu.VMEM((2, tn, d), F32), pltpu.SemaphoreType.DMA((2,))],
        compiler_params=_params(("arbitrary",)),
        name="w_prep",
    )(w_t)


def _inproj_body(x_ref, g_ref, w_ref, ws_ref, o_ref, os_ref, h_ref):
    @pl.when(pl.program_id(1) == 0)
    def _():
        x = x_ref[...]
        r = lax.rsqrt(jnp.mean(x * x, axis=-1, keepdims=True) + RMS_EPS)
        h = (x * r * g_ref[...]).astype(BF16)
        h_ref[...] = h
        os_ref[...] = _dot_nt(h, ws_ref[...].astype(BF16))

    o_ref[...] = _dot(h_ref[...], w_ref[...]).astype(o_ref.dtype)


def in_projection(x, g, w_main, w_t, tm=512, tn=1024):
    s, d = x.shape
    n = w_main.shape[1]
    small_blk = 4 * DN_WIDTH // LANES
    return pl.pallas_call(
        _inproj_body,
        grid=(s // tm, n // tn),
        in_specs=[
            pl.BlockSpec((tm, d), lambda i, j: (i, 0)),
            pl.BlockSpec((1, d), lambda i, j: (0, 0)),
            pl.BlockSpec((d, tn), lambda i, j: (0, j)),
            pl.BlockSpec((LANES, d), lambda i, j: (small_blk, 0)),
        ],
        out_specs=[
            pl.BlockSpec((tm, tn), lambda i, j: (i, j)),
            pl.BlockSpec((tm, LANES), lambda i, j: (i, 0)),
        ],
        out_shape=[jax.ShapeDtypeStruct((s, n), BF16), jax.ShapeDtypeStruct((s, LANES), F32)],
        scratch_shapes=[pltpu.VMEM((tm, d), BF16)],
        compiler_params=_params(("parallel", "arbitrary")),
        name="in_proj",
    )(x, g.reshape(1, d), w_main, w_t)


def _dnprep_body(x_ref, w_ref, o_ref, pad_ref, *, seq, chunk):
    cb = pl.program_id(0)
    n_chunks = seq // chunk
    zeros = jnp.zeros((16, LANES), F32)
    pad_ref[0:16, :] = zeros
    pad_ref[seq + 16:seq + 32, :] = zeros

    def fill(c, carry):
        r0 = pl.multiple_of(c * chunk, chunk)
        pad_ref[pl.ds(r0 + 16, chunk), :] = x_ref[pl.ds(r0, chunk), :].astype(F32)
        return carry

    lax.fori_loop(0, n_chunks, fill, 0)

    w = w_ref[...]
    is_v = cb >= 2 * DN_HEADS
    scale = jnp.where(cb < DN_HEADS, DN_HEAD_DIM ** -0.5, 1.0).astype(F32)

    def body(c, carry):
        r0 = pl.multiple_of(c * chunk, chunk)
        y = pad_ref[pl.ds(r0 + 14, chunk), :] * w[0:1]
        for j in range(1, DN_CONV):
            y = y + pad_ref[pl.ds(r0 + 14 + j, chunk), :] * w[j:j + 1]
        y = y * _sigmoid(y)
        ss = jnp.sum(y * y, axis=-1, keepdims=True)
        yn = y * (lax.rsqrt(ss + 1e-6) * scale)
        o_ref[pl.ds(r0, chunk), :] = jnp.where(is_v, y, yn).astype(o_ref.dtype)
        return carry

    lax.fori_loop(0, n_chunks, body, 0)


def dn_prep(proj, conv_w, chunk=512):
    s = proj.shape[0]
    nb = 3 * DN_WIDTH // LANES
    w = jnp.zeros((8, 3 * DN_WIDTH), F32).at[:DN_CONV].set(conv_w)
    return pl.pallas_call(
        functools.partial(_dnprep_body, seq=s, chunk=chunk),
        grid=(nb,),
        in_specs=[
            pl.BlockSpec((s, LANES), lambda c: (0, c)),
            pl.BlockSpec((8, LANES), lambda c: (0, c)),
        ],
        out_specs=pl.BlockSpec((s, LANES), lambda c: (0, c)),
        out_shape=jax.ShapeDtypeStruct((s, 3 * DN_WIDTH), BF16),
        scratch_shapes=[pltpu.VMEM((s + 32, LANES), F32)],
        compiler_params=_params(("parallel",)),
        name="dn_prep",
    )(proj, w)


def _gates_body(s_ref, par_ref, col_ref, row_ref):
    x = s_ref[...]
    t = x.shape[0]
    lane = lax.broadcasted_iota(I32, x.shape, 1)
    beta = _sigmoid(x)
    z = x + par_ref[1:2, :]
    softplus = jnp.maximum(z, 0.0) + jnp.log(1.0 + jnp.exp(-jnp.abs(z)))
    g = par_ref[0:1, :] * softplus
    ri = lax.broadcasted_iota(I32, (t, t), 0)
    ci = lax.broadcasted_iota(I32, (t, t), 1)
    lower = jnp.where(ci <= ri, 1.0, 0.0).astype(F32)
    upper = jnp.where(ci >= ri, 1.0, 0.0).astype(F32)
    hi = lax.Precision.HIGHEST
    g_prefix = jnp.dot(lower, g, precision=hi, preferred_element_type=F32)
    g_suffix = jnp.dot(upper, g, precision=hi, preferred_element_type=F32)
    cum = jnp.where(lane < 16 + DN_HEADS, g_prefix, g_suffix)
    out = jnp.where(lane < 16, beta, jnp.where(lane < 32, cum, 0.0))
    col_ref[...] = out
    row_ref[...] = out.T


def dn_gates(small, a_log, dt_bias):
    s = small.shape[0]
    par = jnp.zeros((8, LANES), F32)
    par = par.at[0, 16:32].set(-jnp.exp(a_log.reshape(-1).astype(F32)))
    par = par.at[1, 16:32].set(dt_bias.reshape(-1).astype(F32))
    t = DN_TILE
    return pl.pallas_call(
        _gates_body,
        grid=(s // t,),
        in_specs=[
            pl.BlockSpec((t, LANES), lambda i: (i, 0)),
            pl.BlockSpec((8, LANES), lambda i: (0, 0)),
        ],
        out_specs=[
            pl.BlockSpec((t, LANES), lambda i: (i, 0)),
            pl.BlockSpec((LANES, t), lambda i: (0, i)),
        ],
        out_shape=[jax.ShapeDtypeStruct((s, LANES), F32), jax.ShapeDtypeStruct((LANES, s), F32)],
        compiler_params=_params(("parallel",)),
        name="dn_gates",
    )(small, par)


def _dnchunk_body(q_ref, k_ref, v_ref, col_ref, row_ref, u_ref, w_ref, qd_ref, kt_ref, qk_ref, *, heads_per_step):
    c = DN_TILE
    hd = DN_HEAD_DIM
    head0 = pl.program_id(1) * heads_per_step
    col = col_ref[...]
    row = row_ref[...]
    lane = lax.broadcasted_iota(I32, col.shape, 1)
    sub = lax.broadcasted_iota(I32, row.shape, 0)

    def col_pick(idx):
        return jnp.sum(jnp.where(lane == idx, col, 0.0), axis=1, keepdims=True)

    def row_pick(idx):
        return jnp.sum(jnp.where(sub == idx, row, 0.0), axis=0, keepdims=True)

    ri = lax.broadcasted_iota(I32, (c, c), 0)
    ci = lax.broadcasted_iota(I32, (c, c), 1)
    same_block = (ri // DN_BLOCK) == (ci // DN_BLOCK)
    incl = (ri >= ci, ri <= ci)
    strict = (ri > ci, ri < ci)
    heads = range(heads_per_step)
    chains = [(hh, d) for hh in heads for d in range(2)]
    sls = [slice(hh * hd, (hh + 1) * hd) for hh in heads]
    qs = [q_ref[:, sl] for sl in sls]
    ks = [k_ref[:, sl] for sl in sls]
    vs = [v_ref[:, sl] for sl in sls]
    grams = [_dot_nt(k, k) for k in ks]
    qks = [_dot_nt(q, k) for q, k in zip(qs, ks)]
    qfs = [q.astype(F32) for q in qs]
    kfs = [k.astype(F32) for k in ks]
    vfs = [v.astype(F32) for v in vs]

    betas = [col_pick(d * DN_HEADS + head0 + hh) for hh, d in chains]
    g_cols = [col_pick(16 + d * DN_HEADS + head0 + hh) for hh, d in chains]
    g_rows = [row_pick(16 + d * DN_HEADS + head0 + hh) for hh, d in chains]
    totals = [gr[:, c - 1:c] if d == 0 else gr[:, 0:1] for (hh, d), gr in zip(chains, g_rows)]
    decays = [jnp.where(incl[d], jnp.exp(jnp.minimum(gc - gr, 0.0)), 0.0)
              for (hh, d), gc, gr in zip(chains, g_cols, g_rows)]
    lows = [jnp.where(strict[d], b * grams[hh] * dec, 0.0) for (hh, d), b, dec in zip(chains, betas, decays)]
    l_diags = [jnp.where(same_block, low, 0.0) for low in lows]
    l_offs = [(low - ld).astype(BF16) for low, ld in zip(lows, l_diags)]

    def neumann(accs, x_bs, n_steps):
        for _ in range(n_steps):
            x2s = [_dot(x, x) for x in x_bs]
            x_bs = [x2.astype(BF16) for x2 in x2s]
            accs = [a + x2 + _dot(a.astype(BF16), xb) for a, x2, xb in zip(accs, x2s, x_bs)]
        return accs

    d_ms = neumann([-ld for ld in l_diags], [ld.astype(BF16) for ld in l_diags], (DN_BLOCK - 1).bit_length() - 1)
    d_bs = [dm.astype(BF16) for dm in d_ms]
    ms = [lo.astype(F32) + _dot(db, lo) for db, lo in zip(d_bs, l_offs)]
    q_ms = neumann([-m for m in ms], [m.astype(BF16) for m in ms], (c // DN_BLOCK - 1).bit_length() - 1)
    e_cols = [jnp.exp(gc) for gc in g_cols]
    rhss = [jnp.concatenate([vfs[hh] * b, kfs[hh] * b * ec], axis=1) for (hh, d), b, ec in zip(chains, betas, e_cols)]
    r1s = [rhs + _dot(db, rhs.astype(BF16)) for rhs, db in zip(rhss, d_bs)]
    sols = [r1 + _dot(qm.astype(BF16), r1.astype(BF16)) for r1, qm in zip(r1s, q_ms)]
    for i, (hh, d) in enumerate(chains):
        sl = sls[hh]
        u_ref[d, :, sl] = sols[i][:, :hd]
        w_ref[d, :, sl] = sols[i][:, hd:].astype(BF16)
        qd_ref[d, :, sl] = (qfs[hh] * e_cols[i]).astype(BF16)
        kt_ref[d, sl, :] = (kfs[hh] * jnp.exp(totals[i] - g_cols[i])).T.astype(BF16)
        qk_ref[d, hh] = (qks[hh] * decays[i]).astype(BF16)


def dn_chunk(qkv, col, row, heads_per_step=4):
    s = qkv.shape[0]
    c = DN_TILE
    nt = s // c
    hd = DN_HEAD_DIM * heads_per_step
    groups = DN_HEADS // heads_per_step
    return pl.pallas_call(
        functools.partial(_dnchunk_body, heads_per_step=heads_per_step),
        grid=(nt, groups),
        in_specs=[
            pl.BlockSpec((c, hd), lambda t, h: (t, h)),
            pl.BlockSpec((c, hd), lambda t, h: (t, groups + h)),
            pl.BlockSpec((c, hd), lambda t, h: (t, 2 * groups + h)),
            pl.BlockSpec((c, LANES), lambda t, h: (t, 0)),
            pl.BlockSpec((LANES, c), lambda t, h: (0, t)),
        ],
        out_specs=[
            pl.BlockSpec((2, c, hd), lambda t, h: (0, t, h)),
            pl.BlockSpec((2, c, hd), lambda t, h: (0, t, h)),
            pl.BlockSpec((2, c, hd), lambda t, h: (0, t, h)),
            pl.BlockSpec((2, hd, c), lambda t, h: (0, h, t)),
            pl.BlockSpec((2, heads_per_step, c, c), lambda t, h: (0, h, t, 0)),
        ],
        out_shape=[
            jax.ShapeDtypeStruct((2, s, DN_WIDTH), F32),
            jax.ShapeDtypeStruct((2, s, DN_WIDTH), BF16),
            jax.ShapeDtypeStruct((2, s, DN_WIDTH), BF16),
            jax.ShapeDtypeStruct((2, DN_WIDTH, s), BF16),
            jax.ShapeDtypeStruct((2, DN_HEADS, s, c), BF16),
        ],
        compiler_params=_params(("parallel", "parallel")),
        name="dn_chunk",
    )(qkv, qkv, qkv, col, row)


def _dnscan_body(uf, wf, qdf, ktf, qkf, colf, ub, wb, qdb, ktb, qkb, colb, of_ref, ob_ref, st_ref):
    @pl.when(pl.program_id(0) == 0)
    def _():
        st_ref[...] = jnp.zeros(st_ref.shape, F32)

    c = DN_TILE
    hd = DN_HEAD_DIM
    dirs = ((uf, wf, qdf, ktf, qkf, colf, of_ref, c - 1), (ub, wb, qdb, ktb, qkb, colb, ob_ref, 0))
    chains = [(d, h) for d in range(2) for h in range(DN_HEADS)]
    sls = [slice(h * hd, (h + 1) * hd) for h in range(DN_HEADS)]
    e_tots = [jnp.exp(dirs[d][5][dirs[d][7]:dirs[d][7] + 1, :]) for d in range(2)]
    states = [st_ref[d * DN_HEADS + h] for d, h in chains]
    states_b = [st.astype(BF16) for st in states]
    v_news = [dirs[d][0][0, :, sls[h]] - _dot(dirs[d][1][0, :, sls[h]], sb) for (d, h), sb in zip(chains, states_b)]
    v_news_b = [vn.astype(BF16) for vn in v_news]
    outs = [_dot(dirs[d][2][0, :, sls[h]], sb) + _dot(dirs[d][4][0, h], vb)
            for (d, h), sb, vb in zip(chains, states_b, v_news_b)]
    news = [st * e_tots[d][:, 16 + d * DN_HEADS + h:17 + d * DN_HEADS + h] + _dot(dirs[d][3][0, sls[h], :], vb)
            for (d, h), st, vb in zip(chains, states, v_news_b)]
    for (d, h), out, new in zip(chains, outs, news):
        dirs[d][6][:, sls[h]] = out
        st_ref[d * DN_HEADS + h] = new


def dn_scan(u, w, qd, kt, qk, col):
    s = u.shape[1]
    c = DN_TILE
    nt = s // c
    wd = DN_WIDTH

    def specs(d):
        tile = (lambda t: t) if d == 0 else (lambda t: nt - 1 - t)
        return [
            pl.BlockSpec((1, c, wd), lambda t: (d, tile(t), 0)),
            pl.BlockSpec((1, c, wd), lambda t: (d, tile(t), 0)),
            pl.BlockSpec((1, c, wd), lambda t: (d, tile(t), 0)),
            pl.BlockSpec((1, wd, c), lambda t: (d, 0, tile(t))),
            pl.BlockSpec((1, DN_HEADS, c, c), lambda t: (d, 0, tile(t), 0)),
            pl.BlockSpec((c, LANES), lambda t: (tile(t), 0)),
        ]

    return pl.pallas_call(
        _dnscan_body,
        grid=(nt,),
        in_specs=specs(0) + specs(1),
        out_specs=[
            pl.BlockSpec((c, wd), lambda t: (t, 0)),
            pl.BlockSpec((c, wd), lambda t: (nt - 1 - t, 0)),
        ],
        out_shape=[jax.ShapeDtypeStruct((s, wd), F32), jax.ShapeDtypeStruct((s, wd), F32)],
        scratch_shapes=[pltpu.VMEM((2 * DN_HEADS, DN_HEAD_DIM, DN_HEAD_DIM), F32)],
        compiler_params=_params(("arbitrary",)),
        name="dn_scan",
    )(u, w, qd, kt, qk, col, u, w, qd, kt, qk, col)


def _na_bias_table(rpb):
    n_heads, n_off, n_dc = rpb.shape
    return pl.pallas_call(
        functools.partial(_nabias_body, n_off=n_off, n_dc=n_dc),
        grid=(n_heads,),
        in_specs=[pl.BlockSpec(memory_space=pltpu.SMEM)],
        out_specs=pl.BlockSpec((1, n_off - 1, GRID_W, 2 * GRID_W), lambda h: (h, 0, 0, 0)),
        out_shape=jax.ShapeDtypeStruct((n_heads, n_off - 1, GRID_W, 2 * GRID_W), F32),
        compiler_params=_params(("parallel",)),
        name="na_bias",
    )(rpb.astype(F32).reshape(-1))


def _nabias_body(rpb_ref, o_ref, *, n_off, n_dc):
    h = pl.program_id(0)
    c = lax.broadcasted_iota(I32, (GRID_W, 2 * GRID_W), 0)
    lane = lax.broadcasted_iota(I32, (GRID_W, 2 * GRID_W), 1)
    second = lane >= GRID_W
    kc = jnp.where(second, lane - GRID_W, lane)
    cs = jnp.clip(c - NA_COLS // 2, 0, GRID_W - NA_COLS)
    valid = (kc >= cs) & (kc < cs + NA_COLS)
    dc = jnp.where(valid, kc - c + (NA_COLS - 1), -1)
    for m in range(n_off - 1):
        tile = jnp.full((GRID_W, 2 * GRID_W), NEG_BIG, F32)
        for j in range(n_dc):
            first_val = rpb_ref[(h * n_off + m) * n_dc + j]
            second_val = rpb_ref[(h * n_off + m + 1) * n_dc + j]
            tile = jnp.where(dc == j, jnp.where(second, second_val, first_val), tile)
        o_ref[0, m] = tile


def _na_body(q_ref, *refs):
    k_refs = refs[:NA_ROWS]
    v_refs = refs[NA_ROWS:2 * NA_ROWS]
    bias_ref = refs[2 * NA_ROWS]
    o_ref = refs[2 * NA_ROWS + 1]
    lane = lax.broadcasted_iota(I32, (1, LANES), 1)
    heads_per_block = LANES // NA_HEAD_DIM
    scale = NA_HEAD_DIM ** -0.5
    n_pairs = NA_WIDTH // LANES
    heads = [(pair, hh) for pair in range(n_pairs) for hh in range(heads_per_block)]
    sls = [slice(pair * LANES, (pair + 1) * LANES) for pair in range(n_pairs)]
    owns = [(lane // NA_HEAD_DIM) == hh for hh in range(heads_per_block)]
    row_id = pl.program_id(0)
    first_offset = jnp.clip(row_id - NA_ROWS // 2, 0, pl.num_programs(0) - NA_ROWS) - row_id + (NA_ROWS - 1)

    def bias(h):
        return jnp.concatenate([bias_ref[h, first_offset + 2 * m] for m in range(NA_ROWS // 2)], axis=1)

    q2s = [q_ref[:, sl] for sl in sls]
    k2s = [jnp.concatenate([r[:, sl] for r in k_refs], axis=0) for sl in sls]
    scores = [_dot_nt(jnp.where(owns[hh], q2s[pair], jnp.zeros_like(q2s[pair])), k2s[pair]) * scale
              + bias(pair * heads_per_block + hh) for pair, hh in heads]
    maxes = [jnp.max(s, axis=-1, keepdims=True) for s in scores]
    probs = [jnp.exp(s - m) for s, m in zip(scores, maxes)]
    denoms = [jnp.sum(p, axis=-1, keepdims=True) for p in probs]
    v2s = [jnp.concatenate([r[:, sl] for r in v_refs], axis=0) for sl in sls]
    outs = [_dot(p.astype(BF16), jnp.where(owns[hh], v2s[pair], jnp.zeros_like(v2s[pair]))) / den
            for (pair, hh), p, den in zip(heads, probs, denoms)]
    for pair in range(n_pairs):
        acc = outs[pair * heads_per_block]
        for hh in range(1, heads_per_block):
            acc = acc + outs[pair * heads_per_block + hh]
        o_ref[:, sls[pair]] = acc.astype(o_ref.dtype)


def na_attention(proj, col0, rpb):
    s = proj.shape[0]
    rows = s // GRID_W
    assert rows >= NA_ROWS
    qb = col0 // NA_WIDTH
    table = _na_bias_table(rpb)

    def first_row(r):
        return jnp.clip(r - NA_ROWS // 2, 0, rows - NA_ROWS)

    def kv_spec(i, blk):
        return pl.BlockSpec((GRID_W, NA_WIDTH), lambda r: (first_row(r) + i, blk))

    in_specs = [pl.BlockSpec((GRID_W, NA_WIDTH), lambda r: (r, qb))]
    in_specs += [kv_spec(i, qb + 1) for i in range(NA_ROWS)]
    in_specs += [kv_spec(i, qb + 2) for i in range(NA_ROWS)]
    in_specs += [pl.BlockSpec(table.shape, lambda r: (0, 0, 0, 0), pipeline_mode=pl.Buffered(1))]
    return pl.pallas_call(
        _na_body,
        grid=(rows,),
        in_specs=in_specs,
        out_specs=pl.BlockSpec((GRID_W, NA_WIDTH), lambda r: (r, 0)),
        out_shape=jax.ShapeDtypeStruct((s, NA_WIDTH), BF16),
        compiler_params=_params(("parallel",)),
        name="na_attn",
    )(proj, *([proj] * (2 * NA_ROWS)), table)


def _pack_bf16_pair(lo, hi):
    lo_bits = pltpu.bitcast(lo.astype(BF16).astype(F32), U32)
    hi_bits = pltpu.bitcast(hi.astype(BF16).astype(F32), U32)
    return (lo_bits >> 16) | (hi_bits & jnp.uint32(0xFFFF0000))


def _unpack_bf16_pair(packed):
    lo = pltpu.bitcast(packed << 16, F32)
    hi = pltpu.bitcast(packed & jnp.uint32(0xFFFF0000), F32)
    return lo, hi


def _merge_body(of_ref, ob_ref, z_ref, na_ref, ga_ref, gb_ref, x_ref, dng_ref, wa_ref, wb_ref, wo_ref,
                gffn_ref, wr_ref, br_ref, x1_ref, h2p_ref, lg_ref, dn_ref):
    hd = DN_HEAD_DIM
    for h in range(DN_HEADS):
        sl = slice(h * hd, (h + 1) * hd)
        o = of_ref[:, sl] + ob_ref[:, sl]
        r = lax.rsqrt(jnp.mean(o * o, axis=-1, keepdims=True) + RMS_EPS)
        z = z_ref[:, sl].astype(F32)
        dn_ref[:, sl] = (o * r * dng_ref[...] * (z * _sigmoid(z))).astype(BF16)
    y_a = _dot(dn_ref[...], wa_ref[...])
    y_b = _dot(na_ref[...], wb_ref[...])
    mixed = _sigmoid(ga_ref[...].astype(F32)) * y_a + _sigmoid(gb_ref[...].astype(F32)) * y_b
    x1 = x_ref[...] + _dot(mixed.astype(BF16), wo_ref[...])
    x1_ref[...] = x1
    r = lax.rsqrt(jnp.mean(x1 * x1, axis=-1, keepdims=True) + RMS_EPS)
    h2 = x1 * r * gffn_ref[...]
    h2_hi = h2.astype(BF16)
    h2_lo = (h2 - h2_hi.astype(F32)).astype(BF16)
    lg_ref[...] = (_dot(h2_hi, wr_ref[0]) + _dot(h2_lo, wr_ref[0]) + _dot(h2_hi, wr_ref[1])) + br_ref[...]
    half = h2.shape[1] // 2
    n_tiles = half // LANES
    packed = _pack_bf16_pair(h2[:, :half], h2[:, half:])
    for c in range(n_tiles):
        h2p_ref[pl.ds(c, h2.shape[0], stride=n_tiles), :] = packed[:, c * LANES:(c + 1) * LANES]


def merge(o_f, o_b, proj, z_blk, gate_blk, na_out, x, dn_norm_g, w_a, w_b, w_o, norm_ffn_g, w_router, b_router,
          tm=256):
    s, d = x.shape
    ne = w_router.shape[1]
    const = lambda i: (0, 0)
    single = pl.Buffered(1)
    w_router_hi = w_router.astype(BF16)
    w_router_lo = (w_router.astype(F32) - w_router_hi.astype(F32)).astype(BF16)
    w_router_split = jnp.stack([w_router_hi, w_router_lo])
    return pl.pallas_call(
        _merge_body,
        grid=(s // tm,),
        in_specs=[
            pl.BlockSpec((tm, DN_WIDTH), lambda i: (i, 0)),
            pl.BlockSpec((tm, DN_WIDTH), lambda i: (i, 0)),
            pl.BlockSpec((tm, DN_WIDTH), lambda i: (i, z_blk)),
            pl.BlockSpec((tm, NA_WIDTH), lambda i: (i, 0)),
            pl.BlockSpec((tm, d), lambda i: (i, gate_blk)),
            pl.BlockSpec((tm, d), lambda i: (i, gate_blk + 1)),
            pl.BlockSpec((tm, d), lambda i: (i, 0)),
            pl.BlockSpec((1, DN_HEAD_DIM), const),
            pl.BlockSpec((DN_WIDTH, d), const, pipeline_mode=single),
            pl.BlockSpec((NA_WIDTH, d), const, pipeline_mode=single),
            pl.BlockSpec((d, d), const, pipeline_mode=single),
            pl.BlockSpec((1, d), const),
            pl.BlockSpec((2, d, ne), lambda i: (0, 0, 0)),
            pl.BlockSpec((1, ne), const),
        ],
        out_specs=[
            pl.BlockSpec((tm, d), lambda i: (i, 0)),
            pl.BlockSpec((tm * (d // 2 // LANES), LANES), lambda i: (i, 0)),
            pl.BlockSpec((tm, ne), lambda i: (i, 0)),
        ],
        out_shape=[
            jax.ShapeDtypeStruct((s, d), F32),
            jax.ShapeDtypeStruct((s * (d // 2 // LANES), LANES), U32),
            jax.ShapeDtypeStruct((s, ne), F32),
        ],
        scratch_shapes=[pltpu.VMEM((tm, DN_WIDTH), BF16)],
        compiler_params=_params(("parallel",)),
        name="merge",
    )(o_f, o_b, proj, na_out, proj, proj, x, dn_norm_g.reshape(1, -1).astype(F32), w_a, w_b, w_o,
      norm_ffn_g.reshape(1, d).astype(F32), w_router_split, b_router.reshape(1, ne).astype(F32))


def _route_body(lg_ref, ti_ref, tw_ref, cnt_ref, carry_ref):
    @pl.when(pl.program_id(0) == 0)
    def _():
        carry_ref[...] = jnp.zeros(carry_ref.shape, F32)

    lg = lg_ref[...]
    tm, ne = lg.shape
    lane = lax.broadcasted_iota(I32, (tm, ne), 1).astype(F32)
    work = lg
    vals, idxs = [], []
    onehot = jnp.zeros((tm, ne), F32)
    for _ in range(TOP_K):
        m = jnp.max(work, axis=-1, keepdims=True)
        idx = jnp.min(jnp.where(work == m, lane, float(ne)), axis=-1, keepdims=True)
        hit = lane == idx
        vals.append(m)
        idxs.append(idx)
        onehot = onehot + jnp.where(hit, 1.0, 0.0)
        work = jnp.where(hit, -jnp.inf, work)
    exps = [jnp.exp(v - vals[0]) for v in vals]
    denom = exps[0] + exps[1] + exps[2] + exps[3]
    ri = lax.broadcasted_iota(I32, (tm, tm), 0)
    ci = lax.broadcasted_iota(I32, (tm, tm), 1)
    strict = jnp.where(ci < ri, 1.0, 0.0).astype(BF16)
    before = _dot(strict, onehot.astype(BF16)) + carry_ref[0:1, 0:ne]
    lane_o = lax.broadcasted_iota(I32, (tm, LANES), 1)
    ti = jnp.zeros((tm, LANES), I32)
    tw = jnp.zeros((tm, LANES), F32)
    for kk in range(TOP_K):
        rank = jnp.sum(jnp.where(lane == idxs[kk], before, 0.0), axis=-1, keepdims=True).astype(I32)
        ti = jnp.where(lane_o == kk, idxs[kk].astype(I32), ti)
        ti = jnp.where(lane_o == TOP_K + kk, rank, ti)
        tw = jnp.where(lane_o == kk, exps[kk] / denom, tw)
    ti_ref[...] = ti
    tw_ref[...] = tw
    total = carry_ref[0:1, 0:ne] + jnp.sum(onehot, axis=0, keepdims=True)
    carry_ref[0:1, 0:ne] = total
    cnt_ref[...] = jnp.zeros(cnt_ref.shape, F32)
    cnt_ref[0:1, 0:ne] = total


def route(logits, tm=512):
    s, ne = logits.shape
    return pl.pallas_call(
        _route_body,
        grid=(s // tm,),
        in_specs=[pl.BlockSpec((tm, ne), lambda i: (i, 0))],
        out_specs=[
            pl.BlockSpec((tm, LANES), lambda i: (i, 0)),
            pl.BlockSpec((tm, LANES), lambda i: (i, 0)),
            pl.BlockSpec((8, LANES), lambda i: (0, 0)),
        ],
        out_shape=[
            jax.ShapeDtypeStruct((s, LANES), I32),
            jax.ShapeDtypeStruct((s, LANES), F32),
            jax.ShapeDtypeStruct((8, LANES), F32),
        ],
        scratch_shapes=[pltpu.VMEM((8, LANES), F32)],
        compiler_params=_params(("arbitrary",)),
        name="route",
    )(logits)


def _dest_body(ti_ref, ps_ref, d_ref):
    ti = ti_ref[...].astype(F32)
    tm = ti.shape[0]
    lane = lax.broadcasted_iota(I32, (tm, LANES), 1)
    lane_f = lane.astype(F32)
    ps = ps_ref[0:1, :].astype(F32)
    out = jnp.zeros((tm, LANES), F32)
    for kk in range(TOP_K):
        e = jnp.sum(jnp.where(lane == kk, ti, 0.0), axis=-1, keepdims=True)
        rank = jnp.sum(jnp.where(lane == TOP_K + kk, ti, 0.0), axis=-1, keepdims=True)
        start = jnp.sum(jnp.where(lane_f == e, ps, 0.0), axis=-1, keepdims=True)
        out = jnp.where(lane == kk, start + rank, out)
    d_ref[...] = out.astype(I32)


def route_dest(ti, pad_start, tm=512):
    s = ti.shape[0]
    ps = jnp.zeros((8, LANES), I32).at[0, :N_EXPERTS].set(pad_start)
    return pl.pallas_call(
        _dest_body,
        grid=(s // tm,),
        in_specs=[pl.BlockSpec((tm, LANES), lambda i: (i, 0)), pl.BlockSpec((8, LANES), lambda i: (0, 0))],
        out_specs=pl.BlockSpec((tm, LANES), lambda i: (i, 0)),
        out_shape=jax.ShapeDtypeStruct((s, LANES), I32),
        compiler_params=_params(("parallel",)),
        name="route_dest",
    )(ti, ps)


def _scatter_body(pend_ref, padded_ref, dest_ref, h_ref, xr_ref, zero_ref, sem):
    tm = h_ref.shape[0]

    def zero_copy(e):
        return pltpu.make_async_copy(zero_ref, xr_ref.at[pl.ds(pend_ref[e] - MOE_SUB, MOE_SUB)], sem)

    @pl.when(pl.program_id(0) == 0)
    def _():
        zero_ref[...] = jnp.zeros(zero_ref.shape, U32)

        def start(e, carry):
            @pl.when(padded_ref[e] > 0)
            def _():
                zero_copy(e).start()
            return carry

        def wait(e, carry):
            @pl.when(padded_ref[e] > 0)
            def _():
                zero_copy(e).wait()
            return carry

        lax.fori_loop(0, N_EXPERTS, start, 0)
        lax.fori_loop(0, N_EXPERTS, wait, 0)

        def slack_copy(b):
            return pltpu.make_async_copy(zero_ref, xr_ref.at[pl.ds(b * MOE_SUB, MOE_SUB)], sem)

        def slack_start(b, carry):
            slack_copy(b).start()
            return carry

        def slack_wait(b, carry):
            slack_copy(b).wait()
            return carry

        first_slack = pend_ref[N_EXPERTS - 1] // MOE_SUB
        lax.fori_loop(first_slack, xr_ref.shape[0] // MOE_SUB, slack_start, 0)
        lax.fori_loop(first_slack, xr_ref.shape[0] // MOE_SUB, slack_wait, 0)

    def row_copy(r, kk):
        d = dest_ref[r * TOP_K + kk]
        return pltpu.make_async_copy(h_ref.at[r], xr_ref.at[d], sem)

    def start(r, carry):
        for kk in range(TOP_K):
            row_copy(r, kk).start()
        return carry

    lax.fori_loop(0, tm, start, 0)
    all_rows = xr_ref.at[pl.ds(0, tm * TOP_K)]
    pltpu.make_async_copy(all_rows, all_rows, sem).wait()


def moe_scatter(h2p, dest, pad_end, padded, n_rows, tm=256):
    s, nt, _ = h2p.shape
    grid_spec = pltpu.PrefetchScalarGridSpec(
        num_scalar_prefetch=2,
        grid=(s // tm,),
        in_specs=[
            pl.BlockSpec((tm * TOP_K,), lambda i, *_: (i,), memory_space=pltpu.SMEM),
            pl.BlockSpec((tm, nt, LANES), lambda i, *_: (i, 0, 0)),
        ],
        out_specs=pl.BlockSpec(memory_space=pl.ANY),
        scratch_shapes=[pltpu.VMEM((MOE_SUB, nt, LANES), U32), pltpu.SemaphoreType.DMA(())],
    )
    return pl.pallas_call(
        _scatter_body,
        grid_spec=grid_spec,
        out_shape=jax.ShapeDtypeStruct((n_rows, nt, LANES), U32),
        compiler_params=_params(("arbitrary",)),
        name="moe_scatter",
    )(pad_end, padded, dest, h2p)


def _moe_body(ie_ref, ir_ref, inb_ref, ni_ref, xr_ref, wg_ref, wu_ref, wd_ref, bg_ref, bu_ref, bd_ref, y_ref,
              stage_ref, ystage_ref, xb_ref, acc_ref, wgb_ref, wub_ref, wdb_ref, sem_in, sem_out, *, n_chunks):
    w = pl.program_id(0)
    j = pl.program_id(1)
    sb = MOE_SUB
    d = acc_ref.shape[1]
    half = d // 2
    nx = half // LANES
    ny = half // LANES

    @pl.when(w < ni_ref[0])
    def _():
        nb = inb_ref[w]
        r0 = ir_ref[w]

        @pl.when(j == 0)
        def _():
            @pl.when(w == 0)
            def _():
                ystage_ref[0] = jnp.zeros(ystage_ref.shape[1:], U32)

                def slack_copy(b):
                    dst = y_ref.at[pl.ds(pl.multiple_of(b * (sb * ny), sb * ny), sb * ny), :]
                    return pltpu.make_async_copy(ystage_ref.at[0], dst, sem_out.at[0])

                def slack_start(b, carry):
                    slack_copy(b).start()
                    return carry

                def slack_wait(b, carry):
                    slack_copy(b).wait()
                    return carry

                n_blocks = y_ref.shape[0] // (sb * ny)
                lax.fori_loop(ni_ref[1], n_blocks, slack_start, 0)
                lax.fori_loop(ni_ref[1], n_blocks, slack_wait, 0)

            def in_copy(i, slot):
                src = xr_ref.at[pl.ds(pl.multiple_of((r0 + i * sb) * nx, sb * nx), sb * nx), :]
                return pltpu.make_async_copy(src, stage_ref.at[slot], sem_in.at[slot])

            in_copy(0, 0).start()

            def load(i, carry):
                slot = i % 2

                @pl.when(i + 1 < nb)
                def _():
                    in_copy(i + 1, 1 - slot).start()

                in_copy(i, slot).wait()
                rows = pl.ds(pl.multiple_of(i * sb, sb), sb)
                for c in range(nx):
                    lo, hi = _unpack_bf16_pair(stage_ref[slot, pl.ds(c, sb, stride=nx), :])
                    xb_ref[rows, c * LANES:(c + 1) * LANES] = lo.astype(BF16)
                    xb_ref[rows, half + c * LANES:half + (c + 1) * LANES] = hi.astype(BF16)
                return carry

            lax.fori_loop(0, nb, load, 0)

        wgb_ref[...] = wg_ref[0].astype(BF16)
        wub_ref[...] = wu_ref[0].astype(BF16)
        wdb_ref[...] = wd_ref[0].astype(BF16)

        def block(first, n_sub, i, carry):
            rows = pl.ds(pl.multiple_of(i * sb, sb), n_sub * sb)
            xs = xb_ref[rows, :]
            gate = jnp.minimum(_dot(xs, wgb_ref[...]) + bg_ref[0], SWIGLU_LIMIT)
            up = jnp.clip(_dot(xs, wub_ref[...]) + bu_ref[0], -SWIGLU_LIMIT, SWIGLU_LIMIT)
            act = (up + 1.0) * (gate * _sigmoid(SWIGLU_ALPHA * gate))
            contrib = _dot(act.astype(BF16), wdb_ref[...])
            if first:
                acc_ref[rows, :] = contrib
            else:
                acc_ref[rows, :] += contrib
            return carry

        def all_blocks(first):
            def pair(i2, carry):
                return block(first, 2, 2 * i2, carry)

            lax.fori_loop(0, nb // 2, pair, 0)

            @pl.when(nb % 2 == 1)
            def _():
                block(first, 1, nb - 1, 0)

        @pl.when(j == 0)
        def _():
            all_blocks(True)

        @pl.when(j > 0)
        def _():
            all_blocks(False)

        @pl.when(j == n_chunks - 1)
        def _():
            def out_copy(i, slot):
                dst = y_ref.at[pl.ds(pl.multiple_of((r0 + i * sb) * ny, sb * ny), sb * ny), :]
                return pltpu.make_async_copy(ystage_ref.at[slot], dst, sem_out.at[slot])

            def store(i, carry):
                slot = i % 2

                @pl.when(i >= 2)
                def _():
                    out_copy(i - 2, slot).wait()

                rows = pl.ds(pl.multiple_of(i * sb, sb), sb)
                for c in range(ny):
                    lo = slice(c * LANES, (c + 1) * LANES)
                    hi = slice(half + c * LANES, half + (c + 1) * LANES)
                    ystage_ref[slot, pl.ds(c, sb, stride=ny), :] = _pack_bf16_pair(
                        acc_ref[rows, lo] + bd_ref[0, :, lo], acc_ref[rows, hi] + bd_ref[0, :, hi])
                out_copy(i, slot).start()
                return carry

            lax.fori_loop(0, nb, store, 0)

            @pl.when(nb >= 2)
            def _():
                out_copy(nb - 2, nb % 2).wait()

            out_copy(nb - 1, (nb - 1) % 2).wait()


def moe_ffn(x_rows, w_gate_up, b_gate_up, w_down, b_down, item_e, item_row, item_nb, n_items, max_items):
    ne, d, two_de = w_gate_up.shape
    nx = d // 2 // LANES
    ny = d // 2 // LANES
    n_rows = x_rows.shape[0] // nx
    de = two_de // 2
    tn = MOE_TN
    n_chunks = de // tn
    last = n_chunks - 1

    def chunk(w, j, ni):
        return jnp.where(w < ni[0], j, last)

    grid_spec = pltpu.PrefetchScalarGridSpec(
        num_scalar_prefetch=4,
        grid=(max_items, n_chunks),
        in_specs=[
            pl.BlockSpec(memory_space=pl.ANY),
            pl.BlockSpec((1, d, tn), lambda w, j, ie, ir, inb, ni: (ie[w], 0, chunk(w, j, ni))),
            pl.BlockSpec((1, d, tn), lambda w, j, ie, ir, inb, ni: (ie[w], 0, n_chunks + chunk(w, j, ni))),
            pl.BlockSpec((1, tn, d), lambda w, j, ie, ir, inb, ni: (ie[w], chunk(w, j, ni), 0)),
            pl.BlockSpec((1, 1, tn), lambda w, j, ie, ir, inb, ni: (ie[w], 0, chunk(w, j, ni))),
            pl.BlockSpec((1, 1, tn), lambda w, j, ie, ir, inb, ni: (ie[w], 0, n_chunks + chunk(w, j, ni))),
            pl.BlockSpec((1, 1, d), lambda w, j, ie, ir, inb, ni: (ie[w], 0, 0)),
        ],
        out_specs=pl.BlockSpec(memory_space=pl.ANY),
        scratch_shapes=[
            pltpu.VMEM((2, MOE_SUB * nx, LANES), U32),
            pltpu.VMEM((2, MOE_SUB * ny, LANES), U32),
            pltpu.VMEM((MOE_TM, d), BF16),
            pltpu.VMEM((MOE_TM, d), F32),
            pltpu.VMEM((d, tn), BF16),
            pltpu.VMEM((d, tn), BF16),
            pltpu.VMEM((tn, d), BF16),
            pltpu.SemaphoreType.DMA((2,)),
            pltpu.SemaphoreType.DMA((2,)),
        ],
    )
    return pl.pallas_call(
        functools.partial(_moe_body, n_chunks=n_chunks),
        grid_spec=grid_spec,
        out_shape=jax.ShapeDtypeStruct((n_rows * ny, LANES), U32),
        compiler_params=_params(("arbitrary", "arbitrary")),
        name="moe_ffn",
    )(item_e, item_row, item_nb, n_items, x_rows, w_gate_up, w_gate_up, w_down,
      b_gate_up.reshape(ne, 1, two_de), b_gate_up.reshape(ne, 1, two_de), b_down.reshape(ne, 1, d))


def _final_body(dest_ref, dest_next_ref, x1_ref, tw_ref, p_ref, gple_ref, wg_ref, wp_ref, gfin_ref, y_ref, o_ref,
                ybuf_ref, sem, *, last_layer):
    i = pl.program_id(0)
    tm = x1_ref.shape[0]
    half = x1_ref.shape[1] // 2
    ny = half // LANES
    slot = i % 2

    def row_copy(d_ref, to_slot, r, kk):
        dst = ybuf_ref.at[to_slot, kk, pl.ds(pl.multiple_of(r * ny, ny), ny), :]
        return pltpu.make_async_copy(y_ref.at[d_ref[r * TOP_K + kk]], dst, sem.at[to_slot])

    def issue(d_ref, to_slot):
        def body(r, carry):
            for kk in range(TOP_K):
                row_copy(d_ref, to_slot, r, kk).start()
            return carry

        lax.fori_loop(0, tm, body, 0)

    def drain(to_slot):
        pltpu.make_async_copy(ybuf_ref.at[to_slot], ybuf_ref.at[to_slot], sem.at[to_slot]).wait()

    @pl.when(i == 0)
    def _():
        issue(dest_ref, 0)

    @pl.when(i + 1 < pl.num_programs(0))
    def _():
        issue(dest_next_ref, 1 - slot)

    drain(slot)

    tw = tw_ref[...]
    los, his = [], []
    for c in range(ny):
        acc_lo = x1_ref[:, c * LANES:(c + 1) * LANES]
        acc_hi = x1_ref[:, half + c * LANES:half + (c + 1) * LANES]
        for kk in range(TOP_K):
            lo, hi = _unpack_bf16_pair(ybuf_ref[slot, kk, pl.ds(c, tm, stride=ny), :])
            acc_lo = acc_lo + tw[:, kk:kk + 1] * lo
            acc_hi = acc_hi + tw[:, kk:kk + 1] * hi
        los.append(acc_lo)
        his.append(acc_hi)
    x2 = jnp.concatenate(los + his, axis=1)
    r = lax.rsqrt(jnp.mean(x2 * x2, axis=-1, keepdims=True) + RMS_EPS)
    n = (x2 * r * gple_ref[...]).astype(BF16)
    gate = _sigmoid(_dot(n, wg_ref[...]))
    x3 = x2 + gate * _dot(p_ref[...].astype(BF16), wp_ref[...])
    if last_layer:
        r = lax.rsqrt(jnp.mean(x3 * x3, axis=-1, keepdims=True) + RMS_EPS)
        x3 = x3 * r * gfin_ref[...]
    o_ref[...] = x3


def final(dest, x1, tw, p, norm_ple_g, w_gate, w_proj, norm_final_g, y_rows, last_layer, tm=256):
    s, d = x1.shape
    pd = p.shape[1]
    ny = d // 2 // LANES
    n_steps = s // tm
    const = lambda i: (0, 0)
    single = pl.Buffered(1)
    return pl.pallas_call(
        functools.partial(_final_body, last_layer=last_layer),
        grid=(n_steps,),
        in_specs=[
            pl.BlockSpec((tm * TOP_K,), lambda i: (i,), memory_space=pltpu.SMEM),
            pl.BlockSpec((tm * TOP_K,), lambda i: (jnp.minimum(i + 1, n_steps - 1),), memory_space=pltpu.SMEM),
            pl.BlockSpec((tm, d), lambda i: (i, 0)),
            pl.BlockSpec((tm, LANES), lambda i: (i, 0)),
            pl.BlockSpec((tm, pd), lambda i: (i, 0)),
            pl.BlockSpec((1, d), const),
            pl.BlockSpec((d, d), const, pipeline_mode=single),
            pl.BlockSpec((pd, d), const, pipeline_mode=single),
            pl.BlockSpec((1, d), const),
            pl.BlockSpec(memory_space=pl.ANY),
        ],
        out_specs=pl.BlockSpec((tm, d), lambda i: (i, 0)),
        out_shape=jax.ShapeDtypeStruct((s, d), F32),
        scratch_shapes=[pltpu.VMEM((2, TOP_K, tm * ny, LANES), U32), pltpu.SemaphoreType.DMA((2,))],
        compiler_params=_params(("arbitrary",)),
        name="final",
    )(dest, dest, x1, tw, p, norm_ple_g.reshape(1, d).astype(F32), w_gate, w_proj,
      norm_final_g.reshape(1, d).astype(F32), y_rows.reshape(-1, ny, LANES))


def _moe_tables(counts, n_rows):
    sub, tm = MOE_SUB, MOE_TM
    max_items = N_EXPERTS + n_rows // tm
    padded = (counts + sub - 1) // sub * sub
    pad_end = jnp.cumsum(padded)
    pad_start = pad_end - padded
    n_it = (padded + tm - 1) // tm
    it_end = jnp.cumsum(n_it)
    it_start = it_end - n_it
    n_items = it_end[-1]
    w = jnp.arange(max_items, dtype=I32)
    live = w < n_items
    w_eff = jnp.minimum(w, n_items - 1)
    e_w = jnp.minimum(jnp.searchsorted(it_end, w_eff, side="right"), N_EXPERTS - 1).astype(I32)
    m_w = w_eff - it_start[e_w]
    row_w = pad_start[e_w] + m_w * tm
    nb_w = jnp.clip((padded[e_w] - m_w * tm) // sub, 0, tm // sub)
    nb_w = jnp.where(live, nb_w, 0)
    as_i32 = lambda a: a.astype(I32)
    counts_w = jnp.stack([n_items, pad_end[-1] // sub])
    return (as_i32(pad_start), as_i32(pad_end), as_i32(padded), e_w, as_i32(row_w), as_i32(nb_w),
            as_i32(counts_w), max_items)


def _layer(x, p, norm_mix_g, w_in, dn_conv_w, dn_a_log, dn_dt_bias, dn_norm_g, na_rpb, w_branch_a, w_branch_b,
           w_out, norm_ffn_g, w_router, b_router, w_gate_up, b_gate_up, w_down, b_down, norm_ple_g,
           w_ple_gate, w_ple_proj, norm_final_g, last_layer):
    s, d = x.shape
    c_qkv, c_z = 3 * DN_WIDTH, DN_WIDTH
    w_t = jnp.swapaxes(w_in, 0, 1)
    w_main = w_prep(w_t)
    z_blk = c_qkv // DN_WIDTH
    gate_blk = (c_qkv + c_z) // d
    na_col0 = c_qkv + c_z + 2 * d

    proj, small = in_projection(x, norm_mix_g.astype(F32), w_main, w_t)

    qkv = dn_prep(proj, dn_conv_w.astype(F32))
    col, row = dn_gates(small, dn_a_log, dn_dt_bias)
    u, w, qd, kt, qk = dn_chunk(qkv, col, row)
    o_f, o_b = dn_scan(u, w, qd, kt, qk, col)

    na_out = na_attention(proj, na_col0, na_rpb)

    x1, h2p, logits = merge(o_f, o_b, proj, z_blk, gate_blk, na_out, x, dn_norm_g, w_branch_a.astype(BF16),
                            w_branch_b.astype(BF16), w_out.astype(BF16), norm_ffn_g, w_router, b_router)

    ti, tw, cnt = route(logits)
    counts = cnt[0, :N_EXPERTS].astype(I32)
    n_rows = (s * TOP_K + N_EXPERTS * (MOE_SUB - 1) + MOE_SUB - 1) // MOE_SUB * MOE_SUB
    pad_start, pad_end, padded, item_e, item_row, item_nb, n_items, max_items = _moe_tables(counts, n_rows)
    dest = route_dest(ti, pad_start)[:, :TOP_K].reshape(-1)

    nx = d // 2 // LANES
    x_rows = moe_scatter(h2p.reshape(s, nx, LANES), dest, pad_end, padded, n_rows).reshape(n_rows * nx, LANES)
    y_rows = moe_ffn(x_rows, w_gate_up, b_gate_up, w_down, b_down, item_e, item_row, item_nb, n_items, max_items)

    return final(dest, x1, tw, p, norm_ple_g, w_ple_gate.astype(BF16), w_ple_proj.astype(BF16), norm_final_g,
                 y_rows, last_layer)


def kernel(x, p, norm_mix_g, w_in, dn_conv_w, dn_a_log, dn_dt_bias, dn_norm_g, na_rpb, w_branch_a, w_branch_b, w_out, norm_ffn_g, w_router, b_router, w_gate_up, b_gate_up, w_down, b_down, norm_ple_g, w_ple_gate, w_ple_proj, norm_final_g):
    bsz, s, d = x.shape
    depth = w_in.shape[0]
    outs = []
    for b in range(bsz):
        xb = x[b]
        for i in range(depth):
            xb = _layer(xb, p[i, b], norm_mix_g[i], w_in[i], dn_conv_w[i], dn_a_log[i], dn_dt_bias[i], dn_norm_g[i],
                        na_rpb[i], w_branch_a[i], w_branch_b[i], w_out[i], norm_ffn_g[i], w_router[i], b_router[i],
                        w_gate_up[i], b_gate_up[i], w_down[i], b_down[i], norm_ple_g[i], w_ple_gate[i],
                        w_ple_proj[i], norm_final_g, i == depth - 1)
        outs.append(xb)
    return jnp.stack(outs, axis=0)
```

```python
import functools

import jax
import jax.numpy as jnp
import numpy as np
from jax import lax
from jax.experimental import pallas as pl
from jax.experimental.pallas import tpu as pltpu

F32 = jnp.float32
BF16 = jnp.bfloat16
I32 = jnp.int32
U32 = jnp.uint32

GRID_W = 64
DN_HEADS = 8
DN_HEAD_DIM = 128
DN_WIDTH = DN_HEADS * DN_HEAD_DIM
DN_CONV = 5
NA_HEADS = 16
NA_HEAD_DIM = 64
NA_WIDTH = NA_HEADS * NA_HEAD_DIM
NA_ROWS = 8
NA_COLS = 16
N_EXPERTS = 32
TOP_K = 4
SWIGLU_LIMIT = 7.0
SWIGLU_ALPHA = 1.702
RMS_EPS = 1e-6

LANES = 128
VMEM_LIMIT = 56 * 1024 * 1024

DN_TILE = 256
DN_BLOCK = 16
MOE_SUB = 256
MOE_TM = 1536
MOE_TN = 512
NEG_BIG = -1e30


def _sigmoid(x):
    return 1.0 / (1.0 + jnp.exp(-x))


def _dot(a, b):
    return jnp.dot(a, b, preferred_element_type=F32)


def _dot_nt(a, b):
    return lax.dot_general(a, b, (((1,), (1,)), ((), ())), preferred_element_type=F32)


def _params(sem, limit=VMEM_LIMIT):
    return pltpu.CompilerParams(dimension_semantics=sem, vmem_limit_bytes=limit)


def _wprep_body(src_ref, o_ref, buf_ref, sem, *, starts, tn):
    t = pl.program_id(0)
    slot = t % 2

    def tile_copy(tt, to_slot):
        start = jnp.int32(starts[0])
        for k in range(1, len(starts)):
            start = jnp.where(tt >= k, starts[k], start)
        src = src_ref.at[pl.ds(pl.multiple_of(start, 8), tn), :]
        return pltpu.make_async_copy(src, buf_ref.at[to_slot], sem.at[to_slot])

    @pl.when(t == 0)
    def _():
        tile_copy(0, 0).start()

    @pl.when(t + 1 < pl.num_programs(0))
    def _():
        tile_copy(t + 1, 1 - slot).start()

    tile_copy(t, slot).wait()
    o_ref[...] = buf_ref[slot].T.astype(o_ref.dtype)


def w_prep(w_t, tn=1024):
    n_in, d = w_t.shape
    c_small = 4 * DN_HEADS
    o_small = 4 * DN_WIDTH
    o_na = o_small + c_small
    o_gates = o_na + 3 * NA_WIDTH
    assert o_small % tn == 0 and (3 * NA_WIDTH) % tn == 0 and (n_in - o_gates) % tn == 0
    assert o_na % 8 == 0 and o_gates % 8 == 0
    starts = list(range(0, o_small, tn)) + list(range(o_gates, n_in, tn)) + list(range(o_na, o_gates, tn))
    return pl.pallas_call(
        functools.partial(_wprep_body, starts=tuple(starts), tn=tn),
        grid=(len(starts),),
        in_specs=[pl.BlockSpec(memory_space=pl.ANY)],
        out_specs=pl.BlockSpec((d, tn), lambda t: (0, t)),
        out_shape=jax.ShapeDtypeStruct((d, len(starts) * tn), BF16),
        scratch_shapes=[pltpu.VMEM((2, tn, d), F32), pltpu.SemaphoreType.DMA((2,))],
        compiler_params=_params(("arbitrary",)),
        name="w_prep",
    )(w_t)


def _inproj_body(x_ref, g_ref, w_ref, ws_ref, o_ref, os_ref, h_ref):
    @pl.when(pl.program_id(1) == 0)
    def _():
        x = x_ref[...]
        r = lax.rsqrt(jnp.mean(x * x, axis=-1, keepdims=True) + RMS_EPS)
        h = (x * r * g_ref[...]).astype(BF16)
        h_ref[...] = h
        os_ref[...] = _dot_nt(h, ws_ref[...].astype(BF16))

    o_ref[...] = _dot(h_ref[...], w_ref[...]).astype(o_ref.dtype)


def in_projection(x, g, w_main, w_t, tm=512, tn=1024):
    s, d = x.shape
    n = w_main.shape[1]
    small_blk = 4 * DN_WIDTH // LANES
    return pl.pallas_call(
        _inproj_body,
        grid=(s // tm, n // tn),
        in_specs=[
            pl.BlockSpec((tm, d), lambda i, j: (i, 0)),
            pl.BlockSpec((1, d), lambda i, j: (0, 0)),
            pl.BlockSpec((d, tn), lambda i, j: (0, j)),
            pl.BlockSpec((LANES, d), lambda i, j: (small_blk, 0)),
        ],
        out_specs=[
            pl.BlockSpec((tm, tn), lambda i, j: (i, j)),
            pl.BlockSpec((tm, LANES), lambda i, j: (i, 0)),
        ],
        out_shape=[jax.ShapeDtypeStruct((s, n), BF16), jax.ShapeDtypeStruct((s, LANES), F32)],
        scratch_shapes=[pltpu.VMEM((tm, d), BF16)],
        compiler_params=_params(("parallel", "arbitrary")),
        name="in_proj",
    )(x, g.reshape(1, d), w_main, w_t)


def _dnprep_body(x_ref, w_ref, o_ref, pad_ref, *, seq, chunk):
    cb = pl.program_id(0)
    n_chunks = seq // chunk
    zeros = jnp.zeros((16, LANES), F32)
    pad_ref[0:16, :] = zeros
    pad_ref[seq + 16:seq + 32, :] = zeros

    def fill(c, carry):
        r0 = pl.multiple_of(c * chunk, chunk)
        pad_ref[pl.ds(r0 + 16, chunk), :] = x_ref[pl.ds(r0, chunk), :].astype(F32)
        return carry

    lax.fori_loop(0, n_chunks, fill, 0)

    w = w_ref[...]
    is_v = cb >= 2 * DN_HEADS
    scale = jnp.where(cb < DN_HEADS, DN_HEAD_DIM ** -0.5, 1.0).astype(F32)

    def body(c, carry):
        r0 = pl.multiple_of(c * chunk, chunk)
        y = pad_ref[pl.ds(r0 + 14, chunk), :] * w[0:1]
        for j in range(1, DN_CONV):
            y = y + pad_ref[pl.ds(r0 + 14 + j, chunk), :] * w[j:j + 1]
        y = y * _sigmoid(y)
        ss = jnp.sum(y * y, axis=-1, keepdims=True)
        yn = y * (lax.rsqrt(ss + 1e-6) * scale)
        o_ref[pl.ds(r0, chunk), :] = jnp.where(is_v, y, yn).astype(o_ref.dtype)
        return carry

    lax.fori_loop(0, n_chunks, body, 0)


def dn_prep(proj, conv_w, chunk=512):
    s = proj.shape[0]
    nb = 3 * DN_WIDTH // LANES
    w = jnp.zeros((8, 3 * DN_WIDTH), F32).at[:DN_CONV].set(conv_w)
    return pl.pallas_call(
        functools.partial(_dnprep_body, seq=s, chunk=chunk),
        grid=(nb,),
        in_specs=[
            pl.BlockSpec((s, LANES), lambda c: (0, c)),
            pl.BlockSpec((8, LANES), lambda c: (0, c)),
        ],
        out_specs=pl.BlockSpec((s, LANES), lambda c: (0, c)),
        out_shape=jax.ShapeDtypeStruct((s, 3 * DN_WIDTH), BF16),
        scratch_shapes=[pltpu.VMEM((s + 32, LANES), F32)],
        compiler_params=_params(("parallel",)),
        name="dn_prep",
    )(proj, w)


def _gates_body(s_ref, par_ref, col_ref, row_ref):
    x = s_ref[...]
    t = x.shape[0]
    lane = lax.broadcasted_iota(I32, x.shape, 1)
    beta = _sigmoid(x)
    z = x + par_ref[1:2, :]
    softplus = jnp.maximum(z, 0.0) + jnp.log(1.0 + jnp.exp(-jnp.abs(z)))
    g = par_ref[0:1, :] * softplus
    ri = lax.broadcasted_iota(I32, (t, t), 0)
    ci = lax.broadcasted_iota(I32, (t, t), 1)
    lower = jnp.where(ci <= ri, 1.0, 0.0).astype(BF16)
    upper = jnp.where(ci >= ri, 1.0, 0.0).astype(BF16)
    g1 = g.astype(BF16)
    g2 = (g - g1.astype(F32)).astype(BF16)
    g3 = (g - g1.astype(F32) - g2.astype(F32)).astype(BF16)
    g_prefix = _dot(lower, g1) + _dot(lower, g2) + _dot(lower, g3)
    g_suffix = _dot(upper, g1) + _dot(upper, g2) + _dot(upper, g3)
    cum = jnp.where(lane < 16 + DN_HEADS, g_prefix, g_suffix)
    out = jnp.where(lane < 16, beta, jnp.where(lane < 32, cum, 0.0))
    col_ref[...] = out
    row_ref[...] = out.T


def dn_gates(small, a_log, dt_bias):
    s = small.shape[0]
    par = jnp.zeros((8, LANES), F32)
    par = par.at[0, 16:32].set(-jnp.exp(a_log.reshape(-1).astype(F32)))
    par = par.at[1, 16:32].set(dt_bias.reshape(-1).astype(F32))
    t = DN_TILE
    return pl.pallas_call(
        _gates_body,
        grid=(s // t,),
        in_specs=[
            pl.BlockSpec((t, LANES), lambda i: (i, 0)),
            pl.BlockSpec((8, LANES), lambda i: (0, 0)),
        ],
        out_specs=[
            pl.BlockSpec((t, LANES), lambda i: (i, 0)),
            pl.BlockSpec((LANES, t), lambda i: (0, i)),
        ],
        out_shape=[jax.ShapeDtypeStruct((s, LANES), F32), jax.ShapeDtypeStruct((LANES, s), F32)],
        compiler_params=_params(("parallel",)),
        name="dn_gates",
    )(small, par)


def _dnchunk_body(q_ref, k_ref, v_ref, col_ref, row_ref, u_ref, w_ref, qd_ref, kt_ref, qk_ref, *, heads_per_step):
    c = DN_TILE
    hd = DN_HEAD_DIM
    head0 = pl.program_id(1) * heads_per_step
    col = col_ref[...]
    row = row_ref[...]
    lane = lax.broadcasted_iota(I32, col.shape, 1)
    sub = lax.broadcasted_iota(I32, row.shape, 0)

    def col_pick(idx):
        return jnp.sum(jnp.where(lane == idx, col, 0.0), axis=1, keepdims=True)

    def row_pick(idx):
        return jnp.sum(jnp.where(sub == idx, row, 0.0), axis=0, keepdims=True)

    ri = lax.broadcasted_iota(I32, (c, c), 0)
    ci = lax.broadcasted_iota(I32, (c, c), 1)
    same_block = (ri // DN_BLOCK) == (ci // DN_BLOCK)
    incl = (ri >= ci, ri <= ci)
    strict = (ri > ci, ri < ci)
    heads = range(heads_per_step)
    chains = [(hh, d) for hh in heads for d in range(2)]
    sls = [slice(hh * hd, (hh + 1) * hd) for hh in heads]
    qs = [q_ref[:, sl] for sl in sls]
    ks = [k_ref[:, sl] for sl in sls]
    vs = [v_ref[:, sl] for sl in sls]
    grams = [_dot_nt(k, k) for k in ks]
    qks = [_dot_nt(q, k) for q, k in zip(qs, ks)]
    qfs = [q.astype(F32) for q in qs]
    kfs = [k.astype(F32) for k in ks]
    vfs = [v.astype(F32) for v in vs]

    betas = [col_pick(d * DN_HEADS + head0 + hh) for hh, d in chains]
    g_cols = [col_pick(16 + d * DN_HEADS + head0 + hh) for hh, d in chains]
    g_rows = [row_pick(16 + d * DN_HEADS + head0 + hh) for hh, d in chains]
    totals = [gr[:, c - 1:c] if d == 0 else gr[:, 0:1] for (hh, d), gr in zip(chains, g_rows)]
    decays = [jnp.where(incl[d], jnp.exp(jnp.minimum(gc - gr, 0.0)), 0.0)
              for (hh, d), gc, gr in zip(chains, g_cols, g_rows)]
    lows = [jnp.where(strict[d], b * grams[hh] * dec, 0.0) for (hh, d), b, dec in zip(chains, betas, decays)]
    l_diags = [jnp.where(same_block, low, 0.0) for low in lows]
    l_offs = [(low - ld).astype(BF16) for low, ld in zip(lows, l_diags)]

    def neumann(accs, x_bs, n_steps):
        for _ in range(n_steps):
            x2s = [_dot(x, x) for x in x_bs]
            x_bs = [x2.astype(BF16) for x2 in x2s]
            accs = [a + x2 + _dot(a.astype(BF16), xb) for a, x2, xb in zip(accs, x2s, x_bs)]
        return accs

    d_ms = neumann([-ld for ld in l_diags], [ld.astype(BF16) for ld in l_diags], (DN_BLOCK - 1).bit_length() - 1)
    d_bs = [dm.astype(BF16) for dm in d_ms]
    ms = [lo.astype(F32) + _dot(db, lo) for db, lo in zip(d_bs, l_offs)]
    q_ms = neumann([-m for m in ms], [m.astype(BF16) for m in ms], (c // DN_BLOCK - 1).bit_length() - 1)
    e_cols = [jnp.exp(gc) for gc in g_cols]
    rhss = [jnp.concatenate([vfs[hh] * b, kfs[hh] * b * ec], axis=1) for (hh, d), b, ec in zip(chains, betas, e_cols)]
    r1s = [rhs + _dot(db, rhs.astype(BF16)) for rhs, db in zip(rhss, d_bs)]
    sols = [r1 + _dot(qm.astype(BF16), r1.astype(BF16)) for r1, qm in zip(r1s, q_ms)]
    for i, (hh, d) in enumerate(chains):
        sl = sls[hh]
        u_ref[d, :, sl] = sols[i][:, :hd].astype(BF16)
        w_ref[d, :, sl] = sols[i][:, hd:].astype(BF16)
        qd_ref[d, :, sl] = (qfs[hh] * e_cols[i]).astype(BF16)
        kt_ref[d, sl, :] = (kfs[hh] * jnp.exp(totals[i] - g_cols[i])).T.astype(BF16)
        qk_ref[d, hh] = (qks[hh] * decays[i]).astype(BF16)


def dn_chunk(qkv, col, row, heads_per_step=4):
    s = qkv.shape[0]
    c = DN_TILE
    nt = s // c
    hd = DN_HEAD_DIM * heads_per_step
    groups = DN_HEADS // heads_per_step
    return pl.pallas_call(
        functools.partial(_dnchunk_body, heads_per_step=heads_per_step),
        grid=(nt, groups),
        in_specs=[
            pl.BlockSpec((c, hd), lambda t, h: (t, h)),
            pl.BlockSpec((c, hd), lambda t, h: (t, groups + h)),
            pl.BlockSpec((c, hd), lambda t, h: (t, 2 * groups + h)),
            pl.BlockSpec((c, LANES), lambda t, h: (t, 0)),
            pl.BlockSpec((LANES, c), lambda t, h: (0, t)),
        ],
        out_specs=[
            pl.BlockSpec((2, c, hd), lambda t, h: (0, t, h)),
            pl.BlockSpec((2, c, hd), lambda t, h: (0, t, h)),
            pl.BlockSpec((2, c, hd), lambda t, h: (0, t, h)),
            pl.BlockSpec((2, hd, c), lambda t, h: (0, h, t)),
            pl.BlockSpec((2, heads_per_step, c, c), lambda t, h: (0, h, t, 0)),
        ],
        out_shape=[
            jax.ShapeDtypeStruct((2, s, DN_WIDTH), BF16),
            jax.ShapeDtypeStruct((2, s, DN_WIDTH), BF16),
            jax.ShapeDtypeStruct((2, s, DN_WIDTH), BF16),
            jax.ShapeDtypeStruct((2, DN_WIDTH, s), BF16),
            jax.ShapeDtypeStruct((2, DN_HEADS, s, c), BF16),
        ],
        compiler_params=_params(("parallel", "parallel")),
        name="dn_chunk",
    )(qkv, qkv, qkv, col, row)


def _dnscan_body(uf, wf, qdf, ktf, qkf, colf, ub, wb, qdb, ktb, qkb, colb, of_ref, ob_ref, st_ref):
    @pl.when(pl.program_id(0) == 0)
    def _():
        st_ref[...] = jnp.zeros(st_ref.shape, F32)

    c = DN_TILE
    hd = DN_HEAD_DIM
    dirs = ((uf, wf, qdf, ktf, qkf, colf, of_ref, c - 1), (ub, wb, qdb, ktb, qkb, colb, ob_ref, 0))
    chains = [(d, h) for d in range(2) for h in range(DN_HEADS)]
    sls = [slice(h * hd, (h + 1) * hd) for h in range(DN_HEADS)]
    e_tots = [jnp.exp(dirs[d][5][dirs[d][7]:dirs[d][7] + 1, :]) for d in range(2)]
    states = [st_ref[d * DN_HEADS + h] for d, h in chains]
    states_b = [st.astype(BF16) for st in states]
    v_news = [dirs[d][0][0, :, sls[h]] - _dot(dirs[d][1][0, :, sls[h]], sb) for (d, h), sb in zip(chains, states_b)]
    v_news_b = [vn.astype(BF16) for vn in v_news]
    outs = [_dot(dirs[d][2][0, :, sls[h]], sb) + _dot(dirs[d][4][0, h], vb)
            for (d, h), sb, vb in zip(chains, states_b, v_news_b)]
    news = [st * e_tots[d][:, 16 + d * DN_HEADS + h:17 + d * DN_HEADS + h] + _dot(dirs[d][3][0, sls[h], :], vb)
            for (d, h), st, vb in zip(chains, states, v_news_b)]
    for (d, h), out, new in zip(chains, outs, news):
        dirs[d][6][:, sls[h]] = out
        st_ref[d * DN_HEADS + h] = new


def dn_scan(u, w, qd, kt, qk, col):
    s = u.shape[1]
    c = DN_TILE
    nt = s // c
    wd = DN_WIDTH

    def specs(d):
        tile = (lambda t: t) if d == 0 else (lambda t: nt - 1 - t)
        return [
            pl.BlockSpec((1, c, wd), lambda t: (d, tile(t), 0)),
            pl.BlockSpec((1, c, wd), lambda t: (d, tile(t), 0)),
            pl.BlockSpec((1, c, wd), lambda t: (d, tile(t), 0)),
            pl.BlockSpec((1, wd, c), lambda t: (d, 0, tile(t))),
            pl.BlockSpec((1, DN_HEADS, c, c), lambda t: (d, 0, tile(t), 0)),
            pl.BlockSpec((c, LANES), lambda t: (tile(t), 0)),
        ]

    return pl.pallas_call(
        _dnscan_body,
        grid=(nt,),
        in_specs=specs(0) + specs(1),
        out_specs=[
            pl.BlockSpec((c, wd), lambda t: (t, 0)),
            pl.BlockSpec((c, wd), lambda t: (nt - 1 - t, 0)),
        ],
        out_shape=[jax.ShapeDtypeStruct((s, wd), F32), jax.ShapeDtypeStruct((s, wd), F32)],
        scratch_shapes=[pltpu.VMEM((2 * DN_HEADS, DN_HEAD_DIM, DN_HEAD_DIM), F32)],
        compiler_params=_params(("arbitrary",)),
        name="dn_scan",
    )(u, w, qd, kt, qk, col, u, w, qd, kt, qk, col)


def _na_bias_table(rpb):
    n_heads, n_off, n_dc = rpb.shape
    return pl.pallas_call(
        functools.partial(_nabias_body, n_off=n_off, n_dc=n_dc),
        grid=(n_heads,),
        in_specs=[pl.BlockSpec(memory_space=pltpu.SMEM)],
        out_specs=pl.BlockSpec((1, n_off - 1, GRID_W, 2 * GRID_W), lambda h: (h, 0, 0, 0)),
        out_shape=jax.ShapeDtypeStruct((n_heads, n_off - 1, GRID_W, 2 * GRID_W), F32),
        compiler_params=_params(("parallel",)),
        name="na_bias",
    )(rpb.astype(F32).reshape(-1))


def _nabias_body(rpb_ref, o_ref, *, n_off, n_dc):
    h = pl.program_id(0)
    c = lax.broadcasted_iota(I32, (GRID_W, 2 * GRID_W), 0)
    lane = lax.broadcasted_iota(I32, (GRID_W, 2 * GRID_W), 1)
    second = lane >= GRID_W
    kc = jnp.where(second, lane - GRID_W, lane)
    cs = jnp.clip(c - NA_COLS // 2, 0, GRID_W - NA_COLS)
    valid = (kc >= cs) & (kc < cs + NA_COLS)
    dc = jnp.where(valid, kc - c + (NA_COLS - 1), -1)
    for m in range(n_off - 1):
        tile = jnp.full((GRID_W, 2 * GRID_W), NEG_BIG, F32)
        for j in range(n_dc):
            first_val = rpb_ref[(h * n_off + m) * n_dc + j]
            second_val = rpb_ref[(h * n_off + m + 1) * n_dc + j]
            tile = jnp.where(dc == j, jnp.where(second, second_val, first_val), tile)
        o_ref[0, m] = tile


def _na_body(q_ref, k_ref, v_ref, bias_ref, o_ref):
    lane = lax.broadcasted_iota(I32, (1, LANES), 1)
    heads_per_block = LANES // NA_HEAD_DIM
    scale = NA_HEAD_DIM ** -0.5
    n_pairs = NA_WIDTH // LANES
    heads = [(pair, hh) for pair in range(n_pairs) for hh in range(heads_per_block)]
    sls = [slice(pair * LANES, (pair + 1) * LANES) for pair in range(n_pairs)]
    owns = [(lane // NA_HEAD_DIM) == hh for hh in range(heads_per_block)]
    row_id = pl.program_id(0)
    first_offset = jnp.clip(row_id - NA_ROWS // 2, 0, pl.num_programs(0) - NA_ROWS) - row_id + (NA_ROWS - 1)

    def bias(h):
        return jnp.concatenate([bias_ref[h, first_offset + 2 * m] for m in range(NA_ROWS // 2)], axis=1)

    q2s = [q_ref[:, sl] * scale for sl in sls]
    k2s = [k_ref[:, sl] for sl in sls]
    scores = [_dot_nt(jnp.where(owns[hh], q2s[pair], jnp.zeros_like(q2s[pair])), k2s[pair])
              + bias(pair * heads_per_block + hh) for pair, hh in heads]
    maxes = [jnp.max(s, axis=-1, keepdims=True) for s in scores]
    probs = [jnp.exp(s - m) for s, m in zip(scores, maxes)]
    denoms = [jnp.sum(p, axis=-1, keepdims=True) for p in probs]
    v2s = [v_ref[:, sl] for sl in sls]
    outs = [_dot(p.astype(BF16), jnp.where(owns[hh], v2s[pair], jnp.zeros_like(v2s[pair]))) / den
            for (pair, hh), p, den in zip(heads, probs, denoms)]
    for pair in range(n_pairs):
        acc = outs[pair * heads_per_block]
        for hh in range(1, heads_per_block):
            acc = acc + outs[pair * heads_per_block + hh]
        o_ref[:, sls[pair]] = acc.astype(o_ref.dtype)


def na_attention(proj, col0, rpb):
    s = proj.shape[0]
    rows = s // GRID_W
    assert rows >= NA_ROWS
    qb = col0 // NA_WIDTH
    table = _na_bias_table(rpb)

    def first_row(r):
        return jnp.clip(r - NA_ROWS // 2, 0, rows - NA_ROWS)

    def kv_spec(blk):
        return pl.BlockSpec((pl.Element(NA_ROWS * GRID_W), pl.Element(NA_WIDTH)),
                            lambda r: (first_row(r) * GRID_W, blk * NA_WIDTH))

    in_specs = [
        pl.BlockSpec((GRID_W, NA_WIDTH), lambda r: (r, qb)),
        kv_spec(qb + 1),
        kv_spec(qb + 2),
        pl.BlockSpec(table.shape, lambda r: (0, 0, 0, 0), pipeline_mode=pl.Buffered(1)),
    ]
    return pl.pallas_call(
        _na_body,
        grid=(rows,),
        in_specs=in_specs,
        out_specs=pl.BlockSpec((GRID_W, NA_WIDTH), lambda r: (r, 0)),
        out_shape=jax.ShapeDtypeStruct((s, NA_WIDTH), BF16),
        compiler_params=_params(("parallel",)),
        name="na_attn",
    )(proj, proj, proj, table)


def _pack_bf16_pair(lo, hi):
    lo_bits = pltpu.bitcast(lo.astype(BF16).astype(F32), U32)
    hi_bits = pltpu.bitcast(hi.astype(BF16).astype(F32), U32)
    return (lo_bits >> 16) | (hi_bits & jnp.uint32(0xFFFF0000))


def _unpack_bf16_pair(packed):
    lo = pltpu.bitcast(packed << 16, F32)
    hi = pltpu.bitcast(packed & jnp.uint32(0xFFFF0000), F32)
    return lo, hi


def _merge_body(of_ref, ob_ref, z_ref, na_ref, ga_ref, gb_ref, x_ref, dng_ref, wa_ref, wb_ref, wo_ref,
                gffn_ref, wr_ref, br_ref, x1_ref, h2p_ref, lg_ref, dn_ref):
    hd = DN_HEAD_DIM
    for h in range(DN_HEADS):
        sl = slice(h * hd, (h + 1) * hd)
        o = of_ref[:, sl] + ob_ref[:, sl]
        r = lax.rsqrt(jnp.mean(o * o, axis=-1, keepdims=True) + RMS_EPS)
        z = z_ref[:, sl].astype(F32)
        dn_ref[:, sl] = (o * r * dng_ref[...] * (z * _sigmoid(z))).astype(BF16)
    y_a = _dot(dn_ref[...], wa_ref[...])
    y_b = _dot(na_ref[...], wb_ref[...])
    mixed = _sigmoid(ga_ref[...].astype(F32)) * y_a + _sigmoid(gb_ref[...].astype(F32)) * y_b
    x1 = x_ref[...] + _dot(mixed.astype(BF16), wo_ref[...])
    x1_ref[...] = x1
    r = lax.rsqrt(jnp.mean(x1 * x1, axis=-1, keepdims=True) + RMS_EPS)
    h2 = x1 * r * gffn_ref[...]
    h2_hi = h2.astype(BF16)
    h2_lo = (h2 - h2_hi.astype(F32)).astype(BF16)
    lg_ref[...] = (_dot(h2_hi, wr_ref[0]) + _dot(h2_lo, wr_ref[0]) + _dot(h2_hi, wr_ref[1])) + br_ref[...]
    half = h2.shape[1] // 2
    n_tiles = half // LANES
    packed = _pack_bf16_pair(h2[:, :half], h2[:, half:])
    for c in range(n_tiles):
        h2p_ref[pl.ds(c, h2.shape[0], stride=n_tiles), :] = packed[:, c * LANES:(c + 1) * LANES]


def merge(o_f, o_b, proj, z_blk, gate_blk, na_out, x, dn_norm_g, w_a, w_b, w_o, norm_ffn_g, w_router, b_router,
          tm=256):
    s, d = x.shape
    ne = w_router.shape[1]
    const = lambda i: (0, 0)
    single = pl.Buffered(1)
    w_router_hi = w_router.astype(BF16)
    w_router_lo = (w_router.astype(F32) - w_router_hi.astype(F32)).astype(BF16)
    w_router_split = jnp.stack([w_router_hi, w_router_lo])
    return pl.pallas_call(
        _merge_body,
        grid=(s // tm,),
        in_specs=[
            pl.BlockSpec((tm, DN_WIDTH), lambda i: (i, 0)),
            pl.BlockSpec((tm, DN_WIDTH), lambda i: (i, 0)),
            pl.BlockSpec((tm, DN_WIDTH), lambda i: (i, z_blk)),
            pl.BlockSpec((tm, NA_WIDTH), lambda i: (i, 0)),
            pl.BlockSpec((tm, d), lambda i: (i, gate_blk)),
            pl.BlockSpec((tm, d), lambda i: (i, gate_blk + 1)),
            pl.BlockSpec((tm, d), lambda i: (i, 0)),
            pl.BlockSpec((1, DN_HEAD_DIM), const),
            pl.BlockSpec((DN_WIDTH, d), const, pipeline_mode=single),
            pl.BlockSpec((NA_WIDTH, d), const, pipeline_mode=single),
            pl.BlockSpec((d, d), const, pipeline_mode=single),
            pl.BlockSpec((1, d), const),
            pl.BlockSpec((2, d, ne), lambda i: (0, 0, 0)),
            pl.BlockSpec((1, ne), const),
        ],
        out_specs=[
            pl.BlockSpec((tm, d), lambda i: (i, 0)),
            pl.BlockSpec((tm * (d // 2 // LANES), LANES), lambda i: (i, 0)),
            pl.BlockSpec((tm, ne), lambda i: (i, 0)),
        ],
        out_shape=[
            jax.ShapeDtypeStruct((s, d), F32),
            jax.ShapeDtypeStruct((s * (d // 2 // LANES), LANES), U32),
            jax.ShapeDtypeStruct((s, ne), F32),
        ],
        scratch_shapes=[pltpu.VMEM((tm, DN_WIDTH), BF16)],
        compiler_params=_params(("parallel",)),
        name="merge",
    )(o_f, o_b, proj, na_out, proj, proj, x, dn_norm_g.reshape(1, -1).astype(F32), w_a, w_b, w_o,
      norm_ffn_g.reshape(1, d).astype(F32), w_router_split, b_router.reshape(1, ne).astype(F32))


def _route_body(lg_ref, ti_ref, tw_ref, cnt_ref, carry_ref):
    @pl.when(pl.program_id(0) == 0)
    def _():
        carry_ref[...] = jnp.zeros(carry_ref.shape, F32)

    lg = lg_ref[...]
    tm, ne = lg.shape
    lane = lax.broadcasted_iota(I32, (tm, ne), 1).astype(F32)
    work = lg
    vals, idxs = [], []
    onehot = jnp.zeros((tm, ne), F32)
    for _ in range(TOP_K):
        m = jnp.max(work, axis=-1, keepdims=True)
        idx = jnp.min(jnp.where(work == m, lane, float(ne)), axis=-1, keepdims=True)
        hit = lane == idx
        vals.append(m)
        idxs.append(idx)
        onehot = onehot + jnp.where(hit, 1.0, 0.0)
        work = jnp.where(hit, -jnp.inf, work)
    exps = [jnp.exp(v - vals[0]) for v in vals]
    denom = exps[0] + exps[1] + exps[2] + exps[3]
    ri = lax.broadcasted_iota(I32, (tm, tm), 0)
    ci = lax.broadcasted_iota(I32, (tm, tm), 1)
    strict = jnp.where(ci < ri, 1.0, 0.0).astype(BF16)
    before = _dot(strict, onehot.astype(BF16)) + carry_ref[0:1, 0:ne]
    lane_o = lax.broadcasted_iota(I32, (tm, LANES), 1)
    ti = jnp.zeros((tm, LANES), I32)
    tw = jnp.zeros((tm, LANES), F32)
    for kk in range(TOP_K):
        rank = jnp.sum(jnp.where(lane == idxs[kk], before, 0.0), axis=-1, keepdims=True).astype(I32)
        ti = jnp.where(lane_o == kk, idxs[kk].astype(I32), ti)
        ti = jnp.where(lane_o == TOP_K + kk, rank, ti)
        tw = jnp.where(lane_o == kk, exps[kk] / denom, tw)
    ti_ref[...] = ti
    tw_ref[...] = tw
    total = carry_ref[0:1, 0:ne] + jnp.sum(onehot, axis=0, keepdims=True)
    carry_ref[0:1, 0:ne] = total
    cnt_ref[...] = jnp.zeros(cnt_ref.shape, F32)
    cnt_ref[0:1, 0:ne] = total


def route(logits, tm=512):
    s, ne = logits.shape
    return pl.pallas_call(
        _route_body,
        grid=(s // tm,),
        in_specs=[pl.BlockSpec((tm, ne), lambda i: (i, 0))],
        out_specs=[
            pl.BlockSpec((tm, LANES), lambda i: (i, 0)),
            pl.BlockSpec((tm, LANES), lambda i: (i, 0)),
            pl.BlockSpec((8, LANES), lambda i: (0, 0)),
        ],
        out_shape=[
            jax.ShapeDtypeStruct((s, LANES), I32),
            jax.ShapeDtypeStruct((s, LANES), F32),
            jax.ShapeDtypeStruct((8, LANES), F32),
        ],
        scratch_shapes=[pltpu.VMEM((8, LANES), F32)],
        compiler_params=_params(("arbitrary",)),
        name="route",
    )(logits)


def _dest_body(ti_ref, ps_ref, d_ref):
    ti = ti_ref[...].astype(F32)
    tm = ti.shape[0]
    lane = lax.broadcasted_iota(I32, (tm, LANES), 1)
    lane_f = lane.astype(F32)
    ps = ps_ref[0:1, :].astype(F32)
    out = jnp.zeros((tm, LANES), F32)
    for kk in range(TOP_K):
        e = jnp.sum(jnp.where(lane == kk, ti, 0.0), axis=-1, keepdims=True)
        rank = jnp.sum(jnp.where(lane == TOP_K + kk, ti, 0.0), axis=-1, keepdims=True)
        start = jnp.sum(jnp.where(lane_f == e, ps, 0.0), axis=-1, keepdims=True)
        out = jnp.where(lane == kk, start + rank, out)
    d_ref[...] = out.astype(I32)


def route_dest(ti, pad_start, tm=512):
    s = ti.shape[0]
    ps = jnp.zeros((8, LANES), I32).at[0, :N_EXPERTS].set(pad_start)
    return pl.pallas_call(
        _dest_body,
        grid=(s // tm,),
        in_specs=[pl.BlockSpec((tm, LANES), lambda i: (i, 0)), pl.BlockSpec((8, LANES), lambda i: (0, 0))],
        out_specs=pl.BlockSpec((tm, LANES), lambda i: (i, 0)),
        out_shape=jax.ShapeDtypeStruct((s, LANES), I32),
        compiler_params=_params(("parallel",)),
        name="route_dest",
    )(ti, ps)


def _scatter_body(pend_ref, padded_ref, dest_ref, h_ref, xr_ref, zero_ref, sem):
    tm = h_ref.shape[0]

    def zero_copy(e):
        return pltpu.make_async_copy(zero_ref, xr_ref.at[pl.ds(pend_ref[e] - MOE_SUB, MOE_SUB)], sem)

    @pl.when(pl.program_id(0) == 0)
    def _():
        zero_ref[...] = jnp.zeros(zero_ref.shape, U32)

        def start(e, carry):
            @pl.when(padded_ref[e] > 0)
            def _():
                zero_copy(e).start()
            return carry

        def wait(e, carry):
            @pl.when(padded_ref[e] > 0)
            def _():
                zero_copy(e).wait()
            return carry

        lax.fori_loop(0, N_EXPERTS, start, 0)
        lax.fori_loop(0, N_EXPERTS, wait, 0)

        def slack_copy(b):
            return pltpu.make_async_copy(zero_ref, xr_ref.at[pl.ds(b * MOE_SUB, MOE_SUB)], sem)

        def slack_start(b, carry):
            slack_copy(b).start()
            return carry

        def slack_wait(b, carry):
            slack_copy(b).wait()
            return carry

        first_slack = pend_ref[N_EXPERTS - 1] // MOE_SUB
        lax.fori_loop(first_slack, xr_ref.shape[0] // MOE_SUB, slack_start, 0)
        lax.fori_loop(first_slack, xr_ref.shape[0] // MOE_SUB, slack_wait, 0)

    def row_copy(r, kk):
        d = dest_ref[r * TOP_K + kk]
        return pltpu.make_async_copy(h_ref.at[r], xr_ref.at[d], sem)

    def start(r, carry):
        for kk in range(TOP_K):
            row_copy(r, kk).start(priority=kk % 2)
        return carry

    lax.fori_loop(0, tm, start, 0)
    all_rows = xr_ref.at[pl.ds(0, tm * TOP_K)]
    pltpu.make_async_copy(all_rows, all_rows, sem).wait()


def moe_scatter(h2p, dest, pad_end, padded, n_rows, tm=256):
    s, nt, _ = h2p.shape
    grid_spec = pltpu.PrefetchScalarGridSpec(
        num_scalar_prefetch=2,
        grid=(s // tm,),
        in_specs=[
            pl.BlockSpec((tm * TOP_K,), lambda i, *_: (i,), memory_space=pltpu.SMEM),
            pl.BlockSpec((tm, nt, LANES), lambda i, *_: (i, 0, 0)),
        ],
        out_specs=pl.BlockSpec(memory_space=pl.ANY),
        scratch_shapes=[pltpu.VMEM((MOE_SUB, nt, LANES), U32), pltpu.SemaphoreType.DMA(())],
    )
    return pl.pallas_call(
        _scatter_body,
        grid_spec=grid_spec,
        out_shape=jax.ShapeDtypeStruct((n_rows, nt, LANES), U32),
        compiler_params=_params(("arbitrary",)),
        name="moe_scatter",
    )(pad_end, padded, dest, h2p)


def _moe_body(ie_ref, ir_ref, inb_ref, ni_ref, xr_ref, wg_ref, wu_ref, wd_ref, bg_ref, bu_ref, bd_ref, y_ref,
              stage_ref, ystage_ref, xb_ref, acc_ref, wgb_ref, wub_ref, wdb_ref, sem_in, sem_out, *, n_chunks):
    w = pl.program_id(0)
    j = pl.program_id(1)
    sb = MOE_SUB
    d = acc_ref.shape[1]
    half = d // 2
    nx = half // LANES
    ny = half // LANES

    @pl.when(w < ni_ref[0])
    def _():
        nb = inb_ref[w]
        r0 = ir_ref[w]

        @pl.when(j == 0)
        def _():
            @pl.when(w == 0)
            def _():
                ystage_ref[0] = jnp.zeros(ystage_ref.shape[1:], U32)

                def slack_copy(b):
                    dst = y_ref.at[pl.ds(pl.multiple_of(b * (sb * ny), sb * ny), sb * ny), :]
                    return pltpu.make_async_copy(ystage_ref.at[0], dst, sem_out.at[0])

                def slack_start(b, carry):
                    slack_copy(b).start()
                    return carry

                def slack_wait(b, carry):
                    slack_copy(b).wait()
                    return carry

                n_blocks = y_ref.shape[0] // (sb * ny)
                lax.fori_loop(ni_ref[1], n_blocks, slack_start, 0)
                lax.fori_loop(ni_ref[1], n_blocks, slack_wait, 0)

            def in_copy(i, slot):
                src = xr_ref.at[pl.ds(pl.multiple_of((r0 + i * sb) * nx, sb * nx), sb * nx), :]
                return pltpu.make_async_copy(src, stage_ref.at[slot], sem_in.at[slot])

            in_copy(0, 0).start()

            def load(i, carry):
                slot = i % 2

                @pl.when(i + 1 < nb)
                def _():
                    in_copy(i + 1, 1 - slot).start()

                in_copy(i, slot).wait()
                rows = pl.ds(pl.multiple_of(i * sb, sb), sb)
                for c in range(nx):
                    lo, hi = _unpack_bf16_pair(stage_ref[slot, pl.ds(c, sb, stride=nx), :])
                    xb_ref[rows, c * LANES:(c + 1) * LANES] = lo.astype(BF16)
                    xb_ref[rows, half + c * LANES:half + (c + 1) * LANES] = hi.astype(BF16)
                return carry

            lax.fori_loop(0, nb, load, 0)

        wgb_ref[...] = wg_ref[0].astype(BF16)
        wub_ref[...] = wu_ref[0].astype(BF16)
        wdb_ref[...] = wd_ref[0].astype(BF16)

        def block(first, n_sub, i, carry):
            rows = pl.ds(pl.multiple_of(i * sb, sb), n_sub * sb)
            xs = xb_ref[rows, :]
            gate = jnp.minimum(_dot(xs, wgb_ref[...]) + bg_ref[0], SWIGLU_LIMIT)
            up = jnp.clip(_dot(xs, wub_ref[...]) + bu_ref[0], -SWIGLU_LIMIT, SWIGLU_LIMIT)
            act = (up + 1.0) * (gate * _sigmoid(SWIGLU_ALPHA * gate))
            contrib = _dot(act.astype(BF16), wdb_ref[...])
            if first:
                acc_ref[rows, :] = contrib
            else:
                acc_ref[rows, :] += contrib
            return carry

        def all_blocks(first):
            def pair(i2, carry):
                return block(first, 2, 2 * i2, carry)

            lax.fori_loop(0, nb // 2, pair, 0)

            @pl.when(nb % 2 == 1)
            def _():
                block(first, 1, nb - 1, 0)

        @pl.when(j == 0)
        def _():
            all_blocks(True)

        @pl.when(j > 0)
        def _():
            all_blocks(False)

        @pl.when(j == n_chunks - 1)
        def _():
            def out_copy(i, slot):
                dst = y_ref.at[pl.ds(pl.multiple_of((r0 + i * sb) * ny, sb * ny), sb * ny), :]
                return pltpu.make_async_copy(ystage_ref.at[slot], dst, sem_out.at[slot])

            def store(i, carry):
                slot = i % 2

                @pl.when(i >= 2)
                def _():
                    out_copy(i - 2, slot).wait()

                rows = pl.ds(pl.multiple_of(i * sb, sb), sb)
                for c in range(ny):
                    lo = slice(c * LANES, (c + 1) * LANES)
                    hi = slice(half + c * LANES, half + (c + 1) * LANES)
                    ystage_ref[slot, pl.ds(c, sb, stride=ny), :] = _pack_bf16_pair(
                        acc_ref[rows, lo] + bd_ref[0, :, lo], acc_ref[rows, hi] + bd_ref[0, :, hi])
                out_copy(i, slot).start()
                return carry

            lax.fori_loop(0, nb, store, 0)

            @pl.when(nb >= 2)
            def _():
                out_copy(nb - 2, nb % 2).wait()

            out_copy(nb - 1, (nb - 1) % 2).wait()


def moe_ffn(x_rows, w_gate_up, b_gate_up, w_down, b_down, item_e, item_row, item_nb, n_items, max_items):
    ne, d, two_de = w_gate_up.shape
    nx = d // 2 // LANES
    ny = d // 2 // LANES
    n_rows = x_rows.shape[0] // nx
    de = two_de // 2
    tn = MOE_TN
    n_chunks = de // tn
    last = n_chunks - 1

    def chunk(w, j, ni):
        return jnp.where(w < ni[0], j, last)

    grid_spec = pltpu.PrefetchScalarGridSpec(
        num_scalar_prefetch=4,
        grid=(max_items, n_chunks),
        in_specs=[
            pl.BlockSpec(memory_space=pl.ANY),
            pl.BlockSpec((1, d, tn), lambda w, j, ie, ir, inb, ni: (ie[w], 0, chunk(w, j, ni))),
            pl.BlockSpec((1, d, tn), lambda w, j, ie, ir, inb, ni: (ie[w], 0, n_chunks + chunk(w, j, ni))),
            pl.BlockSpec((1, tn, d), lambda w, j, ie, ir, inb, ni: (ie[w], chunk(w, j, ni), 0)),
            pl.BlockSpec((1, 1, tn), lambda w, j, ie, ir, inb, ni: (ie[w], 0, chunk(w, j, ni))),
            pl.BlockSpec((1, 1, tn), lambda w, j, ie, ir, inb, ni: (ie[w], 0, n_chunks + chunk(w, j, ni))),
            pl.BlockSpec((1, 1, d), lambda w, j, ie, ir, inb, ni: (ie[w], 0, 0)),
        ],
        out_specs=pl.BlockSpec(memory_space=pl.ANY),
        scratch_shapes=[
            pltpu.VMEM((2, MOE_SUB * nx, LANES), U32),
            pltpu.VMEM((2, MOE_SUB * ny, LANES), U32),
            pltpu.VMEM((MOE_TM, d), BF16),
            pltpu.VMEM((MOE_TM, d), F32),
            pltpu.VMEM((d, tn), BF16),
            pltpu.VMEM((d, tn), BF16),
            pltpu.VMEM((tn, d), BF16),
            pltpu.SemaphoreType.DMA((2,)),
            pltpu.SemaphoreType.DMA((2,)),
        ],
    )
    return pl.pallas_call(
        functools.partial(_moe_body, n_chunks=n_chunks),
        grid_spec=grid_spec,
        out_shape=jax.ShapeDtypeStruct((n_rows * ny, LANES), U32),
        compiler_params=_params(("arbitrary", "arbitrary")),
        name="moe_ffn",
    )(item_e, item_row, item_nb, n_items, x_rows, w_gate_up, w_gate_up, w_down,
      b_gate_up.reshape(ne, 1, two_de), b_gate_up.reshape(ne, 1, two_de), b_down.reshape(ne, 1, d))


def _final_body(dest_ref, dest_next_ref, x1_ref, tw_ref, p_ref, gple_ref, wg_ref, wp_ref, gfin_ref, y_ref, o_ref,
                ybuf_ref, sem, *, last_layer):
    i = pl.program_id(0)
    tm = x1_ref.shape[0]
    half = x1_ref.shape[1] // 2
    ny = half // LANES
    slot = i % 2

    def row_copy(d_ref, to_slot, r, kk):
        dst = ybuf_ref.at[to_slot, kk, pl.ds(pl.multiple_of(r * ny, ny), ny), :]
        return pltpu.make_async_copy(y_ref.at[d_ref[r * TOP_K + kk]], dst, sem.at[to_slot])

    def issue(d_ref, to_slot):
        def body(r, carry):
            for kk in range(TOP_K):
                row_copy(d_ref, to_slot, r, kk).start(priority=kk % 2)
            return carry

        lax.fori_loop(0, tm, body, 0)

    def drain(to_slot):
        pltpu.make_async_copy(ybuf_ref.at[to_slot], ybuf_ref.at[to_slot], sem.at[to_slot]).wait()

    @pl.when(i == 0)
    def _():
        issue(dest_ref, 0)

    for r in range(tm):
        for kk in range(TOP_K):
            row_copy(dest_next_ref, 1 - slot, r, kk).start(priority=kk % 2)

    drain(slot)

    tw = tw_ref[...]
    los, his = [], []
    for c in range(ny):
        acc_lo = x1_ref[:, c * LANES:(c + 1) * LANES]
        acc_hi = x1_ref[:, half + c * LANES:half + (c + 1) * LANES]
        for kk in range(TOP_K):
            lo, hi = _unpack_bf16_pair(ybuf_ref[slot, kk, pl.ds(c, tm, stride=ny), :])
            acc_lo = acc_lo + tw[:, kk:kk + 1] * lo
            acc_hi = acc_hi + tw[:, kk:kk + 1] * hi
        los.append(acc_lo)
        his.append(acc_hi)
    x2 = jnp.concatenate(los + his, axis=1)
    r = lax.rsqrt(jnp.mean(x2 * x2, axis=-1, keepdims=True) + RMS_EPS)
    n = (x2 * r * gple_ref[...]).astype(BF16)
    gate = _sigmoid(_dot(n, wg_ref[...]))
    x3 = x2 + gate * _dot(p_ref[...].astype(BF16), wp_ref[...])
    if last_layer:
        r = lax.rsqrt(jnp.mean(x3 * x3, axis=-1, keepdims=True) + RMS_EPS)
        x3 = x3 * r * gfin_ref[...]
    o_ref[...] = x3

    @pl.when(i + 1 == pl.num_programs(0))
    def _():
        drain(1 - slot)


def final(dest, x1, tw, p, norm_ple_g, w_gate, w_proj, norm_final_g, y_rows, last_layer, tm=256):
    s, d = x1.shape
    pd = p.shape[1]
    ny = d // 2 // LANES
    n_steps = s // tm
    const = lambda i: (0, 0)
    single = pl.Buffered(1)
    return pl.pallas_call(
        functools.partial(_final_body, last_layer=last_layer),
        grid=(n_steps,),
        in_specs=[
            pl.BlockSpec((tm * TOP_K,), lambda i: (i,), memory_space=pltpu.SMEM),
            pl.BlockSpec((tm * TOP_K,), lambda i: (jnp.minimum(i + 1, n_steps - 1),), memory_space=pltpu.SMEM),
            pl.BlockSpec((tm, d), lambda i: (i, 0)),
            pl.BlockSpec((tm, LANES), lambda i: (i, 0)),
            pl.BlockSpec((tm, pd), lambda i: (i, 0)),
            pl.BlockSpec((1, d), const),
            pl.BlockSpec((d, d), const, pipeline_mode=single),
            pl.BlockSpec((pd, d), const, pipeline_mode=single),
            pl.BlockSpec((1, d), const),
            pl.BlockSpec(memory_space=pl.ANY),
        ],
        out_specs=pl.BlockSpec((tm, d), lambda i: (i, 0)),
        out_shape=jax.ShapeDtypeStruct((s, d), F32),
        scratch_shapes=[pltpu.VMEM((2, TOP_K, tm * ny, LANES), U32), pltpu.SemaphoreType.DMA((2,))],
        compiler_params=_params(("arbitrary",)),
        name="final",
    )(dest, dest, x1, tw, p, norm_ple_g.reshape(1, d).astype(F32), w_gate, w_proj,
      norm_final_g.reshape(1, d).astype(F32), y_rows.reshape(-1, ny, LANES))


def _moe_tables(counts, n_rows):
    sub, tm = MOE_SUB, MOE_TM
    max_items = N_EXPERTS + n_rows // tm
    padded = (counts + sub - 1) // sub * sub
    pad_end = jnp.cumsum(padded)
    pad_start = pad_end - padded
    n_it = (padded + tm - 1) // tm
    it_end = jnp.cumsum(n_it)
    it_start = it_end - n_it
    n_items = it_end[-1]
    w = jnp.arange(max_items, dtype=I32)
    live = w < n_items
    w_eff = jnp.minimum(w, n_items - 1)
    e_w = jnp.minimum(jnp.searchsorted(it_end, w_eff, side="right"), N_EXPERTS - 1).astype(I32)
    m_w = w_eff - it_start[e_w]
    row_w = pad_start[e_w] + m_w * tm
    nb_w = jnp.clip((padded[e_w] - m_w * tm) // sub, 0, tm // sub)
    nb_w = jnp.where(live, nb_w, 0)
    as_i32 = lambda a: a.astype(I32)
    counts_w = jnp.stack([n_items, pad_end[-1] // sub])
    return (as_i32(pad_start), as_i32(pad_end), as_i32(padded), e_w, as_i32(row_w), as_i32(nb_w),
            as_i32(counts_w), max_items)


def _layer(x, p, norm_mix_g, w_in, dn_conv_w, dn_a_log, dn_dt_bias, dn_norm_g, na_rpb, w_branch_a, w_branch_b,
           w_out, norm_ffn_g, w_router, b_router, w_gate_up, b_gate_up, w_down, b_down, norm_ple_g,
           w_ple_gate, w_ple_proj, norm_final_g, last_layer):
    s, d = x.shape
    c_qkv, c_z = 3 * DN_WIDTH, DN_WIDTH
    w_t = jnp.swapaxes(w_in, 0, 1)
    w_main = w_prep(w_t)
    z_blk = c_qkv // DN_WIDTH
    gate_blk = (c_qkv + c_z) // d
    na_col0 = c_qkv + c_z + 2 * d

    proj, small = in_projection(x, norm_mix_g.astype(F32), w_main, w_t)

    qkv = dn_prep(proj, dn_conv_w.astype(F32))
    col, row = dn_gates(small, dn_a_log, dn_dt_bias)
    u, w, qd, kt, qk = dn_chunk(qkv, col, row)
    o_f, o_b = dn_scan(u, w, qd, kt, qk, col)

    na_out = na_attention(proj, na_col0, na_rpb)

    x1, h2p, logits = merge(o_f, o_b, proj, z_blk, gate_blk, na_out, x, dn_norm_g, w_branch_a.astype(BF16),
                            w_branch_b.astype(BF16), w_out.astype(BF16), norm_ffn_g, w_router, b_router)

    ti, tw, cnt = route(logits)
    counts = cnt[0, :N_EXPERTS].astype(I32)
    n_rows = (s * TOP_K + N_EXPERTS * (MOE_SUB - 1) + MOE_SUB - 1) // MOE_SUB * MOE_SUB
    pad_start, pad_end, padded, item_e, item_row, item_nb, n_items, max_items = _moe_tables(counts, n_rows)
    dest = route_dest(ti, pad_start)[:, :TOP_K].reshape(-1)

    nx = d // 2 // LANES
    x_rows = moe_scatter(h2p.reshape(s, nx, LANES), dest, pad_end, padded, n_rows).reshape(n_rows * nx, LANES)
    y_rows = moe_ffn(x_rows, w_gate_up, b_gate_up, w_down, b_down, item_e, item_row, item_nb, n_items, max_items)

    return final(dest, x1, tw, p, norm_ple_g, w_ple_gate.astype(BF16), w_ple_proj.astype(BF16), norm_final_g,
                 y_rows, last_layer)


def kernel(x, p, norm_mix_g, w_in, dn_conv_w, dn_a_log, dn_dt_bias, dn_norm_g, na_rpb, w_branch_a, w_branch_b, w_out, norm_ffn_g, w_router, b_router, w_gate_up, b_gate_up, w_down, b_down, norm_ple_g, w_ple_gate, w_ple_proj, norm_final_g):
    bsz, s, d = x.shape
    depth = w_in.shape[0]
    outs = []
    for b in range(bsz):
        xb = x[b]
        for i in range(depth):
            xb = _layer(xb, p[i, b], norm_mix_g[i], w_in[i], dn_conv_w[i], dn_a_log[i], dn_dt_bias[i], dn_norm_g[i],
                        na_rpb[i], w_branch_a[i], w_branch_b[i], w_out[i], norm_ffn_g[i], w_router[i], b_router[i],
                        w_gate_up[i], b_gate_up[i], w_down[i], b_down[i], norm_ple_g[i], w_ple_gate[i],
                        w_ple_proj[i], norm_final_g, i == depth - 1)
        outs.append(xb)
    return jnp.stack(outs, axis=0)
```

```python
import functools

import jax
import jax.numpy as jnp
import numpy as np
from jax import lax
from jax.experimental import pallas as pl
from jax.experimental.pallas import tpu as pltpu

F32 = jnp.float32
BF16 = jnp.bfloat16
I32 = jnp.int32
U32 = jnp.uint32

GRID_W = 64
DN_HEADS = 8
DN_HEAD_DIM = 128
DN_WIDTH = DN_HEADS * DN_HEAD_DIM
DN_CONV = 5
NA_HEADS = 16
NA_HEAD_DIM = 64
NA_WIDTH = NA_HEADS * NA_HEAD_DIM
NA_ROWS = 8
NA_COLS = 16
N_EXPERTS = 32
TOP_K = 4
SWIGLU_LIMIT = 7.0
SWIGLU_ALPHA = 1.702
RMS_EPS = 1e-6

LANES = 128
VMEM_LIMIT = 56 * 1024 * 1024
MOE_VMEM_LIMIT = 60000 * 1024

DN_TILE = 256
DN_BLOCK = 16
MOE_SUB = 256
MOE_TM = 1536
MOE_TN = 512
NEG_BIG = -1e30


def _sigmoid(x):
    return 1.0 / (1.0 + jnp.exp(-x))


def _dot(a, b):
    return jnp.dot(a, b, preferred_element_type=F32)


def _dot_nt(a, b):
    return lax.dot_general(a, b, (((1,), (1,)), ((), ())), preferred_element_type=F32)


def _params(sem, limit=VMEM_LIMIT):
    return pltpu.CompilerParams(dimension_semantics=sem, vmem_limit_bytes=limit)


def _wprep_body(src_ref, o_ref, buf_ref, sem, *, starts, tn):
    t = pl.program_id(0)
    slot = t % 2

    def tile_copy(tt, to_slot):
        start = jnp.int32(starts[0])
        for k in range(1, len(starts)):
            start = jnp.where(tt >= k, starts[k], start)
        src = src_ref.at[pl.ds(pl.multiple_of(start, 8), tn), :]
        return pltpu.make_async_copy(src, buf_ref.at[to_slot], sem.at[to_slot])

    @pl.when(t == 0)
    def _():
        tile_copy(0, 0).start()

    @pl.when(t + 1 < pl.num_programs(0))
    def _():
        tile_copy(t + 1, 1 - slot).start()

    tile_copy(t, slot).wait()
    o_ref[...] = buf_ref[slot].T.astype(o_ref.dtype)


def w_prep(w_t, tn=1024):
    n_in, d = w_t.shape
    c_small = 4 * DN_HEADS
    o_small = 4 * DN_WIDTH
    o_na = o_small + c_small
    o_gates = o_na + 3 * NA_WIDTH
    assert o_small % tn == 0 and (3 * NA_WIDTH) % tn == 0 and (n_in - o_gates) % tn == 0
    assert o_na % 8 == 0 and o_gates % 8 == 0
    starts = list(range(0, o_small, tn)) + list(range(o_gates, n_in, tn)) + list(range(o_na, o_gates, tn))
    return pl.pallas_call(
        functools.partial(_wprep_body, starts=tuple(starts), tn=tn),
        grid=(len(starts),),
        in_specs=[pl.BlockSpec(memory_space=pl.ANY)],
        out_specs=pl.BlockSpec((d, tn), lambda t: (0, t)),
        out_shape=jax.ShapeDtypeStruct((d, len(starts) * tn), BF16),
        scratch_shapes=[pltpu.VMEM((2, tn, d), F32), pltpu.SemaphoreType.DMA((2,))],
        compiler_params=_params(("arbitrary",)),
        name="w_prep",
    )(w_t)


def _inproj_body(x_ref, g_ref, w_ref, ws_ref, o_ref, os_ref, h_ref):
    @pl.when(pl.program_id(1) == 0)
    def _():
        x = x_ref[...]
        r = lax.rsqrt(jnp.mean(x * x, axis=-1, keepdims=True) + RMS_EPS)
        h = (x * r * g_ref[...]).astype(BF16)
        h_ref[...] = h
        os_ref[...] = _dot_nt(h, ws_ref[...].astype(BF16))

    o_ref[...] = _dot(h_ref[...], w_ref[...]).astype(o_ref.dtype)


def in_projection(x, g, w_main, w_t, tm=1024, tn=1024):
    s, d = x.shape
    n = w_main.shape[1]
    small_blk = 4 * DN_WIDTH // LANES
    return pl.pallas_call(
        _inproj_body,
        grid=(s // tm, n // tn),
        in_specs=[
            pl.BlockSpec((tm, d), lambda i, j: (i, 0)),
            pl.BlockSpec((1, d), lambda i, j: (0, 0)),
            pl.BlockSpec((d, tn), lambda i, j: (0, j)),
            pl.BlockSpec((LANES, d), lambda i, j: (small_blk, 0)),
        ],
        out_specs=[
            pl.BlockSpec((tm, tn), lambda i, j: (i, j)),
            pl.BlockSpec((tm, LANES), lambda i, j: (i, 0)),
        ],
        out_shape=[jax.ShapeDtypeStruct((s, n), BF16), jax.ShapeDtypeStruct((s, LANES), F32)],
        scratch_shapes=[pltpu.VMEM((tm, d), BF16)],
        compiler_params=_params(("parallel", "arbitrary")),
        name="in_proj",
    )(x, g.reshape(1, d), w_main, w_t)


def _dnprep_body(x_ref, w_ref, o_ref, pad_ref, *, seq, chunk):
    cb = pl.program_id(0)
    n_chunks = seq // chunk
    zeros = jnp.zeros((16, LANES), F32)
    pad_ref[0:16, :] = zeros
    pad_ref[seq + 16:seq + 32, :] = zeros

    def fill(c, carry):
        r0 = pl.multiple_of(c * chunk, chunk)
        pad_ref[pl.ds(r0 + 16, chunk), :] = x_ref[pl.ds(r0, chunk), :].astype(F32)
        return carry

    lax.fori_loop(0, n_chunks, fill, 0)

    w = w_ref[...]
    is_v = cb >= 2 * DN_HEADS
    scale = jnp.where(cb < DN_HEADS, DN_HEAD_DIM ** -0.5, 1.0).astype(F32)

    def body(c, carry):
        r0 = pl.multiple_of(c * chunk, chunk)
        y = pad_ref[pl.ds(r0 + 14, chunk), :] * w[0:1]
        for j in range(1, DN_CONV):
            y = y + pad_ref[pl.ds(r0 + 14 + j, chunk), :] * w[j:j + 1]
        y = y * _sigmoid(y)
        ss = jnp.sum(y * y, axis=-1, keepdims=True)
        yn = y * (lax.rsqrt(ss + 1e-6) * scale)
        o_ref[pl.ds(r0, chunk), :] = jnp.where(is_v, y, yn).astype(o_ref.dtype)
        return carry

    lax.fori_loop(0, n_chunks, body, 0)


def dn_prep(proj, conv_w, chunk=512):
    s = proj.shape[0]
    nb = 3 * DN_WIDTH // LANES
    w = jnp.zeros((8, 3 * DN_WIDTH), F32).at[:DN_CONV].set(conv_w)
    return pl.pallas_call(
        functools.partial(_dnprep_body, seq=s, chunk=chunk),
        grid=(nb,),
        in_specs=[
            pl.BlockSpec((s, LANES), lambda c: (0, c)),
            pl.BlockSpec((8, LANES), lambda c: (0, c)),
        ],
        out_specs=pl.BlockSpec((s, LANES), lambda c: (0, c)),
        out_shape=jax.ShapeDtypeStruct((s, 3 * DN_WIDTH), BF16),
        scratch_shapes=[pltpu.VMEM((s + 32, LANES), F32)],
        compiler_params=_params(("parallel",)),
        name="dn_prep",
    )(proj, w)


def _gates_body(s_ref, par_ref, col_ref, row_ref):
    x = s_ref[...]
    t = x.shape[0]
    lane = lax.broadcasted_iota(I32, x.shape, 1)
    beta = _sigmoid(x)
    z = x + par_ref[1:2, :]
    softplus = jnp.maximum(z, 0.0) + jnp.log(1.0 + jnp.exp(-jnp.abs(z)))
    g = par_ref[0:1, :] * softplus
    ri = lax.broadcasted_iota(I32, (t, t), 0)
    ci = lax.broadcasted_iota(I32, (t, t), 1)
    lower = jnp.where(ci <= ri, 1.0, 0.0).astype(BF16)
    upper = jnp.where(ci >= ri, 1.0, 0.0).astype(BF16)
    g1 = g.astype(BF16)
    g2 = (g - g1.astype(F32)).astype(BF16)
    g3 = (g - g1.astype(F32) - g2.astype(F32)).astype(BF16)
    g_prefix = _dot(lower, g1) + _dot(lower, g2) + _dot(lower, g3)
    g_suffix = _dot(upper, g1) + _dot(upper, g2) + _dot(upper, g3)
    cum = jnp.where(lane < 16 + DN_HEADS, g_prefix, g_suffix)
    out = jnp.where(lane < 16, beta, jnp.where(lane < 32, cum, 0.0))
    col_ref[...] = out
    row_ref[...] = out.T


def dn_gates(small, a_log, dt_bias):
    s = small.shape[0]
    par = jnp.zeros((8, LANES), F32)
    par = par.at[0, 16:32].set(-jnp.exp(a_log.reshape(-1).astype(F32)))
    par = par.at[1, 16:32].set(dt_bias.reshape(-1).astype(F32))
    t = DN_TILE
    return pl.pallas_call(
        _gates_body,
        grid=(s // t,),
        in_specs=[
            pl.BlockSpec((t, LANES), lambda i: (i, 0)),
            pl.BlockSpec((8, LANES), lambda i: (0, 0)),
        ],
        out_specs=[
            pl.BlockSpec((t, LANES), lambda i: (i, 0)),
            pl.BlockSpec((LANES, t), lambda i: (0, i)),
        ],
        out_shape=[jax.ShapeDtypeStruct((s, LANES), F32), jax.ShapeDtypeStruct((LANES, s), F32)],
        compiler_params=_params(("parallel",)),
        name="dn_gates",
    )(small, par)


def _dnchunk_body(q_ref, k_ref, v_ref, col_ref, row_ref, u_ref, w_ref, qd_ref, kt_ref, qk_ref, *, heads_per_step):
    c = DN_TILE
    hd = DN_HEAD_DIM
    head0 = pl.program_id(1) * heads_per_step
    col = col_ref[...]
    row = row_ref[...]
    lane = lax.broadcasted_iota(I32, col.shape, 1)
    sub = lax.broadcasted_iota(I32, row.shape, 0)

    def col_pick(idx):
        return jnp.sum(jnp.where(lane == idx, col, 0.0), axis=1, keepdims=True)

    def row_pick(idx):
        return jnp.sum(jnp.where(sub == idx, row, 0.0), axis=0, keepdims=True)

    ri = lax.broadcasted_iota(I32, (c, c), 0)
    ci = lax.broadcasted_iota(I32, (c, c), 1)
    same_block = (ri // DN_BLOCK) == (ci // DN_BLOCK)
    incl = (ri >= ci, ri <= ci)
    strict = (ri > ci, ri < ci)
    heads = range(heads_per_step)
    chains = [(hh, d) for hh in heads for d in range(2)]
    sls = [slice(hh * hd, (hh + 1) * hd) for hh in heads]
    qs = [q_ref[:, sl] for sl in sls]
    ks = [k_ref[:, sl] for sl in sls]
    vs = [v_ref[:, sl] for sl in sls]
    grams = [_dot_nt(k, k) for k in ks]
    qks = [_dot_nt(q, k) for q, k in zip(qs, ks)]
    qfs = [q.astype(F32) for q in qs]
    kfs = [k.astype(F32) for k in ks]
    vfs = [v.astype(F32) for v in vs]

    betas = [col_pick(d * DN_HEADS + head0 + hh) for hh, d in chains]
    g_cols = [col_pick(16 + d * DN_HEADS + head0 + hh) for hh, d in chains]
    g_rows = [row_pick(16 + d * DN_HEADS + head0 + hh) for hh, d in chains]
    totals = [gr[:, c - 1:c] if d == 0 else gr[:, 0:1] for (hh, d), gr in zip(chains, g_rows)]
    decays = [jnp.where(incl[d], jnp.exp(jnp.minimum(gc - gr, 0.0)), 0.0)
              for (hh, d), gc, gr in zip(chains, g_cols, g_rows)]
    lows = [jnp.where(strict[d], b * grams[hh] * dec, 0.0) for (hh, d), b, dec in zip(chains, betas, decays)]
    l_diags = [jnp.where(same_block, low, 0.0) for low in lows]
    l_offs = [(low - ld).astype(BF16) for low, ld in zip(lows, l_diags)]

    def neumann(accs, x_bs, n_steps):
        for _ in range(n_steps):
            x2s = [_dot(x, x) for x in x_bs]
            x_bs = [x2.astype(BF16) for x2 in x2s]
            accs = [a + x2 + _dot(a.astype(BF16), xb) for a, x2, xb in zip(accs, x2s, x_bs)]
        return accs

    d_ms = neumann([-ld for ld in l_diags], [ld.astype(BF16) for ld in l_diags], (DN_BLOCK - 1).bit_length() - 1)
    d_bs = [dm.astype(BF16) for dm in d_ms]
    ms = [lo.astype(F32) + _dot(db, lo) for db, lo in zip(d_bs, l_offs)]
    q_ms = neumann([-m for m in ms], [m.astype(BF16) for m in ms], (c // DN_BLOCK - 1).bit_length() - 1)
    e_cols = [jnp.exp(gc) for gc in g_cols]
    rhss = [jnp.concatenate([vfs[hh] * b, kfs[hh] * b * ec], axis=1) for (hh, d), b, ec in zip(chains, betas, e_cols)]
    r1s = [rhs + _dot(db, rhs.astype(BF16)) for rhs, db in zip(rhss, d_bs)]
    sols = [r1 + _dot(qm.astype(BF16), r1.astype(BF16)) for r1, qm in zip(r1s, q_ms)]
    for i, (hh, d) in enumerate(chains):
        sl = sls[hh]
        u_ref[d, :, sl] = sols[i][:, :hd].astype(BF16)
        w_ref[d, :, sl] = sols[i][:, hd:].astype(BF16)
        qd_ref[d, :, sl] = (qfs[hh] * e_cols[i]).astype(BF16)
        kt_ref[d, sl, :] = (kfs[hh] * jnp.exp(totals[i] - g_cols[i])).T.astype(BF16)
        qk_ref[d, hh] = (qks[hh] * decays[i]).astype(BF16)


def dn_chunk(qkv, col, row, heads_per_step=4):
    s = qkv.shape[0]
    c = DN_TILE
    nt = s // c
    hd = DN_HEAD_DIM * heads_per_step
    groups = DN_HEADS // heads_per_step
    return pl.pallas_call(
        functools.partial(_dnchunk_body, heads_per_step=heads_per_step),
        grid=(nt, groups),
        in_specs=[
            pl.BlockSpec((c, hd), lambda t, h: (t, h)),
            pl.BlockSpec((c, hd), lambda t, h: (t, groups + h)),
            pl.BlockSpec((c, hd), lambda t, h: (t, 2 * groups + h)),
            pl.BlockSpec((c, LANES), lambda t, h: (t, 0)),
            pl.BlockSpec((LANES, c), lambda t, h: (0, t)),
        ],
        out_specs=[
            pl.BlockSpec((2, c, hd), lambda t, h: (0, t, h)),
            pl.BlockSpec((2, c, hd), lambda t, h: (0, t, h)),
            pl.BlockSpec((2, c, hd), lambda t, h: (0, t, h)),
            pl.BlockSpec((2, hd, c), lambda t, h: (0, h, t)),
            pl.BlockSpec((2, heads_per_step, c, c), lambda t, h: (0, h, t, 0)),
        ],
        out_shape=[
            jax.ShapeDtypeStruct((2, s, DN_WIDTH), BF16),
            jax.ShapeDtypeStruct((2, s, DN_WIDTH), BF16),
            jax.ShapeDtypeStruct((2, s, DN_WIDTH), BF16),
            jax.ShapeDtypeStruct((2, DN_WIDTH, s), BF16),
            jax.ShapeDtypeStruct((2, DN_HEADS, s, c), BF16),
        ],
        compiler_params=_params(("parallel", "parallel")),
        name="dn_chunk",
    )(qkv, qkv, qkv, col, row)


def _dnscan_body(uf, wf, qdf, ktf, qkf, colf, ub, wb, qdb, ktb, qkb, colb, of_ref, ob_ref, st_ref):
    @pl.when(pl.program_id(0) == 0)
    def _():
        st_ref[...] = jnp.zeros(st_ref.shape, F32)

    c = DN_TILE
    hd = DN_HEAD_DIM
    dirs = ((uf, wf, qdf, ktf, qkf, colf, of_ref, c - 1), (ub, wb, qdb, ktb, qkb, colb, ob_ref, 0))
    chains = [(d, h) for d in range(2) for h in range(DN_HEADS)]
    sls = [slice(h * hd, (h + 1) * hd) for h in range(DN_HEADS)]
    e_tots = [jnp.exp(dirs[d][5][dirs[d][7]:dirs[d][7] + 1, :]) for d in range(2)]
    states = [st_ref[d * DN_HEADS + h] for d, h in chains]
    states_b = [st.astype(BF16) for st in states]
    v_news = [dirs[d][0][0, :, sls[h]] - _dot(dirs[d][1][0, :, sls[h]], sb) for (d, h), sb in zip(chains, states_b)]
    v_news_b = [vn.astype(BF16) for vn in v_news]
    outs = [_dot(dirs[d][2][0, :, sls[h]], sb) + _dot(dirs[d][4][0, h], vb)
            for (d, h), sb, vb in zip(chains, states_b, v_news_b)]
    news = [st * e_tots[d][:, 16 + d * DN_HEADS + h:17 + d * DN_HEADS + h] + _dot(dirs[d][3][0, sls[h], :], vb)
            for (d, h), st, vb in zip(chains, states, v_news_b)]
    for (d, h), out, new in zip(chains, outs, news):
        dirs[d][6][:, sls[h]] = out
        st_ref[d * DN_HEADS + h] = new


def dn_scan(u, w, qd, kt, qk, col):
    s = u.shape[1]
    c = DN_TILE
    nt = s // c
    wd = DN_WIDTH

    def specs(d):
        tile = (lambda t: t) if d == 0 else (lambda t: nt - 1 - t)
        return [
            pl.BlockSpec((1, c, wd), lambda t: (d, tile(t), 0)),
            pl.BlockSpec((1, c, wd), lambda t: (d, tile(t), 0)),
            pl.BlockSpec((1, c, wd), lambda t: (d, tile(t), 0)),
            pl.BlockSpec((1, wd, c), lambda t: (d, 0, tile(t))),
            pl.BlockSpec((1, DN_HEADS, c, c), lambda t: (d, 0, tile(t), 0)),
            pl.BlockSpec((c, LANES), lambda t: (tile(t), 0)),
        ]

    return pl.pallas_call(
        _dnscan_body,
        grid=(nt,),
        in_specs=specs(0) + specs(1),
        out_specs=[
            pl.BlockSpec((c, wd), lambda t: (t, 0)),
            pl.BlockSpec((c, wd), lambda t: (nt - 1 - t, 0)),
        ],
        out_shape=[jax.ShapeDtypeStruct((s, wd), F32), jax.ShapeDtypeStruct((s, wd), F32)],
        scratch_shapes=[pltpu.VMEM((2 * DN_HEADS, DN_HEAD_DIM, DN_HEAD_DIM), F32)],
        compiler_params=_params(("arbitrary",)),
        name="dn_scan",
    )(u, w, qd, kt, qk, col, u, w, qd, kt, qk, col)


def _na_bias_table(rpb):
    n_heads, n_off, n_dc = rpb.shape
    return pl.pallas_call(
        functools.partial(_nabias_body, n_off=n_off, n_dc=n_dc),
        grid=(n_heads,),
        in_specs=[pl.BlockSpec(memory_space=pltpu.SMEM)],
        out_specs=pl.BlockSpec((1, n_off - 1, GRID_W, 2 * GRID_W), lambda h: (h, 0, 0, 0)),
        out_shape=jax.ShapeDtypeStruct((n_heads, n_off - 1, GRID_W, 2 * GRID_W), F32),
        compiler_params=_params(("parallel",)),
        name="na_bias",
    )(rpb.astype(F32).reshape(-1))


def _nabias_body(rpb_ref, o_ref, *, n_off, n_dc):
    h = pl.program_id(0)
    c = lax.broadcasted_iota(I32, (GRID_W, 2 * GRID_W), 0)
    lane = lax.broadcasted_iota(I32, (GRID_W, 2 * GRID_W), 1)
    second = lane >= GRID_W
    kc = jnp.where(second, lane - GRID_W, lane)
    cs = jnp.clip(c - NA_COLS // 2, 0, GRID_W - NA_COLS)
    valid = (kc >= cs) & (kc < cs + NA_COLS)
    dc = jnp.where(valid, kc - c + (NA_COLS - 1), -1)
    for m in range(n_off - 1):
        tile = jnp.full((GRID_W, 2 * GRID_W), NEG_BIG, F32)
        for j in range(n_dc):
            first_val = rpb_ref[(h * n_off + m) * n_dc + j]
            second_val = rpb_ref[(h * n_off + m + 1) * n_dc + j]
            tile = jnp.where(dc == j, jnp.where(second, second_val, first_val), tile)
        o_ref[0, m] = tile


def _na_body(q_ref, k_ref, v_ref, bias_ref, o_ref):
    lane = lax.broadcasted_iota(I32, (1, LANES), 1)
    heads_per_block = LANES // NA_HEAD_DIM
    scale = NA_HEAD_DIM ** -0.5
    n_pairs = NA_WIDTH // LANES
    heads = [(pair, hh) for pair in range(n_pairs) for hh in range(heads_per_block)]
    sls = [slice(pair * LANES, (pair + 1) * LANES) for pair in range(n_pairs)]
    owns = [(lane // NA_HEAD_DIM) == hh for hh in range(heads_per_block)]
    row_id = pl.program_id(0)
    first_offset = jnp.clip(row_id - NA_ROWS // 2, 0, pl.num_programs(0) - NA_ROWS) - row_id + (NA_ROWS - 1)

    def bias(h):
        return jnp.concatenate([bias_ref[h, first_offset + 2 * m] for m in range(NA_ROWS // 2)], axis=1)

    q2s = [q_ref[:, sl] * scale for sl in sls]
    k2s = [k_ref[:, sl] for sl in sls]
    scores = [_dot_nt(jnp.where(owns[hh], q2s[pair], jnp.zeros_like(q2s[pair])), k2s[pair])
              + bias(pair * heads_per_block + hh) for pair, hh in heads]
    maxes = [jnp.max(s, axis=-1, keepdims=True) for s in scores]
    probs = [jnp.exp(s - m) for s, m in zip(scores, maxes)]
    denoms = [jnp.sum(p, axis=-1, keepdims=True) for p in probs]
    v2s = [v_ref[:, sl] for sl in sls]
    outs = [_dot(p.astype(BF16), jnp.where(owns[hh], v2s[pair], jnp.zeros_like(v2s[pair]))) / den
            for (pair, hh), p, den in zip(heads, probs, denoms)]
    for pair in range(n_pairs):
        acc = outs[pair * heads_per_block]
        for hh in range(1, heads_per_block):
            acc = acc + outs[pair * heads_per_block + hh]
        o_ref[:, sls[pair]] = acc.astype(o_ref.dtype)


def na_attention(proj, col0, rpb):
    s = proj.shape[0]
    rows = s // GRID_W
    assert rows >= NA_ROWS
    qb = col0 // NA_WIDTH
    table = _na_bias_table(rpb)

    def first_row(r):
        return jnp.clip(r - NA_ROWS // 2, 0, rows - NA_ROWS)

    def kv_spec(blk):
        return pl.BlockSpec((pl.Element(NA_ROWS * GRID_W), pl.Element(NA_WIDTH)),
                            lambda r: (first_row(r) * GRID_W, blk * NA_WIDTH))

    in_specs = [
        pl.BlockSpec((GRID_W, NA_WIDTH), lambda r: (r, qb)),
        kv_spec(qb + 1),
        kv_spec(qb + 2),
        pl.BlockSpec(table.shape, lambda r: (0, 0, 0, 0), pipeline_mode=pl.Buffered(1)),
    ]
    return pl.pallas_call(
        _na_body,
        grid=(rows,),
        in_specs=in_specs,
        out_specs=pl.BlockSpec((GRID_W, NA_WIDTH), lambda r: (r, 0)),
        out_shape=jax.ShapeDtypeStruct((s, NA_WIDTH), BF16),
        compiler_params=_params(("parallel",)),
        name="na_attn",
    )(proj, proj, proj, table)


def _pack_bf16_pair(lo, hi):
    lo_bits = pltpu.bitcast(lo.astype(BF16).astype(F32), U32)
    hi_bits = pltpu.bitcast(hi.astype(BF16).astype(F32), U32)
    return (lo_bits >> 16) | (hi_bits & jnp.uint32(0xFFFF0000))


def _unpack_bf16_pair(packed):
    lo = pltpu.bitcast(packed << 16, F32)
    hi = pltpu.bitcast(packed & jnp.uint32(0xFFFF0000), F32)
    return lo, hi


def _merge_body(of_ref, ob_ref, z_ref, na_ref, ga_ref, gb_ref, x_ref, dng_ref, wa_ref, wb_ref, wo_ref,
                gffn_ref, wr_ref, br_ref, x1_ref, h2p_ref, lg_ref, dn_ref):
    hd = DN_HEAD_DIM
    for h in range(DN_HEADS):
        sl = slice(h * hd, (h + 1) * hd)
        o = of_ref[:, sl] + ob_ref[:, sl]
        r = lax.rsqrt(jnp.mean(o * o, axis=-1, keepdims=True) + RMS_EPS)
        z = z_ref[:, sl].astype(F32)
        dn_ref[:, sl] = (o * r * dng_ref[...] * (z * _sigmoid(z))).astype(BF16)
    y_a = _dot(dn_ref[...], wa_ref[...])
    y_b = _dot(na_ref[...], wb_ref[...])
    mixed = _sigmoid(ga_ref[...].astype(F32)) * y_a + _sigmoid(gb_ref[...].astype(F32)) * y_b
    x1 = x_ref[...] + _dot(mixed.astype(BF16), wo_ref[...])
    x1_ref[...] = x1
    r = lax.rsqrt(jnp.mean(x1 * x1, axis=-1, keepdims=True) + RMS_EPS)
    h2 = x1 * r * gffn_ref[...]
    h2_hi = h2.astype(BF16)
    h2_lo = (h2 - h2_hi.astype(F32)).astype(BF16)
    lg_ref[...] = (_dot(h2_hi, wr_ref[0]) + _dot(h2_lo, wr_ref[0]) + _dot(h2_hi, wr_ref[1])) + br_ref[...]
    half = h2.shape[1] // 2
    n_tiles = half // LANES
    packed = _pack_bf16_pair(h2[:, :half], h2[:, half:])
    for c in range(n_tiles):
        h2p_ref[pl.ds(c, h2.shape[0], stride=n_tiles), :] = packed[:, c * LANES:(c + 1) * LANES]


def merge(o_f, o_b, proj, z_blk, gate_blk, na_out, x, dn_norm_g, w_a, w_b, w_o, norm_ffn_g, w_router, b_router,
          tm=256):
    s, d = x.shape
    ne = w_router.shape[1]
    const = lambda i: (0, 0)
    single = pl.Buffered(1)
    w_router_hi = w_router.astype(BF16)
    w_router_lo = (w_router.astype(F32) - w_router_hi.astype(F32)).astype(BF16)
    w_router_split = jnp.stack([w_router_hi, w_router_lo])
    return pl.pallas_call(
        _merge_body,
        grid=(s // tm,),
        in_specs=[
            pl.BlockSpec((tm, DN_WIDTH), lambda i: (i, 0)),
            pl.BlockSpec((tm, DN_WIDTH), lambda i: (i, 0)),
            pl.BlockSpec((tm, DN_WIDTH), lambda i: (i, z_blk)),
            pl.BlockSpec((tm, NA_WIDTH), lambda i: (i, 0)),
            pl.BlockSpec((tm, d), lambda i: (i, gate_blk)),
            pl.BlockSpec((tm, d), lambda i: (i, gate_blk + 1)),
            pl.BlockSpec((tm, d), lambda i: (i, 0)),
            pl.BlockSpec((1, DN_HEAD_DIM), const),
            pl.BlockSpec((DN_WIDTH, d), const, pipeline_mode=single),
            pl.BlockSpec((NA_WIDTH, d), const, pipeline_mode=single),
            pl.BlockSpec((d, d), const, pipeline_mode=single),
            pl.BlockSpec((1, d), const),
            pl.BlockSpec((2, d, ne), lambda i: (0, 0, 0)),
            pl.BlockSpec((1, ne), const),
        ],
        out_specs=[
            pl.BlockSpec((tm, d), lambda i: (i, 0)),
            pl.BlockSpec((tm * (d // 2 // LANES), LANES), lambda i: (i, 0)),
            pl.BlockSpec((tm, ne), lambda i: (i, 0)),
        ],
        out_shape=[
            jax.ShapeDtypeStruct((s, d), F32),
            jax.ShapeDtypeStruct((s * (d // 2 // LANES), LANES), U32),
            jax.ShapeDtypeStruct((s, ne), F32),
        ],
        scratch_shapes=[pltpu.VMEM((tm, DN_WIDTH), BF16)],
        compiler_params=_params(("parallel",)),
        name="merge",
    )(o_f, o_b, proj, na_out, proj, proj, x, dn_norm_g.reshape(1, -1).astype(F32), w_a, w_b, w_o,
      norm_ffn_g.reshape(1, d).astype(F32), w_router_split, b_router.reshape(1, ne).astype(F32))


def _route_body(lg_ref, ti_ref, tw_ref, cnt_ref, carry_ref):
    @pl.when(pl.program_id(0) == 0)
    def _():
        carry_ref[...] = jnp.zeros(carry_ref.shape, F32)

    lg = lg_ref[...]
    tm, ne = lg.shape
    lane = lax.broadcasted_iota(I32, (tm, ne), 1).astype(F32)
    work = lg
    vals, idxs = [], []
    onehot = jnp.zeros((tm, ne), F32)
    for _ in range(TOP_K):
        m = jnp.max(work, axis=-1, keepdims=True)
        idx = jnp.min(jnp.where(work == m, lane, float(ne)), axis=-1, keepdims=True)
        hit = lane == idx
        vals.append(m)
        idxs.append(idx)
        onehot = onehot + jnp.where(hit, 1.0, 0.0)
        work = jnp.where(hit, -jnp.inf, work)
    exps = [jnp.exp(v - vals[0]) for v in vals]
    denom = exps[0] + exps[1] + exps[2] + exps[3]
    ri = lax.broadcasted_iota(I32, (tm, tm), 0)
    ci = lax.broadcasted_iota(I32, (tm, tm), 1)
    strict = jnp.where(ci < ri, 1.0, 0.0).astype(BF16)
    before = _dot(strict, onehot.astype(BF16)) + carry_ref[0:1, 0:ne]
    lane_o = lax.broadcasted_iota(I32, (tm, LANES), 1)
    ti = jnp.zeros((tm, LANES), I32)
    tw = jnp.zeros((tm, LANES), F32)
    for kk in range(TOP_K):
        rank = jnp.sum(jnp.where(lane == idxs[kk], before, 0.0), axis=-1, keepdims=True).astype(I32)
        ti = jnp.where(lane_o == kk, idxs[kk].astype(I32), ti)
        ti = jnp.where(lane_o == TOP_K + kk, rank, ti)
        tw = jnp.where(lane_o == kk, exps[kk] / denom, tw)
    ti_ref[...] = ti
    tw_ref[...] = tw
    total = carry_ref[0:1, 0:ne] + jnp.sum(onehot, axis=0, keepdims=True)
    carry_ref[0:1, 0:ne] = total
    cnt_ref[...] = jnp.zeros(cnt_ref.shape, F32)
    cnt_ref[0:1, 0:ne] = total


def route(logits, tm=512):
    s, ne = logits.shape
    return pl.pallas_call(
        _route_body,
        grid=(s // tm,),
        in_specs=[pl.BlockSpec((tm, ne), lambda i: (i, 0))],
        out_specs=[
            pl.BlockSpec((tm, LANES), lambda i: (i, 0)),
            pl.BlockSpec((tm, LANES), lambda i: (i, 0)),
            pl.BlockSpec((8, LANES), lambda i: (0, 0)),
        ],
        out_shape=[
            jax.ShapeDtypeStruct((s, LANES), I32),
            jax.ShapeDtypeStruct((s, LANES), F32),
            jax.ShapeDtypeStruct((8, LANES), F32),
        ],
        scratch_shapes=[pltpu.VMEM((8, LANES), F32)],
        compiler_params=_params(("arbitrary",)),
        name="route",
    )(logits)


def _dest_body(ti_ref, ps_ref, d_ref):
    ti = ti_ref[...].astype(F32)
    tm = ti.shape[0]
    lane = lax.broadcasted_iota(I32, (tm, LANES), 1)
    lane_f = lane.astype(F32)
    ps = ps_ref[0:1, :].astype(F32)
    out = jnp.zeros((tm, LANES), F32)
    for kk in range(TOP_K):
        e = jnp.sum(jnp.where(lane == kk, ti, 0.0), axis=-1, keepdims=True)
        rank = jnp.sum(jnp.where(lane == TOP_K + kk, ti, 0.0), axis=-1, keepdims=True)
        start = jnp.sum(jnp.where(lane_f == e, ps, 0.0), axis=-1, keepdims=True)
        out = jnp.where(lane == kk, start + rank, out)
    d_ref[...] = out.astype(I32)


def route_dest(ti, pad_start, tm=512):
    s = ti.shape[0]
    ps = jnp.zeros((8, LANES), I32).at[0, :N_EXPERTS].set(pad_start)
    return pl.pallas_call(
        _dest_body,
        grid=(s // tm,),
        in_specs=[pl.BlockSpec((tm, LANES), lambda i: (i, 0)), pl.BlockSpec((8, LANES), lambda i: (0, 0))],
        out_specs=pl.BlockSpec((tm, LANES), lambda i: (i, 0)),
        out_shape=jax.ShapeDtypeStruct((s, LANES), I32),
        compiler_params=_params(("parallel",)),
        name="route_dest",
    )(ti, ps)


def _scatter_body(pend_ref, padded_ref, dest_ref, h_ref, xr_ref, zero_ref, sem):
    tm = h_ref.shape[0]

    def zero_copy(e):
        return pltpu.make_async_copy(zero_ref, xr_ref.at[pl.ds(pend_ref[e] - MOE_SUB, MOE_SUB)], sem)

    @pl.when(pl.program_id(0) == 0)
    def _():
        zero_ref[...] = jnp.zeros(zero_ref.shape, U32)

        def start(e, carry):
            @pl.when(padded_ref[e] > 0)
            def _():
                zero_copy(e).start()
            return carry

        def wait(e, carry):
            @pl.when(padded_ref[e] > 0)
            def _():
                zero_copy(e).wait()
            return carry

        lax.fori_loop(0, N_EXPERTS, start, 0)
        lax.fori_loop(0, N_EXPERTS, wait, 0)

        def slack_copy(b):
            return pltpu.make_async_copy(zero_ref, xr_ref.at[pl.ds(b * MOE_SUB, MOE_SUB)], sem)

        def slack_start(b, carry):
            slack_copy(b).start()
            return carry

        def slack_wait(b, carry):
            slack_copy(b).wait()
            return carry

        first_slack = pend_ref[N_EXPERTS - 1] // MOE_SUB
        lax.fori_loop(first_slack, xr_ref.shape[0] // MOE_SUB, slack_start, 0)
        lax.fori_loop(first_slack, xr_ref.shape[0] // MOE_SUB, slack_wait, 0)

    def row_copy(r, kk):
        d = dest_ref[r * TOP_K + kk]
        return pltpu.make_async_copy(h_ref.at[r], xr_ref.at[d], sem)

    def start(r, carry):
        for kk in range(TOP_K):
            row_copy(r, kk).start(priority=kk % 2)
        return carry

    lax.fori_loop(0, tm, start, 0)
    all_rows = xr_ref.at[pl.ds(0, tm * TOP_K)]
    pltpu.make_async_copy(all_rows, all_rows, sem).wait()


def moe_scatter(h2p, dest, pad_end, padded, n_rows, tm=256):
    s, nt, _ = h2p.shape
    grid_spec = pltpu.PrefetchScalarGridSpec(
        num_scalar_prefetch=2,
        grid=(s // tm,),
        in_specs=[
            pl.BlockSpec((tm * TOP_K,), lambda i, *_: (i,), memory_space=pltpu.SMEM),
            pl.BlockSpec((tm, nt, LANES), lambda i, *_: (i, 0, 0)),
        ],
        out_specs=pl.BlockSpec(memory_space=pl.ANY),
        scratch_shapes=[pltpu.VMEM((MOE_SUB, nt, LANES), U32), pltpu.SemaphoreType.DMA(())],
    )
    return pl.pallas_call(
        _scatter_body,
        grid_spec=grid_spec,
        out_shape=jax.ShapeDtypeStruct((n_rows, nt, LANES), U32),
        compiler_params=_params(("arbitrary",)),
        name="moe_scatter",
    )(pad_end, padded, dest, h2p)


def _moe_body(ie_ref, ir_ref, inb_ref, ni_ref, xr_ref, wg_ref, wu_ref, wd_ref, bg_ref, bu_ref, bd_ref, y_ref,
              stage_ref, xb_ref, acc_ref, wgb_ref, wub_ref, wdb_ref, sem_in, sem_out, *, n_chunks):
    ystage_ref = stage_ref
    w = pl.program_id(0)
    j = pl.program_id(1)
    sb = MOE_SUB
    d = acc_ref.shape[1]
    half = d // 2
    nx = half // LANES
    ny = half // LANES

    @pl.when(w < ni_ref[0])
    def _():
        nb = inb_ref[w]
        r0 = ir_ref[w]

        @pl.when(j == 0)
        def _():
            @pl.when(w == 0)
            def _():
                ystage_ref[0] = jnp.zeros(ystage_ref.shape[1:], U32)

                def slack_copy(b):
                    dst = y_ref.at[pl.ds(pl.multiple_of(b * (sb * ny), sb * ny), sb * ny), :]
                    return pltpu.make_async_copy(ystage_ref.at[0], dst, sem_out.at[0])

                def slack_start(b, carry):
                    slack_copy(b).start()
                    return carry

                def slack_wait(b, carry):
                    slack_copy(b).wait()
                    return carry

                n_blocks = y_ref.shape[0] // (sb * ny)
                lax.fori_loop(ni_ref[1], n_blocks, slack_start, 0)
                lax.fori_loop(ni_ref[1], n_blocks, slack_wait, 0)

            def in_copy(i, slot):
                src = xr_ref.at[pl.ds(pl.multiple_of((r0 + i * sb) * nx, sb * nx), sb * nx), :]
                return pltpu.make_async_copy(src, stage_ref.at[slot], sem_in.at[slot])

            in_copy(0, 0).start()

            def load(i, carry):
                slot = i % 2

                @pl.when(i + 1 < nb)
                def _():
                    in_copy(i + 1, 1 - slot).start()

                in_copy(i, slot).wait()
                rows = pl.ds(pl.multiple_of(i * sb, sb), sb)
                for c in range(nx):
                    lo, hi = _unpack_bf16_pair(stage_ref[slot, pl.ds(c, sb, stride=nx), :])
                    xb_ref[rows, c * LANES:(c + 1) * LANES] = lo.astype(BF16)
                    xb_ref[rows, half + c * LANES:half + (c + 1) * LANES] = hi.astype(BF16)
                return carry

            lax.fori_loop(0, nb, load, 0)

        def cast_weights():
            wgb_ref[...] = wg_ref[0].astype(BF16)
            wub_ref[...] = wu_ref[0].astype(BF16)
            wdb_ref[...] = wd_ref[0].astype(BF16)

        def block(first, n_sub, i, carry):
            start = i * sb if isinstance(i, int) else pl.multiple_of(i * sb, sb)
            rows = pl.ds(start, n_sub * sb)
            xs = xb_ref[rows, :]
            gate = jnp.minimum(_dot(xs, wgb_ref[...]) + bg_ref[0], SWIGLU_LIMIT)
            up = jnp.clip(_dot(xs, wub_ref[...]) + bu_ref[0], -SWIGLU_LIMIT, SWIGLU_LIMIT)
            act = (up + 1.0) * (gate * _sigmoid(SWIGLU_ALPHA * gate))
            contrib = _dot(act.astype(BF16), wdb_ref[...])
            if first:
                acc_ref[rows, :] = contrib
            else:
                acc_ref[rows, :] += contrib
            return carry

        def all_blocks(first):
            def pair(i2, carry):
                return block(first, 2, 2 * i2, carry)

            @pl.when(nb >= 2)
            def _():
                cast_weights()
                block(first, 2, 0, 0)
                lax.fori_loop(1, nb // 2, pair, 0)

                @pl.when(nb % 2 == 1)
                def _():
                    block(first, 1, nb - 1, 0)

            @pl.when(nb < 2)
            def _():
                cast_weights()
                block(first, 1, 0, 0)

        @pl.when(j == 0)
        def _():
            all_blocks(True)

        @pl.when(j > 0)
        def _():
            all_blocks(False)

        @pl.when(j == n_chunks - 1)
        def _():
            def out_copy(i, slot):
                dst = y_ref.at[pl.ds(pl.multiple_of((r0 + i * sb) * ny, sb * ny), sb * ny), :]
                return pltpu.make_async_copy(ystage_ref.at[slot], dst, sem_out.at[slot])

            def store(i, carry):
                slot = i % 2

                @pl.when(i >= 2)
                def _():
                    out_copy(i - 2, slot).wait()

                rows = pl.ds(pl.multiple_of(i * sb, sb), sb)
                for c in range(ny):
                    lo = slice(c * LANES, (c + 1) * LANES)
                    hi = slice(half + c * LANES, half + (c + 1) * LANES)
                    ystage_ref[slot, pl.ds(c, sb, stride=ny), :] = _pack_bf16_pair(
                        acc_ref[rows, lo] + bd_ref[0, :, lo], acc_ref[rows, hi] + bd_ref[0, :, hi])
                out_copy(i, slot).start()
                return carry

            lax.fori_loop(0, nb, store, 0)

            @pl.when(nb >= 2)
            def _():
                out_copy(nb - 2, nb % 2).wait()

            out_copy(nb - 1, (nb - 1) % 2).wait()


def moe_ffn(x_rows, w_gate_up, b_gate_up, w_down, b_down, item_e, item_row, item_nb, n_items, max_items):
    ne, d, two_de = w_gate_up.shape
    nx = d // 2 // LANES
    ny = d // 2 // LANES
    n_rows = x_rows.shape[0] // nx
    de = two_de // 2
    tn = MOE_TN
    n_chunks = de // tn
    last = n_chunks - 1

    def chunk(w, j, ni):
        return jnp.where(w < ni[0], j, last)

    grid_spec = pltpu.PrefetchScalarGridSpec(
        num_scalar_prefetch=4,
        grid=(max_items, n_chunks),
        in_specs=[
            pl.BlockSpec(memory_space=pl.ANY),
            pl.BlockSpec((1, d, tn), lambda w, j, ie, ir, inb, ni: (ie[w], 0, chunk(w, j, ni))),
            pl.BlockSpec((1, d, tn), lambda w, j, ie, ir, inb, ni: (ie[w], 0, n_chunks + chunk(w, j, ni))),
            pl.BlockSpec((1, tn, d), lambda w, j, ie, ir, inb, ni: (ie[w], chunk(w, j, ni), 0)),
            pl.BlockSpec((1, 1, tn), lambda w, j, ie, ir, inb, ni: (ie[w], 0, chunk(w, j, ni))),
            pl.BlockSpec((1, 1, tn), lambda w, j, ie, ir, inb, ni: (ie[w], 0, n_chunks + chunk(w, j, ni))),
            pl.BlockSpec((1, 1, d), lambda w, j, ie, ir, inb, ni: (ie[w], 0, 0)),
        ],
        out_specs=pl.BlockSpec(memory_space=pl.ANY),
        scratch_shapes=[
            pltpu.VMEM((2, MOE_SUB * nx, LANES), U32),
            pltpu.VMEM((MOE_TM, d), BF16),
            pltpu.VMEM((MOE_TM, d), F32),
            pltpu.VMEM((d, tn), BF16),
            pltpu.VMEM((d, tn), BF16),
            pltpu.VMEM((tn, d), BF16),
            pltpu.SemaphoreType.DMA((2,)),
            pltpu.SemaphoreType.DMA((2,)),
        ],
    )
    return pl.pallas_call(
        functools.partial(_moe_body, n_chunks=n_chunks),
        grid_spec=grid_spec,
        out_shape=jax.ShapeDtypeStruct((n_rows * ny, LANES), U32),
        compiler_params=_params(("arbitrary", "arbitrary"), MOE_VMEM_LIMIT),
        name="moe_ffn",
    )(item_e, item_row, item_nb, n_items, x_rows, w_gate_up, w_gate_up, w_down,
      b_gate_up.reshape(ne, 1, two_de), b_gate_up.reshape(ne, 1, two_de), b_down.reshape(ne, 1, d))


def _final_body(dest_ref, dest_next_ref, x1_ref, tw_ref, p_ref, gple_ref, wg_ref, wp_ref, gfin_ref, y_ref, o_ref,
                ybuf_ref, sem, *, last_layer):
    i = pl.program_id(0)
    tm = x1_ref.shape[0]
    half = x1_ref.shape[1] // 2
    ny = half // LANES
    slot = i % 2

    def row_copy(d_ref, to_slot, r, kk):
        dst = ybuf_ref.at[to_slot, kk, pl.ds(pl.multiple_of(r * ny, ny), ny), :]
        return pltpu.make_async_copy(y_ref.at[d_ref[r * TOP_K + kk]], dst, sem.at[to_slot])

    def issue(d_ref, to_slot):
        def body(r, carry):
            for kk in range(TOP_K):
                row_copy(d_ref, to_slot, r, kk).start(priority=kk % 2)
            return carry

        lax.fori_loop(0, tm, body, 0)

    def drain(to_slot):
        pltpu.make_async_copy(ybuf_ref.at[to_slot], ybuf_ref.at[to_slot], sem.at[to_slot]).wait()

    @pl.when(i == 0)
    def _():
        issue(dest_ref, 0)

    for r in range(tm):
        for kk in range(TOP_K):
            row_copy(dest_next_ref, 1 - slot, r, kk).start(priority=kk % 2)

    drain(slot)

    tw = tw_ref[...]
    los, his = [], []
    for c in range(ny):
        acc_lo = x1_ref[:, c * LANES:(c + 1) * LANES]
        acc_hi = x1_ref[:, half + c * LANES:half + (c + 1) * LANES]
        for kk in range(TOP_K):
            lo, hi = _unpack_bf16_pair(ybuf_ref[slot, kk, pl.ds(c, tm, stride=ny), :])
            acc_lo = acc_lo + tw[:, kk:kk + 1] * lo
            acc_hi = acc_hi + tw[:, kk:kk + 1] * hi
        los.append(acc_lo)
        his.append(acc_hi)
    x2 = jnp.concatenate(los + his, axis=1)
    r = lax.rsqrt(jnp.mean(x2 * x2, axis=-1, keepdims=True) + RMS_EPS)
    n = (x2 * r * gple_ref[...]).astype(BF16)
    gate = _sigmoid(_dot(n, wg_ref[...]))
    x3 = x2 + gate * _dot(p_ref[...].astype(BF16), wp_ref[...])
    if last_layer:
        r = lax.rsqrt(jnp.mean(x3 * x3, axis=-1, keepdims=True) + RMS_EPS)
        x3 = x3 * r * gfin_ref[...]
    o_ref[...] = x3

    @pl.when(i + 1 == pl.num_programs(0))
    def _():
        drain(1 - slot)


def final(dest, x1, tw, p, norm_ple_g, w_gate, w_proj, norm_final_g, y_rows, last_layer, tm=256):
    s, d = x1.shape
    pd = p.shape[1]
    ny = d // 2 // LANES
    n_steps = s // tm
    const = lambda i: (0, 0)
    single = pl.Buffered(1)
    return pl.pallas_call(
        functools.partial(_final_body, last_layer=last_layer),
        grid=(n_steps,),
        in_specs=[
            pl.BlockSpec((tm * TOP_K,), lambda i: (i,), memory_space=pltpu.SMEM),
            pl.BlockSpec((tm * TOP_K,), lambda i: (jnp.minimum(i + 1, n_steps - 1),), memory_space=pltpu.SMEM),
            pl.BlockSpec((tm, d), lambda i: (i, 0)),
            pl.BlockSpec((tm, LANES), lambda i: (i, 0)),
            pl.BlockSpec((tm, pd), lambda i: (i, 0)),
            pl.BlockSpec((1, d), const),
            pl.BlockSpec((d, d), const, pipeline_mode=single),
            pl.BlockSpec((pd, d), const, pipeline_mode=single),
            pl.BlockSpec((1, d), const),
            pl.BlockSpec(memory_space=pl.ANY),
        ],
        out_specs=pl.BlockSpec((tm, d), lambda i: (i, 0)),
        out_shape=jax.ShapeDtypeStruct((s, d), F32),
        scratch_shapes=[pltpu.VMEM((2, TOP_K, tm * ny, LANES), U32), pltpu.SemaphoreType.DMA((2,))],
        compiler_params=_params(("arbitrary",)),
        name="final",
    )(dest, dest, x1, tw, p, norm_ple_g.reshape(1, d).astype(F32), w_gate, w_proj,
      norm_final_g.reshape(1, d).astype(F32), y_rows.reshape(-1, ny, LANES))


def _moe_tables(counts, n_rows):
    sub, tm = MOE_SUB, MOE_TM
    max_items = N_EXPERTS + n_rows // tm
    padded = (counts + sub - 1) // sub * sub
    pad_end = jnp.cumsum(padded)
    pad_start = pad_end - padded
    n_it = (padded + tm - 1) // tm
    it_end = jnp.cumsum(n_it)
    it_start = it_end - n_it
    n_items = it_end[-1]
    w = jnp.arange(max_items, dtype=I32)
    live = w < n_items
    w_eff = jnp.minimum(w, n_items - 1)
    e_w = jnp.minimum(jnp.searchsorted(it_end, w_eff, side="right"), N_EXPERTS - 1).astype(I32)
    m_w = w_eff - it_start[e_w]
    row_w = pad_start[e_w] + m_w * tm
    nb_w = jnp.clip((padded[e_w] - m_w * tm) // sub, 0, tm // sub)
    nb_w = jnp.where(live, nb_w, 0)
    as_i32 = lambda a: a.astype(I32)
    counts_w = jnp.stack([n_items, pad_end[-1] // sub])
    return (as_i32(pad_start), as_i32(pad_end), as_i32(padded), e_w, as_i32(row_w), as_i32(nb_w),
            as_i32(counts_w), max_items)


def _layer(x, p, norm_mix_g, w_in, dn_conv_w, dn_a_log, dn_dt_bias, dn_norm_g, na_rpb, w_branch_a, w_branch_b,
           w_out, norm_ffn_g, w_router, b_router, w_gate_up, b_gate_up, w_down, b_down, norm_ple_g,
           w_ple_gate, w_ple_proj, norm_final_g, last_layer):
    s, d = x.shape
    c_qkv, c_z = 3 * DN_WIDTH, DN_WIDTH
    w_t = jnp.swapaxes(w_in, 0, 1)
    w_main = w_prep(w_t)
    z_blk = c_qkv // DN_WIDTH
    gate_blk = (c_qkv + c_z) // d
    na_col0 = c_qkv + c_z + 2 * d

    proj, small = in_projection(x, norm_mix_g.astype(F32), w_main, w_t)

    qkv = dn_prep(proj, dn_conv_w.astype(F32))
    col, row = dn_gates(small, dn_a_log, dn_dt_bias)
    u, w, qd, kt, qk = dn_chunk(qkv, col, row)
    o_f, o_b = dn_scan(u, w, qd, kt, qk, col)

    na_out = na_attention(proj, na_col0, na_rpb)

    x1, h2p, logits = merge(o_f, o_b, proj, z_blk, gate_blk, na_out, x, dn_norm_g, w_branch_a.astype(BF16),
                            w_branch_b.astype(BF16), w_out.astype(BF16), norm_ffn_g, w_router, b_router)

    ti, tw, cnt = route(logits)
    counts = cnt[0, :N_EXPERTS].astype(I32)
    n_rows = (s * TOP_K + N_EXPERTS * (MOE_SUB - 1) + MOE_SUB - 1) // MOE_SUB * MOE_SUB
    pad_start, pad_end, padded, item_e, item_row, item_nb, n_items, max_items = _moe_tables(counts, n_rows)
    dest = route_dest(ti, pad_start)[:, :TOP_K].reshape(-1)

    nx = d // 2 // LANES
    x_rows = moe_scatter(h2p.reshape(s, nx, LANES), dest, pad_end, padded, n_rows).reshape(n_rows * nx, LANES)
    y_rows = moe_ffn(x_rows, w_gate_up, b_gate_up, w_down, b_down, item_e, item_row, item_nb, n_items, max_items)

    return final(dest, x1, tw, p, norm_ple_g, w_ple_gate.astype(BF16), w_ple_proj.astype(BF16), norm_final_g,
                 y_rows, last_layer)


def kernel(x, p, norm_mix_g, w_in, dn_conv_w, dn_a_log, dn_dt_bias, dn_norm_g, na_rpb, w_branch_a, w_branch_b, w_out, norm_ffn_g, w_router, b_router, w_gate_up, b_gate_up, w_down, b_down, norm_ple_g, w_ple_gate, w_ple_proj, norm_final_g):
    bsz, s, d = x.shape
    depth = w_in.shape[0]
    outs = []
    for b in range(bsz):
        xb = x[b]
        for i in range(depth):
            xb = _layer(xb, p[i, b], norm_mix_g[i], w_in[i], dn_conv_w[i], dn_a_log[i], dn_dt_bias[i], dn_norm_g[i],
                        na_rpb[i], w_branch_a[i], w_branch_b[i], w_out[i], norm_ffn_g[i], w_router[i], b_router[i],
                        w_gate_up[i], b_gate_up[i], w_down[i], b_down[i], norm_ple_g[i], w_ple_gate[i],
                        w_ple_proj[i], norm_final_g, i == depth - 1)
        outs.append(xb)
    return jnp.stack(outs, axis=0)
```

```python
import functools

import jax
import jax.numpy as jnp
import numpy as np
from jax import lax
from jax.experimental import pallas as pl
from jax.experimental.pallas import tpu as pltpu

F32 = jnp.float32
BF16 = jnp.bfloat16
I32 = jnp.int32
U32 = jnp.uint32

GRID_W = 64
DN_HEADS = 8
DN_HEAD_DIM = 128
DN_WIDTH = DN_HEADS * DN_HEAD_DIM
DN_CONV = 5
NA_HEADS = 16
NA_HEAD_DIM = 64
NA_WIDTH = NA_HEADS * NA_HEAD_DIM
NA_ROWS = 8
NA_COLS = 16
N_EXPERTS = 32
TOP_K = 4
SWIGLU_LIMIT = 7.0
SWIGLU_ALPHA = 1.702
RMS_EPS = 1e-6

LANES = 128
VMEM_LIMIT = 56 * 1024 * 1024
MOE_VMEM_LIMIT = 60000 * 1024

DN_TILE = 256
DN_BLOCK = 16
MOE_SUB = 256
MOE_TM = 1536
MOE_TN = 512
NEG_BIG = -1e30


def _sigmoid(x):
    return 1.0 / (1.0 + jnp.exp(-x))


def _dot(a, b):
    return jnp.dot(a, b, preferred_element_type=F32)


def _dot_nt(a, b):
    return lax.dot_general(a, b, (((1,), (1,)), ((), ())), preferred_element_type=F32)


def _params(sem, limit=VMEM_LIMIT):
    return pltpu.CompilerParams(dimension_semantics=sem, vmem_limit_bytes=limit)


def _wprep_body(src_ref, o_ref, buf_ref, sem, *, starts, tn):
    t = pl.program_id(0)
    slot = t % 2

    def tile_copy(tt, to_slot):
        start = jnp.int32(starts[0])
        for k in range(1, len(starts)):
            start = jnp.where(tt >= k, starts[k], start)
        src = src_ref.at[pl.ds(pl.multiple_of(start, 8), tn), :]
        return pltpu.make_async_copy(src, buf_ref.at[to_slot], sem.at[to_slot])

    @pl.when(t == 0)
    def _():
        tile_copy(0, 0).start()

    @pl.when(t + 1 < pl.num_programs(0))
    def _():
        tile_copy(t + 1, 1 - slot).start()

    tile_copy(t, slot).wait()
    o_ref[...] = buf_ref[slot].T.astype(o_ref.dtype)


def w_prep(w_t, tn=1024):
    n_in, d = w_t.shape
    c_small = 4 * DN_HEADS
    o_small = 4 * DN_WIDTH
    o_na = o_small + c_small
    o_gates = o_na + 3 * NA_WIDTH
    assert o_small % tn == 0 and (3 * NA_WIDTH) % tn == 0 and (n_in - o_gates) % tn == 0
    assert o_na % 8 == 0 and o_gates % 8 == 0
    starts = list(range(0, o_small, tn)) + list(range(o_gates, n_in, tn)) + list(range(o_na, o_gates, tn))
    return pl.pallas_call(
        functools.partial(_wprep_body, starts=tuple(starts), tn=tn),
        grid=(len(starts),),
        in_specs=[pl.BlockSpec(memory_space=pl.ANY)],
        out_specs=pl.BlockSpec((d, tn), lambda t: (0, t)),
        out_shape=jax.ShapeDtypeStruct((d, len(starts) * tn), BF16),
        scratch_shapes=[pltpu.VMEM((2, tn, d), F32), pltpu.SemaphoreType.DMA((2,))],
        compiler_params=_params(("arbitrary",)),
        name="w_prep",
    )(w_t)


def _inproj_body(x_ref, g_ref, w_ref, ws_ref, o_ref, os_ref, h_ref):
    @pl.when(pl.program_id(1) == 0)
    def _():
        x = x_ref[...]
        r = lax.rsqrt(jnp.mean(x * x, axis=-1, keepdims=True) + RMS_EPS)
        h = (x * r * g_ref[...]).astype(BF16)
        h_ref[...] = h
        os_ref[...] = _dot_nt(h, ws_ref[...].astype(BF16))

    o_ref[...] = _dot(h_ref[...], w_ref[...]).astype(o_ref.dtype)


def in_projection(x, g, w_main, w_t, tm=1024, tn=1024):
    s, d = x.shape
    n = w_main.shape[1]
    small_blk = 4 * DN_WIDTH // LANES
    return pl.pallas_call(
        _inproj_body,
        grid=(s // tm, n // tn),
        in_specs=[
            pl.BlockSpec((tm, d), lambda i, j: (i, 0)),
            pl.BlockSpec((1, d), lambda i, j: (0, 0)),
            pl.BlockSpec((d, tn), lambda i, j: (0, j)),
            pl.BlockSpec((LANES, d), lambda i, j: (small_blk, 0)),
        ],
        out_specs=[
            pl.BlockSpec((tm, tn), lambda i, j: (i, j)),
            pl.BlockSpec((tm, LANES), lambda i, j: (i, 0)),
        ],
        out_shape=[jax.ShapeDtypeStruct((s, n), BF16), jax.ShapeDtypeStruct((s, LANES), F32)],
        scratch_shapes=[pltpu.VMEM((tm, d), BF16)],
        compiler_params=_params(("parallel", "arbitrary")),
        name="in_proj",
    )(x, g.reshape(1, d), w_main, w_t)


def _dnprep_body(x_ref, w_ref, o_ref, pad_ref, *, seq, chunk):
    cb = pl.program_id(0)
    n_chunks = seq // chunk
    zeros = jnp.zeros((16, LANES), F32)
    pad_ref[0:16, :] = zeros
    pad_ref[seq + 16:seq + 32, :] = zeros

    def fill(c, carry):
        r0 = pl.multiple_of(c * chunk, chunk)
        pad_ref[pl.ds(r0 + 16, chunk), :] = x_ref[pl.ds(r0, chunk), :].astype(F32)
        return carry

    lax.fori_loop(0, n_chunks, fill, 0)

    w = w_ref[...]
    is_v = cb >= 2 * DN_HEADS
    scale = jnp.where(cb < DN_HEADS, DN_HEAD_DIM ** -0.5, 1.0).astype(F32)

    def body(c, carry):
        r0 = pl.multiple_of(c * chunk, chunk)
        y = pad_ref[pl.ds(r0 + 14, chunk), :] * w[0:1]
        for j in range(1, DN_CONV):
            y = y + pad_ref[pl.ds(r0 + 14 + j, chunk), :] * w[j:j + 1]
        y = y * _sigmoid(y)
        ss = jnp.sum(y * y, axis=-1, keepdims=True)
        yn = y * (lax.rsqrt(ss + 1e-6) * scale)
        o_ref[pl.ds(r0, chunk), :] = jnp.where(is_v, y, yn).astype(o_ref.dtype)
        return carry

    lax.fori_loop(0, n_chunks, body, 0)


def dn_prep(proj, conv_w, chunk=512):
    s = proj.shape[0]
    nb = 3 * DN_WIDTH // LANES
    w = jnp.zeros((8, 3 * DN_WIDTH), F32).at[:DN_CONV].set(conv_w)
    return pl.pallas_call(
        functools.partial(_dnprep_body, seq=s, chunk=chunk),
        grid=(nb,),
        in_specs=[
            pl.BlockSpec((s, LANES), lambda c: (0, c)),
            pl.BlockSpec((8, LANES), lambda c: (0, c)),
        ],
        out_specs=pl.BlockSpec((s, LANES), lambda c: (0, c)),
        out_shape=jax.ShapeDtypeStruct((s, 3 * DN_WIDTH), BF16),
        scratch_shapes=[pltpu.VMEM((s + 32, LANES), F32)],
        compiler_params=_params(("parallel",)),
        name="dn_prep",
    )(proj, w)


def _gates_body(s_ref, par_ref, col_ref, row_ref):
    x = s_ref[...]
    t = x.shape[0]
    lane = lax.broadcasted_iota(I32, x.shape, 1)
    beta = _sigmoid(x)
    z = x + par_ref[1:2, :]
    softplus = jnp.maximum(z, 0.0) + jnp.log(1.0 + jnp.exp(-jnp.abs(z)))
    g = par_ref[0:1, :] * softplus
    ri = lax.broadcasted_iota(I32, (t, t), 0)
    ci = lax.broadcasted_iota(I32, (t, t), 1)
    lower = jnp.where(ci <= ri, 1.0, 0.0).astype(BF16)
    upper = jnp.where(ci >= ri, 1.0, 0.0).astype(BF16)
    g1 = g.astype(BF16)
    g2 = (g - g1.astype(F32)).astype(BF16)
    g3 = (g - g1.astype(F32) - g2.astype(F32)).astype(BF16)
    g_prefix = _dot(lower, g1) + _dot(lower, g2) + _dot(lower, g3)
    g_suffix = _dot(upper, g1) + _dot(upper, g2) + _dot(upper, g3)
    cum = jnp.where(lane < 16 + DN_HEADS, g_prefix, g_suffix)
    out = jnp.where(lane < 16, beta, jnp.where(lane < 32, cum, 0.0))
    col_ref[...] = out
    row_ref[...] = out.T


def dn_gates(small, a_log, dt_bias):
    s = small.shape[0]
    par = jnp.zeros((8, LANES), F32)
    par = par.at[0, 16:32].set(-jnp.exp(a_log.reshape(-1).astype(F32)))
    par = par.at[1, 16:32].set(dt_bias.reshape(-1).astype(F32))
    t = DN_TILE
    return pl.pallas_call(
        _gates_body,
        grid=(s // t,),
        in_specs=[
            pl.BlockSpec((t, LANES), lambda i: (i, 0)),
            pl.BlockSpec((8, LANES), lambda i: (0, 0)),
        ],
        out_specs=[
            pl.BlockSpec((t, LANES), lambda i: (i, 0)),
            pl.BlockSpec((LANES, t), lambda i: (0, i)),
        ],
        out_shape=[jax.ShapeDtypeStruct((s, LANES), F32), jax.ShapeDtypeStruct((LANES, s), F32)],
        compiler_params=_params(("parallel",)),
        name="dn_gates",
    )(small, par)


def _dnchunk_body(q_ref, k_ref, v_ref, col_ref, row_ref, u_ref, w_ref, qd_ref, kt_ref, qk_ref, *, heads_per_step):
    c = DN_TILE
    hd = DN_HEAD_DIM
    head0 = pl.program_id(1) * heads_per_step
    col = col_ref[...]
    row = row_ref[...]
    lane = lax.broadcasted_iota(I32, col.shape, 1)
    sub = lax.broadcasted_iota(I32, row.shape, 0)

    def col_pick(idx):
        return jnp.sum(jnp.where(lane == idx, col, 0.0), axis=1, keepdims=True)

    def row_pick(idx):
        return jnp.sum(jnp.where(sub == idx, row, 0.0), axis=0, keepdims=True)

    ri = lax.broadcasted_iota(I32, (c, c), 0)
    ci = lax.broadcasted_iota(I32, (c, c), 1)
    same_block = (ri // DN_BLOCK) == (ci // DN_BLOCK)
    incl = (ri >= ci, ri <= ci)
    strict = (ri > ci, ri < ci)
    heads = range(heads_per_step)
    chains = [(hh, d) for hh in heads for d in range(2)]
    sls = [slice(hh * hd, (hh + 1) * hd) for hh in heads]
    qs = [q_ref[:, sl] for sl in sls]
    ks = [k_ref[:, sl] for sl in sls]
    vs = [v_ref[:, sl] for sl in sls]
    grams = [_dot_nt(k, k) for k in ks]
    qks = [_dot_nt(q, k) for q, k in zip(qs, ks)]
    qfs = [q.astype(F32) for q in qs]
    kfs = [k.astype(F32) for k in ks]
    vfs = [v.astype(F32) for v in vs]

    betas = [col_pick(d * DN_HEADS + head0 + hh) for hh, d in chains]
    g_cols = [col_pick(16 + d * DN_HEADS + head0 + hh) for hh, d in chains]
    g_rows = [row_pick(16 + d * DN_HEADS + head0 + hh) for hh, d in chains]
    totals = [gr[:, c - 1:c] if d == 0 else gr[:, 0:1] for (hh, d), gr in zip(chains, g_rows)]
    decays = [jnp.where(incl[d], jnp.exp(jnp.minimum(gc - gr, 0.0)), 0.0)
              for (hh, d), gc, gr in zip(chains, g_cols, g_rows)]
    lows = [jnp.where(strict[d], b * grams[hh] * dec, 0.0) for (hh, d), b, dec in zip(chains, betas, decays)]
    l_diags = [jnp.where(same_block, low, 0.0) for low in lows]
    l_offs = [(low - ld).astype(BF16) for low, ld in zip(lows, l_diags)]

    def neumann(accs, x_bs, n_steps):
        for _ in range(n_steps):
            x2s = [_dot(x, x) for x in x_bs]
            x_bs = [x2.astype(BF16) for x2 in x2s]
            accs = [a + x2 + _dot(a.astype(BF16), xb) for a, x2, xb in zip(accs, x2s, x_bs)]
        return accs

    d_ms = neumann([-ld for ld in l_diags], [ld.astype(BF16) for ld in l_diags], (DN_BLOCK - 1).bit_length() - 1)
    d_bs = [dm.astype(BF16) for dm in d_ms]
    ms = [lo.astype(F32) + _dot(db, lo) for db, lo in zip(d_bs, l_offs)]
    q_ms = neumann([-m for m in ms], [m.astype(BF16) for m in ms], (c // DN_BLOCK - 1).bit_length() - 1)
    e_cols = [jnp.exp(gc) for gc in g_cols]
    rhss = [jnp.concatenate([vfs[hh] * b, kfs[hh] * b * ec], axis=1) for (hh, d), b, ec in zip(chains, betas, e_cols)]
    r1s = [rhs + _dot(db, rhs.astype(BF16)) for rhs, db in zip(rhss, d_bs)]
    sols = [r1 + _dot(qm.astype(BF16), r1.astype(BF16)) for r1, qm in zip(r1s, q_ms)]
    for i, (hh, d) in enumerate(chains):
        sl = sls[hh]
        u_ref[d, :, sl] = sols[i][:, :hd].astype(BF16)
        w_ref[d, :, sl] = sols[i][:, hd:].astype(BF16)
        qd_ref[d, :, sl] = (qfs[hh] * e_cols[i]).astype(BF16)
        kt_ref[d, sl, :] = (kfs[hh] * jnp.exp(totals[i] - g_cols[i])).T.astype(BF16)
        qk_ref[d, hh] = (qks[hh] * decays[i]).astype(BF16)


def dn_chunk(qkv, col, row, heads_per_step=4):
    s = qkv.shape[0]
    c = DN_TILE
    nt = s // c
    hd = DN_HEAD_DIM * heads_per_step
    groups = DN_HEADS // heads_per_step
    return pl.pallas_call(
        functools.partial(_dnchunk_body, heads_per_step=heads_per_step),
        grid=(nt, groups),
        in_specs=[
            pl.BlockSpec((c, hd), lambda t, h: (t, h)),
            pl.BlockSpec((c, hd), lambda t, h: (t, groups + h)),
            pl.BlockSpec((c, hd), lambda t, h: (t, 2 * groups + h)),
            pl.BlockSpec((c, LANES), lambda t, h: (t, 0)),
            pl.BlockSpec((LANES, c), lambda t, h: (0, t)),
        ],
        out_specs=[
            pl.BlockSpec((2, c, hd), lambda t, h: (0, t, h)),
            pl.BlockSpec((2, c, hd), lambda t, h: (0, t, h)),
            pl.BlockSpec((2, c, hd), lambda t, h: (0, t, h)),
            pl.BlockSpec((2, hd, c), lambda t, h: (0, h, t)),
            pl.BlockSpec((2, heads_per_step, c, c), lambda t, h: (0, h, t, 0)),
        ],
        out_shape=[
            jax.ShapeDtypeStruct((2, s, DN_WIDTH), BF16),
            jax.ShapeDtypeStruct((2, s, DN_WIDTH), BF16),
            jax.ShapeDtypeStruct((2, s, DN_WIDTH), BF16),
            jax.ShapeDtypeStruct((2, DN_WIDTH, s), BF16),
            jax.ShapeDtypeStruct((2, DN_HEADS, s, c), BF16),
        ],
        compiler_params=_params(("parallel", "parallel")),
        name="dn_chunk",
    )(qkv, qkv, qkv, col, row)


def _dnscan_body(uf, wf, qdf, ktf, qkf, colf, ub, wb, qdb, ktb, qkb, colb, of_ref, ob_ref, st_ref):
    @pl.when(pl.program_id(0) == 0)
    def _():
        st_ref[...] = jnp.zeros(st_ref.shape, F32)

    c = DN_TILE
    hd = DN_HEAD_DIM
    dirs = ((uf, wf, qdf, ktf, qkf, colf, of_ref, c - 1), (ub, wb, qdb, ktb, qkb, colb, ob_ref, 0))
    chains = [(d, h) for d in range(2) for h in range(DN_HEADS)]
    sls = [slice(h * hd, (h + 1) * hd) for h in range(DN_HEADS)]
    e_tots = [jnp.exp(dirs[d][5][dirs[d][7]:dirs[d][7] + 1, :]) for d in range(2)]
    states = [st_ref[d * DN_HEADS + h] for d, h in chains]
    states_b = [st.astype(BF16) for st in states]
    v_news = [dirs[d][0][0, :, sls[h]] - _dot(dirs[d][1][0, :, sls[h]], sb) for (d, h), sb in zip(chains, states_b)]
    v_news_b = [vn.astype(BF16) for vn in v_news]
    outs = [_dot(dirs[d][2][0, :, sls[h]], sb) + _dot(dirs[d][4][0, h], vb)
            for (d, h), sb, vb in zip(chains, states_b, v_news_b)]
    news = [st * e_tots[d][:, 16 + d * DN_HEADS + h:17 + d * DN_HEADS + h] + _dot(dirs[d][3][0, sls[h], :], vb)
            for (d, h), st, vb in zip(chains, states, v_news_b)]
    for (d, h), out, new in zip(chains, outs, news):
        dirs[d][6][:, sls[h]] = out
        st_ref[d * DN_HEADS + h] = new


def dn_scan(u, w, qd, kt, qk, col):
    s = u.shape[1]
    c = DN_TILE
    nt = s // c
    wd = DN_WIDTH

    def specs(d):
        tile = (lambda t: t) if d == 0 else (lambda t: nt - 1 - t)
        return [
            pl.BlockSpec((1, c, wd), lambda t: (d, tile(t), 0)),
            pl.BlockSpec((1, c, wd), lambda t: (d, tile(t), 0)),
            pl.BlockSpec((1, c, wd), lambda t: (d, tile(t), 0)),
            pl.BlockSpec((1, wd, c), lambda t: (d, 0, tile(t))),
            pl.BlockSpec((1, DN_HEADS, c, c), lambda t: (d, 0, tile(t), 0)),
            pl.BlockSpec((c, LANES), lambda t: (tile(t), 0)),
        ]

    return pl.pallas_call(
        _dnscan_body,
        grid=(nt,),
        in_specs=specs(0) + specs(1),
        out_specs=[
            pl.BlockSpec((c, wd), lambda t: (t, 0)),
            pl.BlockSpec((c, wd), lambda t: (nt - 1 - t, 0)),
        ],
        out_shape=[jax.ShapeDtypeStruct((s, wd), F32), jax.ShapeDtypeStruct((s, wd), F32)],
        scratch_shapes=[pltpu.VMEM((2 * DN_HEADS, DN_HEAD_DIM, DN_HEAD_DIM), F32)],
        compiler_params=_params(("arbitrary",)),
        name="dn_scan",
    )(u, w, qd, kt, qk, col, u, w, qd, kt, qk, col)


def _na_bias_table(rpb):
    n_heads, n_off, n_dc = rpb.shape
    return pl.pallas_call(
        functools.partial(_nabias_body, n_off=n_off, n_dc=n_dc),
        grid=(n_heads,),
        in_specs=[pl.BlockSpec(memory_space=pltpu.SMEM)],
        out_specs=pl.BlockSpec((1, n_off - 1, GRID_W, 2 * GRID_W), lambda h: (h, 0, 0, 0)),
        out_shape=jax.ShapeDtypeStruct((n_heads, n_off - 1, GRID_W, 2 * GRID_W), F32),
        compiler_params=_params(("parallel",)),
        name="na_bias",
    )(rpb.astype(F32).reshape(-1))


def _nabias_body(rpb_ref, o_ref, *, n_off, n_dc):
    h = pl.program_id(0)
    c = lax.broadcasted_iota(I32, (GRID_W, 2 * GRID_W), 0)
    lane = lax.broadcasted_iota(I32, (GRID_W, 2 * GRID_W), 1)
    second = lane >= GRID_W
    kc = jnp.where(second, lane - GRID_W, lane)
    cs = jnp.clip(c - NA_COLS // 2, 0, GRID_W - NA_COLS)
    valid = (kc >= cs) & (kc < cs + NA_COLS)
    dc = jnp.where(valid, kc - c + (NA_COLS - 1), -1)
    for m in range(n_off - 1):
        tile = jnp.full((GRID_W, 2 * GRID_W), NEG_BIG, F32)
        for j in range(n_dc):
            first_val = rpb_ref[(h * n_off + m) * n_dc + j]
            second_val = rpb_ref[(h * n_off + m + 1) * n_dc + j]
            tile = jnp.where(dc == j, jnp.where(second, second_val, first_val), tile)
        o_ref[0, m] = tile


def _na_body(q_ref, k_ref, v_ref, bias_ref, o_ref):
    lane = lax.broadcasted_iota(I32, (1, LANES), 1)
    heads_per_block = LANES // NA_HEAD_DIM
    scale = NA_HEAD_DIM ** -0.5
    n_pairs = NA_WIDTH // LANES
    heads = [(pair, hh) for pair in range(n_pairs) for hh in range(heads_per_block)]
    sls = [slice(pair * LANES, (pair + 1) * LANES) for pair in range(n_pairs)]
    owns = [(lane // NA_HEAD_DIM) == hh for hh in range(heads_per_block)]
    row_id = pl.program_id(0)
    first_offset = jnp.clip(row_id - NA_ROWS // 2, 0, pl.num_programs(0) - NA_ROWS) - row_id + (NA_ROWS - 1)

    def bias(h):
        return jnp.concatenate([bias_ref[h, first_offset + 2 * m] for m in range(NA_ROWS // 2)], axis=1)

    q2s = [q_ref[:, sl] * scale for sl in sls]
    k2s = [k_ref[:, sl] for sl in sls]
    scores = [_dot_nt(jnp.where(owns[hh], q2s[pair], jnp.zeros_like(q2s[pair])), k2s[pair])
              + bias(pair * heads_per_block + hh) for pair, hh in heads]
    maxes = [jnp.max(s, axis=-1, keepdims=True) for s in scores]
    probs = [jnp.exp(s - m) for s, m in zip(scores, maxes)]
    denoms = [jnp.sum(p, axis=-1, keepdims=True) for p in probs]
    v2s = [v_ref[:, sl] for sl in sls]
    outs = [_dot(p.astype(BF16), v2s[pair]) / den for (pair, hh), p, den in zip(heads, probs, denoms)]
    for pair in range(n_pairs):
        acc = outs[pair * heads_per_block]
        for hh in range(1, heads_per_block):
            acc = jnp.where(owns[hh], outs[pair * heads_per_block + hh], acc)
        o_ref[:, sls[pair]] = acc.astype(o_ref.dtype)


def na_attention(proj, col0, rpb):
    s = proj.shape[0]
    rows = s // GRID_W
    assert rows >= NA_ROWS
    qb = col0 // NA_WIDTH
    table = _na_bias_table(rpb)

    def first_row(r):
        return jnp.clip(r - NA_ROWS // 2, 0, rows - NA_ROWS)

    def kv_spec(blk):
        return pl.BlockSpec((pl.Element(NA_ROWS * GRID_W), pl.Element(NA_WIDTH)),
                            lambda r: (first_row(r) * GRID_W, blk * NA_WIDTH))

    in_specs = [
        pl.BlockSpec((GRID_W, NA_WIDTH), lambda r: (r, qb)),
        kv_spec(qb + 1),
        kv_spec(qb + 2),
        pl.BlockSpec(table.shape, lambda r: (0, 0, 0, 0), pipeline_mode=pl.Buffered(1)),
    ]
    return pl.pallas_call(
        _na_body,
        grid=(rows,),
        in_specs=in_specs,
        out_specs=pl.BlockSpec((GRID_W, NA_WIDTH), lambda r: (r, 0)),
        out_shape=jax.ShapeDtypeStruct((s, NA_WIDTH), BF16),
        compiler_params=_params(("parallel",)),
        name="na_attn",
    )(proj, proj, proj, table)


def _pack_bf16_pair(lo, hi):
    lo_bits = pltpu.bitcast(lo.astype(BF16).astype(F32), U32)
    hi_bits = pltpu.bitcast(hi.astype(BF16).astype(F32), U32)
    return (lo_bits >> 16) | (hi_bits & jnp.uint32(0xFFFF0000))


def _unpack_bf16_pair(packed):
    lo = pltpu.bitcast(packed << 16, F32)
    hi = pltpu.bitcast(packed & jnp.uint32(0xFFFF0000), F32)
    return lo, hi


def _merge_body(of_ref, ob_ref, z_ref, na_ref, ga_ref, gb_ref, x_ref, dng_ref, wa_ref, wb_ref, wo_ref,
                gffn_ref, wr_ref, br_ref, x1_ref, h2p_ref, lg_ref, dn_ref):
    hd = DN_HEAD_DIM
    for h in range(DN_HEADS):
        sl = slice(h * hd, (h + 1) * hd)
        o = of_ref[:, sl] + ob_ref[:, sl]
        r = lax.rsqrt(jnp.mean(o * o, axis=-1, keepdims=True) + RMS_EPS)
        z = z_ref[:, sl].astype(F32)
        dn_ref[:, sl] = (o * r * dng_ref[...] * (z * _sigmoid(z))).astype(BF16)
    y_a = _dot(dn_ref[...], wa_ref[...])
    y_b = _dot(na_ref[...], wb_ref[...])
    mixed = _sigmoid(ga_ref[...].astype(F32)) * y_a + _sigmoid(gb_ref[...].astype(F32)) * y_b
    x1 = x_ref[...] + _dot(mixed.astype(BF16), wo_ref[...])
    x1_ref[...] = x1
    r = lax.rsqrt(jnp.mean(x1 * x1, axis=-1, keepdims=True) + RMS_EPS)
    h2 = x1 * r * gffn_ref[...]
    h2_hi = h2.astype(BF16)
    h2_lo = (h2 - h2_hi.astype(F32)).astype(BF16)
    lg_ref[...] = (_dot(h2_hi, wr_ref[0]) + _dot(h2_lo, wr_ref[0]) + _dot(h2_hi, wr_ref[1])) + br_ref[...]
    half = h2.shape[1] // 2
    n_tiles = half // LANES
    packed = _pack_bf16_pair(h2[:, :half], h2[:, half:])
    for c in range(n_tiles):
        h2p_ref[pl.ds(c, h2.shape[0], stride=n_tiles), :] = packed[:, c * LANES:(c + 1) * LANES]


def merge(o_f, o_b, proj, z_blk, gate_blk, na_out, x, dn_norm_g, w_a, w_b, w_o, norm_ffn_g, w_router, b_router,
          tm=256):
    s, d = x.shape
    ne = w_router.shape[1]
    const = lambda i: (0, 0)
    single = pl.Buffered(1)
    w_router_hi = w_router.astype(BF16)
    w_router_lo = (w_router.astype(F32) - w_router_hi.astype(F32)).astype(BF16)
    w_router_split = jnp.stack([w_router_hi, w_router_lo])
    return pl.pallas_call(
        _merge_body,
        grid=(s // tm,),
        in_specs=[
            pl.BlockSpec((tm, DN_WIDTH), lambda i: (i, 0)),
            pl.BlockSpec((tm, DN_WIDTH), lambda i: (i, 0)),
            pl.BlockSpec((tm, DN_WIDTH), lambda i: (i, z_blk)),
            pl.BlockSpec((tm, NA_WIDTH), lambda i: (i, 0)),
            pl.BlockSpec((tm, d), lambda i: (i, gate_blk)),
            pl.BlockSpec((tm, d), lambda i: (i, gate_blk + 1)),
            pl.BlockSpec((tm, d), lambda i: (i, 0)),
            pl.BlockSpec((1, DN_HEAD_DIM), const),
            pl.BlockSpec((DN_WIDTH, d), const, pipeline_mode=single),
            pl.BlockSpec((NA_WIDTH, d), const, pipeline_mode=single),
            pl.BlockSpec((d, d), const, pipeline_mode=single),
            pl.BlockSpec((1, d), const),
            pl.BlockSpec((2, d, ne), lambda i: (0, 0, 0)),
            pl.BlockSpec((1, ne), const),
        ],
        out_specs=[
            pl.BlockSpec((tm, d), lambda i: (i, 0)),
            pl.BlockSpec((tm * (d // 2 // LANES), LANES), lambda i: (i, 0)),
            pl.BlockSpec((tm, ne), lambda i: (i, 0)),
        ],
        out_shape=[
            jax.ShapeDtypeStruct((s, d), F32),
            jax.ShapeDtypeStruct((s * (d // 2 // LANES), LANES), U32),
            jax.ShapeDtypeStruct((s, ne), F32),
        ],
        scratch_shapes=[pltpu.VMEM((tm, DN_WIDTH), BF16)],
        compiler_params=_params(("parallel",)),
        name="merge",
    )(o_f, o_b, proj, na_out, proj, proj, x, dn_norm_g.reshape(1, -1).astype(F32), w_a, w_b, w_o,
      norm_ffn_g.reshape(1, d).astype(F32), w_router_split, b_router.reshape(1, ne).astype(F32))


def _route_body(lg_ref, ti_ref, tw_ref, cnt_ref, carry_ref):
    @pl.when(pl.program_id(0) == 0)
    def _():
        carry_ref[...] = jnp.zeros(carry_ref.shape, F32)

    lg = lg_ref[...]
    tm, ne = lg.shape
    lane = lax.broadcasted_iota(I32, (tm, ne), 1).astype(F32)
    work = lg
    vals, idxs = [], []
    onehot = jnp.zeros((tm, ne), F32)
    for _ in range(TOP_K):
        m = jnp.max(work, axis=-1, keepdims=True)
        idx = jnp.min(jnp.where(work == m, lane, float(ne)), axis=-1, keepdims=True)
        hit = lane == idx
        vals.append(m)
        idxs.append(idx)
        onehot = onehot + jnp.where(hit, 1.0, 0.0)
        work = jnp.where(hit, -jnp.inf, work)
    exps = [jnp.exp(v - vals[0]) for v in vals]
    denom = exps[0] + exps[1] + exps[2] + exps[3]
    ri = lax.broadcasted_iota(I32, (tm, tm), 0)
    ci = lax.broadcasted_iota(I32, (tm, tm), 1)
    strict = jnp.where(ci < ri, 1.0, 0.0).astype(BF16)
    before = _dot(strict, onehot.astype(BF16)) + carry_ref[0:1, 0:ne]
    lane_o = lax.broadcasted_iota(I32, (tm, LANES), 1)
    ti = jnp.zeros((tm, LANES), I32)
    tw = jnp.zeros((tm, LANES), F32)
    for kk in range(TOP_K):
        rank = jnp.sum(jnp.where(lane == idxs[kk], before, 0.0), axis=-1, keepdims=True).astype(I32)
        ti = jnp.where(lane_o == kk, idxs[kk].astype(I32), ti)
        ti = jnp.where(lane_o == TOP_K + kk, rank, ti)
        tw = jnp.where(lane_o == kk, exps[kk] / denom, tw)
    ti_ref[...] = ti
    tw_ref[...] = tw
    total = carry_ref[0:1, 0:ne] + jnp.sum(onehot, axis=0, keepdims=True)
    carry_ref[0:1, 0:ne] = total
    cnt_ref[...] = jnp.zeros(cnt_ref.shape, F32)
    cnt_ref[0:1, 0:ne] = total


def route(logits, tm=512):
    s, ne = logits.shape
    return pl.pallas_call(
        _route_body,
        grid=(s // tm,),
        in_specs=[pl.BlockSpec((tm, ne), lambda i: (i, 0))],
        out_specs=[
            pl.BlockSpec((tm, LANES), lambda i: (i, 0)),
            pl.BlockSpec((tm, LANES), lambda i: (i, 0)),
            pl.BlockSpec((8, LANES), lambda i: (0, 0)),
        ],
        out_shape=[
            jax.ShapeDtypeStruct((s, LANES), I32),
            jax.ShapeDtypeStruct((s, LANES), F32),
            jax.ShapeDtypeStruct((8, LANES), F32),
        ],
        scratch_shapes=[pltpu.VMEM((8, LANES), F32)],
        compiler_params=_params(("arbitrary",)),
        name="route",
    )(logits)


def _dest_body(ti_ref, ps_ref, d_ref):
    ti = ti_ref[...].astype(F32)
    tm = ti.shape[0]
    lane = lax.broadcasted_iota(I32, (tm, LANES), 1)
    lane_f = lane.astype(F32)
    ps = ps_ref[0:1, :].astype(F32)
    out = jnp.zeros((tm, LANES), F32)
    for kk in range(TOP_K):
        e = jnp.sum(jnp.where(lane == kk, ti, 0.0), axis=-1, keepdims=True)
        rank = jnp.sum(jnp.where(lane == TOP_K + kk, ti, 0.0), axis=-1, keepdims=True)
        start = jnp.sum(jnp.where(lane_f == e, ps, 0.0), axis=-1, keepdims=True)
        out = jnp.where(lane == kk, start + rank, out)
    d_ref[...] = out.astype(I32)


def route_dest(ti, pad_start, tm=512):
    s = ti.shape[0]
    ps = jnp.zeros((8, LANES), I32).at[0, :N_EXPERTS].set(pad_start)
    return pl.pallas_call(
        _dest_body,
        grid=(s // tm,),
        in_specs=[pl.BlockSpec((tm, LANES), lambda i: (i, 0)), pl.BlockSpec((8, LANES), lambda i: (0, 0))],
        out_specs=pl.BlockSpec((tm, LANES), lambda i: (i, 0)),
        out_shape=jax.ShapeDtypeStruct((s, LANES), I32),
        compiler_params=_params(("parallel",)),
        name="route_dest",
    )(ti, ps)


def _scatter_body(pend_ref, padded_ref, dest_ref, h_ref, xr_ref, zero_ref, sem):
    tm = h_ref.shape[0]

    def zero_copy(e):
        return pltpu.make_async_copy(zero_ref, xr_ref.at[pl.ds(pend_ref[e] - MOE_SUB, MOE_SUB)], sem)

    @pl.when(pl.program_id(0) == 0)
    def _():
        zero_ref[...] = jnp.zeros(zero_ref.shape, U32)

        def start(e, carry):
            @pl.when(padded_ref[e] > 0)
            def _():
                zero_copy(e).start()
            return carry

        def wait(e, carry):
            @pl.when(padded_ref[e] > 0)
            def _():
                zero_copy(e).wait()
            return carry

        lax.fori_loop(0, N_EXPERTS, start, 0)
        lax.fori_loop(0, N_EXPERTS, wait, 0)

        def slack_copy(b):
            return pltpu.make_async_copy(zero_ref, xr_ref.at[pl.ds(b * MOE_SUB, MOE_SUB)], sem)

        def slack_start(b, carry):
            slack_copy(b).start()
            return carry

        def slack_wait(b, carry):
            slack_copy(b).wait()
            return carry

        first_slack = pend_ref[N_EXPERTS - 1] // MOE_SUB
        lax.fori_loop(first_slack, xr_ref.shape[0] // MOE_SUB, slack_start, 0)
        lax.fori_loop(first_slack, xr_ref.shape[0] // MOE_SUB, slack_wait, 0)

    def row_copy(r, kk):
        d = dest_ref[r * TOP_K + kk]
        return pltpu.make_async_copy(h_ref.at[r], xr_ref.at[d], sem)

    def start(r, carry):
        for kk in range(TOP_K):
            row_copy(r, kk).start(priority=kk % 2)
        return carry

    lax.fori_loop(0, tm, start, 0)
    all_rows = xr_ref.at[pl.ds(0, tm * TOP_K)]
    pltpu.make_async_copy(all_rows, all_rows, sem).wait()


def moe_scatter(h2p, dest, pad_end, padded, n_rows, tm=256):
    s, nt, _ = h2p.shape
    grid_spec = pltpu.PrefetchScalarGridSpec(
        num_scalar_prefetch=2,
        grid=(s // tm,),
        in_specs=[
            pl.BlockSpec((tm * TOP_K,), lambda i, *_: (i,), memory_space=pltpu.SMEM),
            pl.BlockSpec((tm, nt, LANES), lambda i, *_: (i, 0, 0)),
        ],
        out_specs=pl.BlockSpec(memory_space=pl.ANY),
        scratch_shapes=[pltpu.VMEM((MOE_SUB, nt, LANES), U32), pltpu.SemaphoreType.DMA(())],
    )
    return pl.pallas_call(
        _scatter_body,
        grid_spec=grid_spec,
        out_shape=jax.ShapeDtypeStruct((n_rows, nt, LANES), U32),
        compiler_params=_params(("arbitrary",)),
        name="moe_scatter",
    )(pad_end, padded, dest, h2p)


def _moe_body(ie_ref, ir_ref, inb_ref, ni_ref, xr_ref, wg_ref, wu_ref, wd_ref, bg_ref, bu_ref, bd_ref, y_ref,
              stage_ref, xb_ref, acc_ref, wgb_ref, wub_ref, wdb_ref, sem_in, sem_out, *, n_chunks):
    ystage_ref = stage_ref
    w = pl.program_id(0)
    j = pl.program_id(1)
    sb = MOE_SUB
    d = acc_ref.shape[1]
    half = d // 2
    nx = half // LANES
    ny = half // LANES

    @pl.when(w < ni_ref[0])
    def _():
        nb = inb_ref[w]
        r0 = ir_ref[w]

        @pl.when(j == 0)
        def _():
            @pl.when(w == 0)
            def _():
                ystage_ref[0] = jnp.zeros(ystage_ref.shape[1:], U32)

                def slack_copy(b):
                    dst = y_ref.at[pl.ds(pl.multiple_of(b * (sb * ny), sb * ny), sb * ny), :]
                    return pltpu.make_async_copy(ystage_ref.at[0], dst, sem_out.at[0])

                def slack_start(b, carry):
                    slack_copy(b).start()
                    return carry

                def slack_wait(b, carry):
                    slack_copy(b).wait()
                    return carry

                n_blocks = y_ref.shape[0] // (sb * ny)
                lax.fori_loop(ni_ref[1], n_blocks, slack_start, 0)
                lax.fori_loop(ni_ref[1], n_blocks, slack_wait, 0)

            def in_copy(i, slot):
                src = xr_ref.at[pl.ds(pl.multiple_of((r0 + i * sb) * nx, sb * nx), sb * nx), :]
                return pltpu.make_async_copy(src, stage_ref.at[slot], sem_in.at[slot])

            in_copy(0, 0).start(priority=1)

            def load(i, carry):
                slot = i % 2

                @pl.when(i + 1 < nb)
                def _():
                    in_copy(i + 1, 1 - slot).start(priority=1)

                in_copy(i, slot).wait()
                rows = pl.ds(pl.multiple_of(i * sb, sb), sb)
                for c in range(nx):
                    lo, hi = _unpack_bf16_pair(stage_ref[slot, pl.ds(c, sb, stride=nx), :])
                    xb_ref[rows, c * LANES:(c + 1) * LANES] = lo.astype(BF16)
                    xb_ref[rows, half + c * LANES:half + (c + 1) * LANES] = hi.astype(BF16)
                return carry

            lax.fori_loop(0, nb, load, 0)

        def cast_weights():
            wgb_ref[...] = wg_ref[0].astype(BF16)
            wub_ref[...] = wu_ref[0].astype(BF16)
            wdb_ref[...] = wd_ref[0].astype(BF16)

        def block(first, n_sub, i, carry):
            start = i * sb if isinstance(i, int) else pl.multiple_of(i * sb, sb)
            rows = pl.ds(start, n_sub * sb)
            xs = xb_ref[rows, :]
            gate = jnp.minimum(_dot(xs, wgb_ref[...]) + bg_ref[0], SWIGLU_LIMIT)
            up = jnp.clip(_dot(xs, wub_ref[...]) + bu_ref[0], -SWIGLU_LIMIT, SWIGLU_LIMIT)
            act = (up + 1.0) * (gate * _sigmoid(SWIGLU_ALPHA * gate))
            contrib = _dot(act.astype(BF16), wdb_ref[...])
            if first:
                acc_ref[rows, :] = contrib
            else:
                acc_ref[rows, :] += contrib
            return carry

        def all_blocks(first):
            def pair(i2, carry):
                return block(first, 2, 2 * i2, carry)

            @pl.when(nb >= 2)
            def _():
                cast_weights()
                block(first, 2, 0, 0)
                lax.fori_loop(1, nb // 2, pair, 0)

                @pl.when(nb % 2 == 1)
                def _():
                    block(first, 1, nb - 1, 0)

            @pl.when(nb < 2)
            def _():
                cast_weights()
                block(first, 1, 0, 0)

        @pl.when(j == 0)
        def _():
            all_blocks(True)

        @pl.when(j > 0)
        def _():
            all_blocks(False)

        @pl.when(j == n_chunks - 1)
        def _():
            def out_copy(i, slot):
                dst = y_ref.at[pl.ds(pl.multiple_of((r0 + i * sb) * ny, sb * ny), sb * ny), :]
                return pltpu.make_async_copy(ystage_ref.at[slot], dst, sem_out.at[slot])

            def store(i, carry):
                slot = i % 2

                @pl.when(i >= 2)
                def _():
                    out_copy(i - 2, slot).wait()

                rows = pl.ds(pl.multiple_of(i * sb, sb), sb)
                for c in range(ny):
                    lo = slice(c * LANES, (c + 1) * LANES)
                    hi = slice(half + c * LANES, half + (c + 1) * LANES)
                    ystage_ref[slot, pl.ds(c, sb, stride=ny), :] = _pack_bf16_pair(
                        acc_ref[rows, lo] + bd_ref[0, :, lo], acc_ref[rows, hi] + bd_ref[0, :, hi])
                out_copy(i, slot).start(priority=1)
                return carry

            lax.fori_loop(0, nb, store, 0)

            @pl.when(nb >= 2)
            def _():
                out_copy(nb - 2, nb % 2).wait()

            out_copy(nb - 1, (nb - 1) % 2).wait()


def moe_ffn(x_rows, w_gate_up, b_gate_up, w_down, b_down, item_e, item_row, item_nb, n_items, max_items):
    ne, d, two_de = w_gate_up.shape
    nx = d // 2 // LANES
    ny = d // 2 // LANES
    n_rows = x_rows.shape[0] // nx
    de = two_de // 2
    tn = MOE_TN
    n_chunks = de // tn
    last = n_chunks - 1

    def chunk(w, j, ni):
        return jnp.where(w < ni[0], j, last)

    grid_spec = pltpu.PrefetchScalarGridSpec(
        num_scalar_prefetch=4,
        grid=(max_items, n_chunks),
        in_specs=[
            pl.BlockSpec(memory_space=pl.ANY),
            pl.BlockSpec((1, d, tn), lambda w, j, ie, ir, inb, ni: (ie[w], 0, chunk(w, j, ni))),
            pl.BlockSpec((1, d, tn), lambda w, j, ie, ir, inb, ni: (ie[w], 0, n_chunks + chunk(w, j, ni))),
            pl.BlockSpec((1, tn, d), lambda w, j, ie, ir, inb, ni: (ie[w], chunk(w, j, ni), 0)),
            pl.BlockSpec((1, 1, tn), lambda w, j, ie, ir, inb, ni: (ie[w], 0, chunk(w, j, ni))),
            pl.BlockSpec((1, 1, tn), lambda w, j, ie, ir, inb, ni: (ie[w], 0, n_chunks + chunk(w, j, ni))),
            pl.BlockSpec((1, 1, d), lambda w, j, ie, ir, inb, ni: (ie[w], 0, 0)),
        ],
        out_specs=pl.BlockSpec(memory_space=pl.ANY),
        scratch_shapes=[
            pltpu.VMEM((2, MOE_SUB * nx, LANES), U32),
            pltpu.VMEM((MOE_TM, d), BF16),
            pltpu.VMEM((MOE_TM, d), F32),
            pltpu.VMEM((d, tn), BF16),
            pltpu.VMEM((d, tn), BF16),
            pltpu.VMEM((tn, d), BF16),
            pltpu.SemaphoreType.DMA((2,)),
            pltpu.SemaphoreType.DMA((2,)),
        ],
    )
    return pl.pallas_call(
        functools.partial(_moe_body, n_chunks=n_chunks),
        grid_spec=grid_spec,
        out_shape=jax.ShapeDtypeStruct((n_rows * ny, LANES), U32),
        compiler_params=_params(("arbitrary", "arbitrary"), MOE_VMEM_LIMIT),
        name="moe_ffn",
    )(item_e, item_row, item_nb, n_items, x_rows, w_gate_up, w_gate_up, w_down,
      b_gate_up.reshape(ne, 1, two_de), b_gate_up.reshape(ne, 1, two_de), b_down.reshape(ne, 1, d))


def _final_body(dest_ref, dest_next_ref, x1_ref, tw_ref, p_ref, gple_ref, wg_ref, wp_ref, gfin_ref, y_ref, o_ref,
                ybuf_ref, sem, *, last_layer):
    i = pl.program_id(0)
    tm = x1_ref.shape[0]
    half = x1_ref.shape[1] // 2
    ny = half // LANES
    slot = i % 2

    def row_copy(d_ref, to_slot, r, kk):
        dst = ybuf_ref.at[to_slot, kk, pl.ds(pl.multiple_of(r * ny, ny), ny), :]
        return pltpu.make_async_copy(y_ref.at[d_ref[r * TOP_K + kk]], dst, sem.at[to_slot])

    def issue(d_ref, to_slot):
        def body(r, carry):
            for kk in range(TOP_K):
                row_copy(d_ref, to_slot, r, kk).start(priority=kk % 2)
            return carry

        lax.fori_loop(0, tm, body, 0)

    def drain(to_slot):
        pltpu.make_async_copy(ybuf_ref.at[to_slot], ybuf_ref.at[to_slot], sem.at[to_slot]).wait()

    @pl.when(i == 0)
    def _():
        issue(dest_ref, 0)

    for r in range(tm):
        for kk in range(TOP_K):
            row_copy(dest_next_ref, 1 - slot, r, kk).start(priority=kk % 2)

    drain(slot)

    tw = tw_ref[...]
    los, his = [], []
    for c in range(ny):
        acc_lo = x1_ref[:, c * LANES:(c + 1) * LANES]
        acc_hi = x1_ref[:, half + c * LANES:half + (c + 1) * LANES]
        for kk in range(TOP_K):
            lo, hi = _unpack_bf16_pair(ybuf_ref[slot, kk, pl.ds(c, tm, stride=ny), :])
            acc_lo = acc_lo + tw[:, kk:kk + 1] * lo
            acc_hi = acc_hi + tw[:, kk:kk + 1] * hi
        los.append(acc_lo)
        his.append(acc_hi)
    x2 = jnp.concatenate(los + his, axis=1)
    r = lax.rsqrt(jnp.mean(x2 * x2, axis=-1, keepdims=True) + RMS_EPS)
    n = (x2 * r * gple_ref[...]).astype(BF16)
    gate = _sigmoid(_dot(n, wg_ref[...]))
    x3 = x2 + gate * _dot(p_ref[...].astype(BF16), wp_ref[...])
    if last_layer:
        r = lax.rsqrt(jnp.mean(x3 * x3, axis=-1, keepdims=True) + RMS_EPS)
        x3 = x3 * r * gfin_ref[...]
    o_ref[...] = x3

    @pl.when(i + 1 == pl.num_programs(0))
    def _():
        drain(1 - slot)


def final(dest, x1, tw, p, norm_ple_g, w_gate, w_proj, norm_final_g, y_rows, last_layer, tm=256):
    s, d = x1.shape
    pd = p.shape[1]
    ny = d // 2 // LANES
    n_steps = s // tm
    const = lambda i: (0, 0)
    single = pl.Buffered(1)
    return pl.pallas_call(
        functools.partial(_final_body, last_layer=last_layer),
        grid=(n_steps,),
        in_specs=[
            pl.BlockSpec((tm * TOP_K,), lambda i: (i,), memory_space=pltpu.SMEM),
            pl.BlockSpec((tm * TOP_K,), lambda i: (jnp.minimum(i + 1, n_steps - 1),), memory_space=pltpu.SMEM),
            pl.BlockSpec((tm, d), lambda i: (i, 0)),
            pl.BlockSpec((tm, LANES), lambda i: (i, 0)),
            pl.BlockSpec((tm, pd), lambda i: (i, 0)),
            pl.BlockSpec((1, d), const),
            pl.BlockSpec((d, d), const, pipeline_mode=single),
            pl.BlockSpec((pd, d), const, pipeline_mode=single),
            pl.BlockSpec((1, d), const),
            pl.BlockSpec(memory_space=pl.ANY),
        ],
        out_specs=pl.BlockSpec((tm, d), lambda i: (i, 0)),
        out_shape=jax.ShapeDtypeStruct((s, d), F32),
        scratch_shapes=[pltpu.VMEM((2, TOP_K, tm * ny, LANES), U32), pltpu.SemaphoreType.DMA((2,))],
        compiler_params=_params(("arbitrary",)),
        name="final",
    )(dest, dest, x1, tw, p, norm_ple_g.reshape(1, d).astype(F32), w_gate, w_proj,
      norm_final_g.reshape(1, d).astype(F32), y_rows.reshape(-1, ny, LANES))


def _moe_tables(counts, n_rows):
    sub, tm = MOE_SUB, MOE_TM
    max_items = N_EXPERTS + n_rows // tm
    padded = (counts + sub - 1) // sub * sub
    pad_end = jnp.cumsum(padded)
    pad_start = pad_end - padded
    n_it = (padded + tm - 1) // tm
    it_end = jnp.cumsum(n_it)
    it_start = it_end - n_it
    n_items = it_end[-1]
    w = jnp.arange(max_items, dtype=I32)
    live = w < n_items
    w_eff = jnp.minimum(w, n_items - 1)
    e_w = jnp.minimum(jnp.sum((it_end[None, :] <= w_eff[:, None]).astype(I32), axis=1), N_EXPERTS - 1)
    m_w = w_eff - it_start[e_w]
    row_w = pad_start[e_w] + m_w * tm
    nb_w = jnp.clip((padded[e_w] - m_w * tm) // sub, 0, tm // sub)
    nb_w = jnp.where(live, nb_w, 0)
    as_i32 = lambda a: a.astype(I32)
    counts_w = jnp.stack([n_items, pad_end[-1] // sub])
    return (as_i32(pad_start), as_i32(pad_end), as_i32(padded), e_w, as_i32(row_w), as_i32(nb_w),
            as_i32(counts_w), max_items)


def _layer(x, p, norm_mix_g, w_in, dn_conv_w, dn_a_log, dn_dt_bias, dn_norm_g, na_rpb, w_branch_a, w_branch_b,
           w_out, norm_ffn_g, w_router, b_router, w_gate_up, b_gate_up, w_down, b_down, norm_ple_g,
           w_ple_gate, w_ple_proj, norm_final_g, last_layer):
    s, d = x.shape
    c_qkv, c_z = 3 * DN_WIDTH, DN_WIDTH
    w_t = jnp.swapaxes(w_in, 0, 1)
    w_main = w_prep(w_t)
    z_blk = c_qkv // DN_WIDTH
    gate_blk = (c_qkv + c_z) // d
    na_col0 = c_qkv + c_z + 2 * d

    proj, small = in_projection(x, norm_mix_g.astype(F32), w_main, w_t)

    qkv = dn_prep(proj, dn_conv_w.astype(F32))
    col, row = dn_gates(small, dn_a_log, dn_dt_bias)
    u, w, qd, kt, qk = dn_chunk(qkv, col, row)
    o_f, o_b = dn_scan(u, w, qd, kt, qk, col)

    na_out = na_attention(proj, na_col0, na_rpb)

    x1, h2p, logits = merge(o_f, o_b, proj, z_blk, gate_blk, na_out, x, dn_norm_g, w_branch_a.astype(BF16),
                            w_branch_b.astype(BF16), w_out.astype(BF16), norm_ffn_g, w_router, b_router)

    ti, tw, cnt = route(logits)
    counts = cnt[0, :N_EXPERTS].astype(I32)
    n_rows = (s * TOP_K + N_EXPERTS * (MOE_SUB - 1) + MOE_SUB - 1) // MOE_SUB * MOE_SUB
    pad_start, pad_end, padded, item_e, item_row, item_nb, n_items, max_items = _moe_tables(counts, n_rows)
    dest = route_dest(ti, pad_start)[:, :TOP_K].reshape(-1)

    nx = d // 2 // LANES
    x_rows = moe_scatter(h2p.reshape(s, nx, LANES), dest, pad_end, padded, n_rows).reshape(n_rows * nx, LANES)
    y_rows = moe_ffn(x_rows, w_gate_up, b_gate_up, w_down, b_down, item_e, item_row, item_nb, n_items, max_items)

    return final(dest, x1, tw, p, norm_ple_g, w_ple_gate.astype(BF16), w_ple_proj.astype(BF16), norm_final_g,
                 y_rows, last_layer)


def kernel(x, p, norm_mix_g, w_in, dn_conv_w, dn_a_log, dn_dt_bias, dn_norm_g, na_rpb, w_branch_a, w_branch_b, w_out, norm_ffn_g, w_router, b_router, w_gate_up, b_gate_up, w_down, b_down, norm_ple_g, w_ple_gate, w_ple_proj, norm_final_g):
    bsz, s, d = x.shape
    depth = w_in.shape[0]
    outs = []
    for b in range(bsz):
        xb = x[b]
        for i in range(depth):
            xb = _layer(xb, p[i, b], norm_mix_g[i], w_in[i], dn_conv_w[i], dn_a_log[i], dn_dt_bias[i], dn_norm_g[i],
                        na_rpb[i], w_branch_a[i], w_branch_b[i], w_out[i], norm_ffn_g[i], w_router[i], b_router[i],
                        w_gate_up[i], b_gate_up[i], w_down[i], b_down[i], norm_ple_g[i], w_ple_gate[i],
                        w_ple_proj[i], norm_final_g, i == depth - 1)
        outs.append(xb)
    return jnp.stack(outs, axis=0)
```

```python
import functools

import jax
import jax.numpy as jnp
import numpy as np
from jax import lax
from jax.experimental import pallas as pl
from jax.experimental.pallas import tpu as pltpu

F32 = jnp.float32
BF16 = jnp.bfloat16
I32 = jnp.int32
U32 = jnp.uint32

GRID_W = 64
DN_HEADS = 8
DN_HEAD_DIM = 128
DN_WIDTH = DN_HEADS * DN_HEAD_DIM
DN_CONV = 5
NA_HEADS = 16
NA_HEAD_DIM = 64
NA_WIDTH = NA_HEADS * NA_HEAD_DIM
NA_ROWS = 8
NA_COLS = 16
N_EXPERTS = 32
TOP_K = 4
SWIGLU_LIMIT = 7.0
SWIGLU_ALPHA = 1.702
RMS_EPS = 1e-6

LANES = 128
VMEM_LIMIT = 56 * 1024 * 1024
MOE_VMEM_LIMIT = 60000 * 1024

DN_TILE = 256
DN_BLOCK = 16
DN_PAD = 16
DN_G_LANE = 2 * DN_HEADS
MOE_SUB = 256
MOE_TM = 1536
MOE_TN = 512
NEG_BIG = -1e30


def _sigmoid(x):
    return 1.0 / (1.0 + jnp.exp(-x))


def _dot(a, b):
    return jnp.dot(a, b, preferred_element_type=F32)


def _dot_nt(a, b):
    return lax.dot_general(a, b, (((1,), (1,)), ((), ())), preferred_element_type=F32)


def _params(sem, limit=VMEM_LIMIT):
    return pltpu.CompilerParams(dimension_semantics=sem, vmem_limit_bytes=limit)


def _wprep_body(src_ref, o_ref, buf_ref, sem, *, starts, tn):
    t = pl.program_id(0)
    slot = t % 2

    def tile_copy(tt, to_slot):
        start = jnp.int32(starts[0])
        for k in range(1, len(starts)):
            start = jnp.where(tt >= k, starts[k], start)
        src = src_ref.at[pl.ds(pl.multiple_of(start, 8), tn), :]
        return pltpu.make_async_copy(src, buf_ref.at[to_slot], sem.at[to_slot])

    @pl.when(t == 0)
    def _():
        tile_copy(0, 0).start()

    @pl.when(t + 1 < pl.num_programs(0))
    def _():
        tile_copy(t + 1, 1 - slot).start()

    tile_copy(t, slot).wait()
    o_ref[...] = buf_ref[slot].T.astype(o_ref.dtype)


def w_prep(w_t, tn=1024):
    n_in, d = w_t.shape
    c_small = 4 * DN_HEADS
    o_small = 4 * DN_WIDTH
    o_na = o_small + c_small
    o_gates = o_na + 3 * NA_WIDTH
    assert o_small % tn == 0 and (3 * NA_WIDTH) % tn == 0 and (n_in - o_gates) % tn == 0
    assert o_na % 8 == 0 and o_gates % 8 == 0
    starts = list(range(0, o_small, tn)) + list(range(o_gates, n_in, tn)) + list(range(o_na, o_gates, tn))
    return pl.pallas_call(
        functools.partial(_wprep_body, starts=tuple(starts), tn=tn),
        grid=(len(starts),),
        in_specs=[pl.BlockSpec(memory_space=pl.ANY)],
        out_specs=pl.BlockSpec((d, tn), lambda t: (0, t)),
        out_shape=jax.ShapeDtypeStruct((d, len(starts) * tn), BF16),
        scratch_shapes=[pltpu.VMEM((2, tn, d), F32), pltpu.SemaphoreType.DMA((2,))],
        compiler_params=_params(("arbitrary",)),
        name="w_prep",
    )(w_t)


def _inproj_body(x_ref, g_ref, w_ref, ws_ref, o_ref, os_ref, h_ref):
    @pl.when(pl.program_id(1) == 0)
    def _():
        x = x_ref[...]
        r = lax.rsqrt(jnp.mean(x * x, axis=-1, keepdims=True) + RMS_EPS)
        h = (x * r * g_ref[...]).astype(BF16)
        h_ref[...] = h
        os_ref[...] = _dot_nt(h, ws_ref[...].astype(BF16))

    o_ref[...] = _dot(h_ref[...], w_ref[...]).astype(o_ref.dtype)


def in_projection(x, g, w_main, w_t, tm=1024, tn=1024):
    s, d = x.shape
    n = w_main.shape[1]
    small_blk = 4 * DN_WIDTH // LANES
    return pl.pallas_call(
        _inproj_body,
        grid=(s // tm, n // tn),
        in_specs=[
            pl.BlockSpec((tm, d), lambda i, j: (i, 0)),
            pl.BlockSpec((1, d), lambda i, j: (0, 0)),
            pl.BlockSpec((d, tn), lambda i, j: (0, j)),
            pl.BlockSpec((LANES, d), lambda i, j: (small_blk, 0)),
        ],
        out_specs=[
            pl.BlockSpec((tm, tn), lambda i, j: (i, j)),
            pl.BlockSpec((tm, LANES), lambda i, j: (i, 0)),
        ],
        out_shape=[jax.ShapeDtypeStruct((s, n), BF16), jax.ShapeDtypeStruct((s, LANES), F32)],
        scratch_shapes=[pltpu.VMEM((tm, d), BF16)],
        compiler_params=_params(("parallel", "arbitrary")),
        name="in_proj",
    )(x, g.reshape(1, d), w_main, w_t)


def _dnprep_body(x_ref, w_ref, o_ref, pad_ref, *, seq, chunk):
    cb = pl.program_id(0)
    n_chunks = seq // chunk
    zeros = jnp.zeros((DN_PAD, LANES), F32)
    pad_ref[0:DN_PAD, :] = zeros
    pad_ref[seq + DN_PAD:seq + 2 * DN_PAD, :] = zeros

    def fill(c, carry):
        r0 = pl.multiple_of(c * chunk, chunk)
        pad_ref[pl.ds(r0 + DN_PAD, chunk), :] = x_ref[pl.ds(r0, chunk), :].astype(F32)
        return carry

    lax.fori_loop(0, n_chunks, fill, 0)

    w = w_ref[...]
    is_v = cb >= 2 * DN_HEADS
    scale = jnp.where(cb < DN_HEADS, DN_HEAD_DIM ** -0.5, 1.0).astype(F32)

    def body(c, carry):
        r0 = pl.multiple_of(c * chunk, chunk)
        first = DN_PAD - (DN_CONV - 1) // 2
        y = pad_ref[pl.ds(r0 + first, chunk), :] * w[0:1]
        for j in range(1, DN_CONV):
            y = y + pad_ref[pl.ds(r0 + first + j, chunk), :] * w[j:j + 1]
        y = y * _sigmoid(y)
        ss = jnp.sum(y * y, axis=-1, keepdims=True)
        yn = y * (lax.rsqrt(ss + 1e-6) * scale)
        o_ref[pl.ds(r0, chunk), :] = jnp.where(is_v, y, yn).astype(o_ref.dtype)
        return carry

    lax.fori_loop(0, n_chunks, body, 0)


def dn_prep(proj, conv_w, chunk=512):
    s = proj.shape[0]
    nb = 3 * DN_WIDTH // LANES
    w = jnp.zeros((8, 3 * DN_WIDTH), F32).at[:DN_CONV].set(conv_w)
    return pl.pallas_call(
        functools.partial(_dnprep_body, seq=s, chunk=chunk),
        grid=(nb,),
        in_specs=[
            pl.BlockSpec((s, LANES), lambda c: (0, c)),
            pl.BlockSpec((8, LANES), lambda c: (0, c)),
        ],
        out_specs=pl.BlockSpec((s, LANES), lambda c: (0, c)),
        out_shape=jax.ShapeDtypeStruct((s, 3 * DN_WIDTH), BF16),
        scratch_shapes=[pltpu.VMEM((s + 2 * DN_PAD, LANES), F32)],
        compiler_params=_params(("parallel",)),
        name="dn_prep",
    )(proj, w)


def _gates_body(s_ref, par_ref, col_ref, row_ref):
    x = s_ref[...]
    t = x.shape[0]
    lane = lax.broadcasted_iota(I32, x.shape, 1)
    beta = _sigmoid(x)
    z = x + par_ref[1:2, :]
    softplus = jnp.maximum(z, 0.0) + jnp.log(1.0 + jnp.exp(-jnp.abs(z)))
    g = par_ref[0:1, :] * softplus
    ri = lax.broadcasted_iota(I32, (t, t), 0)
    ci = lax.broadcasted_iota(I32, (t, t), 1)
    lower = jnp.where(ci <= ri, 1.0, 0.0).astype(BF16)
    upper = jnp.where(ci >= ri, 1.0, 0.0).astype(BF16)
    g1 = g.astype(BF16)
    g2 = (g - g1.astype(F32)).astype(BF16)
    g3 = (g - g1.astype(F32) - g2.astype(F32)).astype(BF16)
    g_prefix = _dot(lower, g1) + _dot(lower, g2) + _dot(lower, g3)
    g_suffix = _dot(upper, g1) + _dot(upper, g2) + _dot(upper, g3)
    cum = jnp.where(lane < DN_G_LANE + DN_HEADS, g_prefix, g_suffix)
    out = jnp.where(lane < DN_G_LANE, beta, jnp.where(lane < 2 * DN_G_LANE, cum, 0.0))
    col_ref[...] = out
    row_ref[...] = out.T


def dn_gates(small, a_log, dt_bias):
    s = small.shape[0]
    par = jnp.zeros((8, LANES), F32)
    par = par.at[0, DN_G_LANE:2 * DN_G_LANE].set(-jnp.exp(a_log.reshape(-1).astype(F32)))
    par = par.at[1, DN_G_LANE:2 * DN_G_LANE].set(dt_bias.reshape(-1).astype(F32))
    t = DN_TILE
    return pl.pallas_call(
        _gates_body,
        grid=(s // t,),
        in_specs=[
            pl.BlockSpec((t, LANES), lambda i: (i, 0)),
            pl.BlockSpec((8, LANES), lambda i: (0, 0)),
        ],
        out_specs=[
            pl.BlockSpec((t, LANES), lambda i: (i, 0)),
            pl.BlockSpec((LANES, t), lambda i: (0, i)),
        ],
        out_shape=[jax.ShapeDtypeStruct((s, LANES), F32), jax.ShapeDtypeStruct((LANES, s), F32)],
        compiler_params=_params(("parallel",)),
        name="dn_gates",
    )(small, par)


def _dnchunk_body(q_ref, k_ref, v_ref, col_ref, row_ref, u_ref, w_ref, qd_ref, kt_ref, qk_ref, *, heads_per_step):
    c = DN_TILE
    hd = DN_HEAD_DIM
    head0 = pl.program_id(1) * heads_per_step
    col = col_ref[...]
    row = row_ref[...]
    lane = lax.broadcasted_iota(I32, col.shape, 1)
    sub = lax.broadcasted_iota(I32, row.shape, 0)

    def col_pick(idx):
        return jnp.sum(jnp.where(lane == idx, col, 0.0), axis=1, keepdims=True)

    def row_pick(idx):
        return jnp.sum(jnp.where(sub == idx, row, 0.0), axis=0, keepdims=True)

    ri = lax.broadcasted_iota(I32, (c, c), 0)
    ci = lax.broadcasted_iota(I32, (c, c), 1)
    same_block = (ri // DN_BLOCK) == (ci // DN_BLOCK)
    incl = (ri >= ci, ri <= ci)
    strict = (ri > ci, ri < ci)
    heads = range(heads_per_step)
    chains = [(hh, d) for hh in heads for d in range(2)]
    sls = [slice(hh * hd, (hh + 1) * hd) for hh in heads]
    qs = [q_ref[:, sl] for sl in sls]
    ks = [k_ref[:, sl] for sl in sls]
    vs = [v_ref[:, sl] for sl in sls]
    grams = [_dot_nt(k, k) for k in ks]
    qks = [_dot_nt(q, k) for q, k in zip(qs, ks)]
    qfs = [q.astype(F32) for q in qs]
    kfs = [k.astype(F32) for k in ks]
    vfs = [v.astype(F32) for v in vs]

    betas = [col_pick(d * DN_HEADS + head0 + hh) for hh, d in chains]
    g_cols = [col_pick(DN_G_LANE + d * DN_HEADS + head0 + hh) for hh, d in chains]
    g_rows = [row_pick(DN_G_LANE + d * DN_HEADS + head0 + hh) for hh, d in chains]
    totals = [gr[:, c - 1:c] if d == 0 else gr[:, 0:1] for (hh, d), gr in zip(chains, g_rows)]
    decays = [jnp.where(incl[d], jnp.exp(jnp.minimum(gc - gr, 0.0)), 0.0)
              for (hh, d), gc, gr in zip(chains, g_cols, g_rows)]
    lows = [jnp.where(strict[d], b * grams[hh] * dec, 0.0) for (hh, d), b, dec in zip(chains, betas, decays)]
    l_diags = [jnp.where(same_block, low, 0.0) for low in lows]
    l_offs = [(low - ld).astype(BF16) for low, ld in zip(lows, l_diags)]

    def neumann(accs, x_bs, n_steps):
        for _ in range(n_steps):
            x2s = [_dot(x, x) for x in x_bs]
            x_bs = [x2.astype(BF16) for x2 in x2s]
            accs = [a + x2 + _dot(a.astype(BF16), xb) for a, x2, xb in zip(accs, x2s, x_bs)]
        return accs

    d_ms = neumann([-ld for ld in l_diags], [ld.astype(BF16) for ld in l_diags], (DN_BLOCK - 1).bit_length() - 1)
    d_bs = [dm.astype(BF16) for dm in d_ms]
    ms = [lo.astype(F32) + _dot(db, lo) for db, lo in zip(d_bs, l_offs)]
    q_ms = neumann([-m for m in ms], [m.astype(BF16) for m in ms], (c // DN_BLOCK - 1).bit_length() - 1)
    e_cols = [jnp.exp(gc) for gc in g_cols]
    rhss = [jnp.concatenate([vfs[hh] * b, kfs[hh] * b * ec], axis=1) for (hh, d), b, ec in zip(chains, betas, e_cols)]
    r1s = [rhs + _dot(db, rhs.astype(BF16)) for rhs, db in zip(rhss, d_bs)]
    sols = [r1 + _dot(qm.astype(BF16), r1.astype(BF16)) for r1, qm in zip(r1s, q_ms)]
    for i, (hh, d) in enumerate(chains):
        sl = sls[hh]
        u_ref[d, :, sl] = sols[i][:, :hd].astype(BF16)
        w_ref[d, :, sl] = sols[i][:, hd:].astype(BF16)
        qd_ref[d, :, sl] = (qfs[hh] * e_cols[i]).astype(BF16)
        kt_ref[d, sl, :] = (kfs[hh] * jnp.exp(totals[i] - g_cols[i])).T.astype(BF16)
        qk_ref[d, hh] = (qks[hh] * decays[i]).astype(BF16)


def dn_chunk(qkv, col, row, heads_per_step=4):
    s = qkv.shape[0]
    c = DN_TILE
    nt = s // c
    hd = DN_HEAD_DIM * heads_per_step
    groups = DN_HEADS // heads_per_step
    return pl.pallas_call(
        functools.partial(_dnchunk_body, heads_per_step=heads_per_step),
        grid=(nt, groups),
        in_specs=[
            pl.BlockSpec((c, hd), lambda t, h: (t, h)),
            pl.BlockSpec((c, hd), lambda t, h: (t, groups + h)),
            pl.BlockSpec((c, hd), lambda t, h: (t, 2 * groups + h)),
            pl.BlockSpec((c, LANES), lambda t, h: (t, 0)),
            pl.BlockSpec((LANES, c), lambda t, h: (0, t)),
        ],
        out_specs=[
            pl.BlockSpec((2, c, hd), lambda t, h: (0, t, h)),
            pl.BlockSpec((2, c, hd), lambda t, h: (0, t, h)),
            pl.BlockSpec((2, c, hd), lambda t, h: (0, t, h)),
            pl.BlockSpec((2, hd, c), lambda t, h: (0, h, t)),
            pl.BlockSpec((2, heads_per_step, c, c), lambda t, h: (0, h, t, 0)),
        ],
        out_shape=[
            jax.ShapeDtypeStruct((2, s, DN_WIDTH), BF16),
            jax.ShapeDtypeStruct((2, s, DN_WIDTH), BF16),
            jax.ShapeDtypeStruct((2, s, DN_WIDTH), BF16),
            jax.ShapeDtypeStruct((2, DN_WIDTH, s), BF16),
            jax.ShapeDtypeStruct((2, DN_HEADS, s, c), BF16),
        ],
        compiler_params=_params(("parallel", "parallel")),
        name="dn_chunk",
    )(qkv, qkv, qkv, col, row)


def _dnscan_body(uf, wf, qdf, ktf, qkf, colf, ub, wb, qdb, ktb, qkb, colb, of_ref, ob_ref, st_ref):
    @pl.when(pl.program_id(0) == 0)
    def _():
        st_ref[...] = jnp.zeros(st_ref.shape, F32)

    c = DN_TILE
    hd = DN_HEAD_DIM
    dirs = ((uf, wf, qdf, ktf, qkf, colf, of_ref, c - 1), (ub, wb, qdb, ktb, qkb, colb, ob_ref, 0))
    chains = [(d, h) for d in range(2) for h in range(DN_HEADS)]
    sls = [slice(h * hd, (h + 1) * hd) for h in range(DN_HEADS)]
    e_tots = [jnp.exp(dirs[d][5][dirs[d][7]:dirs[d][7] + 1, :]) for d in range(2)]
    states = [st_ref[d * DN_HEADS + h] for d, h in chains]
    states_b = [st.astype(BF16) for st in states]
    v_news = [dirs[d][0][0, :, sls[h]] - _dot(dirs[d][1][0, :, sls[h]], sb) for (d, h), sb in zip(chains, states_b)]
    v_news_b = [vn.astype(BF16) for vn in v_news]
    outs = [_dot(dirs[d][2][0, :, sls[h]], sb) + _dot(dirs[d][4][0, h], vb)
            for (d, h), sb, vb in zip(chains, states_b, v_news_b)]
    news = [st * e_tots[d][:, DN_G_LANE + d * DN_HEADS + h:DN_G_LANE + d * DN_HEADS + h + 1]
            + _dot(dirs[d][3][0, sls[h], :], vb)
            for (d, h), st, vb in zip(chains, states, v_news_b)]
    for (d, h), out, new in zip(chains, outs, news):
        dirs[d][6][:, sls[h]] = out
        st_ref[d * DN_HEADS + h] = new


def dn_scan(u, w, qd, kt, qk, col):
    s = u.shape[1]
    c = DN_TILE
    nt = s // c
    wd = DN_WIDTH

    def specs(d):
        tile = (lambda t: t) if d == 0 else (lambda t: nt - 1 - t)
        return [
            pl.BlockSpec((1, c, wd), lambda t: (d, tile(t), 0)),
            pl.BlockSpec((1, c, wd), lambda t: (d, tile(t), 0)),
            pl.BlockSpec((1, c, wd), lambda t: (d, tile(t), 0)),
            pl.BlockSpec((1, wd, c), lambda t: (d, 0, tile(t))),
            pl.BlockSpec((1, DN_HEADS, c, c), lambda t: (d, 0, tile(t), 0)),
            pl.BlockSpec((c, LANES), lambda t: (tile(t), 0)),
        ]

    return pl.pallas_call(
        _dnscan_body,
        grid=(nt,),
        in_specs=specs(0) + specs(1),
        out_specs=[
            pl.BlockSpec((c, wd), lambda t: (t, 0)),
            pl.BlockSpec((c, wd), lambda t: (nt - 1 - t, 0)),
        ],
        out_shape=[jax.ShapeDtypeStruct((s, wd), F32), jax.ShapeDtypeStruct((s, wd), F32)],
        scratch_shapes=[pltpu.VMEM((2 * DN_HEADS, DN_HEAD_DIM, DN_HEAD_DIM), F32)],
        compiler_params=_params(("arbitrary",)),
        name="dn_scan",
    )(u, w, qd, kt, qk, col, u, w, qd, kt, qk, col)


def _na_bias_table(rpb):
    n_heads, n_off, n_dc = rpb.shape
    return pl.pallas_call(
        functools.partial(_nabias_body, n_off=n_off, n_dc=n_dc),
        grid=(n_heads,),
        in_specs=[pl.BlockSpec(memory_space=pltpu.SMEM)],
        out_specs=pl.BlockSpec((1, n_off - 1, GRID_W, 2 * GRID_W), lambda h: (h, 0, 0, 0)),
        out_shape=jax.ShapeDtypeStruct((n_heads, n_off - 1, GRID_W, 2 * GRID_W), F32),
        compiler_params=_params(("parallel",)),
        name="na_bias",
    )(rpb.astype(F32).reshape(-1))


def _nabias_body(rpb_ref, o_ref, *, n_off, n_dc):
    h = pl.program_id(0)
    c = lax.broadcasted_iota(I32, (GRID_W, 2 * GRID_W), 0)
    lane = lax.broadcasted_iota(I32, (GRID_W, 2 * GRID_W), 1)
    second = lane >= GRID_W
    kc = jnp.where(second, lane - GRID_W, lane)
    cs = jnp.clip(c - NA_COLS // 2, 0, GRID_W - NA_COLS)
    valid = (kc >= cs) & (kc < cs + NA_COLS)
    dc = jnp.where(valid, kc - c + (NA_COLS - 1), -1)
    for m in range(n_off - 1):
        tile = jnp.full((GRID_W, 2 * GRID_W), NEG_BIG, F32)
        for j in range(n_dc):
            first_val = rpb_ref[(h * n_off + m) * n_dc + j]
            second_val = rpb_ref[(h * n_off + m + 1) * n_dc + j]
            tile = jnp.where(dc == j, jnp.where(second, second_val, first_val), tile)
        o_ref[0, m] = tile


def _na_body(q_ref, k_ref, v_ref, bias_ref, o_ref, *, rows_per_step, n_rows):
    lane = lax.broadcasted_iota(I32, (1, LANES), 1)
    heads_per_block = LANES // NA_HEAD_DIM
    scale = NA_HEAD_DIM ** -0.5
    n_pairs = NA_WIDTH // LANES
    n_keys = NA_ROWS * GRID_W
    sls = [slice(pair * LANES, (pair + 1) * LANES) for pair in range(n_pairs)]
    owns = [(lane // NA_HEAD_DIM) == hh for hh in range(heads_per_block)]

    def window_start(row):
        return jnp.clip(row - NA_ROWS // 2, 0, n_rows - NA_ROWS)

    row0 = pl.program_id(0) * rows_per_step
    block_start = jnp.minimum(window_start(row0), n_rows - (NA_ROWS + rows_per_step - 1))
    starts = [window_start(row0 + a) for a in range(rows_per_step)]
    key_rows = [pl.ds(pl.multiple_of((starts[a] - block_start) * GRID_W, GRID_W), n_keys)
                for a in range(rows_per_step)]
    first_offsets = [starts[a] - (row0 + a) + (NA_ROWS - 1) for a in range(rows_per_step)]

    def bias(a, h):
        return jnp.concatenate([bias_ref[h, first_offsets[a] + 2 * m] for m in range(NA_ROWS // 2)], axis=1)

    chains = [(a, pair, hh) for a in range(rows_per_step) for pair in range(n_pairs) for hh in range(heads_per_block)]
    q2s = {(a, pair): q_ref[a * GRID_W:(a + 1) * GRID_W, sls[pair]] * scale
           for a in range(rows_per_step) for pair in range(n_pairs)}
    k2s = {(a, pair): k_ref[key_rows[a], sls[pair]] for a in range(rows_per_step) for pair in range(n_pairs)}
    scores = [_dot_nt(jnp.where(owns[hh], q2s[a, pair], jnp.zeros_like(q2s[a, pair])), k2s[a, pair])
              + bias(a, pair * heads_per_block + hh) for a, pair, hh in chains]
    maxes = [jnp.max(s, axis=-1, keepdims=True) for s in scores]
    probs = [jnp.exp(s - m) for s, m in zip(scores, maxes)]
    denoms = [jnp.sum(p, axis=-1, keepdims=True) for p in probs]
    v2s = {(a, pair): v_ref[key_rows[a], sls[pair]] for a in range(rows_per_step) for pair in range(n_pairs)}
    outs = {chain: _dot(p.astype(BF16), v2s[chain[0], chain[1]]) / den
            for chain, p, den in zip(chains, probs, denoms)}
    for a in range(rows_per_step):
        for pair in range(n_pairs):
            acc = outs[a, pair, 0]
            for hh in range(1, heads_per_block):
                acc = jnp.where(owns[hh], outs[a, pair, hh], acc)
            o_ref[a * GRID_W:(a + 1) * GRID_W, sls[pair]] = acc.astype(o_ref.dtype)


def na_attention(proj, col0, rpb, rows_per_step=2):
    s = proj.shape[0]
    rows = s // GRID_W
    win = NA_ROWS + rows_per_step - 1
    assert rows >= win and rows % rows_per_step == 0
    qb = col0 // NA_WIDTH
    table = _na_bias_table(rpb)

    def block_start(step):
        first = jnp.clip(step * rows_per_step - NA_ROWS // 2, 0, rows - NA_ROWS)
        return jnp.minimum(first, rows - win)

    def kv_spec(blk):
        return pl.BlockSpec((pl.Element(win * GRID_W), pl.Element(NA_WIDTH)),
                            lambda r: (block_start(r) * GRID_W, blk * NA_WIDTH))

    in_specs = [
        pl.BlockSpec((rows_per_step * GRID_W, NA_WIDTH), lambda r: (r, qb)),
        kv_spec(qb + 1),
        kv_spec(qb + 2),
        pl.BlockSpec(table.shape, lambda r: (0, 0, 0, 0), pipeline_mode=pl.Buffered(1)),
    ]
    return pl.pallas_call(
        functools.partial(_na_body, rows_per_step=rows_per_step, n_rows=rows),
        grid=(rows // rows_per_step,),
        in_specs=in_specs,
        out_specs=pl.BlockSpec((rows_per_step * GRID_W, NA_WIDTH), lambda r: (r, 0)),
        out_shape=jax.ShapeDtypeStruct((s, NA_WIDTH), BF16),
        compiler_params=_params(("parallel",)),
        name="na_attn",
    )(proj, proj, proj, table)


def _pack_bf16_pair(lo, hi):
    lo_bits = pltpu.bitcast(lo.astype(BF16).astype(F32), U32)
    hi_bits = pltpu.bitcast(hi.astype(BF16).astype(F32), U32)
    return (lo_bits >> 16) | (hi_bits & jnp.uint32(0xFFFF0000))


def _unpack_bf16_pair(packed):
    lo = pltpu.bitcast(packed << 16, F32)
    hi = pltpu.bitcast(packed & jnp.uint32(0xFFFF0000), F32)
    return lo, hi


def _merge_body(of_ref, ob_ref, z_ref, na_ref, ga_ref, gb_ref, x_ref, dng_ref, wa_ref, wb_ref, wo_ref,
                gffn_ref, wr_ref, br_ref, x1_ref, h2p_ref, lg_ref, dn_ref):
    hd = DN_HEAD_DIM
    for h in range(DN_HEADS):
        sl = slice(h * hd, (h + 1) * hd)
        o = of_ref[:, sl] + ob_ref[:, sl]
        r = lax.rsqrt(jnp.mean(o * o, axis=-1, keepdims=True) + RMS_EPS)
        z = z_ref[:, sl].astype(F32)
        dn_ref[:, sl] = (o * r * dng_ref[...] * (z * _sigmoid(z))).astype(BF16)
    y_a = _dot(dn_ref[...], wa_ref[...])
    y_b = _dot(na_ref[...], wb_ref[...])
    mixed = _sigmoid(ga_ref[...].astype(F32)) * y_a + _sigmoid(gb_ref[...].astype(F32)) * y_b
    x1 = x_ref[...] + _dot(mixed.astype(BF16), wo_ref[...])
    x1_ref[...] = x1
    r = lax.rsqrt(jnp.mean(x1 * x1, axis=-1, keepdims=True) + RMS_EPS)
    h2 = x1 * r * gffn_ref[...]
    h2_hi = h2.astype(BF16)
    h2_lo = (h2 - h2_hi.astype(F32)).astype(BF16)
    lg_ref[...] = (_dot(h2_hi, wr_ref[0]) + _dot(h2_lo, wr_ref[0]) + _dot(h2_hi, wr_ref[1])) + br_ref[...]
    half = h2.shape[1] // 2
    n_tiles = half // LANES
    packed = _pack_bf16_pair(h2[:, :half], h2[:, half:])
    for c in range(n_tiles):
        h2p_ref[pl.ds(c, h2.shape[0], stride=n_tiles), :] = packed[:, c * LANES:(c + 1) * LANES]


def merge(o_f, o_b, proj, z_blk, gate_blk, na_out, x, dn_norm_g, w_a, w_b, w_o, norm_ffn_g, w_router, b_router,
          tm=256):
    s, d = x.shape
    ne = w_router.shape[1]
    const = lambda i: (0, 0)
    single = pl.Buffered(1)
    w_router_hi = w_router.astype(BF16)
    w_router_lo = (w_router.astype(F32) - w_router_hi.astype(F32)).astype(BF16)
    w_router_split = jnp.stack([w_router_hi, w_router_lo])
    return pl.pallas_call(
        _merge_body,
        grid=(s // tm,),
        in_specs=[
            pl.BlockSpec((tm, DN_WIDTH), lambda i: (i, 0)),
            pl.BlockSpec((tm, DN_WIDTH), lambda i: (i, 0)),
            pl.BlockSpec((tm, DN_WIDTH), lambda i: (i, z_blk)),
            pl.BlockSpec((tm, NA_WIDTH), lambda i: (i, 0)),
            pl.BlockSpec((tm, d), lambda i: (i, gate_blk)),
            pl.BlockSpec((tm, d), lambda i: (i, gate_blk + 1)),
            pl.BlockSpec((tm, d), lambda i: (i, 0)),
            pl.BlockSpec((1, DN_HEAD_DIM), const),
            pl.BlockSpec((DN_WIDTH, d), const, pipeline_mode=single),
            pl.BlockSpec((NA_WIDTH, d), const, pipeline_mode=single),
            pl.BlockSpec((d, d), const, pipeline_mode=single),
            pl.BlockSpec((1, d), const),
            pl.BlockSpec((2, d, ne), lambda i: (0, 0, 0)),
            pl.BlockSpec((1, ne), const),
        ],
        out_specs=[
            pl.BlockSpec((tm, d), lambda i: (i, 0)),
            pl.BlockSpec((tm * (d // 2 // LANES), LANES), lambda i: (i, 0)),
            pl.BlockSpec((tm, ne), lambda i: (i, 0)),
        ],
        out_shape=[
            jax.ShapeDtypeStruct((s, d), F32),
            jax.ShapeDtypeStruct((s * (d // 2 // LANES), LANES), U32),
            jax.ShapeDtypeStruct((s, ne), F32),
        ],
        scratch_shapes=[pltpu.VMEM((tm, DN_WIDTH), BF16)],
        compiler_params=_params(("parallel",)),
        name="merge",
    )(o_f, o_b, proj, na_out, proj, proj, x, dn_norm_g.reshape(1, -1).astype(F32), w_a, w_b, w_o,
      norm_ffn_g.reshape(1, d).astype(F32), w_router_split, b_router.reshape(1, ne).astype(F32))


def _route_body(lg_ref, ti_ref, tw_ref, cnt_ref, carry_ref):
    @pl.when(pl.program_id(0) == 0)
    def _():
        carry_ref[...] = jnp.zeros(carry_ref.shape, F32)

    lg = lg_ref[...]
    tm, ne = lg.shape
    lane = lax.broadcasted_iota(I32, (tm, ne), 1).astype(F32)
    work = lg
    vals, idxs = [], []
    onehot = jnp.zeros((tm, ne), F32)
    for _ in range(TOP_K):
        m = jnp.max(work, axis=-1, keepdims=True)
        idx = jnp.min(jnp.where(work == m, lane, float(ne)), axis=-1, keepdims=True)
        hit = lane == idx
        vals.append(m)
        idxs.append(idx)
        onehot = onehot + jnp.where(hit, 1.0, 0.0)
        work = jnp.where(hit, -jnp.inf, work)
    exps = [jnp.exp(v - vals[0]) for v in vals]
    denom = exps[0] + exps[1] + exps[2] + exps[3]
    ri = lax.broadcasted_iota(I32, (tm, tm), 0)
    ci = lax.broadcasted_iota(I32, (tm, tm), 1)
    strict = jnp.where(ci < ri, 1.0, 0.0).astype(BF16)
    before = _dot(strict, onehot.astype(BF16)) + carry_ref[0:1, 0:ne]
    lane_o = lax.broadcasted_iota(I32, (tm, LANES), 1)
    ti = jnp.zeros((tm, LANES), I32)
    tw = jnp.zeros((tm, LANES), F32)
    for kk in range(TOP_K):
        rank = jnp.sum(jnp.where(lane == idxs[kk], before, 0.0), axis=-1, keepdims=True).astype(I32)
        ti = jnp.where(lane_o == kk, idxs[kk].astype(I32), ti)
        ti = jnp.where(lane_o == TOP_K + kk, rank, ti)
        tw = jnp.where(lane_o == kk, exps[kk] / denom, tw)
    ti_ref[...] = ti
    tw_ref[...] = tw
    total = carry_ref[0:1, 0:ne] + jnp.sum(onehot, axis=0, keepdims=True)
    carry_ref[0:1, 0:ne] = total
    cnt_ref[...] = jnp.zeros(cnt_ref.shape, F32)
    cnt_ref[0:1, 0:ne] = total


def route(logits, tm=512):
    s, ne = logits.shape
    return pl.pallas_call(
        _route_body,
        grid=(s // tm,),
        in_specs=[pl.BlockSpec((tm, ne), lambda i: (i, 0))],
        out_specs=[
            pl.BlockSpec((tm, LANES), lambda i: (i, 0)),
            pl.BlockSpec((tm, LANES), lambda i: (i, 0)),
            pl.BlockSpec((8, LANES), lambda i: (0, 0)),
        ],
        out_shape=[
            jax.ShapeDtypeStruct((s, LANES), I32),
            jax.ShapeDtypeStruct((s, LANES), F32),
            jax.ShapeDtypeStruct((8, LANES), F32),
        ],
        scratch_shapes=[pltpu.VMEM((8, LANES), F32)],
        compiler_params=_params(("arbitrary",)),
        name="route",
    )(logits)


def _dest_body(ti_ref, ps_ref, d_ref):
    ti = ti_ref[...].astype(F32)
    tm = ti.shape[0]
    lane = lax.broadcasted_iota(I32, (tm, LANES), 1)
    lane_f = lane.astype(F32)
    ps = ps_ref[0:1, :].astype(F32)
    out = jnp.zeros((tm, LANES), F32)
    for kk in range(TOP_K):
        e = jnp.sum(jnp.where(lane == kk, ti, 0.0), axis=-1, keepdims=True)
        rank = jnp.sum(jnp.where(lane == TOP_K + kk, ti, 0.0), axis=-1, keepdims=True)
        start = jnp.sum(jnp.where(lane_f == e, ps, 0.0), axis=-1, keepdims=True)
        out = jnp.where(lane == kk, start + rank, out)
    d_ref[...] = out.astype(I32)


def route_dest(ti, pad_start, tm=512):
    s = ti.shape[0]
    ps = jnp.zeros((8, LANES), I32).at[0, :N_EXPERTS].set(pad_start)
    return pl.pallas_call(
        _dest_body,
        grid=(s // tm,),
        in_specs=[pl.BlockSpec((tm, LANES), lambda i: (i, 0)), pl.BlockSpec((8, LANES), lambda i: (0, 0))],
        out_specs=pl.BlockSpec((tm, LANES), lambda i: (i, 0)),
        out_shape=jax.ShapeDtypeStruct((s, LANES), I32),
        compiler_params=_params(("parallel",)),
        name="route_dest",
    )(ti, ps)


def _scatter_body(pend_ref, padded_ref, dest_ref, h_ref, xr_ref, zero_ref, sem):
    tm = h_ref.shape[0]

    def zero_copy(e):
        return pltpu.make_async_copy(zero_ref, xr_ref.at[pl.ds(pend_ref[e] - MOE_SUB, MOE_SUB)], sem)

    @pl.when(pl.program_id(0) == 0)
    def _():
        zero_ref[...] = jnp.zeros(zero_ref.shape, U32)

        def start(e, carry):
            @pl.when(padded_ref[e] > 0)
            def _():
                zero_copy(e).start()
            return carry

        def wait(e, carry):
            @pl.when(padded_ref[e] > 0)
            def _():
                zero_copy(e).wait()
            return carry

        lax.fori_loop(0, N_EXPERTS, start, 0)
        lax.fori_loop(0, N_EXPERTS, wait, 0)

        def slack_copy(b):
            return pltpu.make_async_copy(zero_ref, xr_ref.at[pl.ds(b * MOE_SUB, MOE_SUB)], sem)

        def slack_start(b, carry):
            slack_copy(b).start()
            return carry

        def slack_wait(b, carry):
            slack_copy(b).wait()
            return carry

        first_slack = pend_ref[N_EXPERTS - 1] // MOE_SUB
        lax.fori_loop(first_slack, xr_ref.shape[0] // MOE_SUB, slack_start, 0)
        lax.fori_loop(first_slack, xr_ref.shape[0] // MOE_SUB, slack_wait, 0)

    def row_copy(r, kk):
        d = dest_ref[r * TOP_K + kk]
        return pltpu.make_async_copy(h_ref.at[r], xr_ref.at[d], sem)

    def start(r, carry):
        for kk in range(TOP_K):
            row_copy(r, kk).start(priority=kk % 2)
        return carry

    lax.fori_loop(0, tm, start, 0)
    all_rows = xr_ref.at[pl.ds(0, tm * TOP_K)]
    pltpu.make_async_copy(all_rows, all_rows, sem).wait()


def moe_scatter(h2p, dest, pad_end, padded, n_rows, tm=256):
    s, nt, _ = h2p.shape
    grid_spec = pltpu.PrefetchScalarGridSpec(
        num_scalar_prefetch=2,
        grid=(s // tm,),
        in_specs=[
            pl.BlockSpec((tm * TOP_K,), lambda i, *_: (i,), memory_space=pltpu.SMEM),
            pl.BlockSpec((tm, nt, LANES), lambda i, *_: (i, 0, 0)),
        ],
        out_specs=pl.BlockSpec(memory_space=pl.ANY),
        scratch_shapes=[pltpu.VMEM((MOE_SUB, nt, LANES), U32), pltpu.SemaphoreType.DMA(())],
    )
    return pl.pallas_call(
        _scatter_body,
        grid_spec=grid_spec,
        out_shape=jax.ShapeDtypeStruct((n_rows, nt, LANES), U32),
        compiler_params=_params(("arbitrary",)),
        name="moe_scatter",
    )(pad_end, padded, dest, h2p)


def _moe_body(ie_ref, ir_ref, inb_ref, ni_ref, xr_ref, wg_ref, wu_ref, wd_ref, bg_ref, bu_ref, bd_ref, y_ref,
              stage_ref, xb_ref, acc_ref, wgb_ref, wub_ref, wdb_ref, sem_in, sem_out, *, n_chunks):
    ystage_ref = stage_ref
    w = pl.program_id(0)
    j = pl.program_id(1)
    sb = MOE_SUB
    d = acc_ref.shape[1]
    half = d // 2
    nx = half // LANES
    ny = half // LANES

    @pl.when(w < ni_ref[0])
    def _():
        nb = inb_ref[w]
        r0 = ir_ref[w]

        def in_copy(i, slot):
            src = xr_ref.at[pl.ds(pl.multiple_of((r0 + i * sb) * nx, sb * nx), sb * nx), :]
            return pltpu.make_async_copy(src, stage_ref.at[slot], sem_in.at[slot])

        @pl.when(j == 0)
        def _():
            @pl.when(w == 0)
            def _():
                ystage_ref[0] = jnp.zeros(ystage_ref.shape[1:], U32)

                def slack_copy(b):
                    dst = y_ref.at[pl.ds(pl.multiple_of(b * (sb * ny), sb * ny), sb * ny), :]
                    return pltpu.make_async_copy(ystage_ref.at[0], dst, sem_out.at[0])

                def slack_start(b, carry):
                    slack_copy(b).start()
                    return carry

                def slack_wait(b, carry):
                    slack_copy(b).wait()
                    return carry

                n_blocks = y_ref.shape[0] // (sb * ny)
                lax.fori_loop(ni_ref[1], n_blocks, slack_start, 0)
                lax.fori_loop(ni_ref[1], n_blocks, slack_wait, 0)

            in_copy(0, 0).start(priority=1)

            def load(i, carry):
                slot = i % 2

                @pl.when(i + 1 < nb)
                def _():
                    in_copy(i + 1, 1 - slot).start(priority=1)

                in_copy(i, slot).wait()
                rows = pl.ds(pl.multiple_of(i * sb, sb), sb)
                for c in range(nx):
                    lo, hi = _unpack_bf16_pair(stage_ref[slot, pl.ds(c, sb, stride=nx), :])
                    xb_ref[rows, c * LANES:(c + 1) * LANES] = lo.astype(BF16)
                    xb_ref[rows, half + c * LANES:half + (c + 1) * LANES] = hi.astype(BF16)
                return carry

            lax.fori_loop(0, nb, load, 0)

        def cast_weights():
            wgb_ref[...] = wg_ref[0].astype(BF16)
            wub_ref[...] = wu_ref[0].astype(BF16)
            wdb_ref[...] = wd_ref[0].astype(BF16)

        def block(first, n_sub, i, carry):
            start = i * sb if isinstance(i, int) else pl.multiple_of(i * sb, sb)
            rows = pl.ds(start, n_sub * sb)
            xs = xb_ref[rows, :]
            gate = jnp.minimum(_dot(xs, wgb_ref[...]) + bg_ref[0], SWIGLU_LIMIT)
            up = jnp.clip(_dot(xs, wub_ref[...]) + bu_ref[0], -SWIGLU_LIMIT, SWIGLU_LIMIT)
            act = (up + 1.0) * (gate * _sigmoid(SWIGLU_ALPHA * gate))
            contrib = _dot(act.astype(BF16), wdb_ref[...])
            if first:
                acc_ref[rows, :] = contrib
            else:
                acc_ref[rows, :] += contrib
            return carry

        def all_blocks(first):
            def pair(i2, carry):
                return block(first, 2, 2 * i2, carry)

            @pl.when(nb >= 2)
            def _():
                cast_weights()
                block(first, 2, 0, 0)
                lax.fori_loop(1, nb // 2, pair, 0)

                @pl.when(nb % 2 == 1)
                def _():
                    block(first, 1, nb - 1, 0)

            @pl.when(nb < 2)
            def _():
                cast_weights()
                block(first, 1, 0, 0)

        @pl.when(j == 0)
        def _():
            all_blocks(True)

        @pl.when(j > 0)
        def _():
            all_blocks(False)

        @pl.when(j == n_chunks - 1)
        def _():
            def out_copy(i, slot):
                dst = y_ref.at[pl.ds(pl.multiple_of((r0 + i * sb) * ny, sb * ny), sb * ny), :]
                return pltpu.make_async_copy(ystage_ref.at[slot], dst, sem_out.at[slot])

            def store(i, carry):
                slot = i % 2

                @pl.when(i >= 2)
                def _():
                    out_copy(i - 2, slot).wait()

                rows = pl.ds(pl.multiple_of(i * sb, sb), sb)
                for c in range(ny):
                    lo = slice(c * LANES, (c + 1) * LANES)
                    hi = slice(half + c * LANES, half + (c + 1) * LANES)
                    ystage_ref[slot, pl.ds(c, sb, stride=ny), :] = _pack_bf16_pair(
                        acc_ref[rows, lo] + bd_ref[0, :, lo], acc_ref[rows, hi] + bd_ref[0, :, hi])
                out_copy(i, slot).start(priority=1)
                return carry

            lax.fori_loop(0, nb, store, 0)

            @pl.when(nb >= 2)
            def _():
                out_copy(nb - 2, nb % 2).wait()

            out_copy(nb - 1, (nb - 1) % 2).wait()


def moe_ffn(x_rows, w_gate_up, b_gate_up, w_down, b_down, item_e, item_row, item_nb, n_items, max_items):
    ne, d, two_de = w_gate_up.shape
    nx = d // 2 // LANES
    ny = d // 2 // LANES
    n_rows = x_rows.shape[0] // nx
    de = two_de // 2
    tn = MOE_TN
    n_chunks = de // tn
    last = n_chunks - 1

    def chunk(w, j, ni):
        return jnp.where(w < ni[0], j, last)

    grid_spec = pltpu.PrefetchScalarGridSpec(
        num_scalar_prefetch=4,
        grid=(max_items, n_chunks),
        in_specs=[
            pl.BlockSpec(memory_space=pl.ANY),
            pl.BlockSpec((1, d, tn), lambda w, j, ie, ir, inb, ni: (ie[w], 0, chunk(w, j, ni))),
            pl.BlockSpec((1, d, tn), lambda w, j, ie, ir, inb, ni: (ie[w], 0, n_chunks + chunk(w, j, ni))),
            pl.BlockSpec((1, tn, d), lambda w, j, ie, ir, inb, ni: (ie[w], chunk(w, j, ni), 0)),
            pl.BlockSpec((1, 1, tn), lambda w, j, ie, ir, inb, ni: (ie[w], 0, chunk(w, j, ni))),
            pl.BlockSpec((1, 1, tn), lambda w, j, ie, ir, inb, ni: (ie[w], 0, n_chunks + chunk(w, j, ni))),
            pl.BlockSpec((1, 1, d), lambda w, j, ie, ir, inb, ni: (ie[w], 0, 0)),
        ],
        out_specs=pl.BlockSpec(memory_space=pl.ANY),
        scratch_shapes=[
            pltpu.VMEM((2, MOE_SUB * nx, LANES), U32),
            pltpu.VMEM((MOE_TM, d), BF16),
            pltpu.VMEM((MOE_TM, d), F32),
            pltpu.VMEM((d, tn), BF16),
            pltpu.VMEM((d, tn), BF16),
            pltpu.VMEM((tn, d), BF16),
            pltpu.SemaphoreType.DMA((2,)),
            pltpu.SemaphoreType.DMA((2,)),
        ],
    )
    return pl.pallas_call(
        functools.partial(_moe_body, n_chunks=n_chunks),
        grid_spec=grid_spec,
        out_shape=jax.ShapeDtypeStruct((n_rows * ny, LANES), U32),
        compiler_params=_params(("arbitrary", "arbitrary"), MOE_VMEM_LIMIT),
        name="moe_ffn",
    )(item_e, item_row, item_nb, n_items, x_rows, w_gate_up, w_gate_up, w_down,
      b_gate_up.reshape(ne, 1, two_de), b_gate_up.reshape(ne, 1, two_de), b_down.reshape(ne, 1, d))


def _final_body(dest_ref, dest_next_ref, x1_ref, tw_ref, p_ref, gple_ref, wg_ref, wp_ref, gfin_ref, y_ref, o_ref,
                ybuf_ref, sem, *, last_layer):
    i = pl.program_id(0)
    tm = x1_ref.shape[0]
    half = x1_ref.shape[1] // 2
    ny = half // LANES
    slot = i % 2

    def row_copy(d_ref, to_slot, r, kk):
        dst = ybuf_ref.at[to_slot, kk, pl.ds(pl.multiple_of(r * ny, ny), ny), :]
        return pltpu.make_async_copy(y_ref.at[d_ref[r * TOP_K + kk]], dst, sem.at[to_slot])

    def issue(d_ref, to_slot):
        def body(r, carry):
            for kk in range(TOP_K):
                row_copy(d_ref, to_slot, r, kk).start(priority=kk % 2)
            return carry

        lax.fori_loop(0, tm, body, 0)

    def drain(to_slot):
        pltpu.make_async_copy(ybuf_ref.at[to_slot], ybuf_ref.at[to_slot], sem.at[to_slot]).wait()

    @pl.when(i == 0)
    def _():
        issue(dest_ref, 0)

    for r in range(tm):
        for kk in range(TOP_K):
            row_copy(dest_next_ref, 1 - slot, r, kk).start(priority=kk % 2)

    drain(slot)

    tw = tw_ref[...]
    los, his = [], []
    for c in range(ny):
        acc_lo = x1_ref[:, c * LANES:(c + 1) * LANES]
        acc_hi = x1_ref[:, half + c * LANES:half + (c + 1) * LANES]
        for kk in range(TOP_K):
            lo, hi = _unpack_bf16_pair(ybuf_ref[slot, kk, pl.ds(c, tm, stride=ny), :])
            acc_lo = acc_lo + tw[:, kk:kk + 1] * lo
            acc_hi = acc_hi + tw[:, kk:kk + 1] * hi
        los.append(acc_lo)
        his.append(acc_hi)
    x2 = jnp.concatenate(los + his, axis=1)
    r = lax.rsqrt(jnp.mean(x2 * x2, axis=-1, keepdims=True) + RMS_EPS)
    n = (x2 * r * gple_ref[...]).astype(BF16)
    gate = _sigmoid(_dot(n, wg_ref[...]))
    x3 = x2 + gate * _dot(p_ref[...].astype(BF16), wp_ref[...])
    if last_layer:
        r = lax.rsqrt(jnp.mean(x3 * x3, axis=-1, keepdims=True) + RMS_EPS)
        x3 = x3 * r * gfin_ref[...]
    o_ref[...] = x3

    @pl.when(i + 1 == pl.num_programs(0))
    def _():
        drain(1 - slot)


def final(dest, x1, tw, p, norm_ple_g, w_gate, w_proj, norm_final_g, y_rows, last_layer, tm=256):
    s, d = x1.shape
    pd = p.shape[1]
    ny = d // 2 // LANES
    n_steps = s // tm
    const = lambda i: (0, 0)
    single = pl.Buffered(1)
    return pl.pallas_call(
        functools.partial(_final_body, last_layer=last_layer),
        grid=(n_steps,),
        in_specs=[
            pl.BlockSpec((tm * TOP_K,), lambda i: (i,), memory_space=pltpu.SMEM),
            pl.BlockSpec((tm * TOP_K,), lambda i: (jnp.minimum(i + 1, n_steps - 1),), memory_space=pltpu.SMEM),
            pl.BlockSpec((tm, d), lambda i: (i, 0)),
            pl.BlockSpec((tm, LANES), lambda i: (i, 0)),
            pl.BlockSpec((tm, pd), lambda i: (i, 0)),
            pl.BlockSpec((1, d), const),
            pl.BlockSpec((d, d), const, pipeline_mode=single),
            pl.BlockSpec((pd, d), const, pipeline_mode=single),
            pl.BlockSpec((1, d), const),
            pl.BlockSpec(memory_space=pl.ANY),
        ],
        out_specs=pl.BlockSpec((tm, d), lambda i: (i, 0)),
        out_shape=jax.ShapeDtypeStruct((s, d), F32),
        scratch_shapes=[pltpu.VMEM((2, TOP_K, tm * ny, LANES), U32), pltpu.SemaphoreType.DMA((2,))],
        compiler_params=_params(("arbitrary",)),
        name="final",
    )(dest, dest, x1, tw, p, norm_ple_g.reshape(1, d).astype(F32), w_gate, w_proj,
      norm_final_g.reshape(1, d).astype(F32), y_rows.reshape(-1, ny, LANES))


def _moe_tables(counts, n_rows):
    sub, tm = MOE_SUB, MOE_TM
    max_items = N_EXPERTS + n_rows // tm
    padded = (counts + sub - 1) // sub * sub
    pad_end = jnp.cumsum(padded)
    pad_start = pad_end - padded
    n_it = (padded + tm - 1) // tm
    it_end = jnp.cumsum(n_it)
    it_start = it_end - n_it
    n_items = it_end[-1]
    w = jnp.arange(max_items, dtype=I32)
    live = w < n_items
    w_eff = jnp.minimum(w, n_items - 1)
    e_w = jnp.minimum(jnp.sum((it_end[None, :] <= w_eff[:, None]).astype(I32), axis=1), N_EXPERTS - 1)
    m_w = w_eff - it_start[e_w]
    row_w = pad_start[e_w] + m_w * tm
    nb_w = jnp.clip((padded[e_w] - m_w * tm) // sub, 0, tm // sub)
    nb_w = jnp.where(live, nb_w, 0)
    as_i32 = lambda a: a.astype(I32)
    counts_w = jnp.stack([n_items, pad_end[-1] // sub])
    return (as_i32(pad_start), as_i32(pad_end), as_i32(padded), e_w, as_i32(row_w), as_i32(nb_w),
            as_i32(counts_w), max_items)


def _layer(x, p, norm_mix_g, w_in, dn_conv_w, dn_a_log, dn_dt_bias, dn_norm_g, na_rpb, w_branch_a, w_branch_b,
           w_out, norm_ffn_g, w_router, b_router, w_gate_up, b_gate_up, w_down, b_down, norm_ple_g,
           w_ple_gate, w_ple_proj, norm_final_g, last_layer):
    s, d = x.shape
    c_qkv, c_z = 3 * DN_WIDTH, DN_WIDTH
    w_t = jnp.swapaxes(w_in, 0, 1)
    w_main = w_prep(w_t)
    z_blk = c_qkv // DN_WIDTH
    gate_blk = (c_qkv + c_z) // d
    na_col0 = c_qkv + c_z + 2 * d

    proj, small = in_projection(x, norm_mix_g.astype(F32), w_main, w_t)

    qkv = dn_prep(proj, dn_conv_w.astype(F32))
    col, row = dn_gates(small, dn_a_log, dn_dt_bias)
    u, w, qd, kt, qk = dn_chunk(qkv, col, row)
    o_f, o_b = dn_scan(u, w, qd, kt, qk, col)

    na_out = na_attention(proj, na_col0, na_rpb)

    x1, h2p, logits = merge(o_f, o_b, proj, z_blk, gate_blk, na_out, x, dn_norm_g, w_branch_a.astype(BF16),
                            w_branch_b.astype(BF16), w_out.astype(BF16), norm_ffn_g, w_router, b_router)

    ti, tw, cnt = route(logits)
    counts = cnt[0, :N_EXPERTS].astype(I32)
    n_rows = (s * TOP_K + N_EXPERTS * (MOE_SUB - 1) + MOE_SUB - 1) // MOE_SUB * MOE_SUB
    pad_start, pad_end, padded, item_e, item_row, item_nb, n_items, max_items = _moe_tables(counts, n_rows)
    dest = route_dest(ti, pad_start)[:, :TOP_K].reshape(-1)

    nx = d // 2 // LANES
    x_rows = moe_scatter(h2p.reshape(s, nx, LANES), dest, pad_end, padded, n_rows).reshape(n_rows * nx, LANES)
    y_rows = moe_ffn(x_rows, w_gate_up, b_gate_up, w_down, b_down, item_e, item_row, item_nb, n_items, max_items)

    return final(dest, x1, tw, p, norm_ple_g, w_ple_gate.astype(BF16), w_ple_proj.astype(BF16), norm_final_g,
                 y_rows, last_layer)


def kernel(x, p, norm_mix_g, w_in, dn_conv_w, dn_a_log, dn_dt_bias, dn_norm_g, na_rpb, w_branch_a, w_branch_b, w_out, norm_ffn_g, w_router, b_router, w_gate_up, b_gate_up, w_down, b_down, norm_ple_g, w_ple_gate, w_ple_proj, norm_final_g):
    bsz, s, d = x.shape
    depth = w_in.shape[0]
    outs = []
    for b in range(bsz):
        xb = x[b]
        for i in range(depth):
            xb = _layer(xb, p[i, b], norm_mix_g[i], w_in[i], dn_conv_w[i], dn_a_log[i], dn_dt_bias[i], dn_norm_g[i],
                        na_rpb[i], w_branch_a[i], w_branch_b[i], w_out[i], norm_ffn_g[i], w_router[i], b_router[i],
                        w_gate_up[i], b_gate_up[i], w_down[i], b_down[i], norm_ple_g[i], w_ple_gate[i],
                        w_ple_proj[i], norm_final_g, i == depth - 1)
        outs.append(xb)
    return jnp.stack(outs, axis=0)
```

```python
import functools

import jax
import jax.numpy as jnp
import numpy as np
from jax import lax
from jax.experimental import pallas as pl
from jax.experimental.pallas import tpu as pltpu

F32 = jnp.float32
BF16 = jnp.bfloat16
I32 = jnp.int32
U32 = jnp.uint32

GRID_W = 64
DN_HEADS = 8
DN_HEAD_DIM = 128
DN_WIDTH = DN_HEADS * DN_HEAD_DIM
DN_CONV = 5
NA_HEADS = 16
NA_HEAD_DIM = 64
NA_WIDTH = NA_HEADS * NA_HEAD_DIM
NA_ROWS = 8
NA_COLS = 16
N_EXPERTS = 32
TOP_K = 4
SWIGLU_LIMIT = 7.0
SWIGLU_ALPHA = 1.702
RMS_EPS = 1e-6

LANES = 128
VMEM_LIMIT = 56 * 1024 * 1024
MOE_VMEM_LIMIT = 60000 * 1024

DN_TILE = 256
DN_BLOCK = 16
DN_PAD = 16
DN_G_LANE = 2 * DN_HEADS
MOE_SUB = 256
MOE_TM = 1536
MOE_TN = 512
NEG_BIG = -1e30


def _sigmoid(x):
    return 1.0 / (1.0 + jnp.exp(-x))


def _dot(a, b):
    return jnp.dot(a, b, preferred_element_type=F32)


def _dot_nt(a, b):
    return lax.dot_general(a, b, (((1,), (1,)), ((), ())), preferred_element_type=F32)


def _params(sem, limit=VMEM_LIMIT):
    return pltpu.CompilerParams(dimension_semantics=sem, vmem_limit_bytes=limit)


def _wprep_body(src_ref, o_ref, buf_ref, sem, *, starts, tn):
    t = pl.program_id(0)
    slot = t % 2

    def tile_copy(tt, to_slot):
        start = jnp.int32(starts[0])
        for k in range(1, len(starts)):
            start = jnp.where(tt >= k, starts[k], start)
        src = src_ref.at[pl.ds(pl.multiple_of(start, 8), tn), :]
        return pltpu.make_async_copy(src, buf_ref.at[to_slot], sem.at[to_slot])

    @pl.when(t == 0)
    def _():
        tile_copy(0, 0).start()

    @pl.when(t + 1 < pl.num_programs(0))
    def _():
        tile_copy(t + 1, 1 - slot).start()

    tile_copy(t, slot).wait()
    o_ref[...] = buf_ref[slot].T.astype(o_ref.dtype)


def w_prep(w_t, tn=1024):
    n_in, d = w_t.shape
    c_small = 4 * DN_HEADS
    o_small = 4 * DN_WIDTH
    o_na = o_small + c_small
    o_gates = o_na + 3 * NA_WIDTH
    assert o_small % tn == 0 and (3 * NA_WIDTH) % tn == 0 and (n_in - o_gates) % tn == 0
    assert o_na % 8 == 0 and o_gates % 8 == 0
    starts = list(range(0, o_small, tn)) + list(range(o_gates, n_in, tn)) + list(range(o_na, o_gates, tn))
    return pl.pallas_call(
        functools.partial(_wprep_body, starts=tuple(starts), tn=tn),
        grid=(len(starts),),
        in_specs=[pl.BlockSpec(memory_space=pl.ANY)],
        out_specs=pl.BlockSpec((d, tn), lambda t: (0, t)),
        out_shape=jax.ShapeDtypeStruct((d, len(starts) * tn), BF16),
        scratch_shapes=[pltpu.VMEM((2, tn, d), F32), pltpu.SemaphoreType.DMA((2,))],
        compiler_params=_params(("arbitrary",)),
        name="w_prep",
    )(w_t)


def _inproj_body(x_ref, g_ref, w_ref, ws_ref, o_ref, os_ref, h_ref):
    @pl.when(pl.program_id(1) == 0)
    def _():
        x = x_ref[...]
        r = lax.rsqrt(jnp.mean(x * x, axis=-1, keepdims=True) + RMS_EPS)
        h = (x * r * g_ref[...]).astype(BF16)
        h_ref[...] = h
        os_ref[...] = _dot_nt(h, ws_ref[...].astype(BF16))

    o_ref[...] = _dot(h_ref[...], w_ref[...]).astype(o_ref.dtype)


def in_projection(x, g, w_main, w_t, tm=1024, tn=1024):
    s, d = x.shape
    n = w_main.shape[1]
    small_blk = 4 * DN_WIDTH // LANES
    return pl.pallas_call(
        _inproj_body,
        grid=(s // tm, n // tn),
        in_specs=[
            pl.BlockSpec((tm, d), lambda i, j: (i, 0)),
            pl.BlockSpec((1, d), lambda i, j: (0, 0)),
            pl.BlockSpec((d, tn), lambda i, j: (0, j)),
            pl.BlockSpec((LANES, d), lambda i, j: (small_blk, 0)),
        ],
        out_specs=[
            pl.BlockSpec((tm, tn), lambda i, j: (i, j)),
            pl.BlockSpec((tm, LANES), lambda i, j: (i, 0)),
        ],
        out_shape=[jax.ShapeDtypeStruct((s, n), BF16), jax.ShapeDtypeStruct((s, LANES), F32)],
        scratch_shapes=[pltpu.VMEM((tm, d), BF16)],
        compiler_params=_params(("parallel", "arbitrary")),
        name="in_proj",
    )(x, g.reshape(1, d), w_main, w_t)


def _dnprep_body(x_ref, w_ref, o_ref, pad_ref, *, seq, chunk):
    cb = pl.program_id(0)
    n_chunks = seq // chunk
    zeros = jnp.zeros((DN_PAD, LANES), F32)
    pad_ref[0:DN_PAD, :] = zeros
    pad_ref[seq + DN_PAD:seq + 2 * DN_PAD, :] = zeros

    def fill(c, carry):
        r0 = pl.multiple_of(c * chunk, chunk)
        pad_ref[pl.ds(r0 + DN_PAD, chunk), :] = x_ref[pl.ds(r0, chunk), :].astype(F32)
        return carry

    lax.fori_loop(0, n_chunks, fill, 0)

    w = w_ref[...]
    is_v = cb >= 2 * DN_HEADS
    scale = jnp.where(cb < DN_HEADS, DN_HEAD_DIM ** -0.5, 1.0).astype(F32)

    def body(c, carry):
        r0 = pl.multiple_of(c * chunk, chunk)
        first = DN_PAD - (DN_CONV - 1) // 2
        y = pad_ref[pl.ds(r0 + first, chunk), :] * w[0:1]
        for j in range(1, DN_CONV):
            y = y + pad_ref[pl.ds(r0 + first + j, chunk), :] * w[j:j + 1]
        y = y * _sigmoid(y)
        ss = jnp.sum(y * y, axis=-1, keepdims=True)
        yn = y * (lax.rsqrt(ss + 1e-6) * scale)
        o_ref[pl.ds(r0, chunk), :] = jnp.where(is_v, y, yn).astype(o_ref.dtype)
        return carry

    lax.fori_loop(0, n_chunks, body, 0)


def dn_prep(proj, conv_w, chunk=512):
    s = proj.shape[0]
    nb = 3 * DN_WIDTH // LANES
    w = jnp.zeros((8, 3 * DN_WIDTH), F32).at[:DN_CONV].set(conv_w)
    return pl.pallas_call(
        functools.partial(_dnprep_body, seq=s, chunk=chunk),
        grid=(nb,),
        in_specs=[
            pl.BlockSpec((s, LANES), lambda c: (0, c)),
            pl.BlockSpec((8, LANES), lambda c: (0, c)),
        ],
        out_specs=pl.BlockSpec((s, LANES), lambda c: (0, c)),
        out_shape=jax.ShapeDtypeStruct((s, 3 * DN_WIDTH), BF16),
        scratch_shapes=[pltpu.VMEM((s + 2 * DN_PAD, LANES), F32)],
        compiler_params=_params(("parallel",)),
        name="dn_prep",
    )(proj, w)


def _gates_body(s_ref, par_ref, col_ref, row_ref):
    x = s_ref[...]
    t = x.shape[0]
    lane = lax.broadcasted_iota(I32, x.shape, 1)
    beta = _sigmoid(x)
    z = x + par_ref[1:2, :]
    softplus = jnp.maximum(z, 0.0) + jnp.log(1.0 + jnp.exp(-jnp.abs(z)))
    g = par_ref[0:1, :] * softplus
    ri = lax.broadcasted_iota(I32, (t, t), 0)
    ci = lax.broadcasted_iota(I32, (t, t), 1)
    lower = jnp.where(ci <= ri, 1.0, 0.0).astype(BF16)
    upper = jnp.where(ci >= ri, 1.0, 0.0).astype(BF16)
    g1 = g.astype(BF16)
    g2 = (g - g1.astype(F32)).astype(BF16)
    g3 = (g - g1.astype(F32) - g2.astype(F32)).astype(BF16)
    g_prefix = _dot(lower, g1) + _dot(lower, g2) + _dot(lower, g3)
    g_suffix = _dot(upper, g1) + _dot(upper, g2) + _dot(upper, g3)
    cum = jnp.where(lane < DN_G_LANE + DN_HEADS, g_prefix, g_suffix)
    out = jnp.where(lane < DN_G_LANE, beta, jnp.where(lane < 2 * DN_G_LANE, cum, 0.0))
    col_ref[...] = out
    row_ref[...] = out.T


def dn_gates(small, a_log, dt_bias):
    s = small.shape[0]
    par = jnp.zeros((8, LANES), F32)
    par = par.at[0, DN_G_LANE:2 * DN_G_LANE].set(-jnp.exp(a_log.reshape(-1).astype(F32)))
    par = par.at[1, DN_G_LANE:2 * DN_G_LANE].set(dt_bias.reshape(-1).astype(F32))
    t = DN_TILE
    return pl.pallas_call(
        _gates_body,
        grid=(s // t,),
        in_specs=[
            pl.BlockSpec((t, LANES), lambda i: (i, 0)),
            pl.BlockSpec((8, LANES), lambda i: (0, 0)),
        ],
        out_specs=[
            pl.BlockSpec((t, LANES), lambda i: (i, 0)),
            pl.BlockSpec((LANES, t), lambda i: (0, i)),
        ],
        out_shape=[jax.ShapeDtypeStruct((s, LANES), F32), jax.ShapeDtypeStruct((LANES, s), F32)],
        compiler_params=_params(("parallel",)),
        name="dn_gates",
    )(small, par)


def _dnchunk_body(q_ref, k_ref, v_ref, col_ref, row_ref, u_ref, w_ref, qd_ref, kt_ref, qk_ref, *, heads_per_step):
    c = DN_TILE
    hd = DN_HEAD_DIM
    head0 = pl.program_id(1) * heads_per_step
    col = col_ref[...]
    row = row_ref[...]
    lane = lax.broadcasted_iota(I32, col.shape, 1)
    sub = lax.broadcasted_iota(I32, row.shape, 0)

    def col_pick(idx):
        return jnp.sum(jnp.where(lane == idx, col, 0.0), axis=1, keepdims=True)

    def row_pick(idx):
        return jnp.sum(jnp.where(sub == idx, row, 0.0), axis=0, keepdims=True)

    ri = lax.broadcasted_iota(I32, (c, c), 0)
    ci = lax.broadcasted_iota(I32, (c, c), 1)
    same_block = (ri // DN_BLOCK) == (ci // DN_BLOCK)
    incl = (ri >= ci, ri <= ci)
    strict = (ri > ci, ri < ci)
    heads = range(heads_per_step)
    chains = [(hh, d) for hh in heads for d in range(2)]
    sls = [slice(hh * hd, (hh + 1) * hd) for hh in heads]
    qs = [q_ref[:, sl] for sl in sls]
    ks = [k_ref[:, sl] for sl in sls]
    vs = [v_ref[:, sl] for sl in sls]
    grams = [_dot_nt(k, k) for k in ks]
    qks = [_dot_nt(q, k) for q, k in zip(qs, ks)]
    qfs = [q.astype(F32) for q in qs]
    kfs = [k.astype(F32) for k in ks]
    vfs = [v.astype(F32) for v in vs]

    betas = [col_pick(d * DN_HEADS + head0 + hh) for hh, d in chains]
    g_cols = [col_pick(DN_G_LANE + d * DN_HEADS + head0 + hh) for hh, d in chains]
    g_rows = [row_pick(DN_G_LANE + d * DN_HEADS + head0 + hh) for hh, d in chains]
    totals = [gr[:, c - 1:c] if d == 0 else gr[:, 0:1] for (hh, d), gr in zip(chains, g_rows)]
    decays = [jnp.where(incl[d], jnp.exp(jnp.minimum(gc - gr, 0.0)), 0.0)
              for (hh, d), gc, gr in zip(chains, g_cols, g_rows)]
    lows = [jnp.where(strict[d], b * grams[hh] * dec, 0.0) for (hh, d), b, dec in zip(chains, betas, decays)]
    l_diags = [jnp.where(same_block, low, 0.0) for low in lows]
    l_offs = [(low - ld).astype(BF16) for low, ld in zip(lows, l_diags)]

    def neumann(accs, x_bs, n_steps):
        for _ in range(n_steps):
            x2s = [_dot(x, x) for x in x_bs]
            x_bs = [x2.astype(BF16) for x2 in x2s]
            accs = [a + x2 + _dot(a.astype(BF16), xb) for a, x2, xb in zip(accs, x2s, x_bs)]
        return accs

    d_ms = neumann([-ld for ld in l_diags], [ld.astype(BF16) for ld in l_diags], (DN_BLOCK - 1).bit_length() - 1)
    d_bs = [dm.astype(BF16) for dm in d_ms]
    ms = [lo.astype(F32) + _dot(db, lo) for db, lo in zip(d_bs, l_offs)]
    q_ms = neumann([-m for m in ms], [m.astype(BF16) for m in ms], (c // DN_BLOCK - 1).bit_length() - 1)
    e_cols = [jnp.exp(gc) for gc in g_cols]
    rhss = [jnp.concatenate([vfs[hh] * b, kfs[hh] * b * ec], axis=1) for (hh, d), b, ec in zip(chains, betas, e_cols)]
    r1s = [rhs + _dot(db, rhs.astype(BF16)) for rhs, db in zip(rhss, d_bs)]
    sols = [r1 + _dot(qm.astype(BF16), r1.astype(BF16)) for r1, qm in zip(r1s, q_ms)]
    for i, (hh, d) in enumerate(chains):
        sl = sls[hh]
        u_ref[d, :, sl] = sols[i][:, :hd].astype(BF16)
        w_ref[d, :, sl] = sols[i][:, hd:].astype(BF16)
        qd_ref[d, :, sl] = (qfs[hh] * e_cols[i]).astype(BF16)
        kt_ref[d, sl, :] = (kfs[hh] * jnp.exp(totals[i] - g_cols[i])).T.astype(BF16)
        qk_ref[d, hh] = (qks[hh] * decays[i]).astype(BF16)


def dn_chunk(qkv, col, row, heads_per_step=4):
    s = qkv.shape[0]
    c = DN_TILE
    nt = s // c
    hd = DN_HEAD_DIM * heads_per_step
    groups = DN_HEADS // heads_per_step
    return pl.pallas_call(
        functools.partial(_dnchunk_body, heads_per_step=heads_per_step),
        grid=(nt, groups),
        in_specs=[
            pl.BlockSpec((c, hd), lambda t, h: (t, h)),
            pl.BlockSpec((c, hd), lambda t, h: (t, groups + h)),
            pl.BlockSpec((c, hd), lambda t, h: (t, 2 * groups + h)),
            pl.BlockSpec((c, LANES), lambda t, h: (t, 0)),
            pl.BlockSpec((LANES, c), lambda t, h: (0, t)),
        ],
        out_specs=[
            pl.BlockSpec((2, c, hd), lambda t, h: (0, t, h)),
            pl.BlockSpec((2, c, hd), lambda t, h: (0, t, h)),
            pl.BlockSpec((2, c, hd), lambda t, h: (0, t, h)),
            pl.BlockSpec((2, hd, c), lambda t, h: (0, h, t)),
            pl.BlockSpec((2, heads_per_step, c, c), lambda t, h: (0, h, t, 0)),
        ],
        out_shape=[
            jax.ShapeDtypeStruct((2, s, DN_WIDTH), BF16),
            jax.ShapeDtypeStruct((2, s, DN_WIDTH), BF16),
            jax.ShapeDtypeStruct((2, s, DN_WIDTH), BF16),
            jax.ShapeDtypeStruct((2, DN_WIDTH, s), BF16),
            jax.ShapeDtypeStruct((2, DN_HEADS, s, c), BF16),
        ],
        compiler_params=_params(("parallel", "parallel")),
        name="dn_chunk",
    )(qkv, qkv, qkv, col, row)


def _dnscan_body(uf, wf, qdf, ktf, qkf, colf, ub, wb, qdb, ktb, qkb, colb, of_ref, ob_ref, st_ref):
    @pl.when(pl.program_id(0) == 0)
    def _():
        st_ref[...] = jnp.zeros(st_ref.shape, F32)

    c = DN_TILE
    hd = DN_HEAD_DIM
    dirs = ((uf, wf, qdf, ktf, qkf, colf, of_ref, c - 1), (ub, wb, qdb, ktb, qkb, colb, ob_ref, 0))
    chains = [(d, h) for d in range(2) for h in range(DN_HEADS)]
    sls = [slice(h * hd, (h + 1) * hd) for h in range(DN_HEADS)]
    e_tots = [jnp.exp(dirs[d][5][dirs[d][7]:dirs[d][7] + 1, :]) for d in range(2)]
    states = [st_ref[d * DN_HEADS + h] for d, h in chains]
    states_b = [st.astype(BF16) for st in states]
    v_news = [dirs[d][0][0, :, sls[h]] - _dot(dirs[d][1][0, :, sls[h]], sb) for (d, h), sb in zip(chains, states_b)]
    v_news_b = [vn.astype(BF16) for vn in v_news]
    outs = [_dot(dirs[d][2][0, :, sls[h]], sb) + _dot(dirs[d][4][0, h], vb)
            for (d, h), sb, vb in zip(chains, states_b, v_news_b)]
    news = [st * e_tots[d][:, DN_G_LANE + d * DN_HEADS + h:DN_G_LANE + d * DN_HEADS + h + 1]
            + _dot(dirs[d][3][0, sls[h], :], vb)
            for (d, h), st, vb in zip(chains, states, v_news_b)]
    for (d, h), out, new in zip(chains, outs, news):
        dirs[d][6][:, sls[h]] = out
        st_ref[d * DN_HEADS + h] = new


def dn_scan(u, w, qd, kt, qk, col):
    s = u.shape[1]
    c = DN_TILE
    nt = s // c
    wd = DN_WIDTH

    def specs(d):
        tile = (lambda t: t) if d == 0 else (lambda t: nt - 1 - t)
        return [
            pl.BlockSpec((1, c, wd), lambda t: (d, tile(t), 0)),
            pl.BlockSpec((1, c, wd), lambda t: (d, tile(t), 0)),
            pl.BlockSpec((1, c, wd), lambda t: (d, tile(t), 0)),
            pl.BlockSpec((1, wd, c), lambda t: (d, 0, tile(t))),
            pl.BlockSpec((1, DN_HEADS, c, c), lambda t: (d, 0, tile(t), 0)),
            pl.BlockSpec((c, LANES), lambda t: (tile(t), 0)),
        ]

    return pl.pallas_call(
        _dnscan_body,
        grid=(nt,),
        in_specs=specs(0) + specs(1),
        out_specs=[
            pl.BlockSpec((c, wd), lambda t: (t, 0)),
            pl.BlockSpec((c, wd), lambda t: (nt - 1 - t, 0)),
        ],
        out_shape=[jax.ShapeDtypeStruct((s, wd), F32), jax.ShapeDtypeStruct((s, wd), F32)],
        scratch_shapes=[pltpu.VMEM((2 * DN_HEADS, DN_HEAD_DIM, DN_HEAD_DIM), F32)],
        compiler_params=_params(("arbitrary",)),
        name="dn_scan",
    )(u, w, qd, kt, qk, col, u, w, qd, kt, qk, col)


def _na_bias_table(rpb):
    n_heads, n_off, n_dc = rpb.shape
    return pl.pallas_call(
        functools.partial(_nabias_body, n_off=n_off, n_dc=n_dc),
        grid=(n_heads,),
        in_specs=[pl.BlockSpec(memory_space=pltpu.SMEM)],
        out_specs=pl.BlockSpec((1, n_off - 1, GRID_W, 2 * GRID_W), lambda h: (h, 0, 0, 0)),
        out_shape=jax.ShapeDtypeStruct((n_heads, n_off - 1, GRID_W, 2 * GRID_W), F32),
        compiler_params=_params(("parallel",)),
        name="na_bias",
    )(rpb.astype(F32).reshape(-1))


def _nabias_body(rpb_ref, o_ref, *, n_off, n_dc):
    h = pl.program_id(0)
    c = lax.broadcasted_iota(I32, (GRID_W, 2 * GRID_W), 0)
    lane = lax.broadcasted_iota(I32, (GRID_W, 2 * GRID_W), 1)
    second = lane >= GRID_W
    kc = jnp.where(second, lane - GRID_W, lane)
    cs = jnp.clip(c - NA_COLS // 2, 0, GRID_W - NA_COLS)
    valid = (kc >= cs) & (kc < cs + NA_COLS)
    dc = jnp.where(valid, kc - c + (NA_COLS - 1), -1)
    for m in range(n_off - 1):
        tile = jnp.full((GRID_W, 2 * GRID_W), NEG_BIG, F32)
        for j in range(n_dc):
            first_val = rpb_ref[(h * n_off + m) * n_dc + j]
            second_val = rpb_ref[(h * n_off + m + 1) * n_dc + j]
            tile = jnp.where(dc == j, jnp.where(second, second_val, first_val), tile)
        o_ref[0, m] = tile


def _na_body(q_ref, k_ref, v_ref, bias_ref, o_ref, *, rows_per_step, n_rows):
    lane = lax.broadcasted_iota(I32, (1, LANES), 1)
    heads_per_block = LANES // NA_HEAD_DIM
    scale = NA_HEAD_DIM ** -0.5
    n_pairs = NA_WIDTH // LANES
    n_keys = NA_ROWS * GRID_W
    sls = [slice(pair * LANES, (pair + 1) * LANES) for pair in range(n_pairs)]
    owns = [(lane // NA_HEAD_DIM) == hh for hh in range(heads_per_block)]

    def window_start(row):
        return jnp.clip(row - NA_ROWS // 2, 0, n_rows - NA_ROWS)

    row0 = pl.program_id(0) * rows_per_step
    block_start = jnp.minimum(window_start(row0), n_rows - (NA_ROWS + rows_per_step - 1))
    starts = [window_start(row0 + a) for a in range(rows_per_step)]
    key_rows = [pl.ds(pl.multiple_of((starts[a] - block_start) * GRID_W, GRID_W), n_keys)
                for a in range(rows_per_step)]
    first_offsets = [starts[a] - (row0 + a) + (NA_ROWS - 1) for a in range(rows_per_step)]

    def bias(a, h):
        return jnp.concatenate([bias_ref[h, first_offsets[a] + 2 * m] for m in range(NA_ROWS // 2)], axis=1)

    chains = [(a, pair, hh) for a in range(rows_per_step) for pair in range(n_pairs) for hh in range(heads_per_block)]
    q2s = {(a, pair): q_ref[a * GRID_W:(a + 1) * GRID_W, sls[pair]] * scale
           for a in range(rows_per_step) for pair in range(n_pairs)}
    k2s = {(a, pair): k_ref[key_rows[a], sls[pair]] for a in range(rows_per_step) for pair in range(n_pairs)}
    scores = [_dot_nt(jnp.where(owns[hh], q2s[a, pair], jnp.zeros_like(q2s[a, pair])), k2s[a, pair])
              + bias(a, pair * heads_per_block + hh) for a, pair, hh in chains]
    maxes = [jnp.max(s, axis=-1, keepdims=True) for s in scores]
    probs = [jnp.exp(s - m) for s, m in zip(scores, maxes)]
    denoms = [jnp.sum(p, axis=-1, keepdims=True) for p in probs]
    v2s = {(a, pair): v_ref[key_rows[a], sls[pair]] for a in range(rows_per_step) for pair in range(n_pairs)}
    outs = {chain: _dot(p.astype(BF16), v2s[chain[0], chain[1]]) / den
            for chain, p, den in zip(chains, probs, denoms)}
    for a in range(rows_per_step):
        for pair in range(n_pairs):
            acc = outs[a, pair, 0]
            for hh in range(1, heads_per_block):
                acc = jnp.where(owns[hh], outs[a, pair, hh], acc)
            o_ref[a * GRID_W:(a + 1) * GRID_W, sls[pair]] = acc.astype(o_ref.dtype)


def na_attention(proj, col0, rpb, rows_per_step=4):
    s = proj.shape[0]
    rows = s // GRID_W
    win = NA_ROWS + rows_per_step - 1
    assert rows >= win and rows % rows_per_step == 0
    qb = col0 // NA_WIDTH
    table = _na_bias_table(rpb)

    def block_start(step):
        first = jnp.clip(step * rows_per_step - NA_ROWS // 2, 0, rows - NA_ROWS)
        return jnp.minimum(first, rows - win)

    def kv_spec(blk):
        return pl.BlockSpec((pl.Element(win * GRID_W), pl.Element(NA_WIDTH)),
                            lambda r: (block_start(r) * GRID_W, blk * NA_WIDTH))

    in_specs = [
        pl.BlockSpec((rows_per_step * GRID_W, NA_WIDTH), lambda r: (r, qb)),
        kv_spec(qb + 1),
        kv_spec(qb + 2),
        pl.BlockSpec(table.shape, lambda r: (0, 0, 0, 0), pipeline_mode=pl.Buffered(1)),
    ]
    return pl.pallas_call(
        functools.partial(_na_body, rows_per_step=rows_per_step, n_rows=rows),
        grid=(rows // rows_per_step,),
        in_specs=in_specs,
        out_specs=pl.BlockSpec((rows_per_step * GRID_W, NA_WIDTH), lambda r: (r, 0)),
        out_shape=jax.ShapeDtypeStruct((s, NA_WIDTH), BF16),
        compiler_params=_params(("parallel",)),
        name="na_attn",
    )(proj, proj, proj, table)


def _pack_bf16_pair(lo, hi):
    lo_bits = pltpu.bitcast(lo.astype(BF16).astype(F32), U32)
    hi_bits = pltpu.bitcast(hi.astype(BF16).astype(F32), U32)
    return (lo_bits >> 16) | (hi_bits & jnp.uint32(0xFFFF0000))


def _unpack_bf16_pair(packed):
    lo = pltpu.bitcast(packed << 16, F32)
    hi = pltpu.bitcast(packed & jnp.uint32(0xFFFF0000), F32)
    return lo, hi


def _merge_body(of_ref, ob_ref, z_ref, na_ref, ga_ref, gb_ref, x_ref, dng_ref, wa_ref, wb_ref, wo_ref,
                gffn_ref, wr_ref, br_ref, x1_ref, h2p_ref, lg_ref, dn_ref):
    hd = DN_HEAD_DIM
    for h in range(DN_HEADS):
        sl = slice(h * hd, (h + 1) * hd)
        o = of_ref[:, sl] + ob_ref[:, sl]
        r = lax.rsqrt(jnp.mean(o * o, axis=-1, keepdims=True) + RMS_EPS)
        z = z_ref[:, sl].astype(F32)
        dn_ref[:, sl] = (o * r * dng_ref[...] * (z * _sigmoid(z))).astype(BF16)
    y_a = _dot(dn_ref[...], wa_ref[...])
    y_b = _dot(na_ref[...], wb_ref[...])
    mixed = _sigmoid(ga_ref[...].astype(F32)) * y_a + _sigmoid(gb_ref[...].astype(F32)) * y_b
    x1 = x_ref[...] + _dot(mixed.astype(BF16), wo_ref[...])
    x1_ref[...] = x1
    r = lax.rsqrt(jnp.mean(x1 * x1, axis=-1, keepdims=True) + RMS_EPS)
    h2 = x1 * r * gffn_ref[...]
    h2_hi = h2.astype(BF16)
    h2_lo = (h2 - h2_hi.astype(F32)).astype(BF16)
    lg_ref[...] = (_dot(h2_hi, wr_ref[0]) + _dot(h2_lo, wr_ref[0]) + _dot(h2_hi, wr_ref[1])) + br_ref[...]
    half = h2.shape[1] // 2
    n_tiles = half // LANES
    packed = _pack_bf16_pair(h2[:, :half], h2[:, half:])
    for c in range(n_tiles):
        h2p_ref[pl.ds(c, h2.shape[0], stride=n_tiles), :] = packed[:, c * LANES:(c + 1) * LANES]


def merge(o_f, o_b, proj, z_blk, gate_blk, na_out, x, dn_norm_g, w_a, w_b, w_o, norm_ffn_g, w_router, b_router,
          tm=256):
    s, d = x.shape
    ne = w_router.shape[1]
    const = lambda i: (0, 0)
    single = pl.Buffered(1)
    w_router_hi = w_router.astype(BF16)
    w_router_lo = (w_router.astype(F32) - w_router_hi.astype(F32)).astype(BF16)
    w_router_split = jnp.stack([w_router_hi, w_router_lo])
    return pl.pallas_call(
        _merge_body,
        grid=(s // tm,),
        in_specs=[
            pl.BlockSpec((tm, DN_WIDTH), lambda i: (i, 0)),
            pl.BlockSpec((tm, DN_WIDTH), lambda i: (i, 0)),
            pl.BlockSpec((tm, DN_WIDTH), lambda i: (i, z_blk)),
            pl.BlockSpec((tm, NA_WIDTH), lambda i: (i, 0)),
            pl.BlockSpec((tm, d), lambda i: (i, gate_blk)),
            pl.BlockSpec((tm, d), lambda i: (i, gate_blk + 1)),
            pl.BlockSpec((tm, d), lambda i: (i, 0)),
            pl.BlockSpec((1, DN_HEAD_DIM), const),
            pl.BlockSpec((DN_WIDTH, d), const, pipeline_mode=single),
            pl.BlockSpec((NA_WIDTH, d), const, pipeline_mode=single),
            pl.BlockSpec((d, d), const, pipeline_mode=single),
            pl.BlockSpec((1, d), const),
            pl.BlockSpec((2, d, ne), lambda i: (0, 0, 0)),
            pl.BlockSpec((1, ne), const),
        ],
        out_specs=[
            pl.BlockSpec((tm, d), lambda i: (i, 0)),
            pl.BlockSpec((tm * (d // 2 // LANES), LANES), lambda i: (i, 0)),
            pl.BlockSpec((tm, ne), lambda i: (i, 0)),
        ],
        out_shape=[
            jax.ShapeDtypeStruct((s, d), F32),
            jax.ShapeDtypeStruct((s * (d // 2 // LANES), LANES), U32),
            jax.ShapeDtypeStruct((s, ne), F32),
        ],
        scratch_shapes=[pltpu.VMEM((tm, DN_WIDTH), BF16)],
        compiler_params=_params(("parallel",)),
        name="merge",
    )(o_f, o_b, proj, na_out, proj, proj, x, dn_norm_g.reshape(1, -1).astype(F32), w_a, w_b, w_o,
      norm_ffn_g.reshape(1, d).astype(F32), w_router_split, b_router.reshape(1, ne).astype(F32))


def _route_body(lg_ref, ti_ref, tw_ref, cnt_ref, carry_ref):
    @pl.when(pl.program_id(0) == 0)
    def _():
        carry_ref[...] = jnp.zeros(carry_ref.shape, F32)

    lg = lg_ref[...]
    tm, ne = lg.shape
    lane = lax.broadcasted_iota(I32, (tm, ne), 1).astype(F32)
    work = lg
    vals, idxs = [], []
    onehot = jnp.zeros((tm, ne), F32)
    for _ in range(TOP_K):
        m = jnp.max(work, axis=-1, keepdims=True)
        idx = jnp.min(jnp.where(work == m, lane, float(ne)), axis=-1, keepdims=True)
        hit = lane == idx
        vals.append(m)
        idxs.append(idx)
        onehot = onehot + jnp.where(hit, 1.0, 0.0)
        work = jnp.where(hit, -jnp.inf, work)
    exps = [jnp.exp(v - vals[0]) for v in vals]
    denom = exps[0] + exps[1] + exps[2] + exps[3]
    ri = lax.broadcasted_iota(I32, (tm, tm), 0)
    ci = lax.broadcasted_iota(I32, (tm, tm), 1)
    strict = jnp.where(ci < ri, 1.0, 0.0).astype(BF16)
    before = _dot(strict, onehot.astype(BF16)) + carry_ref[0:1, 0:ne]
    lane_o = lax.broadcasted_iota(I32, (tm, LANES), 1)
    ti = jnp.zeros((tm, LANES), I32)
    tw = jnp.zeros((tm, LANES), F32)
    for kk in range(TOP_K):
        rank = jnp.sum(jnp.where(lane == idxs[kk], before, 0.0), axis=-1, keepdims=True).astype(I32)
        ti = jnp.where(lane_o == kk, idxs[kk].astype(I32), ti)
        ti = jnp.where(lane_o == TOP_K + kk, rank, ti)
        tw = jnp.where(lane_o == kk, exps[kk] / denom, tw)
    ti_ref[...] = ti
    tw_ref[...] = tw
    total = carry_ref[0:1, 0:ne] + jnp.sum(onehot, axis=0, keepdims=True)
    carry_ref[0:1, 0:ne] = total
    cnt_ref[...] = jnp.zeros(cnt_ref.shape, F32)
    cnt_ref[0:1, 0:ne] = total


def route(logits, tm=1024):
    s, ne = logits.shape
    tm = min(tm, s)
    return pl.pallas_call(
        _route_body,
        grid=(s // tm,),
        in_specs=[pl.BlockSpec((tm, ne), lambda i: (i, 0))],
        out_specs=[
            pl.BlockSpec((tm, LANES), lambda i: (i, 0)),
            pl.BlockSpec((tm, LANES), lambda i: (i, 0)),
            pl.BlockSpec((8, LANES), lambda i: (0, 0)),
        ],
        out_shape=[
            jax.ShapeDtypeStruct((s, LANES), I32),
            jax.ShapeDtypeStruct((s, LANES), F32),
            jax.ShapeDtypeStruct((8, LANES), F32),
        ],
        scratch_shapes=[pltpu.VMEM((8, LANES), F32)],
        compiler_params=_params(("arbitrary",)),
        name="route",
    )(logits)


def _dest_body(ti_ref, ps_ref, d_ref):
    ti = ti_ref[...].astype(F32)
    tm = ti.shape[0]
    lane = lax.broadcasted_iota(I32, (tm, LANES), 1)
    lane_f = lane.astype(F32)
    ps = ps_ref[0:1, :].astype(F32)
    out = jnp.zeros((tm, LANES), F32)
    for kk in range(TOP_K):
        e = jnp.sum(jnp.where(lane == kk, ti, 0.0), axis=-1, keepdims=True)
        rank = jnp.sum(jnp.where(lane == TOP_K + kk, ti, 0.0), axis=-1, keepdims=True)
        start = jnp.sum(jnp.where(lane_f == e, ps, 0.0), axis=-1, keepdims=True)
        out = jnp.where(lane == kk, start + rank, out)
    d_ref[...] = out.astype(I32)


def route_dest(ti, pad_start, tm=2048):
    s = ti.shape[0]
    tm = min(tm, s)
    ps = jnp.zeros((8, LANES), I32).at[0, :N_EXPERTS].set(pad_start)
    return pl.pallas_call(
        _dest_body,
        grid=(s // tm,),
        in_specs=[pl.BlockSpec((tm, LANES), lambda i: (i, 0)), pl.BlockSpec((8, LANES), lambda i: (0, 0))],
        out_specs=pl.BlockSpec((tm, LANES), lambda i: (i, 0)),
        out_shape=jax.ShapeDtypeStruct((s, LANES), I32),
        compiler_params=_params(("parallel",)),
        name="route_dest",
    )(ti, ps)


def _scatter_body(pend_ref, padded_ref, dest_ref, h_ref, xr_ref, zero_ref, sem):
    tm = h_ref.shape[0]

    def zero_copy(e):
        return pltpu.make_async_copy(zero_ref, xr_ref.at[pl.ds(pend_ref[e] - MOE_SUB, MOE_SUB)], sem)

    @pl.when(pl.program_id(0) == 0)
    def _():
        zero_ref[...] = jnp.zeros(zero_ref.shape, U32)

        def start(e, carry):
            @pl.when(padded_ref[e] > 0)
            def _():
                zero_copy(e).start()
            return carry

        def wait(e, carry):
            @pl.when(padded_ref[e] > 0)
            def _():
                zero_copy(e).wait()
            return carry

        lax.fori_loop(0, N_EXPERTS, start, 0)
        lax.fori_loop(0, N_EXPERTS, wait, 0)

        def slack_copy(b):
            return pltpu.make_async_copy(zero_ref, xr_ref.at[pl.ds(b * MOE_SUB, MOE_SUB)], sem)

        def slack_start(b, carry):
            slack_copy(b).start()
            return carry

        def slack_wait(b, carry):
            slack_copy(b).wait()
            return carry

        first_slack = pend_ref[N_EXPERTS - 1] // MOE_SUB
        lax.fori_loop(first_slack, xr_ref.shape[0] // MOE_SUB, slack_start, 0)
        lax.fori_loop(first_slack, xr_ref.shape[0] // MOE_SUB, slack_wait, 0)

    def row_copy(r, kk):
        d = dest_ref[r * TOP_K + kk]
        return pltpu.make_async_copy(h_ref.at[r], xr_ref.at[d], sem)

    def start(r, carry):
        for kk in range(TOP_K):
            row_copy(r, kk).start(priority=kk % 2)
        return carry

    lax.fori_loop(0, tm, start, 0)
    all_rows = xr_ref.at[pl.ds(0, tm * TOP_K)]
    pltpu.make_async_copy(all_rows, all_rows, sem).wait()


def moe_scatter(h2p, dest, pad_end, padded, n_rows, tm=256):
    s, nt, _ = h2p.shape
    grid_spec = pltpu.PrefetchScalarGridSpec(
        num_scalar_prefetch=2,
        grid=(s // tm,),
        in_specs=[
            pl.BlockSpec((tm * TOP_K,), lambda i, *_: (i,), memory_space=pltpu.SMEM),
            pl.BlockSpec((tm, nt, LANES), lambda i, *_: (i, 0, 0)),
        ],
        out_specs=pl.BlockSpec(memory_space=pl.ANY),
        scratch_shapes=[pltpu.VMEM((MOE_SUB, nt, LANES), U32), pltpu.SemaphoreType.DMA(())],
    )
    return pl.pallas_call(
        _scatter_body,
        grid_spec=grid_spec,
        out_shape=jax.ShapeDtypeStruct((n_rows, nt, LANES), U32),
        compiler_params=_params(("arbitrary",)),
        name="moe_scatter",
    )(pad_end, padded, dest, h2p)


def _moe_body(ie_ref, ir_ref, inb_ref, ni_ref, xr_ref, wg_ref, wu_ref, wd_ref, bg_ref, bu_ref, bd_ref, y_ref,
              stage_ref, xb_ref, acc_ref, wgb_ref, wub_ref, wdb_ref, sem_in, sem_out, *, n_chunks):
    ystage_ref = stage_ref
    w = pl.program_id(0)
    j = pl.program_id(1)
    sb = MOE_SUB
    d = acc_ref.shape[1]
    half = d // 2
    nx = half // LANES
    ny = half // LANES

    @pl.when(w < ni_ref[0])
    def _():
        nb = inb_ref[w]
        r0 = ir_ref[w]

        def in_copy(i, slot):
            src = xr_ref.at[pl.ds(pl.multiple_of((r0 + i * sb) * nx, sb * nx), sb * nx), :]
            return pltpu.make_async_copy(src, stage_ref.at[slot], sem_in.at[slot])

        @pl.when(j == 0)
        def _():
            @pl.when(w == 0)
            def _():
                ystage_ref[0] = jnp.zeros(ystage_ref.shape[1:], U32)

                def slack_copy(b):
                    dst = y_ref.at[pl.ds(pl.multiple_of(b * (sb * ny), sb * ny), sb * ny), :]
                    return pltpu.make_async_copy(ystage_ref.at[0], dst, sem_out.at[0])

                def slack_start(b, carry):
                    slack_copy(b).start()
                    return carry

                def slack_wait(b, carry):
                    slack_copy(b).wait()
                    return carry

                n_blocks = y_ref.shape[0] // (sb * ny)
                lax.fori_loop(ni_ref[1], n_blocks, slack_start, 0)
                lax.fori_loop(ni_ref[1], n_blocks, slack_wait, 0)

            in_copy(0, 0).start(priority=1)

            def load(i, carry):
                slot = i % 2

                @pl.when(i + 1 < nb)
                def _():
                    in_copy(i + 1, 1 - slot).start(priority=1)

                in_copy(i, slot).wait()
                rows = pl.ds(pl.multiple_of(i * sb, sb), sb)
                for c in range(nx):
                    lo, hi = _unpack_bf16_pair(stage_ref[slot, pl.ds(c, sb, stride=nx), :])
                    xb_ref[rows, c * LANES:(c + 1) * LANES] = lo.astype(BF16)
                    xb_ref[rows, half + c * LANES:half + (c + 1) * LANES] = hi.astype(BF16)
                return carry

            lax.fori_loop(0, nb, load, 0)

        def cast_weights():
            wgb_ref[...] = wg_ref[0].astype(BF16)
            wub_ref[...] = wu_ref[0].astype(BF16)
            wdb_ref[...] = wd_ref[0].astype(BF16)

        def block(first, n_sub, i, carry):
            start = i * sb if isinstance(i, int) else pl.multiple_of(i * sb, sb)
            rows = pl.ds(start, n_sub * sb)
            xs = xb_ref[rows, :]
            gate = jnp.minimum(_dot(xs, wgb_ref[...]) + bg_ref[0], SWIGLU_LIMIT)
            up = jnp.clip(_dot(xs, wub_ref[...]) + bu_ref[0], -SWIGLU_LIMIT, SWIGLU_LIMIT)
            act = (up + 1.0) * (gate * _sigmoid(SWIGLU_ALPHA * gate))
            contrib = _dot(act.astype(BF16), wdb_ref[...])
            if first:
                acc_ref[rows, :] = contrib
            else:
                acc_ref[rows, :] += contrib
            return carry

        def all_blocks(first):
            def pair(i2, carry):
                return block(first, 2, 2 * i2, carry)

            @pl.when(nb >= 2)
            def _():
                cast_weights()
                block(first, 2, 0, 0)
                lax.fori_loop(1, nb // 2, pair, 0)

                @pl.when(nb % 2 == 1)
                def _():
                    block(first, 1, nb - 1, 0)

            @pl.when(nb < 2)
            def _():
                cast_weights()
                block(first, 1, 0, 0)

        @pl.when(j == 0)
        def _():
            all_blocks(True)

        @pl.when(j > 0)
        def _():
            all_blocks(False)

        @pl.when(j == n_chunks - 1)
        def _():
            def out_copy(i, slot):
                dst = y_ref.at[pl.ds(pl.multiple_of((r0 + i * sb) * ny, sb * ny), sb * ny), :]
                return pltpu.make_async_copy(ystage_ref.at[slot], dst, sem_out.at[slot])

            def store(i, carry):
                slot = i % 2

                @pl.when(i >= 2)
                def _():
                    out_copy(i - 2, slot).wait()

                rows = pl.ds(pl.multiple_of(i * sb, sb), sb)
                for c in range(ny):
                    lo = slice(c * LANES, (c + 1) * LANES)
                    hi = slice(half + c * LANES, half + (c + 1) * LANES)
                    ystage_ref[slot, pl.ds(c, sb, stride=ny), :] = _pack_bf16_pair(
                        acc_ref[rows, lo] + bd_ref[0, :, lo], acc_ref[rows, hi] + bd_ref[0, :, hi])
                out_copy(i, slot).start(priority=1)
                return carry

            lax.fori_loop(0, nb, store, 0)

            @pl.when(nb >= 2)
            def _():
                out_copy(nb - 2, nb % 2).wait()

            out_copy(nb - 1, (nb - 1) % 2).wait()


def moe_ffn(x_rows, w_gate_up, b_gate_up, w_down, b_down, item_e, item_row, item_nb, n_items, max_items):
    ne, d, two_de = w_gate_up.shape
    nx = d // 2 // LANES
    ny = d // 2 // LANES
    n_rows = x_rows.shape[0] // nx
    de = two_de // 2
    tn = MOE_TN
    n_chunks = de // tn
    last = n_chunks - 1

    def chunk(w, j, ni):
        return jnp.where(w < ni[0], j, last)

    grid_spec = pltpu.PrefetchScalarGridSpec(
        num_scalar_prefetch=4,
        grid=(max_items, n_chunks),
        in_specs=[
            pl.BlockSpec(memory_space=pl.ANY),
            pl.BlockSpec((1, d, tn), lambda w, j, ie, ir, inb, ni: (ie[w], 0, chunk(w, j, ni))),
            pl.BlockSpec((1, d, tn), lambda w, j, ie, ir, inb, ni: (ie[w], 0, n_chunks + chunk(w, j, ni))),
            pl.BlockSpec((1, tn, d), lambda w, j, ie, ir, inb, ni: (ie[w], chunk(w, j, ni), 0)),
            pl.BlockSpec((1, 1, tn), lambda w, j, ie, ir, inb, ni: (ie[w], 0, chunk(w, j, ni))),
            pl.BlockSpec((1, 1, tn), lambda w, j, ie, ir, inb, ni: (ie[w], 0, n_chunks + chunk(w, j, ni))),
            pl.BlockSpec((1, 1, d), lambda w, j, ie, ir, inb, ni: (ie[w], 0, 0)),
        ],
        out_specs=pl.BlockSpec(memory_space=pl.ANY),
        scratch_shapes=[
            pltpu.VMEM((2, MOE_SUB * nx, LANES), U32),
            pltpu.VMEM((MOE_TM, d), BF16),
            pltpu.VMEM((MOE_TM, d), F32),
            pltpu.VMEM((d, tn), BF16),
            pltpu.VMEM((d, tn), BF16),
            pltpu.VMEM((tn, d), BF16),
            pltpu.SemaphoreType.DMA((2,)),
            pltpu.SemaphoreType.DMA((2,)),
        ],
    )
    return pl.pallas_call(
        functools.partial(_moe_body, n_chunks=n_chunks),
        grid_spec=grid_spec,
        out_shape=jax.ShapeDtypeStruct((n_rows * ny, LANES), U32),
        compiler_params=_params(("arbitrary", "arbitrary"), MOE_VMEM_LIMIT),
        name="moe_ffn",
    )(item_e, item_row, item_nb, n_items, x_rows, w_gate_up, w_gate_up, w_down,
      b_gate_up.reshape(ne, 1, two_de), b_gate_up.reshape(ne, 1, two_de), b_down.reshape(ne, 1, d))


def _final_body(dest_ref, dest_next_ref, x1_ref, tw_ref, p_ref, gple_ref, wg_ref, wp_ref, gfin_ref, y_ref, o_ref,
                ybuf_ref, sem, *, last_layer):
    i = pl.program_id(0)
    tm = x1_ref.shape[0]
    half = x1_ref.shape[1] // 2
    ny = half // LANES
    slot = i % 2

    def row_copy(d_ref, to_slot, r, kk):
        dst = ybuf_ref.at[to_slot, kk, pl.ds(pl.multiple_of(r * ny, ny), ny), :]
        return pltpu.make_async_copy(y_ref.at[d_ref[r * TOP_K + kk]], dst, sem.at[to_slot])

    def issue(d_ref, to_slot):
        def body(r, carry):
            for kk in range(TOP_K):
                row_copy(d_ref, to_slot, r, kk).start(priority=kk % 2)
            return carry

        lax.fori_loop(0, tm, body, 0)

    def drain(to_slot):
        pltpu.make_async_copy(ybuf_ref.at[to_slot], ybuf_ref.at[to_slot], sem.at[to_slot]).wait()

    @pl.when(i == 0)
    def _():
        issue(dest_ref, 0)

    drain(slot)

    for r in range(tm):
        for kk in range(TOP_K):
            row_copy(dest_next_ref, 1 - slot, r, kk).start(priority=kk % 2)

    tw = tw_ref[...]
    los, his = [], []
    for c in range(ny):
        acc_lo = x1_ref[:, c * LANES:(c + 1) * LANES]
        acc_hi = x1_ref[:, half + c * LANES:half + (c + 1) * LANES]
        for kk in range(TOP_K):
            lo, hi = _unpack_bf16_pair(ybuf_ref[slot, kk, pl.ds(c, tm, stride=ny), :])
            acc_lo = acc_lo + tw[:, kk:kk + 1] * lo
            acc_hi = acc_hi + tw[:, kk:kk + 1] * hi
        los.append(acc_lo)
        his.append(acc_hi)
    x2 = jnp.concatenate(los + his, axis=1)
    r = lax.rsqrt(jnp.mean(x2 * x2, axis=-1, keepdims=True) + RMS_EPS)
    n = (x2 * r * gple_ref[...]).astype(BF16)
    gate = _sigmoid(_dot(n, wg_ref[...]))
    x3 = x2 + gate * _dot(p_ref[...].astype(BF16), wp_ref[...])
    if last_layer:
        r = lax.rsqrt(jnp.mean(x3 * x3, axis=-1, keepdims=True) + RMS_EPS)
        x3 = x3 * r * gfin_ref[...]
    o_ref[...] = x3

    @pl.when(i + 1 == pl.num_programs(0))
    def _():
        drain(1 - slot)


def final(dest, x1, tw, p, norm_ple_g, w_gate, w_proj, norm_final_g, y_rows, last_layer, tm=256):
    s, d = x1.shape
    pd = p.shape[1]
    ny = d // 2 // LANES
    n_steps = s // tm
    const = lambda i: (0, 0)
    single = pl.Buffered(1)
    return pl.pallas_call(
        functools.partial(_final_body, last_layer=last_layer),
        grid=(n_steps,),
        in_specs=[
            pl.BlockSpec((tm * TOP_K,), lambda i: (i,), memory_space=pltpu.SMEM),
            pl.BlockSpec((tm * TOP_K,), lambda i: (jnp.minimum(i + 1, n_steps - 1),), memory_space=pltpu.SMEM),
            pl.BlockSpec((tm, d), lambda i: (i, 0)),
            pl.BlockSpec((tm, LANES), lambda i: (i, 0)),
            pl.BlockSpec((tm, pd), lambda i: (i, 0)),
            pl.BlockSpec((1, d), const),
            pl.BlockSpec((d, d), const, pipeline_mode=single),
            pl.BlockSpec((pd, d), const, pipeline_mode=single),
            pl.BlockSpec((1, d), const),
            pl.BlockSpec(memory_space=pl.ANY),
        ],
        out_specs=pl.BlockSpec((tm, d), lambda i: (i, 0)),
        out_shape=jax.ShapeDtypeStruct((s, d), F32),
        scratch_shapes=[pltpu.VMEM((2, TOP_K, tm * ny, LANES), U32), pltpu.SemaphoreType.DMA((2,))],
        compiler_params=_params(("arbitrary",)),
        name="final",
    )(dest, dest, x1, tw, p, norm_ple_g.reshape(1, d).astype(F32), w_gate, w_proj,
      norm_final_g.reshape(1, d).astype(F32), y_rows.reshape(-1, ny, LANES))


def _moe_tables(counts, n_rows):
    sub, tm = MOE_SUB, MOE_TM
    max_items = N_EXPERTS + n_rows // tm
    padded = (counts + sub - 1) // sub * sub
    pad_end = jnp.cumsum(padded)
    pad_start = pad_end - padded
    n_it = (padded + tm - 1) // tm
    it_end = jnp.cumsum(n_it)
    it_start = it_end - n_it
    n_items = it_end[-1]
    w = jnp.arange(max_items, dtype=I32)
    live = w < n_items
    w_eff = jnp.minimum(w, n_items - 1)
    e_w = jnp.minimum(jnp.sum((it_end[None, :] <= w_eff[:, None]).astype(I32), axis=1), N_EXPERTS - 1)
    m_w = w_eff - it_start[e_w]
    row_w = pad_start[e_w] + m_w * tm
    nb_w = jnp.clip((padded[e_w] - m_w * tm) // sub, 0, tm // sub)
    nb_w = jnp.where(live, nb_w, 0)
    as_i32 = lambda a: a.astype(I32)
    counts_w = jnp.stack([n_items, pad_end[-1] // sub])
    return (as_i32(pad_start), as_i32(pad_end), as_i32(padded), e_w, as_i32(row_w), as_i32(nb_w),
            as_i32(counts_w), max_items)


def _layer(x, p, norm_mix_g, w_in, dn_conv_w, dn_a_log, dn_dt_bias, dn_norm_g, na_rpb, w_branch_a, w_branch_b,
           w_out, norm_ffn_g, w_router, b_router, w_gate_up, b_gate_up, w_down, b_down, norm_ple_g,
           w_ple_gate, w_ple_proj, norm_final_g, last_layer):
    s, d = x.shape
    c_qkv, c_z = 3 * DN_WIDTH, DN_WIDTH
    w_t = jnp.swapaxes(w_in, 0, 1)
    w_main = w_prep(w_t)
    z_blk = c_qkv // DN_WIDTH
    gate_blk = (c_qkv + c_z) // d
    na_col0 = c_qkv + c_z + 2 * d

    proj, small = in_projection(x, norm_mix_g.astype(F32), w_main, w_t)

    qkv = dn_prep(proj, dn_conv_w.astype(F32))
    col, row = dn_gates(small, dn_a_log, dn_dt_bias)
    u, w, qd, kt, qk = dn_chunk(qkv, col, row)
    o_f, o_b = dn_scan(u, w, qd, kt, qk, col)

    na_out = na_attention(proj, na_col0, na_rpb)

    x1, h2p, logits = merge(o_f, o_b, proj, z_blk, gate_blk, na_out, x, dn_norm_g, w_branch_a.astype(BF16),
                            w_branch_b.astype(BF16), w_out.astype(BF16), norm_ffn_g, w_router, b_router)

    ti, tw, cnt = route(logits)
    counts = cnt[0, :N_EXPERTS].astype(I32)
    n_rows = (s * TOP_K + N_EXPERTS * (MOE_SUB - 1) + MOE_SUB - 1) // MOE_SUB * MOE_SUB
    pad_start, pad_end, padded, item_e, item_row, item_nb, n_items, max_items = _moe_tables(counts, n_rows)
    dest = route_dest(ti, pad_start)[:, :TOP_K].reshape(-1)

    nx = d // 2 // LANES
    x_rows = moe_scatter(h2p.reshape(s, nx, LANES), dest, pad_end, padded, n_rows).reshape(n_rows * nx, LANES)
    y_rows = moe_ffn(x_rows, w_gate_up, b_gate_up, w_down, b_down, item_e, item_row, item_nb, n_items, max_items)

    return final(dest, x1, tw, p, norm_ple_g, w_ple_gate.astype(BF16), w_ple_proj.astype(BF16), norm_final_g,
                 y_rows, last_layer)


def kernel(x, p, norm_mix_g, w_in, dn_conv_w, dn_a_log, dn_dt_bias, dn_norm_g, na_rpb, w_branch_a, w_branch_b, w_out, norm_ffn_g, w_router, b_router, w_gate_up, b_gate_up, w_down, b_down, norm_ple_g, w_ple_gate, w_ple_proj, norm_final_g):
    bsz, s, d = x.shape
    depth = w_in.shape[0]
    outs = []
    for b in range(bsz):
        xb = x[b]
        for i in range(depth):
            xb = _layer(xb, p[i, b], norm_mix_g[i], w_in[i], dn_conv_w[i], dn_a_log[i], dn_dt_bias[i], dn_norm_g[i],
                        na_rpb[i], w_branch_a[i], w_branch_b[i], w_out[i], norm_ffn_g[i], w_router[i], b_router[i],
                        w_gate_up[i], b_gate_up[i], w_down[i], b_down[i], norm_ple_g[i], w_ple_gate[i],
                        w_ple_proj[i], norm_final_g, i == depth - 1)
        outs.append(xb)
    return jnp.stack(outs, axis=0)
```
